```python
import math
import jax
import jax.numpy as jnp
from jax import lax
import numpy as np


D_MODEL = 1024
BATCH = 4
SEQ = 4096
DEPTH = 2

N_ATTN_HEADS = 8
HEAD_DIM = 64
ATTN_WIDTH = N_ATTN_HEADS * HEAD_DIM
LRU_WIDTH = D_MODEL // 2
LRU_BLOCKS = 8
LRU_BLOCK = LRU_WIDTH // LRU_BLOCKS
LRU_CONV = 4
LRU_C = 8.0
CONV_WIDTH = D_MODEL // 2
CONV_KERNEL = 31
MOBA_BLOCK = 256
MOBA_TOPK = 3
MOBA_QCHUNK = 64
IDX_HEADS = 8
IDX_DIM = 64
DSA_TOPK_MAX = 256
DSA_QCHUNK = 128
REL_BUCKETS = 32
REL_MAX_EXACT = REL_BUCKETS // 2
REL_MAX_DIST = 128
D_FF = 2816
EPS = 1e-6
NEG = -1e30
EVEN_SPLITS = (LRU_WIDTH, LRU_WIDTH, ATTN_WIDTH, ATTN_WIDTH, ATTN_WIDTH)
ODD_SPLITS = (2 * CONV_WIDTH, ATTN_WIDTH, ATTN_WIDTH, ATTN_WIDTH, IDX_HEADS * IDX_DIM, IDX_DIM, IDX_HEADS)

kernel_name = 'hybrid_rglru_moba_conformer_dsa_macaron'


def _split(z, sizes):
    idx = list(np.cumsum(sizes)[:-1])
    return jnp.split(z, idx, axis=-1)


def rms_norm(x, g):
    xf = x.astype(jnp.float32)
    y = xf * lax.rsqrt(jnp.mean(xf * xf, axis=-1, keepdims=True) + EPS)
    return (y * g.astype(jnp.float32)).astype(x.dtype)


def layer_norm(x, g, b):
    xf = x.astype(jnp.float32)
    mu = jnp.mean(xf, axis=-1, keepdims=True)
    var = jnp.mean(jnp.square(xf - mu), axis=-1, keepdims=True)
    y = (xf - mu) * lax.rsqrt(var + EPS)
    return (y * g.astype(jnp.float32) + b.astype(jnp.float32)).astype(x.dtype)


def swiglu(x, w_gate, w_up, w_down):
    return (jax.nn.silu(x @ w_gate) * (x @ w_up)) @ w_down


def causal_depthwise_conv(x, w, b):
    width = w.shape[0]
    y = lax.conv_general_dilated(
        x, w[:, None, :].astype(x.dtype), window_strides=(1,), padding=[(width - 1, 0)],
        dimension_numbers=('NWC', 'WIO', 'NWC'), feature_group_count=x.shape[-1])
    return y + b.astype(x.dtype)


def rel_bucket(dist):
    n = jnp.maximum(dist, 0)
    nf = jnp.maximum(n, 1).astype(jnp.float32)
    large = REL_MAX_EXACT + (jnp.log(nf / REL_MAX_EXACT) / math.log(REL_MAX_DIST / REL_MAX_EXACT)
                             * (REL_BUCKETS - REL_MAX_EXACT)).astype(jnp.int32)
    large = jnp.minimum(large, REL_BUCKETS - 1)
    return jnp.where(n < REL_MAX_EXACT, n, large)


def head_rms(x, g):
    return rms_norm(x, g)


def _lru_combine(e1, e2):
    a1, b1 = e1
    a2, b2 = e2
    return a1 * a2, a2 * b1 + b2


def rg_lru(x, w_a, b_a, w_x, b_x, lam):
    bsz, seq, width = x.shape
    xb = x.reshape(bsz, seq, LRU_BLOCKS, LRU_BLOCK)
    r = jax.nn.sigmoid(jnp.einsum('bsgi,gij->bsgj', xb, w_a).reshape(bsz, seq, width) + b_a)
    i = jax.nn.sigmoid(jnp.einsum('bsgi,gij->bsgj', xb, w_x).reshape(bsz, seq, width) + b_x)
    log_a = -LRU_C * r.astype(jnp.float32) * jax.nn.softplus(-lam.astype(jnp.float32))
    a = jnp.exp(log_a)
    mult = jnp.sqrt(-jnp.expm1(2.0 * log_a))
    u = mult * (i * x).astype(jnp.float32)
    _, h = lax.associative_scan(_lru_combine, (a, u), axis=1)
    return h.astype(x.dtype)


def moba_attention(q, k, v, rel_bias):
    bsz, seq, nh, hd = q.shape
    nb = -(-seq // MOBA_BLOCK)
    pad = nb * MOBA_BLOCK - seq
    qh = q.transpose(0, 2, 1, 3)
    kh = jnp.pad(k.transpose(0, 2, 1, 3), ((0, 0), (0, 0), (0, pad), (0, 0)))
    vh = jnp.pad(v.transpose(0, 2, 1, 3), ((0, 0), (0, 0), (0, pad), (0, 0)))
    kblk = kh.reshape(bsz, nh, nb, MOBA_BLOCK, hd)
    vblk = vh.reshape(bsz, nh, nb, MOBA_BLOCK, hd)
    kmean = jnp.mean(kblk.astype(jnp.float32), axis=3).astype(k.dtype)
    topk = min(MOBA_TOPK, nb)
    scale = hd ** -0.5
    bidx = jnp.arange(bsz)[:, None, None, None]
    hidx = jnp.arange(nh)[None, :, None, None]
    h5 = jnp.arange(nh)[None, :, None, None, None]
    offs = jnp.arange(MOBA_BLOCK)
    blk_ids = jnp.arange(nb)

    def chunk(t0):
        qc = lax.dynamic_slice_in_dim(qh, t0, MOBA_QCHUNK, axis=2)
        t = t0 + jnp.arange(MOBA_QCHUNK)
        own = t0 // MOBA_BLOCK
        gate = jnp.einsum('bhcd,bhnd->bhcn', qc, kmean).astype(jnp.float32)
        gate = jnp.where(blk_ids < own, gate, NEG)
        _, sel = lax.top_k(gate, topk)
        valid = jnp.arange(topk) < own
        ksel = kblk[bidx, hidx, sel]
        vsel = vblk[bidx, hidx, sel]
        s_pos = sel[..., None] * MOBA_BLOCK + offs
        lsel = jnp.einsum('bhcd,bhcjsd->bhcjs', qc, ksel).astype(jnp.float32) * scale
        lsel = lsel + rel_bias[rel_bucket(t[None, None, :, None, None] - s_pos), h5].astype(jnp.float32)
        lsel = jnp.where(valid[:, None], lsel, NEG)
        kown = lax.dynamic_slice_in_dim(kh, own * MOBA_BLOCK, MOBA_BLOCK, axis=2)
        vown = lax.dynamic_slice_in_dim(vh, own * MOBA_BLOCK, MOBA_BLOCK, axis=2)
        s_own = own * MOBA_BLOCK + offs
        lown = jnp.einsum('bhcd,bhsd->bhcs', qc, kown).astype(jnp.float32) * scale
        lown = lown + rel_bias[rel_bucket(t[:, None] - s_own[None, :])].transpose(2, 0, 1).astype(jnp.float32)
        lown = jnp.where(s_own[None, :] <= t[:, None], lown, NEG)
        logits = jnp.concatenate([lsel.reshape(bsz, nh, MOBA_QCHUNK, topk * MOBA_BLOCK), lown], axis=-1)
        p = jax.nn.softmax(logits, axis=-1).astype(v.dtype)
        psel = p[..., :topk * MOBA_BLOCK].reshape(bsz, nh, MOBA_QCHUNK, topk, MOBA_BLOCK)
        pown = p[..., topk * MOBA_BLOCK:]
        return (jnp.einsum('bhcjs,bhcjsd->bhcd', psel, vsel)
                + jnp.einsum('bhcs,bhsd->bhcd', pown, vown))

    starts = jnp.arange(seq // MOBA_QCHUNK) * MOBA_QCHUNK
    outs = lax.map(chunk, starts)
    return outs.transpose(1, 0, 3, 2, 4).reshape(bsz, seq, nh * hd)


def dsa_attention(q, k, v, qi, ki, wi, rel_bias):
    bsz, seq, nh, hd = q.shape
    n_sel = min(DSA_TOPK_MAX, seq // 4)
    qchunk = min(DSA_QCHUNK, seq)
    scale = hd ** -0.5
    idx_scale = IDX_DIM ** -0.5 * IDX_HEADS ** -0.5
    bidx = jnp.arange(bsz)[:, None, None]
    key_pos = jnp.arange(seq)

    def chunk(t0):
        t = t0 + jnp.arange(qchunk)
        qic = lax.dynamic_slice_in_dim(qi, t0, qchunk, axis=1)
        wic = lax.dynamic_slice_in_dim(wi, t0, qchunk, axis=1)
        sc = jax.nn.relu(jnp.einsum('bchd,bsd->bchs', qic, ki).astype(jnp.float32))
        isc = jnp.einsum('bchs,bch->bcs', sc, wic.astype(jnp.float32)) * idx_scale
        isc = jnp.where(key_pos[None, None, :] <= t[None, :, None], isc, NEG)
        _, sel = lax.top_k(isc, n_sel)
        valid = sel <= t[None, :, None]
        ksel = k[bidx, sel]
        vsel = v[bidx, sel]
        qc = lax.dynamic_slice_in_dim(q, t0, qchunk, axis=1)
        logit = jnp.einsum('bchd,bckhd->bhck', qc, ksel).astype(jnp.float32) * scale
        bias = rel_bias[rel_bucket(t[None, :, None] - sel)].transpose(0, 3, 1, 2).astype(jnp.float32)
        logit = jnp.where(valid[:, None], logit + bias, NEG)
        p = jax.nn.softmax(logit, axis=-1).astype(v.dtype)
        return jnp.einsum('bhck,bckhd->bchd', p, vsel)

    starts = jnp.arange(seq // qchunk) * qchunk
    outs = lax.map(chunk, starts)
    return outs.transpose(1, 0, 2, 3, 4).reshape(bsz, seq, nh * hd)


def even_mixer(h, w_in, conv_w, conv_b, ra_w, ra_b, ix_w, ix_b, lam, q_norm, k_norm, w_out, rel_bias):
    bsz, seq, _ = h.shape
    gate, xr, q, k, v = _split(h @ w_in, EVEN_SPLITS)
    xr = causal_depthwise_conv(xr, conv_w, conv_b)
    ya = rg_lru(xr, ra_w, ra_b, ix_w, ix_b, lam) * jax.nn.gelu(gate)
    q = head_rms(q.reshape(bsz, seq, N_ATTN_HEADS, HEAD_DIM), q_norm)
    k = head_rms(k.reshape(bsz, seq, N_ATTN_HEADS, HEAD_DIM), k_norm)
    v = v.reshape(bsz, seq, N_ATTN_HEADS, HEAD_DIM)
    yb = moba_attention(q, k, v, rel_bias)
    return jnp.concatenate([ya, yb], axis=-1) @ w_out


def odd_mixer(h, w_in, dw_w, dw_b, ln_g, ln_b, q_norm, k_norm, w_out, rel_bias):
    bsz, seq, _ = h.shape
    glu, q, k, v, qi, ki, wi = _split(h @ w_in, ODD_SPLITS)
    ca, cg = jnp.split(glu, 2, axis=-1)
    c = ca * jax.nn.sigmoid(cg)
    c = causal_depthwise_conv(c, dw_w, dw_b)
    c = jax.nn.silu(layer_norm(c, ln_g, ln_b))
    q = head_rms(q.reshape(bsz, seq, N_ATTN_HEADS, HEAD_DIM), q_norm)
    k = head_rms(k.reshape(bsz, seq, N_ATTN_HEADS, HEAD_DIM), k_norm)
    v = v.reshape(bsz, seq, N_ATTN_HEADS, HEAD_DIM)
    qi = qi.reshape(bsz, seq, IDX_HEADS, IDX_DIM)
    yd = dsa_attention(q, k, v, qi, ki, wi, rel_bias)
    return jnp.concatenate([c, yd], axis=-1) @ w_out


def setup_inputs(seed: int = 0) -> dict:
    key = jax.random.key(seed)
    ks = iter(jax.random.split(key, 40))
    n_even = (DEPTH + 1) // 2
    n_odd = DEPTH // 2

    def nrm(shape, fan_in):
        return jax.random.normal(next(ks), shape, jnp.float32) * fan_in ** -0.5

    def gain(shape):
        return 1.0 + 0.02 * jax.random.normal(next(ks), shape, jnp.float32)

    def bias(shape):
        return 0.01 * jax.random.normal(next(ks), shape, jnp.float32)

    x = jax.random.normal(next(ks), (BATCH, SEQ, D_MODEL), jnp.float32)
    rel_bias = 0.2 * jax.random.normal(next(ks), (REL_BUCKETS, N_ATTN_HEADS), jnp.float32)
    ffn1_norm = gain((DEPTH, D_MODEL))
    ffn1_w_gate = nrm((DEPTH, D_MODEL, D_FF), D_MODEL)
    ffn1_w_up = nrm((DEPTH, D_MODEL, D_FF), D_MODEL)
    ffn1_w_down = nrm((DEPTH, D_FF, D_MODEL), D_FF)
    mix_norm = gain((DEPTH, D_MODEL))
    ffn2_norm = gain((DEPTH, D_MODEL))
    ffn2_w_gate = nrm((DEPTH, D_MODEL, D_FF), D_MODEL)
    ffn2_w_up = nrm((DEPTH, D_MODEL, D_FF), D_MODEL)
    ffn2_w_down = nrm((DEPTH, D_FF, D_MODEL), D_FF)
    ev_w_in = nrm((n_even, D_MODEL, sum(EVEN_SPLITS)), D_MODEL)
    ev_conv_w = nrm((n_even, LRU_CONV, LRU_WIDTH), LRU_CONV)
    ev_conv_b = bias((n_even, LRU_WIDTH))
    ev_ra_w = nrm((n_even, LRU_BLOCKS, LRU_BLOCK, LRU_BLOCK), LRU_BLOCK)
    ev_ra_b = bias((n_even, LRU_WIDTH))
    ev_ix_w = nrm((n_even, LRU_BLOCKS, LRU_BLOCK, LRU_BLOCK), LRU_BLOCK)
    ev_ix_b = bias((n_even, LRU_WIDTH))
    a0 = jax.random.uniform(next(ks), (n_even, LRU_WIDTH), jnp.float32, 0.9, 0.999)
    s = a0 ** (1.0 / LRU_C)
    ev_lambda = jnp.log(s) - jnp.log1p(-s)
    ev_q_norm = gain((n_even, HEAD_DIM))
    ev_k_norm = gain((n_even, HEAD_DIM))
    ev_w_out = nrm((n_even, LRU_WIDTH + ATTN_WIDTH, D_MODEL), LRU_WIDTH + ATTN_WIDTH)
    od_w_in = nrm((n_odd, D_MODEL, sum(ODD_SPLITS)), D_MODEL)
    od_dw_w = nrm((n_odd, CONV_KERNEL, CONV_WIDTH), CONV_KERNEL)
    od_dw_b = bias((n_odd, CONV_WIDTH))
    od_ln_g = gain((n_odd, CONV_WIDTH))
    od_ln_b = bias((n_odd, CONV_WIDTH))
    od_q_norm = gain((n_odd, HEAD_DIM))
    od_k_norm = gain((n_odd, HEAD_DIM))
    od_w_out = nrm((n_odd, CONV_WIDTH + ATTN_WIDTH, D_MODEL), CONV_WIDTH + ATTN_WIDTH)
    return {
        'x': x, 'rel_bias': rel_bias,
        'ffn1_norm': ffn1_norm, 'ffn1_w_gate': ffn1_w_gate, 'ffn1_w_up': ffn1_w_up, 'ffn1_w_down': ffn1_w_down,
        'mix_norm': mix_norm,
        'ffn2_norm': ffn2_norm, 'ffn2_w_gate': ffn2_w_gate, 'ffn2_w_up': ffn2_w_up, 'ffn2_w_down': ffn2_w_down,
        'ev_w_in': ev_w_in, 'ev_conv_w': ev_conv_w, 'ev_conv_b': ev_conv_b, 'ev_ra_w': ev_ra_w, 'ev_ra_b': ev_ra_b,
        'ev_ix_w': ev_ix_w, 'ev_ix_b': ev_ix_b, 'ev_lambda': ev_lambda, 'ev_q_norm': ev_q_norm,
        'ev_k_norm': ev_k_norm, 'ev_w_out': ev_w_out,
        'od_w_in': od_w_in, 'od_dw_w': od_dw_w, 'od_dw_b': od_dw_b, 'od_ln_g': od_ln_g, 'od_ln_b': od_ln_b,
        'od_q_norm': od_q_norm, 'od_k_norm': od_k_norm, 'od_w_out': od_w_out,
    }


def reference(x, rel_bias,
              ffn1_norm, ffn1_w_gate, ffn1_w_up, ffn1_w_down,
              mix_norm,
              ffn2_norm, ffn2_w_gate, ffn2_w_up, ffn2_w_down,
              ev_w_in, ev_conv_w, ev_conv_b, ev_ra_w, ev_ra_b, ev_ix_w, ev_ix_b, ev_lambda,
              ev_q_norm, ev_k_norm, ev_w_out,
              od_w_in, od_dw_w, od_dw_b, od_ln_g, od_ln_b, od_q_norm, od_k_norm, od_w_out):
    for i in range(DEPTH):
        h = rms_norm(x, ffn1_norm[i])
        x = x + 0.5 * swiglu(h, ffn1_w_gate[i], ffn1_w_up[i], ffn1_w_down[i])
        h = rms_norm(x, mix_norm[i])
        j = i // 2
        if i % 2 == 0:
            x = x + even_mixer(h, ev_w_in[j], ev_conv_w[j], ev_conv_b[j], ev_ra_w[j], ev_ra_b[j],
                               ev_ix_w[j], ev_ix_b[j], ev_lambda[j], ev_q_norm[j], ev_k_norm[j],
                               ev_w_out[j], rel_bias)
        else:
            x = x + odd_mixer(h, od_w_in[j], od_dw_w[j], od_dw_b[j], od_ln_g[j], od_ln_b[j],
                              od_q_norm[j], od_k_norm[j], od_w_out[j], rel_bias)
        h = rms_norm(x, ffn2_norm[i])
        x = x + 0.5 * swiglu(h, ffn2_w_gate[i], ffn2_w_up[i], ffn2_w_down[i])
    return x
```

```python
import functools
import math

import jax
import jax.numpy as jnp
from jax import lax
from jax.experimental import pallas as pl
from jax.experimental.pallas import tpu as pltpu

F32 = jnp.float32
BF16 = jnp.bfloat16

N_HEADS = 8
HEAD_DIM = 64
ATTN_WIDTH = N_HEADS * HEAD_DIM
LRU_BLOCKS = 8
LRU_C = 8.0
MOBA_BLOCK = 256
MOBA_TOPK = 3
IDX_HEADS = 8
IDX_DIM = 64
DSA_TOPK_MAX = 256
REL_BUCKETS = 32
REL_MAX_EXACT = REL_BUCKETS // 2
REL_MAX_DIST = 128
EPS = 1e-6
NEG = -1e30
M_INIT = -1e29
ATTN_TILE = 256
HEAD_GROUP_LANES = 256
BISECT_ITERS = 32
VMEM_LIMIT = 56 * 1024 * 1024


def _cparams(sem):
    return pltpu.CompilerParams(dimension_semantics=sem, vmem_limit_bytes=VMEM_LIMIT)


def _dot(a, b):
    return jnp.dot(a, b, preferred_element_type=F32)


def _dot_nt(a, b):
    return lax.dot_general(a, b, (((1,), (1,)), ((), ())), preferred_element_type=F32)


def _split_bf16(x):
    hi = x.astype(BF16)
    lo = (x - hi.astype(F32)).astype(BF16)
    return hi, lo


def _rms_rows(x, g):
    return x * lax.rsqrt(jnp.mean(x * x, axis=-1, keepdims=True) + EPS) * g


def _head_rms(z, gmat, g):
    hi, lo = _split_bf16(z * z)
    ms = _dot(hi, gmat) + _dot(lo, gmat)
    return z * lax.rsqrt(ms + EPS) * g


def _const_spec(shape):
    nd = len(shape)
    return pl.BlockSpec(shape, lambda *_: (0,) * nd, pipeline_mode=pl.Buffered(1))


def _ffn_kernel(x_ref, g_ref, wg_ref, wu_ref, wd_ref, o_ref, *, ff_chunk):
    x = x_ref[...]
    hn = _rms_rows(x, g_ref[...]).astype(BF16)
    acc = jnp.zeros(x.shape, F32)
    d_ff = wg_ref.shape[1]
    for c in range(d_ff // ff_chunk):
        sl = slice(c * ff_chunk, (c + 1) * ff_chunk)
        gt = _dot(hn, wg_ref[:, sl])
        ut = _dot(hn, wu_ref[:, sl])
        a = (gt * jax.nn.sigmoid(gt) * ut).astype(BF16)
        acc = acc + _dot(a, wd_ref[sl, :])
    o_ref[...] = x + 0.5 * acc


def _ffn(x, g, wg, wu, wd, *, tm=512, ff_chunk=256):
    n, d = x.shape
    d_ff = wg.shape[1]
    assert n % tm == 0 and d_ff % ff_chunk == 0
    return pl.pallas_call(
        functools.partial(_ffn_kernel, ff_chunk=ff_chunk),
        grid=(n // tm,),
        in_specs=[pl.BlockSpec((tm, d), lambda i: (i, 0)),
                  _const_spec((1, d)), _const_spec((d, d_ff)), _const_spec((d, d_ff)),
                  _const_spec((d_ff, d))],
        out_specs=pl.BlockSpec((tm, d), lambda i: (i, 0)),
        out_shape=jax.ShapeDtypeStruct((n, d), F32),
        compiler_params=_cparams(("parallel",)),
        name="ffn",
    )(x, g.reshape(1, d), wg.astype(BF16), wu.astype(BF16), wd.astype(BF16))


def _mixin_even_kernel(x_ref, g_ref, w_ref, gmat_ref, qn_ref, kn_ref,
                       gate_o, xr_o, q_o, k_o, v_o, km_o):
    hn = _rms_rows(x_ref[...], g_ref[...]).astype(BF16)
    w = ATTN_WIDTH
    gate_o[...] = _dot(hn, w_ref[:, 0:w])
    xr_o[...] = _dot(hn, w_ref[:, w:2 * w])
    gmat = gmat_ref[...]
    q = _head_rms(_dot(hn, w_ref[:, 2 * w:3 * w]), gmat, qn_ref[...]) * (HEAD_DIM ** -0.5)
    k = _head_rms(_dot(hn, w_ref[:, 3 * w:4 * w]), gmat, kn_ref[...])
    q_o[...] = q.astype(BF16)
    k_o[...] = k.astype(BF16)
    v_o[...] = _dot(hn, w_ref[:, 4 * w:5 * w]).astype(BF16)
    for r in range(k.shape[0] // MOBA_BLOCK):
        km_o[r] = jnp.mean(k[r * MOBA_BLOCK:(r + 1) * MOBA_BLOCK], axis=0, keepdims=True)


def _group_mean_matrix():
    idx = jnp.arange(ATTN_WIDTH) // HEAD_DIM
    return (idx[:, None] == idx[None, :]).astype(BF16) * (1.0 / HEAD_DIM)


def _mixin_even(x, g, w_in, q_norm, k_norm, *, tm=512):
    n, d = x.shape
    w = ATTN_WIDTH
    assert n % tm == 0 and tm % MOBA_BLOCK == 0 and w_in.shape[1] == 5 * w
    row = lambda i: (i, 0)
    blk = tm // MOBA_BLOCK
    outs = pl.pallas_call(
        _mixin_even_kernel,
        grid=(n // tm,),
        in_specs=[pl.BlockSpec((tm, d), row), _const_spec((1, d)), _const_spec((d, 5 * w)),
                  _const_spec((w, w)), _const_spec((1, w)), _const_spec((1, w))],
        out_specs=[pl.BlockSpec((tm, w), row)] * 5 + [pl.BlockSpec((blk, 1, w), lambda i: (i, 0, 0))],
        out_shape=[jax.ShapeDtypeStruct((n, w), F32), jax.ShapeDtypeStruct((n, w), F32),
                   jax.ShapeDtypeStruct((n, w), BF16), jax.ShapeDtypeStruct((n, w), BF16),
                   jax.ShapeDtypeStruct((n, w), BF16),
                   jax.ShapeDtypeStruct((n // MOBA_BLOCK, 1, w), F32)],
        compiler_params=_cparams(("parallel",)),
        name="mixin_even",
    )(x, g.reshape(1, d), w_in.astype(BF16), _group_mean_matrix(),
      jnp.tile(q_norm, N_HEADS).reshape(1, w), jnp.tile(k_norm, N_HEADS).reshape(1, w))
    return outs


def _mixin_odd_kernel(x_ref, g_ref, w_ref, ws_ref, gmat_ref, qn_ref, kn_ref,
                      c_o, q_o, k_o, v_o, qi_o, ki_o, wi_o):
    hn = _rms_rows(x_ref[...], g_ref[...]).astype(BF16)
    w = ATTN_WIDTH
    ca = _dot(hn, w_ref[:, 0:w])
    cg = _dot(hn, w_ref[:, w:2 * w])
    c_o[...] = ca * jax.nn.sigmoid(cg)
    gmat = gmat_ref[...]
    q = _head_rms(_dot(hn, w_ref[:, 2 * w:3 * w]), gmat, qn_ref[...]) * (HEAD_DIM ** -0.5)
    k = _head_rms(_dot(hn, w_ref[:, 3 * w:4 * w]), gmat, kn_ref[...])
    q_o[...] = q.astype(BF16)
    k_o[...] = k.astype(BF16)
    v_o[...] = _dot(hn, w_ref[:, 4 * w:5 * w]).astype(BF16)
    qi_o[...] = _dot(hn, w_ref[:, 5 * w:6 * w]).astype(BF16)
    small = _dot(hn, ws_ref[...])
    ki_o[...] = small[:, 0:128].astype(BF16)
    wi_o[...] = small[:, 128:256] * (IDX_DIM ** -0.5 * IDX_HEADS ** -0.5)


def _mixin_odd(x, g, w_in, q_norm, k_norm, *, tm=512):
    n, d = x.shape
    w = ATTN_WIDTH
    assert n % tm == 0 and w_in.shape[1] == 6 * w + IDX_DIM + IDX_HEADS
    w_main = w_in[:, :6 * w].astype(BF16)
    w_ki = w_in[:, 6 * w:6 * w + IDX_DIM]
    w_wi = jnp.pad(w_in[:, 6 * w + IDX_DIM:], ((0, 0), (0, 128 - IDX_HEADS)))
    w_small = jnp.concatenate([w_ki, w_ki, w_wi], axis=1).astype(BF16)
    row = lambda i: (i, 0)
    return pl.pallas_call(
        _mixin_odd_kernel,
        grid=(n // tm,),
        in_specs=[pl.BlockSpec((tm, d), row), _const_spec((1, d)), _const_spec((d, 6 * w)),
                  _const_spec((d, 256)), _const_spec((w, w)), _const_spec((1, w)), _const_spec((1, w))],
        out_specs=[pl.BlockSpec((tm, w), row)] * 5 + [pl.BlockSpec((tm, 128), row)] * 2,
        out_shape=[jax.ShapeDtypeStruct((n, w), F32)] + [jax.ShapeDtypeStruct((n, w), BF16)] * 4
        + [jax.ShapeDtypeStruct((n, 128), BF16), jax.ShapeDtypeStruct((n, 128), F32)],
        compiler_params=_cparams(("parallel",)),
        name="mixin_odd",
    )(x, g.reshape(1, d), w_main, w_small, _group_mean_matrix(),
      jnp.tile(q_norm, N_HEADS).reshape(1, w), jnp.tile(k_norm, N_HEADS).reshape(1, w))


def _lru_kernel(gate_ref, xr_ref, cw_ref, cb_ref, wa_ref, ba_ref, wx_ref, bx_ref, sp_ref,
                o_ref, xbuf, a_s, u_s, h_s, hc, *, ts):
    j = pl.program_id(1)

    @pl.when(j == 0)
    def _():
        xbuf[0:8, :] = jnp.zeros((8, xbuf.shape[1]), F32)
        hc[...] = jnp.zeros(hc.shape, F32)

    xbuf[8:8 + ts, :] = xr_ref[...]
    xc = cb_ref[...] + cw_ref[0:1, :] * xbuf[5:5 + ts, :]
    for k in range(1, 4):
        xc = xc + cw_ref[k:k + 1, :] * xbuf[5 + k:5 + k + ts, :]
    xbuf[0:8, :] = xbuf[ts:ts + 8, :]

    xcb = xc.astype(BF16)
    r = jax.nn.sigmoid(_dot(xcb, wa_ref[...]) + ba_ref[...])
    ig = jax.nn.sigmoid(_dot(xcb, wx_ref[...]) + bx_ref[...])
    log_a = -LRU_C * r * sp_ref[...]
    a = jnp.exp(log_a)
    a_s[...] = a
    u_s[...] = jnp.sqrt(-jnp.tanh(log_a) * (a * a + 1.0)) * (ig * xc)

    row = lax.broadcasted_iota(jnp.int32, (8, a_s.shape[1]), 0)

    def body(g, carry):
        r0 = pl.multiple_of(g * 8, 8)
        a = a_s[pl.ds(r0, 8), :]
        u = u_s[pl.ds(r0, 8), :]
        for s in (1, 2, 4):
            ok = row >= s
            a_sh = jnp.where(ok, pltpu.roll(a, s, 0), 1.0)
            u_sh = jnp.where(ok, pltpu.roll(u, s, 0), 0.0)
            u = a * u_sh + u
            a = a * a_sh
        h = a * carry + u
        h_s[pl.ds(r0, 8), :] = h
        return h[7:8, :]

    hc[...] = lax.fori_loop(0, ts // 8, body, hc[...], unroll=4)
    o_ref[...] = (h_s[...] * jax.nn.gelu(gate_ref[...])).astype(BF16)


def _block_diag(wb):
    nb, bs, _ = wb.shape
    eye = jnp.eye(nb, dtype=wb.dtype)
    return (eye[:, None, :, None] * wb[:, :, None, :]).reshape(nb * bs, nb * bs)


def _lru(gate, xr, conv_w, conv_b, ra_w, ra_b, ix_w, ix_b, lam, *, bsz, ts=256):
    n, w = xr.shape
    seq = n // bsz
    assert seq % ts == 0
    nt = seq // ts
    row = lambda b, j: (b * nt + j, 0)
    vec = lambda v: v.reshape(1, w).astype(F32)
    return pl.pallas_call(
        functools.partial(_lru_kernel, ts=ts),
        grid=(bsz, nt),
        in_specs=[pl.BlockSpec((ts, w), row), pl.BlockSpec((ts, w), row),
                  _const_spec((conv_w.shape[0], w)), _const_spec((1, w)),
                  _const_spec((w, w)), _const_spec((1, w)), _const_spec((w, w)), _const_spec((1, w)),
                  _const_spec((1, w))],
        out_specs=pl.BlockSpec((ts, w), row),
        out_shape=jax.ShapeDtypeStruct((n, w), BF16),
        scratch_shapes=[pltpu.VMEM((ts + 8, w), F32), pltpu.VMEM((ts, w), F32),
                        pltpu.VMEM((ts, w), F32), pltpu.VMEM((ts, w), F32), pltpu.VMEM((1, w), F32)],
        compiler_params=_cparams(("arbitrary", "arbitrary")),
        name="rg_lru",
    )(gate, xr, conv_w, vec(conv_b), _block_diag(ra_w).astype(BF16), vec(ra_b),
      _block_diag(ix_w).astype(BF16), vec(ix_b), vec(jax.nn.softplus(-lam)))


def _conf_kernel(c_ref, w_ref, b_ref, g_ref, beta_ref, o_ref, cbuf, *, ts, halo, width):
    j = pl.program_id(1)

    @pl.when(j == 0)
    def _():
        cbuf[0:halo, :] = jnp.zeros((halo, cbuf.shape[1]), F32)

    cbuf[halo:halo + ts, :] = c_ref[...]
    base = halo - (width - 1)
    y = b_ref[...] + w_ref[0:1, :] * cbuf[base:base + ts, :]
    for k in range(1, width):
        y = y + w_ref[k:k + 1, :] * cbuf[base + k:base + k + ts, :]
    cbuf[0:halo, :] = cbuf[ts:ts + halo, :]

    mu = jnp.mean(y, axis=-1, keepdims=True)
    yc = y - mu
    var = jnp.mean(yc * yc, axis=-1, keepdims=True)
    z = yc * lax.rsqrt(var + EPS) * g_ref[...] + beta_ref[...]
    o_ref[...] = (z * jax.nn.sigmoid(z)).astype(BF16)


def _conf(c, dw_w, dw_b, ln_g, ln_b, *, bsz, ts=256, halo=32):
    n, w = c.shape
    seq = n // bsz
    width = dw_w.shape[0]
    assert seq % ts == 0 and width - 1 <= halo <= ts
    nt = seq // ts
    row = lambda b, j: (b * nt + j, 0)
    vec = lambda v: v.reshape(1, w).astype(F32)
    return pl.pallas_call(
        functools.partial(_conf_kernel, ts=ts, halo=halo, width=width),
        grid=(bsz, nt),
        in_specs=[pl.BlockSpec((ts, w), row), _const_spec((width, w)),
                  _const_spec((1, w)), _const_spec((1, w)), _const_spec((1, w))],
        out_specs=pl.BlockSpec((ts, w), row),
        out_shape=jax.ShapeDtypeStruct((n, w), BF16),
        scratch_shapes=[pltpu.VMEM((ts + halo, w), F32)],
        compiler_params=_cparams(("arbitrary", "arbitrary")),
        name="conformer_conv",
    )(c, dw_w, vec(dw_b), vec(ln_g), vec(ln_b))


def _rel_bucket(dist):
    n = jnp.maximum(dist, 0)
    nf = jnp.maximum(n, 1).astype(F32)
    large = REL_MAX_EXACT + (jnp.log(nf / REL_MAX_EXACT) / math.log(REL_MAX_DIST / REL_MAX_EXACT)
                             * (REL_BUCKETS - REL_MAX_EXACT)).astype(jnp.int32)
    large = jnp.minimum(large, REL_BUCKETS - 1)
    return jnp.where(n < REL_MAX_EXACT, n, large)


def _bias_tiles(rel_bias, t):
    assert t > REL_MAX_DIST
    table = rel_bias[_rel_bucket(jnp.arange(2 * t))]
    i = jnp.arange(t)[:, None]
    j = jnp.arange(t)[None, :]
    d0 = table[jnp.maximum(i - j, 0)].transpose(2, 0, 1)
    d1 = table[t + i - j].transpose(2, 0, 1)
    far = jnp.broadcast_to(rel_bias[REL_BUCKETS - 1][:, None], (N_HEADS, 128))
    return d0.astype(F32), d1.astype(F32), far.astype(F32)


def _head_lane_mask(hh):
    lane = lax.broadcasted_iota(jnp.int32, (1, HEAD_GROUP_LANES), 1)
    return (lane >= hh * HEAD_DIM) & (lane < (hh + 1) * HEAD_DIM)


def _softmax_step(s, h, g, hh, v_g, m_ref, l_ref, acc_ref):
    gs = slice(g * HEAD_GROUP_LANES, (g + 1) * HEAD_GROUP_LANES)
    m_old = m_ref[h]
    m_new = jnp.maximum(m_old, jnp.max(s, axis=-1, keepdims=True))
    alpha = jnp.exp(m_old - m_new)
    p = jnp.exp(s - m_new)
    l_ref[h] = alpha * l_ref[h] + jnp.sum(p, axis=-1, keepdims=True)
    m_ref[h] = m_new
    pv = _dot(p.astype(BF16), v_g)
    acc = acc_ref[:, gs]
    acc_ref[:, gs] = jnp.where(_head_lane_mask(hh), alpha * acc + pv, acc)


def _attn_init(m_ref, l_ref, acc_ref):
    m_ref[...] = jnp.full(m_ref.shape, M_INIT, F32)
    l_ref[...] = jnp.zeros(l_ref.shape, F32)
    acc_ref[...] = jnp.zeros(acc_ref.shape, F32)


def _attn_finish(o_ref, l_ref, acc_ref):
    for g in range(ATTN_WIDTH // HEAD_GROUP_LANES):
        gs = slice(g * HEAD_GROUP_LANES, (g + 1) * HEAD_GROUP_LANES)
        inv = jnp.zeros((acc_ref.shape[0], HEAD_GROUP_LANES), F32)
        for hh in range(HEAD_GROUP_LANES // HEAD_DIM):
            h = g * (HEAD_GROUP_LANES // HEAD_DIM) + hh
            inv = jnp.where(_head_lane_mask(hh), 1.0 / l_ref[h], inv)
        o_ref[:, gs] = (acc_ref[:, gs] * inv).astype(BF16)


def _tile_bias(kind, h, d0_ref, d1_ref, far_ref):
    if kind == "diag":
        return d0_ref[h]
    if kind == "near":
        return d1_ref[h]
    return far_ref[h:h + 1, 0:1]


def _moba_kernel(q_ref, k_ref, v_ref, km_ref, d0_ref, d1_ref, far_ref, o_ref,
                 acc_ref, m_ref, l_ref, sel_ref, *, n_blocks):
    i = pl.program_id(1)
    t = ATTN_TILE
    hpg = HEAD_GROUP_LANES // HEAD_DIM
    col128 = lax.broadcasted_iota(jnp.int32, (t, 128), 1)
    _attn_init(m_ref, l_ref, acc_ref)

    for h in range(N_HEADS):
        g, hh = divmod(h, hpg)
        gs = slice(g * HEAD_GROUP_LANES, (g + 1) * HEAD_GROUP_LANES)
        qm = jnp.where(_head_lane_mask(hh), q_ref[:, gs], jnp.zeros((), BF16))
        km_hi, km_lo = _split_bf16(km_ref[:, gs])
        gate = _dot_nt(qm, km_hi) + _dot_nt(qm, km_lo)
        valid = col128 < i
        gate = jnp.where(valid, gate, NEG)
        rank = jnp.zeros((t, 128), F32)
        for j in range(n_blocks):
            gj = gate[:, j:j + 1]
            beats = (gj > gate) | ((gj == gate) & (col128 > j))
            rank = rank + jnp.where(beats, 1.0, 0.0)
        sel_ref[h] = jnp.where(valid & (rank < MOBA_TOPK), 1.0, 0.0)

    row = lax.broadcasted_iota(jnp.int32, (t, t), 0)
    col = lax.broadcasted_iota(jnp.int32, (t, t), 1)

    def tile(kt, kind):
        r0 = pl.multiple_of(kt * t, t)
        for g in range(ATTN_WIDTH // HEAD_GROUP_LANES):
            gs = slice(g * HEAD_GROUP_LANES, (g + 1) * HEAD_GROUP_LANES)
            k_g = k_ref[pl.ds(r0, t), gs]
            v_g = v_ref[pl.ds(r0, t), gs]
            q_g = q_ref[:, gs]
            for hh in range(hpg):
                h = g * hpg + hh
                qm = jnp.where(_head_lane_mask(hh), q_g, jnp.zeros((), BF16))
                s = _dot_nt(qm, k_g) + _tile_bias(kind, h, d0_ref, d1_ref, far_ref)
                if kind == "diag":
                    mask = row >= col
                else:
                    picked = jnp.sum(jnp.where(col128 == kt, sel_ref[h], 0.0), axis=-1, keepdims=True)
                    mask = picked > 0.0
                s = jnp.where(mask, s, NEG)
                _softmax_step(s, h, g, hh, v_g, m_ref, l_ref, acc_ref)

    def far_body(kt, carry):
        tile(kt, "far")
        return carry

    lax.fori_loop(0, jnp.maximum(i - 1, 0), far_body, 0)

    @pl.when(i >= 1)
    def _():
        tile(i - 1, "near")

    tile(i, "diag")
    _attn_finish(o_ref, l_ref, acc_ref)


def _moba(q, k, v, kmean, bias, *, bsz):
    n, w = q.shape
    seq = n // bsz
    t = ATTN_TILE
    assert seq % t == 0 and t == MOBA_BLOCK and w == ATTN_WIDTH
    nq = seq // t
    assert nq <= 128
    km = jnp.pad(kmean.reshape(bsz, nq, w), ((0, 0), (0, 128 - nq), (0, 0)))
    d0, d1, far = bias
    q3, k3, v3 = (a.reshape(bsz, seq, w) for a in (q, k, v))
    out = pl.pallas_call(
        functools.partial(_moba_kernel, n_blocks=nq),
        grid=(bsz, nq),
        in_specs=[pl.BlockSpec((None, t, w), lambda b, i: (b, i, 0)),
                  pl.BlockSpec((None, seq, w), lambda b, i: (b, 0, 0)),
                  pl.BlockSpec((None, seq, w), lambda b, i: (b, 0, 0)),
                  pl.BlockSpec((None, 128, w), lambda b, i: (b, 0, 0)),
                  _const_spec((N_HEADS, t, t)), _const_spec((N_HEADS, t, t)), _const_spec((N_HEADS, 128))],
        out_specs=pl.BlockSpec((None, t, w), lambda b, i: (b, i, 0)),
        out_shape=jax.ShapeDtypeStruct((bsz, seq, w), BF16),
        scratch_shapes=[pltpu.VMEM((t, w), F32), pltpu.VMEM((N_HEADS, t, 1), F32),
                        pltpu.VMEM((N_HEADS, t, 1), F32), pltpu.VMEM((N_HEADS, t, 128), F32)],
        compiler_params=_cparams(("parallel", "parallel")),
        name="moba_attention",
    )(q3, k3, v3, km, d0, d1, far)
    return out.reshape(n, w)


def _dsa_kernel(q_ref, k_ref, v_ref, qi_ref, ki_ref, wi_ref, d0_ref, d1_ref, far_ref, tri_ref, o_ref,
                acc_ref, m_ref, l_ref, isc_ref, bc_ref, *, n_sel):
    i = pl.program_id(1)
    t = ATTN_TILE
    hpg = HEAD_GROUP_LANES // HEAD_DIM
    row = lax.broadcasted_iota(jnp.int32, (t, t), 0)
    col = lax.broadcasted_iota(jnp.int32, (t, t), 1)
    lane128 = lax.broadcasted_iota(jnp.int32, (1, 128), 1)
    _attn_init(m_ref, l_ref, acc_ref)

    def index_tile(kt, diag):
        r0 = pl.multiple_of(kt * t, t)
        ki2 = ki_ref[pl.ds(r0, t), :]
        acc = jnp.zeros((t, t), F32)
        for pr in range(IDX_HEADS // 2):
            q2 = qi_ref[:, pr * 128:(pr + 1) * 128]
            for half in range(2):
                h = 2 * pr + half
                hm = (lane128 >= half * IDX_DIM) & (lane128 < (half + 1) * IDX_DIM)
                s = _dot_nt(jnp.where(hm, q2, jnp.zeros((), BF16)), ki2)
                acc = acc + jnp.maximum(s, 0.0) * wi_ref[:, h:h + 1]
        if diag:
            acc = jnp.where(row >= col, acc, -jnp.inf)
        isc_ref[kt] = acc

    def index_body(kt, carry):
        index_tile(kt, False)
        return carry

    lax.fori_loop(0, i, index_body, 0)
    index_tile(i, True)

    def fold(x, op):
        return op(x[:, 0:128], x[:, 128:256])

    def minmax_body(kt, carry):
        mn, mx = carry
        x = isc_ref[kt]
        mx = jnp.maximum(mx, fold(x, jnp.maximum))
        mn = jnp.minimum(mn, fold(jnp.where(x == -jnp.inf, jnp.inf, x), jnp.minimum))
        return mn, mx

    mn, mx = lax.fori_loop(0, i + 1, minmax_body,
                           (jnp.full((t, 128), jnp.inf, F32), jnp.full((t, 128), -jnp.inf, F32)))
    lo = jnp.min(mn, axis=-1, keepdims=True)
    mx = jnp.max(mx, axis=-1, keepdims=True)
    hi = mx + jnp.abs(mx) * 1e-3 + 1e-30
    n_valid = (i * t + 1 + lax.broadcasted_iota(jnp.int32, (t, 1), 0)).astype(F32)
    want = jnp.minimum(n_valid, float(n_sel))

    def count_ge(thr):
        def body(kt, c):
            x = isc_ref[kt]
            ge = jnp.where(x >= thr, 1.0, 0.0)
            return c + fold(ge, jnp.add)
        c = lax.fori_loop(0, i + 1, body, jnp.zeros((t, 128), F32))
        return jnp.sum(c, axis=-1, keepdims=True)

    def bisect_body(_, carry):
        lo, hi, c_lo, c_hi = carry
        mid = 0.5 * (lo + hi)
        c_mid = count_ge(mid)
        up = c_mid >= want
        return (jnp.where(up, mid, lo), jnp.where(up, hi, mid),
                jnp.where(up, c_mid, c_lo), jnp.where(up, c_hi, c_mid))

    lo, hi, c_lo, c_hi = lax.fori_loop(0, BISECT_ITERS, bisect_body,
                                       (lo, hi, n_valid, jnp.zeros((t, 1), F32)))
    need = want - c_hi
    banded = jnp.max(c_lo - want) > 0.0
    bc_ref[...] = jnp.zeros(bc_ref.shape, F32)

    def select_mask(kt):
        x = isc_ref[kt]

        def plain():
            return jnp.where(x >= lo, 1.0, 0.0)

        def with_band():
            band = jnp.where((x >= lo) & (x < hi), 1.0, 0.0)
            before = bc_ref[...] + _dot(band.astype(BF16), tri_ref[...])
            bc_ref[...] = bc_ref[...] + jnp.sum(band, axis=-1, keepdims=True)
            return jnp.where((x >= hi) | ((band > 0.0) & (before < need)), 1.0, 0.0)

        return lax.cond(banded, with_band, plain) > 0.0

    def tile(kt, kind):
        r0 = pl.multiple_of(kt * t, t)
        mask = select_mask(kt)
        for g in range(ATTN_WIDTH // HEAD_GROUP_LANES):
            gs = slice(g * HEAD_GROUP_LANES, (g + 1) * HEAD_GROUP_LANES)
            k_g = k_ref[pl.ds(r0, t), gs]
            v_g = v_ref[pl.ds(r0, t), gs]
            q_g = q_ref[:, gs]
            for hh in range(hpg):
                h = g * hpg + hh
                qm = jnp.where(_head_lane_mask(hh), q_g, jnp.zeros((), BF16))
                s = _dot_nt(qm, k_g) + _tile_bias(kind, h, d0_ref, d1_ref, far_ref)
                s = jnp.where(mask, s, NEG)
                _softmax_step(s, h, g, hh, v_g, m_ref, l_ref, acc_ref)

    def far_body(kt, carry):
        tile(kt, "far")
        return carry

    lax.fori_loop(0, jnp.maximum(i - 1, 0), far_body, 0)

    @pl.when(i >= 1)
    def _():
        tile(i - 1, "near")

    tile(i, "diag")
    _attn_finish(o_ref, l_ref, acc_ref)


def _dsa(q, k, v, qi, ki2, wi, bias, *, bsz):
    n, w = q.shape
    seq = n // bsz
    t = ATTN_TILE
    assert seq % t == 0 and w == ATTN_WIDTH
    nq = seq // t
    n_sel = min(DSA_TOPK_MAX, seq // 4)
    d0, d1, far = bias
    tri = (jnp.arange(t)[:, None] < jnp.arange(t)[None, :]).astype(BF16)
    r3 = lambda a: a.reshape(bsz, seq, a.shape[-1])
    tile_spec = lambda width: pl.BlockSpec((None, t, width), lambda b, i: (b, i, 0))
    seq_spec = lambda width: pl.BlockSpec((None, seq, width), lambda b, i: (b, 0, 0))
    out = pl.pallas_call(
        functools.partial(_dsa_kernel, n_sel=n_sel),
        grid=(bsz, nq),
        in_specs=[tile_spec(w), seq_spec(w), seq_spec(w), tile_spec(w), seq_spec(128), tile_spec(128),
                  _const_spec((N_HEADS, t, t)), _const_spec((N_HEADS, t, t)), _const_spec((N_HEADS, 128)),
                  _const_spec((t, t))],
        out_specs=tile_spec(w),
        out_shape=jax.ShapeDtypeStruct((bsz, seq, w), BF16),
        scratch_shapes=[pltpu.VMEM((t, w), F32), pltpu.VMEM((N_HEADS, t, 1), F32),
                        pltpu.VMEM((N_HEADS, t, 1), F32), pltpu.VMEM((nq, t, t), F32),
                        pltpu.VMEM((t, 1), F32)],
        compiler_params=_cparams(("parallel", "parallel")),
        name="dsa_attention",
    )(r3(q), r3(k), r3(v), r3(qi), r3(ki2), r3(wi), d0, d1, far, tri)
    return out.reshape(n, w)


def _mixout_kernel(x_ref, ya_ref, yb_ref, wa_ref, wb_ref, o_ref):
    o_ref[...] = x_ref[...] + _dot(ya_ref[...], wa_ref[...]) + _dot(yb_ref[...], wb_ref[...])


def _mixout(x, ya, yb, w_out, *, tm=512):
    n, d = x.shape
    w = ya.shape[1]
    assert n % tm == 0
    row = lambda i: (i, 0)
    return pl.pallas_call(
        _mixout_kernel,
        grid=(n // tm,),
        in_specs=[pl.BlockSpec((tm, d), row), pl.BlockSpec((tm, w), row), pl.BlockSpec((tm, w), row),
                  _const_spec((w, d)), _const_spec((w, d))],
        out_specs=pl.BlockSpec((tm, d), row),
        out_shape=jax.ShapeDtypeStruct((n, d), F32),
        compiler_params=_cparams(("parallel",)),
        name="mixout",
    )(x, ya, yb, w_out[:w].astype(BF16), w_out[w:].astype(BF16))


def kernel(x, rel_bias, ffn1_norm, ffn1_w_gate, ffn1_w_up, ffn1_w_down, mix_norm, ffn2_norm, ffn2_w_gate, ffn2_w_up, ffn2_w_down, ev_w_in, ev_conv_w, ev_conv_b, ev_ra_w, ev_ra_b, ev_ix_w, ev_ix_b, ev_lambda, ev_q_norm, ev_k_norm, ev_w_out, od_w_in, od_dw_w, od_dw_b, od_ln_g, od_ln_b, od_q_norm, od_k_norm, od_w_out):
    bsz, seq, d = x.shape
    depth = ffn1_norm.shape[0]
    bias = _bias_tiles(rel_bias, ATTN_TILE)
    h = x.reshape(bsz * seq, d)
    for i in range(depth):
        h = _ffn(h, ffn1_norm[i], ffn1_w_gate[i], ffn1_w_up[i], ffn1_w_down[i])
        j = i // 2
        if i % 2 == 0:
            gate, xr, q, k, v, kmean = _mixin_even(h, mix_norm[i], ev_w_in[j], ev_q_norm[j], ev_k_norm[j])
            ya = _lru(gate, xr, ev_conv_w[j], ev_conv_b[j], ev_ra_w[j], ev_ra_b[j],
                      ev_ix_w[j], ev_ix_b[j], ev_lambda[j], bsz=bsz)
            yb = _moba(q, k, v, kmean, bias, bsz=bsz)
            h = _mixout(h, ya, yb, ev_w_out[j])
        else:
            c, q, k, v, qi, ki2, wi = _mixin_odd(h, mix_norm[i], od_w_in[j], od_q_norm[j], od_k_norm[j])
            yc = _conf(c, od_dw_w[j], od_dw_b[j], od_ln_g[j], od_ln_b[j], bsz=bsz)
            yd = _dsa(q, k, v, qi, ki2, wi, bias, bsz=bsz)
            h = _mixout(h, yc, yd, od_w_out[j])
        h = _ffn(h, ffn2_norm[i], ffn2_w_gate[i], ffn2_w_up[i], ffn2_w_down[i])
    return h.reshape(bsz, seq, d)
```

```python
import functools
import math

import numpy as np
import jax
import jax.numpy as jnp
from jax import lax
from jax.experimental import pallas as pl
from jax.experimental.pallas import tpu as pltpu

F32 = jnp.float32
BF16 = jnp.bfloat16

N_HEADS = 8
HEAD_DIM = 64
ATTN_WIDTH = N_HEADS * HEAD_DIM
LRU_C = 8.0
MOBA_BLOCK = 256
MOBA_TOPK = 3
IDX_HEADS = 8
IDX_DIM = 64
DSA_TOPK_MAX = 256
REL_BUCKETS = 32
REL_MAX_EXACT = REL_BUCKETS // 2
REL_MAX_DIST = 128
EPS = 1e-6
NEG = -1e30
M_INIT = -1e29
ATTN_TILE = 256
SLOT = 128
SLOT_WIDTH = N_HEADS * SLOT
FEAT0 = HEAD_DIM
BISECT_ITERS = 32
VMEM_LIMIT = 56 * 1024 * 1024


def _cparams(sem):
    return pltpu.CompilerParams(dimension_semantics=sem, vmem_limit_bytes=VMEM_LIMIT)


def _dot(a, b):
    return jnp.dot(a, b, preferred_element_type=F32)


def _dot_nt(a, b):
    return lax.dot_general(a, b, (((1,), (1,)), ((), ())), preferred_element_type=F32)


def _split_bf16(x):
    hi = x.astype(BF16)
    lo = (x - hi.astype(F32)).astype(BF16)
    return hi, lo


def _rms_rows(x, g):
    return x * lax.rsqrt(jnp.mean(x * x, axis=-1, keepdims=True) + EPS) * g


def _head_rms(z, gmat, g):
    hi, lo = _split_bf16(z * z)
    ms = _dot(hi, gmat) + _dot(lo, gmat)
    return z * lax.rsqrt(ms + EPS) * g


def _const_spec(shape):
    nd = len(shape)
    return pl.BlockSpec(shape, lambda *_: (0,) * nd, pipeline_mode=pl.Buffered(1))


def _slot(h):
    return slice(h * SLOT, (h + 1) * SLOT)


def _to_slots(x):
    low = lax.broadcasted_iota(jnp.int32, (1, SLOT), 1) < HEAD_DIM
    slots = []
    for p in range(N_HEADS // 2):
        chunk = x[:, p * SLOT:(p + 1) * SLOT]
        slots.append(jnp.where(low, chunk, 0.0))
        slots.append(jnp.where(low, pltpu.roll(chunk, HEAD_DIM, 1), 0.0))
    return slots


def _from_slots(slots):
    low = lax.broadcasted_iota(jnp.int32, (1, SLOT), 1) < HEAD_DIM
    chunks = [jnp.where(low, slots[2 * p], pltpu.roll(slots[2 * p + 1], HEAD_DIM, 1))
              for p in range(N_HEADS // 2)]
    return jnp.concatenate(chunks, axis=1)


def _ffn_kernel(x_ref, g_ref, wg_ref, wu_ref, wd_ref, o_ref, *, ff_chunk):
    x = x_ref[...]
    hn = _rms_rows(x, g_ref[...]).astype(BF16)
    acc = jnp.zeros(x.shape, F32)
    d_ff = wg_ref.shape[1]
    for c in range(d_ff // ff_chunk):
        sl = slice(c * ff_chunk, (c + 1) * ff_chunk)
        gt = _dot(hn, wg_ref[:, sl])
        ut = _dot(hn, wu_ref[:, sl])
        a = (gt * jax.nn.sigmoid(gt) * ut).astype(BF16)
        acc = acc + _dot(a, wd_ref[sl, :])
    o_ref[...] = x + 0.5 * acc


def _ffn(x, g, wg, wu, wd, *, tm=512, ff_chunk=256):
    n, d = x.shape
    d_ff = wg.shape[1]
    assert n % tm == 0 and d_ff % ff_chunk == 0
    return pl.pallas_call(
        functools.partial(_ffn_kernel, ff_chunk=ff_chunk),
        grid=(n // tm,),
        in_specs=[pl.BlockSpec((tm, d), lambda i: (i, 0)),
                  _const_spec((1, d)), _const_spec((d, d_ff)), _const_spec((d, d_ff)),
                  _const_spec((d_ff, d))],
        out_specs=pl.BlockSpec((tm, d), lambda i: (i, 0)),
        out_shape=jax.ShapeDtypeStruct((n, d), F32),
        compiler_params=_cparams(("parallel",)),
        name="ffn",
    )(x, g.reshape(1, d), wg.astype(BF16), wu.astype(BF16), wd.astype(BF16))


def _store_qkv_slots(hn, w_ref, gmat_ref, qn_ref, kn_ref, q_o, k_o, v_o, col0, k_feat):
    w = ATTN_WIDTH
    gmat = gmat_ref[...]
    q = _head_rms(_dot(hn, w_ref[:, col0:col0 + w]), gmat, qn_ref[...]) * (HEAD_DIM ** -0.5)
    k = _head_rms(_dot(hn, w_ref[:, col0 + w:col0 + 2 * w]), gmat, kn_ref[...])
    v = _dot(hn, w_ref[:, col0 + 2 * w:col0 + 3 * w])
    lane = lax.broadcasted_iota(jnp.int32, (1, SLOT), 1)
    k_one = k_feat(lane)
    for h, (qs, ks, vs) in enumerate(zip(_to_slots(q), _to_slots(k), _to_slots(v))):
        q_o[:, _slot(h)] = qs.astype(BF16)
        if k_one is not None:
            ks = jnp.where(k_one, 1.0, ks)
        k_o[:, _slot(h)] = ks.astype(BF16)
        v_o[:, _slot(h)] = jnp.where(lane == FEAT0, 1.0, vs).astype(BF16)
    return k


def _mixin_even_kernel(x_ref, g_ref, w_ref, gmat_ref, qn_ref, kn_ref,
                       gate_o, xr_o, q_o, k_o, v_o, km_o, *, n_blocks):
    tm = x_ref.shape[0]
    hn = _rms_rows(x_ref[...], g_ref[...]).astype(BF16)
    w = ATTN_WIDTH
    gate_o[...] = _dot(hn, w_ref[:, 0:w])
    xr_o[...] = _dot(hn, w_ref[:, w:2 * w])
    blocks_per_tile = tm // MOBA_BLOCK
    row_block = lax.broadcasted_iota(jnp.int32, (tm, 1), 0) // MOBA_BLOCK
    block = (pl.program_id(0) * blocks_per_tile + row_block) % n_blocks
    k = _store_qkv_slots(hn, w_ref, gmat_ref, qn_ref, kn_ref, q_o, k_o, v_o, 2 * w,
                         lambda lane: lane == FEAT0 + block)
    for r in range(blocks_per_tile):
        km_o[r] = jnp.mean(k[r * MOBA_BLOCK:(r + 1) * MOBA_BLOCK], axis=0, keepdims=True)


def _group_mean_matrix():
    idx = jnp.arange(ATTN_WIDTH) // HEAD_DIM
    return (idx[:, None] == idx[None, :]).astype(BF16) * (1.0 / HEAD_DIM)


def _mixin_even(x, g, w_in, q_norm, k_norm, *, n_blocks, tm=512):
    n, d = x.shape
    w = ATTN_WIDTH
    assert n % tm == 0 and tm % MOBA_BLOCK == 0 and w_in.shape[1] == 5 * w
    assert FEAT0 + n_blocks <= SLOT
    row = lambda i: (i, 0)
    blk = tm // MOBA_BLOCK
    slot_shape = jax.ShapeDtypeStruct((n, SLOT_WIDTH), BF16)
    return pl.pallas_call(
        functools.partial(_mixin_even_kernel, n_blocks=n_blocks),
        grid=(n // tm,),
        in_specs=[pl.BlockSpec((tm, d), row), _const_spec((1, d)), _const_spec((d, 5 * w)),
                  _const_spec((w, w)), _const_spec((1, w)), _const_spec((1, w))],
        out_specs=[pl.BlockSpec((tm, w), row)] * 2 + [pl.BlockSpec((tm, SLOT_WIDTH), row)] * 3
        + [pl.BlockSpec((blk, 1, w), lambda i: (i, 0, 0))],
        out_shape=[jax.ShapeDtypeStruct((n, w), F32), jax.ShapeDtypeStruct((n, w), F32),
                   slot_shape, slot_shape, slot_shape,
                   jax.ShapeDtypeStruct((n // MOBA_BLOCK, 1, w), F32)],
        compiler_params=_cparams(("parallel",)),
        name="mixin_even",
    )(x, g.reshape(1, d), w_in.astype(BF16), _group_mean_matrix(),
      jnp.tile(q_norm, N_HEADS).reshape(1, w), jnp.tile(k_norm, N_HEADS).reshape(1, w))


def _mixin_odd_kernel(x_ref, g_ref, w_ref, ws_ref, gmat_ref, qn_ref, kn_ref,
                      c_o, q_o, k_o, v_o, qi_o, ki_o, wi_o):
    hn = _rms_rows(x_ref[...], g_ref[...]).astype(BF16)
    w = ATTN_WIDTH
    ca = _dot(hn, w_ref[:, 0:w])
    cg = _dot(hn, w_ref[:, w:2 * w])
    c_o[...] = ca * jax.nn.sigmoid(cg)
    _store_qkv_slots(hn, w_ref, gmat_ref, qn_ref, kn_ref, q_o, k_o, v_o, 2 * w, lambda lane: None)
    qi_o[...] = _dot(hn, w_ref[:, 5 * w:6 * w]).astype(BF16)
    small = _dot(hn, ws_ref[...])
    ki_o[...] = small[:, 0:128].astype(BF16)
    wi_o[...] = small[:, 128:256] * (IDX_DIM ** -0.5 * IDX_HEADS ** -0.5)


def _mixin_odd(x, g, w_in, q_norm, k_norm, *, tm=512):
    n, d = x.shape
    w = ATTN_WIDTH
    assert n % tm == 0 and w_in.shape[1] == 6 * w + IDX_DIM + IDX_HEADS
    w_main = w_in[:, :6 * w].astype(BF16)
    w_ki = w_in[:, 6 * w:6 * w + IDX_DIM]
    w_wi = jnp.pad(w_in[:, 6 * w + IDX_DIM:], ((0, 0), (0, 128 - IDX_HEADS)))
    w_small = jnp.concatenate([w_ki, w_ki, w_wi], axis=1).astype(BF16)
    row = lambda i: (i, 0)
    slot_shape = jax.ShapeDtypeStruct((n, SLOT_WIDTH), BF16)
    return pl.pallas_call(
        _mixin_odd_kernel,
        grid=(n // tm,),
        in_specs=[pl.BlockSpec((tm, d), row), _const_spec((1, d)), _const_spec((d, 6 * w)),
                  _const_spec((d, 256)), _const_spec((w, w)), _const_spec((1, w)), _const_spec((1, w))],
        out_specs=[pl.BlockSpec((tm, w), row)] + [pl.BlockSpec((tm, SLOT_WIDTH), row)] * 3
        + [pl.BlockSpec((tm, w), row)] + [pl.BlockSpec((tm, 128), row)] * 2,
        out_shape=[jax.ShapeDtypeStruct((n, w), F32), slot_shape, slot_shape, slot_shape,
                   jax.ShapeDtypeStruct((n, w), BF16),
                   jax.ShapeDtypeStruct((n, 128), BF16), jax.ShapeDtypeStruct((n, 128), F32)],
        compiler_params=_cparams(("parallel",)),
        name="mixin_odd",
    )(x, g.reshape(1, d), w_main, w_small, _group_mean_matrix(),
      jnp.tile(q_norm, N_HEADS).reshape(1, w), jnp.tile(k_norm, N_HEADS).reshape(1, w))


def _lru_kernel(gate_ref, xr_ref, cw_ref, cb_ref, wa_ref, ba_ref, wx_ref, bx_ref, sp_ref,
                o_ref, xbuf, a_s, u_s, h_s, hc, *, ts):
    j = pl.program_id(1)

    @pl.when(j == 0)
    def _():
        xbuf[0:8, :] = jnp.zeros((8, xbuf.shape[1]), F32)
        hc[...] = jnp.zeros(hc.shape, F32)

    xbuf[8:8 + ts, :] = xr_ref[...]
    xc = cb_ref[...] + cw_ref[0:1, :] * xbuf[5:5 + ts, :]
    for k in range(1, 4):
        xc = xc + cw_ref[k:k + 1, :] * xbuf[5 + k:5 + k + ts, :]
    xbuf[0:8, :] = xbuf[ts:ts + 8, :]

    xcb = xc.astype(BF16)
    r = jax.nn.sigmoid(_dot(xcb, wa_ref[...]) + ba_ref[...])
    ig = jax.nn.sigmoid(_dot(xcb, wx_ref[...]) + bx_ref[...])
    log_a = -LRU_C * r * sp_ref[...]
    a = jnp.exp(log_a)
    a_s[...] = a
    u_s[...] = jnp.sqrt(-jnp.tanh(log_a) * (a * a + 1.0)) * (ig * xc)

    row = lax.broadcasted_iota(jnp.int32, (8, a_s.shape[1]), 0)

    def body(g, carry):
        r0 = pl.multiple_of(g * 8, 8)
        a = a_s[pl.ds(r0, 8), :]
        u = u_s[pl.ds(r0, 8), :]
        for s in (1, 2, 4):
            ok = row >= s
            a_sh = jnp.where(ok, pltpu.roll(a, s, 0), 1.0)
            u_sh = jnp.where(ok, pltpu.roll(u, s, 0), 0.0)
            u = a * u_sh + u
            a = a * a_sh
        h = a * carry + u
        h_s[pl.ds(r0, 8), :] = h
        return h[7:8, :]

    hc[...] = lax.fori_loop(0, ts // 8, body, hc[...], unroll=4)
    o_ref[...] = (h_s[...] * jax.nn.gelu(gate_ref[...])).astype(BF16)


def _block_diag(wb):
    nb, bs, _ = wb.shape
    eye = jnp.eye(nb, dtype=wb.dtype)
    return (eye[:, None, :, None] * wb[:, :, None, :]).reshape(nb * bs, nb * bs)


def _lru(gate, xr, conv_w, conv_b, ra_w, ra_b, ix_w, ix_b, lam, *, bsz, ts=256):
    n, w = xr.shape
    seq = n // bsz
    assert seq % ts == 0
    nt = seq // ts
    row = lambda b, j: (b * nt + j, 0)
    vec = lambda v: v.reshape(1, w).astype(F32)
    return pl.pallas_call(
        functools.partial(_lru_kernel, ts=ts),
        grid=(bsz, nt),
        in_specs=[pl.BlockSpec((ts, w), row), pl.BlockSpec((ts, w), row),
                  _const_spec((conv_w.shape[0], w)), _const_spec((1, w)),
                  _const_spec((w, w)), _const_spec((1, w)), _const_spec((w, w)), _const_spec((1, w)),
                  _const_spec((1, w))],
        out_specs=pl.BlockSpec((ts, w), row),
        out_shape=jax.ShapeDtypeStruct((n, w), BF16),
        scratch_shapes=[pltpu.VMEM((ts + 8, w), F32), pltpu.VMEM((ts, w), F32),
                        pltpu.VMEM((ts, w), F32), pltpu.VMEM((ts, w), F32), pltpu.VMEM((1, w), F32)],
        compiler_params=_cparams(("arbitrary", "arbitrary")),
        name="rg_lru",
    )(gate, xr, conv_w, vec(conv_b), _block_diag(ra_w).astype(BF16), vec(ra_b),
      _block_diag(ix_w).astype(BF16), vec(ix_b), vec(jax.nn.softplus(-lam)))


def _conf_kernel(c_ref, w_ref, b_ref, g_ref, beta_ref, o_ref, cbuf, *, ts, halo, width):
    j = pl.program_id(1)

    @pl.when(j == 0)
    def _():
        cbuf[0:halo, :] = jnp.zeros((halo, cbuf.shape[1]), F32)

    cbuf[halo:halo + ts, :] = c_ref[...]
    base = halo - (width - 1)
    y = b_ref[...] + w_ref[0:1, :] * cbuf[base:base + ts, :]
    for k in range(1, width):
        y = y + w_ref[k:k + 1, :] * cbuf[base + k:base + k + ts, :]
    cbuf[0:halo, :] = cbuf[ts:ts + halo, :]

    mu = jnp.mean(y, axis=-1, keepdims=True)
    yc = y - mu
    var = jnp.mean(yc * yc, axis=-1, keepdims=True)
    z = yc * lax.rsqrt(var + EPS) * g_ref[...] + beta_ref[...]
    o_ref[...] = (z * jax.nn.sigmoid(z)).astype(BF16)


def _conf(c, dw_w, dw_b, ln_g, ln_b, *, bsz, ts=256, halo=32):
    n, w = c.shape
    seq = n // bsz
    width = dw_w.shape[0]
    assert seq % ts == 0 and width - 1 <= halo <= ts
    nt = seq // ts
    row = lambda b, j: (b * nt + j, 0)
    vec = lambda v: v.reshape(1, w).astype(F32)
    return pl.pallas_call(
        functools.partial(_conf_kernel, ts=ts, halo=halo, width=width),
        grid=(bsz, nt),
        in_specs=[pl.BlockSpec((ts, w), row), _const_spec((width, w)),
                  _const_spec((1, w)), _const_spec((1, w)), _const_spec((1, w))],
        out_specs=pl.BlockSpec((ts, w), row),
        out_shape=jax.ShapeDtypeStruct((n, w), BF16),
        scratch_shapes=[pltpu.VMEM((ts + halo, w), F32)],
        compiler_params=_cparams(("arbitrary", "arbitrary")),
        name="conformer_conv",
    )(c, dw_w, vec(dw_b), vec(ln_g), vec(ln_b))


def _bucket_tiles(t):
    assert t > REL_MAX_DIST
    n = np.arange(2 * t)
    nf = np.maximum(n, 1).astype(np.float32)
    large = REL_MAX_EXACT + (np.log(nf / np.float32(REL_MAX_EXACT))
                             / np.float32(math.log(REL_MAX_DIST / REL_MAX_EXACT))
                             * np.float32(REL_BUCKETS - REL_MAX_EXACT)).astype(np.int32)
    bucket = np.where(n < REL_MAX_EXACT, n, np.minimum(large, REL_BUCKETS - 1)).astype(np.int32)
    i = np.arange(t)[:, None]
    j = np.arange(t)[None, :]
    return np.stack([bucket[np.maximum(i - j, 0)], bucket[t + i - j]])


def _build_bias(idx_ref, rb_ref, d0_ref, d1_ref):
    t = idx_ref.shape[1]
    causal = (lax.broadcasted_iota(jnp.int32, (t, t), 0) >= lax.broadcasted_iota(jnp.int32, (t, t), 1))
    for which, dst in ((0, d0_ref), (1, d1_ref)):
        idx = idx_ref[which]
        for h in range(N_HEADS):
            acc = jnp.zeros((t, t), F32)
            for b in range(REL_BUCKETS):
                acc = jnp.where(idx == b, rb_ref[b, h], acc)
            dst[h] = jnp.where(causal, acc, NEG) if which == 0 else acc


def _attn_init(m_ref, acc_ref):
    m_ref[...] = jnp.full(m_ref.shape, M_INIT, F32)
    acc_ref[...] = jnp.zeros(acc_ref.shape, F32)


def _softmax_step(s, h, v_h, m_ref, acc_ref):
    m_old = m_ref[h]
    m_new = jnp.maximum(m_old, jnp.max(s, axis=-1, keepdims=True))
    p = jnp.exp(s - jnp.tile(m_new, (1, s.shape[1] // SLOT)))
    m_ref[h] = m_new
    acc_ref[h] = jnp.exp(m_old - m_new) * acc_ref[h] + _dot(p.astype(BF16), v_h)


def _attn_finish(o_ref, acc_ref):
    outs = []
    for h in range(N_HEADS):
        acc = acc_ref[h]
        outs.append(acc * (1.0 / acc[:, FEAT0:FEAT0 + 1]))
    o_ref[...] = _from_slots(outs).astype(BF16)


def _far_bias(rb_ref, h):
    return rb_ref[REL_BUCKETS - 1, h]


def _moba_kernel(rb_ref, q_ref, k_ref, v_ref, km_ref, idx_ref, o_ref,
                 acc_ref, m_ref, qs_ref, d0_ref, d1_ref, *, n_blocks):
    i = pl.program_id(1)
    t = ATTN_TILE
    nbp = -(-n_blocks // 8) * 8

    @pl.when((pl.program_id(0) == 0) & (i == 0))
    def _():
        _build_bias(idx_ref, rb_ref, d0_ref, d1_ref)

    _attn_init(m_ref, acc_ref)

    blk = lax.broadcasted_iota(jnp.int32, (nbp, t), 0)
    past = blk < i
    for h in range(N_HEADS):
        qh = q_ref[:, _slot(h)]
        km_hi, km_lo = _split_bf16(km_ref[:, _slot(h)])
        gate_t = _dot_nt(km_hi, qh) + _dot_nt(km_lo, qh)
        g = jnp.where(past, gate_t[FEAT0:FEAT0 + nbp, :], NEG)
        rank = jnp.zeros((nbp, t), F32)
        for j in range(n_blocks):
            gj = g[j:j + 1, :]
            beats = (gj > g) | ((gj == g) & (blk > j))
            rank = rank + jnp.where(beats, 1.0, 0.0)
        flag = jnp.where(past & (rank >= MOBA_TOPK), NEG, 0.0)
        flag_t = jnp.concatenate([jnp.zeros((FEAT0, t), F32), flag,
                                  jnp.zeros((SLOT - FEAT0 - nbp, t), F32)], axis=0)
        qs_ref[:, _slot(h)] = (qh.astype(F32) + flag_t.T).astype(BF16)

    def tile(kt, kind):
        r0 = pl.multiple_of(kt * t, t)
        for h in range(N_HEADS):
            s = _dot_nt(qs_ref[:, _slot(h)], k_ref[pl.ds(r0, t), _slot(h)])
            if kind == "diag":
                s = s + d0_ref[h]
            elif kind == "near":
                s = s + d1_ref[h]
            else:
                s = s + _far_bias(rb_ref, h)
            _softmax_step(s, h, v_ref[pl.ds(r0, t), _slot(h)], m_ref, acc_ref)

    def far_body(kt, carry):
        tile(kt, "far")
        return carry

    lax.fori_loop(0, jnp.maximum(i - 1, 0), far_body, 0)

    @pl.when(i >= 1)
    def _():
        tile(i - 1, "near")

    tile(i, "diag")
    _attn_finish(o_ref, acc_ref)


def _smem_spec():
    return pl.BlockSpec(memory_space=pltpu.SMEM)


def _moba(q, k, v, kmean, rel_bias, *, bsz):
    n = q.shape[0]
    seq = n // bsz
    t = ATTN_TILE
    assert seq % t == 0 and t == MOBA_BLOCK
    nq = seq // t
    assert FEAT0 + nq <= SLOT
    km = kmean.reshape(bsz, nq, N_HEADS, HEAD_DIM)
    km = jnp.pad(km, ((0, 0), (FEAT0, SLOT - FEAT0 - nq), (0, 0), (0, SLOT - HEAD_DIM)))
    km = km.reshape(bsz, SLOT, SLOT_WIDTH)
    r3 = lambda a: a.reshape(bsz, seq, SLOT_WIDTH)
    seq_spec = pl.BlockSpec((None, seq, SLOT_WIDTH), lambda b, i: (b, 0, 0), pipeline_mode=pl.Buffered(1))
    out = pl.pallas_call(
        functools.partial(_moba_kernel, n_blocks=nq),
        grid=(bsz, nq),
        in_specs=[_smem_spec(),
                  pl.BlockSpec((None, t, SLOT_WIDTH), lambda b, i: (b, i, 0)), seq_spec, seq_spec,
                  pl.BlockSpec((None, SLOT, SLOT_WIDTH), lambda b, i: (b, 0, 0)),
                  _const_spec((2, t, t))],
        out_specs=pl.BlockSpec((None, t, ATTN_WIDTH), lambda b, i: (b, i, 0)),
        out_shape=jax.ShapeDtypeStruct((bsz, seq, ATTN_WIDTH), BF16),
        scratch_shapes=[pltpu.VMEM((N_HEADS, t, SLOT), F32), pltpu.VMEM((N_HEADS, t, SLOT), F32),
                        pltpu.VMEM((t, SLOT_WIDTH), BF16),
                        pltpu.VMEM((N_HEADS, t, t), F32), pltpu.VMEM((N_HEADS, t, t), F32)],
        compiler_params=_cparams(("arbitrary", "arbitrary")),
        name="moba_attention",
    )(rel_bias, r3(q), r3(k), r3(v), km, jnp.asarray(_bucket_tiles(t)))
    return out.reshape(n, ATTN_WIDTH)


def _dsa_kernel(rb_ref, q_ref, k_ref, v_ref, qi_ref, ki_ref, wi_ref, idx_ref, tri_ref, o_ref,
                acc_ref, m_ref, isc_ref, bc_ref, d0_ref, d1_ref, *, n_sel):
    i = pl.program_id(1)
    t = ATTN_TILE
    row = lax.broadcasted_iota(jnp.int32, (t, t), 0)
    col = lax.broadcasted_iota(jnp.int32, (t, t), 1)
    lane128 = lax.broadcasted_iota(jnp.int32, (1, 128), 1)

    @pl.when((pl.program_id(0) == 0) & (i == 0))
    def _():
        _build_bias(idx_ref, rb_ref, d0_ref, d1_ref)

    _attn_init(m_ref, acc_ref)

    def index_tile(kt, diag):
        r0 = pl.multiple_of(kt * t, t)
        ki2 = ki_ref[pl.ds(r0, t), :]
        acc = jnp.zeros((t, t), F32)
        for pr in range(IDX_HEADS // 2):
            q2 = qi_ref[:, pr * 128:(pr + 1) * 128]
            for half in range(2):
                h = 2 * pr + half
                hm = (lane128 >= half * IDX_DIM) & (lane128 < (half + 1) * IDX_DIM)
                s = _dot_nt(jnp.where(hm, q2, jnp.zeros((), BF16)), ki2)
                acc = acc + jnp.maximum(s, 0.0) * wi_ref[:, h:h + 1]
        if diag:
            acc = jnp.where(row >= col, acc, -jnp.inf)
        isc_ref[kt] = acc

    def index_body(kt, carry):
        index_tile(kt, False)
        return carry

    lax.fori_loop(0, i, index_body, 0)
    index_tile(i, True)

    def fold(x, op):
        return op(x[:, 0:128], x[:, 128:256])

    def minmax_body(kt, carry):
        mn, mx = carry
        x = isc_ref[kt]
        mx = jnp.maximum(mx, fold(x, jnp.maximum))
        mn = jnp.minimum(mn, fold(jnp.where(x == -jnp.inf, jnp.inf, x), jnp.minimum))
        return mn, mx

    mn, mx = lax.fori_loop(0, i + 1, minmax_body,
                           (jnp.full((t, 128), jnp.inf, F32), jnp.full((t, 128), -jnp.inf, F32)))
    lo = jnp.min(mn, axis=-1, keepdims=True)
    mx = jnp.max(mx, axis=-1, keepdims=True)
    hi = mx + jnp.abs(mx) * 1e-3 + 1e-30
    n_valid = (i * t + 1 + lax.broadcasted_iota(jnp.int32, (t, 1), 0)).astype(F32)
    want = jnp.minimum(n_valid, float(n_sel))

    def count_ge(thr):
        def body(kt, c):
            x = isc_ref[kt]
            ge = jnp.where(x >= thr, 1.0, 0.0)
            return c + fold(ge, jnp.add)
        c = lax.fori_loop(0, i + 1, body, jnp.zeros((t, 128), F32))
        return jnp.sum(c, axis=-1, keepdims=True)

    def bisect_body(_, carry):
        lo, hi, c_lo, c_hi = carry
        mid = 0.5 * (lo + hi)
        c_mid = count_ge(mid)
        up = c_mid >= want
        return (jnp.where(up, mid, lo), jnp.where(up, hi, mid),
                jnp.where(up, c_mid, c_lo), jnp.where(up, c_hi, c_mid))

    lo, hi, c_lo, c_hi = lax.fori_loop(0, BISECT_ITERS, bisect_body,
                                       (lo, hi, n_valid, jnp.zeros((t, 1), F32)))
    need = want - c_hi
    banded = jnp.max(c_lo - want) > 0.0
    bc_ref[...] = jnp.zeros(bc_ref.shape, F32)

    def select_mask(kt):
        x = isc_ref[kt]

        def plain():
            return jnp.where(x >= lo, 1.0, 0.0)

        def with_band():
            band = jnp.where((x >= lo) & (x < hi), 1.0, 0.0)
            before = bc_ref[...] + _dot(band.astype(BF16), tri_ref[...])
            bc_ref[...] = bc_ref[...] + jnp.sum(band, axis=-1, keepdims=True)
            return jnp.where((x >= hi) | ((band > 0.0) & (before < need)), 1.0, 0.0)

        return lax.cond(banded, with_band, plain) > 0.0

    def tile(kt, kind):
        r0 = pl.multiple_of(kt * t, t)
        mask = select_mask(kt)
        for h in range(N_HEADS):
            s = _dot_nt(q_ref[:, _slot(h)], k_ref[pl.ds(r0, t), _slot(h)])
            if kind == "diag":
                s = s + d0_ref[h]
            elif kind == "near":
                s = s + d1_ref[h]
            else:
                s = s + _far_bias(rb_ref, h)
            s = jnp.where(mask, s, NEG)
            _softmax_step(s, h, v_ref[pl.ds(r0, t), _slot(h)], m_ref, acc_ref)

    def far_body(kt, carry):
        tile(kt, "far")
        return carry

    lax.fori_loop(0, jnp.maximum(i - 1, 0), far_body, 0)

    @pl.when(i >= 1)
    def _():
        tile(i - 1, "near")

    tile(i, "diag")
    _attn_finish(o_ref, acc_ref)


def _dsa(q, k, v, qi, ki2, wi, rel_bias, *, bsz):
    n = q.shape[0]
    seq = n // bsz
    t = ATTN_TILE
    assert seq % t == 0
    nq = seq // t
    n_sel = min(DSA_TOPK_MAX, seq // 4)
    tri = (jnp.arange(t)[:, None] < jnp.arange(t)[None, :]).astype(BF16)
    r3 = lambda a: a.reshape(bsz, seq, a.shape[-1])
    tile_spec = lambda width: pl.BlockSpec((None, t, width), lambda b, i: (b, i, 0))
    seq_spec = lambda width: pl.BlockSpec((None, seq, width), lambda b, i: (b, 0, 0),
                                          pipeline_mode=pl.Buffered(1))
    out = pl.pallas_call(
        functools.partial(_dsa_kernel, n_sel=n_sel),
        grid=(bsz, nq),
        in_specs=[_smem_spec(), tile_spec(SLOT_WIDTH), seq_spec(SLOT_WIDTH), seq_spec(SLOT_WIDTH),
                  tile_spec(ATTN_WIDTH), seq_spec(128), tile_spec(128),
                  _const_spec((2, t, t)), _const_spec((t, t))],
        out_specs=tile_spec(ATTN_WIDTH),
        out_shape=jax.ShapeDtypeStruct((bsz, seq, ATTN_WIDTH), BF16),
        scratch_shapes=[pltpu.VMEM((N_HEADS, t, SLOT), F32), pltpu.VMEM((N_HEADS, t, SLOT), F32),
                        pltpu.VMEM((nq, t, t), F32), pltpu.VMEM((t, 1), F32),
                        pltpu.VMEM((N_HEADS, t, t), F32), pltpu.VMEM((N_HEADS, t, t), F32)],
        compiler_params=_cparams(("arbitrary", "arbitrary")),
        name="dsa_attention",
    )(rel_bias, r3(q), r3(k), r3(v), r3(qi), r3(ki2), r3(wi), jnp.asarray(_bucket_tiles(t)), tri)
    return out.reshape(n, ATTN_WIDTH)


def _mixout_kernel(x_ref, ya_ref, yb_ref, wa_ref, wb_ref, o_ref):
    o_ref[...] = x_ref[...] + _dot(ya_ref[...], wa_ref[...]) + _dot(yb_ref[...], wb_ref[...])


def _mixout(x, ya, yb, w_out, *, tm=512):
    n, d = x.shape
    w = ya.shape[1]
    assert n % tm == 0
    row = lambda i: (i, 0)
    return pl.pallas_call(
        _mixout_kernel,
        grid=(n // tm,),
        in_specs=[pl.BlockSpec((tm, d), row), pl.BlockSpec((tm, w), row), pl.BlockSpec((tm, w), row),
                  _const_spec((w, d)), _const_spec((w, d))],
        out_specs=pl.BlockSpec((tm, d), row),
        out_shape=jax.ShapeDtypeStruct((n, d), F32),
        compiler_params=_cparams(("parallel",)),
        name="mixout",
    )(x, ya, yb, w_out[:w].astype(BF16), w_out[w:].astype(BF16))


def kernel(x, rel_bias, ffn1_norm, ffn1_w_gate, ffn1_w_up, ffn1_w_down, mix_norm, ffn2_norm, ffn2_w_gate, ffn2_w_up, ffn2_w_down, ev_w_in, ev_conv_w, ev_conv_b, ev_ra_w, ev_ra_b, ev_ix_w, ev_ix_b, ev_lambda, ev_q_norm, ev_k_norm, ev_w_out, od_w_in, od_dw_w, od_dw_b, od_ln_g, od_ln_b, od_q_norm, od_k_norm, od_w_out):
    bsz, seq, d = x.shape
    depth = ffn1_norm.shape[0]
    h = x.reshape(bsz * seq, d)
    for i in range(depth):
        h = _ffn(h, ffn1_norm[i], ffn1_w_gate[i], ffn1_w_up[i], ffn1_w_down[i])
        j = i // 2
        if i % 2 == 0:
            gate, xr, q, k, v, kmean = _mixin_even(h, mix_norm[i], ev_w_in[j], ev_q_norm[j], ev_k_norm[j],
                                                   n_blocks=seq // MOBA_BLOCK)
            ya = _lru(gate, xr, ev_conv_w[j], ev_conv_b[j], ev_ra_w[j], ev_ra_b[j],
                      ev_ix_w[j], ev_ix_b[j], ev_lambda[j], bsz=bsz)
            yb = _moba(q, k, v, kmean, rel_bias, bsz=bsz)
            h = _mixout(h, ya, yb, ev_w_out[j])
        else:
            c, q, k, v, qi, ki2, wi = _mixin_odd(h, mix_norm[i], od_w_in[j], od_q_norm[j], od_k_norm[j])
            yc = _conf(c, od_dw_w[j], od_dw_b[j], od_ln_g[j], od_ln_b[j], bsz=bsz)
            yd = _dsa(q, k, v, qi, ki2, wi, rel_bias, bsz=bsz)
            h = _mixout(h, yc, yd, od_w_out[j])
        h = _ffn(h, ffn2_norm[i], ffn2_w_gate[i], ffn2_w_up[i], ffn2_w_down[i])
    return h.reshape(bsz, seq, d)
```

```python
import functools
import math

import numpy as np
import jax
import jax.numpy as jnp
from jax import lax
from jax.experimental import pallas as pl
from jax.experimental.pallas import tpu as pltpu

F32 = jnp.float32
BF16 = jnp.bfloat16

N_HEADS = 8
HEAD_DIM = 64
ATTN_WIDTH = N_HEADS * HEAD_DIM
LRU_C = 8.0
MOBA_BLOCK = 256
MOBA_TOPK = 3
IDX_HEADS = 8
IDX_DIM = 64
DSA_TOPK_MAX = 256
REL_BUCKETS = 32
REL_MAX_EXACT = REL_BUCKETS // 2
REL_MAX_DIST = 128
EPS = 1e-6
NEG = -1e30
M_INIT = -1e29
ATTN_TILE = 256
SLOT = 128
SLOT_WIDTH = N_HEADS * SLOT
FEAT0 = HEAD_DIM
HEADS_AHEAD = 8
BISECT_ITERS = 32
LOG2E = math.log2(math.e)
Q_SCALE = HEAD_DIM ** -0.5 * LOG2E
VMEM_LIMIT = 56 * 1024 * 1024


def _cparams(sem):
    return pltpu.CompilerParams(dimension_semantics=sem, vmem_limit_bytes=VMEM_LIMIT)


def _dot(a, b):
    return jnp.dot(a, b, preferred_element_type=F32)


def _dot_nt(a, b):
    return lax.dot_general(a, b, (((1,), (1,)), ((), ())), preferred_element_type=F32)


def _split_bf16(x):
    hi = x.astype(BF16)
    lo = (x - hi.astype(F32)).astype(BF16)
    return hi, lo


def _rms_rows(x, g):
    return x * lax.rsqrt(jnp.mean(x * x, axis=-1, keepdims=True) + EPS) * g


def _head_rms(z, gmat, g):
    hi, lo = _split_bf16(z * z)
    ms = _dot(hi, gmat) + _dot(lo, gmat)
    return z * lax.rsqrt(ms + EPS) * g


def _const_spec(shape):
    nd = len(shape)
    return pl.BlockSpec(shape, lambda *_: (0,) * nd, pipeline_mode=pl.Buffered(1))


def _slot(h):
    return slice(h * SLOT, (h + 1) * SLOT)


def _to_slots(x):
    low = lax.broadcasted_iota(jnp.int32, (1, SLOT), 1) < HEAD_DIM
    slots = []
    for p in range(N_HEADS // 2):
        chunk = x[:, p * SLOT:(p + 1) * SLOT]
        slots.append(jnp.where(low, chunk, 0.0))
        slots.append(jnp.where(low, pltpu.roll(chunk, HEAD_DIM, 1), 0.0))
    return slots


def _ffn_kernel(x_ref, g_ref, wg_ref, wu_ref, wd_ref, o_ref, *, ff_chunk):
    x = x_ref[...]
    hn = _rms_rows(x, g_ref[...]).astype(BF16)
    acc = jnp.zeros(x.shape, F32)
    d_ff = wg_ref.shape[1]
    for c in range(d_ff // ff_chunk):
        sl = slice(c * ff_chunk, (c + 1) * ff_chunk)
        gt = _dot(hn, wg_ref[:, sl])
        ut = _dot(hn, wu_ref[:, sl])
        a = (gt * jax.nn.sigmoid(gt) * ut).astype(BF16)
        acc = acc + _dot(a, wd_ref[sl, :])
    o_ref[...] = x + 0.5 * acc


def _ffn(x, g, wg, wu, wd, *, tm=512, ff_chunk=256):
    n, d = x.shape
    d_ff = wg.shape[1]
    assert n % tm == 0 and d_ff % ff_chunk == 0
    return pl.pallas_call(
        functools.partial(_ffn_kernel, ff_chunk=ff_chunk),
        grid=(n // tm,),
        in_specs=[pl.BlockSpec((tm, d), lambda i: (i, 0)),
                  _const_spec((1, d)), _const_spec((d, d_ff)), _const_spec((d, d_ff)),
                  _const_spec((d_ff, d))],
        out_specs=pl.BlockSpec((tm, d), lambda i: (i, 0)),
        out_shape=jax.ShapeDtypeStruct((n, d), F32),
        compiler_params=_cparams(("parallel",)),
        name="ffn",
    )(x, g.reshape(1, d), wg.astype(BF16), wu.astype(BF16), wd.astype(BF16))


def _store_qkv_slots(hn, w_ref, gmat_ref, qn_ref, kn_ref, q_o, k_o, v_o, col0, k_feat):
    w = ATTN_WIDTH
    gmat = gmat_ref[...]
    q = _head_rms(_dot(hn, w_ref[:, col0:col0 + w]), gmat, qn_ref[...]) * Q_SCALE
    k = _head_rms(_dot(hn, w_ref[:, col0 + w:col0 + 2 * w]), gmat, kn_ref[...])
    v = _dot(hn, w_ref[:, col0 + 2 * w:col0 + 3 * w])
    lane = lax.broadcasted_iota(jnp.int32, (1, SLOT), 1)
    k_one = k_feat(lane)
    for h, (qs, ks, vs) in enumerate(zip(_to_slots(q), _to_slots(k), _to_slots(v))):
        q_o[:, _slot(h)] = qs.astype(BF16)
        if k_one is not None:
            ks = jnp.where(k_one, 1.0, ks)
        k_o[:, _slot(h)] = ks.astype(BF16)
        vs = jnp.where(lane == FEAT0, 1.0, vs)
        for r in range(vs.shape[0] // ATTN_TILE):
            v_o[r, _slot(h), :] = vs[r * ATTN_TILE:(r + 1) * ATTN_TILE].T.astype(BF16)
    return k


def _mixin_even_kernel(x_ref, g_ref, w_ref, gmat_ref, qn_ref, kn_ref,
                       gate_o, xr_o, q_o, k_o, v_o, km_o, *, n_blocks):
    tm = x_ref.shape[0]
    hn = _rms_rows(x_ref[...], g_ref[...]).astype(BF16)
    w = ATTN_WIDTH
    gate_o[...] = _dot(hn, w_ref[:, 0:w])
    xr_o[...] = _dot(hn, w_ref[:, w:2 * w])
    blocks_per_tile = tm // MOBA_BLOCK
    row_block = lax.broadcasted_iota(jnp.int32, (tm, 1), 0) // MOBA_BLOCK
    block = (pl.program_id(0) * blocks_per_tile + row_block) % n_blocks
    k = _store_qkv_slots(hn, w_ref, gmat_ref, qn_ref, kn_ref, q_o, k_o, v_o, 2 * w,
                         lambda lane: lane == FEAT0 + block)
    for r in range(blocks_per_tile):
        km_o[r] = jnp.mean(k[r * MOBA_BLOCK:(r + 1) * MOBA_BLOCK], axis=0, keepdims=True)


def _vt_spec(tm):
    return pl.BlockSpec((tm // ATTN_TILE, SLOT_WIDTH, ATTN_TILE), lambda i: (i, 0, 0))


def _vt_shape(n):
    return jax.ShapeDtypeStruct((n // ATTN_TILE, SLOT_WIDTH, ATTN_TILE), BF16)


def _group_mean_matrix():
    idx = jnp.arange(ATTN_WIDTH) // HEAD_DIM
    return (idx[:, None] == idx[None, :]).astype(BF16) * (1.0 / HEAD_DIM)


def _mixin_even(x, g, w_in, q_norm, k_norm, *, n_blocks, tm=512):
    n, d = x.shape
    w = ATTN_WIDTH
    assert n % tm == 0 and tm % MOBA_BLOCK == 0 and tm % ATTN_TILE == 0 and w_in.shape[1] == 5 * w
    assert FEAT0 + n_blocks <= SLOT
    row = lambda i: (i, 0)
    blk = tm // MOBA_BLOCK
    slot_shape = jax.ShapeDtypeStruct((n, SLOT_WIDTH), BF16)
    return pl.pallas_call(
        functools.partial(_mixin_even_kernel, n_blocks=n_blocks),
        grid=(n // tm,),
        in_specs=[pl.BlockSpec((tm, d), row), _const_spec((1, d)), _const_spec((d, 5 * w)),
                  _const_spec((w, w)), _const_spec((1, w)), _const_spec((1, w))],
        out_specs=[pl.BlockSpec((tm, w), row)] * 2 + [pl.BlockSpec((tm, SLOT_WIDTH), row)] * 2
        + [_vt_spec(tm), pl.BlockSpec((blk, 1, w), lambda i: (i, 0, 0))],
        out_shape=[jax.ShapeDtypeStruct((n, w), F32), jax.ShapeDtypeStruct((n, w), F32),
                   slot_shape, slot_shape, _vt_shape(n),
                   jax.ShapeDtypeStruct((n // MOBA_BLOCK, 1, w), F32)],
        compiler_params=_cparams(("parallel",)),
        name="mixin_even",
    )(x, g.reshape(1, d), w_in.astype(BF16), _group_mean_matrix(),
      jnp.tile(q_norm, N_HEADS).reshape(1, w), jnp.tile(k_norm, N_HEADS).reshape(1, w))


def _mixin_odd_kernel(x_ref, g_ref, w_ref, ws_ref, gmat_ref, qn_ref, kn_ref,
                      c_o, q_o, k_o, v_o, qi_o, ki_o, wi_o):
    hn = _rms_rows(x_ref[...], g_ref[...]).astype(BF16)
    w = ATTN_WIDTH
    ca = _dot(hn, w_ref[:, 0:w])
    cg = _dot(hn, w_ref[:, w:2 * w])
    c_o[...] = ca * jax.nn.sigmoid(cg)
    _store_qkv_slots(hn, w_ref, gmat_ref, qn_ref, kn_ref, q_o, k_o, v_o, 2 * w, lambda lane: None)
    qi_o[...] = _dot(hn, w_ref[:, 5 * w:6 * w]).astype(BF16)
    small = _dot(hn, ws_ref[...])
    ki_o[...] = small[:, 0:128].astype(BF16)
    wi = small[:, 128:256] * (IDX_DIM ** -0.5 * IDX_HEADS ** -0.5)
    wi_o[...] = wi.T[0:IDX_HEADS, :]


def _mixin_odd(x, g, w_in, q_norm, k_norm, *, tm=512):
    n, d = x.shape
    w = ATTN_WIDTH
    assert n % tm == 0 and tm % ATTN_TILE == 0 and w_in.shape[1] == 6 * w + IDX_DIM + IDX_HEADS
    w_main = w_in[:, :6 * w].astype(BF16)
    w_ki = w_in[:, 6 * w:6 * w + IDX_DIM]
    w_wi = jnp.pad(w_in[:, 6 * w + IDX_DIM:], ((0, 0), (0, 128 - IDX_HEADS)))
    w_small = jnp.concatenate([w_ki, w_ki, w_wi], axis=1).astype(BF16)
    row = lambda i: (i, 0)
    slot_shape = jax.ShapeDtypeStruct((n, SLOT_WIDTH), BF16)
    return pl.pallas_call(
        _mixin_odd_kernel,
        grid=(n // tm,),
        in_specs=[pl.BlockSpec((tm, d), row), _const_spec((1, d)), _const_spec((d, 6 * w)),
                  _const_spec((d, 256)), _const_spec((w, w)), _const_spec((1, w)), _const_spec((1, w))],
        out_specs=[pl.BlockSpec((tm, w), row)] + [pl.BlockSpec((tm, SLOT_WIDTH), row)] * 2
        + [_vt_spec(tm), pl.BlockSpec((tm, w), row), pl.BlockSpec((tm, 128), row),
           pl.BlockSpec((IDX_HEADS, tm), lambda i: (0, i))],
        out_shape=[jax.ShapeDtypeStruct((n, w), F32), slot_shape, slot_shape, _vt_shape(n),
                   jax.ShapeDtypeStruct((n, w), BF16),
                   jax.ShapeDtypeStruct((n, 128), BF16), jax.ShapeDtypeStruct((IDX_HEADS, n), F32)],
        compiler_params=_cparams(("parallel",)),
        name="mixin_odd",
    )(x, g.reshape(1, d), w_main, w_small, _group_mean_matrix(),
      jnp.tile(q_norm, N_HEADS).reshape(1, w), jnp.tile(k_norm, N_HEADS).reshape(1, w))


def _lru_kernel(gate_ref, xr_ref, cw_ref, cb_ref, wa_ref, ba_ref, wx_ref, bx_ref, sp_ref,
                o_ref, xbuf, a_s, u_s, h_s, hc, *, ts):
    j = pl.program_id(1)

    @pl.when(j == 0)
    def _():
        xbuf[0:8, :] = jnp.zeros((8, xbuf.shape[1]), F32)
        hc[...] = jnp.zeros(hc.shape, F32)

    xbuf[8:8 + ts, :] = xr_ref[...]
    xc = cb_ref[...] + cw_ref[0:1, :] * xbuf[5:5 + ts, :]
    for k in range(1, 4):
        xc = xc + cw_ref[k:k + 1, :] * xbuf[5 + k:5 + k + ts, :]
    xbuf[0:8, :] = xbuf[ts:ts + 8, :]

    xcb = xc.astype(BF16)
    r = jax.nn.sigmoid(_dot(xcb, wa_ref[...]) + ba_ref[...])
    ig = jax.nn.sigmoid(_dot(xcb, wx_ref[...]) + bx_ref[...])
    log_a = -LRU_C * r * sp_ref[...]
    a = jnp.exp(log_a)
    a_s[...] = a
    u_s[...] = jnp.sqrt(-jnp.tanh(log_a) * (a * a + 1.0)) * (ig * xc)

    row = lax.broadcasted_iota(jnp.int32, (8, a_s.shape[1]), 0)

    def body(g, carry):
        r0 = pl.multiple_of(g * 8, 8)
        a = a_s[pl.ds(r0, 8), :]
        u = u_s[pl.ds(r0, 8), :]
        for s in (1, 2, 4):
            ok = row >= s
            a_sh = jnp.where(ok, pltpu.roll(a, s, 0), 1.0)
            u_sh = jnp.where(ok, pltpu.roll(u, s, 0), 0.0)
            u = a * u_sh + u
            a = a * a_sh
        h = a * carry + u
        h_s[pl.ds(r0, 8), :] = h
        return h[7:8, :]

    hc[...] = lax.fori_loop(0, ts // 8, body, hc[...], unroll=4)
    o_ref[...] = (h_s[...] * jax.nn.gelu(gate_ref[...])).astype(BF16)


def _block_diag(wb):
    nb, bs, _ = wb.shape
    eye = jnp.eye(nb, dtype=wb.dtype)
    return (eye[:, None, :, None] * wb[:, :, None, :]).reshape(nb * bs, nb * bs)


def _lru(gate, xr, conv_w, conv_b, ra_w, ra_b, ix_w, ix_b, lam, *, bsz, ts=256):
    n, w = xr.shape
    seq = n // bsz
    assert seq % ts == 0
    nt = seq // ts
    row = lambda b, j: (b * nt + j, 0)
    vec = lambda v: v.reshape(1, w).astype(F32)
    return pl.pallas_call(
        functools.partial(_lru_kernel, ts=ts),
        grid=(bsz, nt),
        in_specs=[pl.BlockSpec((ts, w), row), pl.BlockSpec((ts, w), row),
                  _const_spec((conv_w.shape[0], w)), _const_spec((1, w)),
                  _const_spec((w, w)), _const_spec((1, w)), _const_spec((w, w)), _const_spec((1, w)),
                  _const_spec((1, w))],
        out_specs=pl.BlockSpec((ts, w), row),
        out_shape=jax.ShapeDtypeStruct((n, w), BF16),
        scratch_shapes=[pltpu.VMEM((ts + 8, w), F32), pltpu.VMEM((ts, w), F32),
                        pltpu.VMEM((ts, w), F32), pltpu.VMEM((ts, w), F32), pltpu.VMEM((1, w), F32)],
        compiler_params=_cparams(("arbitrary", "arbitrary")),
        name="rg_lru",
    )(gate, xr, conv_w, vec(conv_b), _block_diag(ra_w).astype(BF16), vec(ra_b),
      _block_diag(ix_w).astype(BF16), vec(ix_b), vec(jax.nn.softplus(-lam)))


def _conf_kernel(c_ref, w_ref, b_ref, g_ref, beta_ref, o_ref, cbuf, *, ts, halo, width):
    j = pl.program_id(1)

    @pl.when(j == 0)
    def _():
        cbuf[0:halo, :] = jnp.zeros((halo, cbuf.shape[1]), F32)

    cbuf[halo:halo + ts, :] = c_ref[...]
    base = halo - (width - 1)
    y = b_ref[...] + w_ref[0:1, :] * cbuf[base:base + ts, :]
    for k in range(1, width):
        y = y + w_ref[k:k + 1, :] * cbuf[base + k:base + k + ts, :]
    cbuf[0:halo, :] = cbuf[ts:ts + halo, :]

    mu = jnp.mean(y, axis=-1, keepdims=True)
    yc = y - mu
    var = jnp.mean(yc * yc, axis=-1, keepdims=True)
    z = yc * lax.rsqrt(var + EPS) * g_ref[...] + beta_ref[...]
    o_ref[...] = (z * jax.nn.sigmoid(z)).astype(BF16)


def _conf(c, dw_w, dw_b, ln_g, ln_b, *, bsz, ts=256, halo=32):
    n, w = c.shape
    seq = n // bsz
    width = dw_w.shape[0]
    assert seq % ts == 0 and width - 1 <= halo <= ts
    nt = seq // ts
    row = lambda b, j: (b * nt + j, 0)
    vec = lambda v: v.reshape(1, w).astype(F32)
    return pl.pallas_call(
        functools.partial(_conf_kernel, ts=ts, halo=halo, width=width),
        grid=(bsz, nt),
        in_specs=[pl.BlockSpec((ts, w), row), _const_spec((width, w)),
                  _const_spec((1, w)), _const_spec((1, w)), _const_spec((1, w))],
        out_specs=pl.BlockSpec((ts, w), row),
        out_shape=jax.ShapeDtypeStruct((n, w), BF16),
        scratch_shapes=[pltpu.VMEM((ts + halo, w), F32)],
        compiler_params=_cparams(("arbitrary", "arbitrary")),
        name="conformer_conv",
    )(c, dw_w, vec(dw_b), vec(ln_g), vec(ln_b))


def _bucket_tiles(t):
    assert t > REL_MAX_DIST
    n = np.arange(2 * t)
    nf = np.maximum(n, 1).astype(np.float32)
    large = REL_MAX_EXACT + (np.log(nf / np.float32(REL_MAX_EXACT))
                             / np.float32(math.log(REL_MAX_DIST / REL_MAX_EXACT))
                             * np.float32(REL_BUCKETS - REL_MAX_EXACT)).astype(np.int32)
    bucket = np.where(n < REL_MAX_EXACT, n, np.minimum(large, REL_BUCKETS - 1)).astype(np.int32)
    qry = np.arange(t)[None, :]
    key = np.arange(t)[:, None]
    return np.stack([bucket[np.maximum(qry - key, 0)], bucket[t + qry - key]])


def _build_bias(idx_ref, rb_ref, d0_ref, d1_ref):
    t = idx_ref.shape[1]
    causal = (lax.broadcasted_iota(jnp.int32, (t, t), 0) <= lax.broadcasted_iota(jnp.int32, (t, t), 1))
    for which, dst in ((0, d0_ref), (1, d1_ref)):
        idx = idx_ref[which]
        for h in range(N_HEADS):
            far = rb_ref[REL_BUCKETS - 1, h]
            acc = jnp.zeros((t, t), F32)
            for b in range(REL_BUCKETS - 1):
                acc = jnp.where(idx == b, (rb_ref[b, h] - far) * LOG2E, acc)
            dst[h] = jnp.where(causal, acc, NEG) if which == 0 else acc


def _attn_init(m_ref, acc_ref):
    m_ref[...] = jnp.full(m_ref.shape, M_INIT, F32)
    acc_ref[...] = jnp.zeros(acc_ref.shape, F32)


def _softmax_step(s, h, vt_h, m_ref, acc_ref):
    m_old = m_ref[h]
    m_new = jnp.maximum(m_old, jnp.max(s, axis=0, keepdims=True))
    p = jnp.exp2(s - m_new)
    m_ref[h] = m_new
    acc_ref[h] = jnp.exp2(m_old - m_new) * acc_ref[h] + _dot(vt_h, p.astype(BF16))


def _heads_pipelined(logits, vt_slot, m_ref, acc_ref):
    ahead = [logits(h) for h in range(HEADS_AHEAD)]
    for h in range(N_HEADS):
        if h + HEADS_AHEAD < N_HEADS:
            ahead.append(logits(h + HEADS_AHEAD))
        _softmax_step(ahead[h], h, vt_slot(h), m_ref, acc_ref)


def _attn_finish(o_ref, acc_ref):
    parts = []
    for h in range(N_HEADS):
        acc = acc_ref[h]
        parts.append(acc[0:HEAD_DIM] * (1.0 / acc[FEAT0:FEAT0 + 1]))
    o_ref[...] = jnp.concatenate(parts, axis=0).T.astype(BF16)


def _smem_spec():
    return pl.BlockSpec(memory_space=pltpu.SMEM)


def _moba_kernel(rb_ref, q_ref, k_ref, vt_ref, km_ref, idx_ref, o_ref,
                 acc_ref, m_ref, qs_ref, d0_ref, d1_ref, *, n_blocks):
    i = pl.program_id(1)
    t = ATTN_TILE
    nbp = -(-n_blocks // 8) * 8

    @pl.when((pl.program_id(0) == 0) & (i == 0))
    def _():
        _build_bias(idx_ref, rb_ref, d0_ref, d1_ref)

    _attn_init(m_ref, acc_ref)

    blk = lax.broadcasted_iota(jnp.int32, (nbp, t), 0)
    past = blk < i
    for h in range(N_HEADS):
        qh = q_ref[:, _slot(h)]
        km_hi, km_lo = _split_bf16(km_ref[:, _slot(h)])
        gate_t = _dot_nt(km_hi, qh) + _dot_nt(km_lo, qh)
        g = jnp.where(past, gate_t[FEAT0:FEAT0 + nbp, :], NEG)
        rank = jnp.zeros((nbp, t), F32)
        for j in range(n_blocks):
            gj = g[j:j + 1, :]
            beats = (gj > g) | ((gj == g) & (blk > j))
            rank = rank + jnp.where(beats, 1.0, 0.0)
        flag = jnp.where(past & (rank >= MOBA_TOPK), NEG, 0.0)
        flag_t = jnp.concatenate([jnp.zeros((FEAT0, t), F32), flag,
                                  jnp.zeros((SLOT - FEAT0 - nbp, t), F32)], axis=0)
        qs_ref[:, _slot(h)] = (qh.astype(F32) + flag_t.T).astype(BF16)

    def tile(kt, kind):
        r0 = pl.multiple_of(kt * t, t)

        def logits(h):
            s = _dot_nt(k_ref[pl.ds(r0, t), _slot(h)], qs_ref[:, _slot(h)])
            if kind == "diag":
                s = s + d0_ref[h]
            elif kind == "near":
                s = s + d1_ref[h]
            return s

        _heads_pipelined(logits, lambda h: vt_ref[kt, _slot(h), :], m_ref, acc_ref)

    def far_body(kt, carry):
        tile(kt, "far")
        return carry

    lax.fori_loop(0, jnp.maximum(i - 1, 0), far_body, 0)

    @pl.when(i >= 1)
    def _():
        tile(i - 1, "near")

    tile(i, "diag")
    _attn_finish(o_ref, acc_ref)


def _moba(q, k, vt, kmean, rel_bias, *, bsz):
    n = q.shape[0]
    seq = n // bsz
    t = ATTN_TILE
    assert seq % t == 0 and t == MOBA_BLOCK
    nq = seq // t
    assert FEAT0 + nq <= SLOT
    km = kmean.reshape(bsz, nq, N_HEADS, HEAD_DIM)
    km = jnp.pad(km, ((0, 0), (FEAT0, SLOT - FEAT0 - nq), (0, 0), (0, SLOT - HEAD_DIM)))
    km = km.reshape(bsz, SLOT, SLOT_WIDTH)
    r3 = lambda a: a.reshape(bsz, seq, SLOT_WIDTH)
    seq_spec = pl.BlockSpec((None, seq, SLOT_WIDTH), lambda b, i: (b, 0, 0), pipeline_mode=pl.Buffered(1))
    out = pl.pallas_call(
        functools.partial(_moba_kernel, n_blocks=nq),
        grid=(bsz, nq),
        in_specs=[_smem_spec(),
                  pl.BlockSpec((None, t, SLOT_WIDTH), lambda b, i: (b, i, 0)), seq_spec,
                  pl.BlockSpec((nq, SLOT_WIDTH, t), lambda b, i: (b, 0, 0), pipeline_mode=pl.Buffered(1)),
                  pl.BlockSpec((None, SLOT, SLOT_WIDTH), lambda b, i: (b, 0, 0)),
                  _const_spec((2, t, t))],
        out_specs=pl.BlockSpec((None, t, ATTN_WIDTH), lambda b, i: (b, i, 0)),
        out_shape=jax.ShapeDtypeStruct((bsz, seq, ATTN_WIDTH), BF16),
        scratch_shapes=[pltpu.VMEM((N_HEADS, SLOT, t), F32), pltpu.VMEM((N_HEADS, 1, t), F32),
                        pltpu.VMEM((t, SLOT_WIDTH), BF16),
                        pltpu.VMEM((N_HEADS, t, t), F32), pltpu.VMEM((N_HEADS, t, t), F32)],
        compiler_params=_cparams(("arbitrary", "arbitrary")),
        name="moba_attention",
    )(rel_bias, r3(q), r3(k), vt, km, jnp.asarray(_bucket_tiles(t)))
    return out.reshape(n, ATTN_WIDTH)


def _dsa_kernel(rb_ref, q_ref, k_ref, vt_ref, qi_ref, ki_ref, wit_ref, idx_ref, tri_ref, o_ref,
                acc_ref, m_ref, isc_ref, bc_ref, d0_ref, d1_ref, *, n_sel):
    i = pl.program_id(1)
    t = ATTN_TILE
    key = lax.broadcasted_iota(jnp.int32, (t, t), 0)
    qry = lax.broadcasted_iota(jnp.int32, (t, t), 1)
    lane128 = lax.broadcasted_iota(jnp.int32, (1, 128), 1)

    @pl.when((pl.program_id(0) == 0) & (i == 0))
    def _():
        _build_bias(idx_ref, rb_ref, d0_ref, d1_ref)

    _attn_init(m_ref, acc_ref)

    def index_tile(kt, diag):
        r0 = pl.multiple_of(kt * t, t)
        ki2 = ki_ref[pl.ds(r0, t), :]
        acc = jnp.zeros((t, t), F32)
        for pr in range(IDX_HEADS // 2):
            q2 = qi_ref[:, pr * 128:(pr + 1) * 128]
            for half in range(2):
                h = 2 * pr + half
                hm = (lane128 >= half * IDX_DIM) & (lane128 < (half + 1) * IDX_DIM)
                s = _dot_nt(ki2, jnp.where(hm, q2, jnp.zeros((), BF16)))
                acc = acc + jnp.maximum(s, 0.0) * wit_ref[h:h + 1, :]
        if diag:
            acc = jnp.where(key <= qry, acc, -jnp.inf)
        isc_ref[kt] = acc

    def index_body(kt, carry):
        index_tile(kt, False)
        return carry

    lax.fori_loop(0, i, index_body, 0)
    index_tile(i, True)

    def fold8(x, op):
        return op(x.reshape(t // 8, 8, t), axis=0)

    def minmax_body(kt, carry):
        mn, mx = carry
        x = isc_ref[kt]
        mx = jnp.maximum(mx, fold8(x, jnp.max))
        mn = jnp.minimum(mn, fold8(jnp.where(x == -jnp.inf, jnp.inf, x), jnp.min))
        return mn, mx

    mn, mx = lax.fori_loop(0, i + 1, minmax_body,
                           (jnp.full((8, t), jnp.inf, F32), jnp.full((8, t), -jnp.inf, F32)))
    lo0 = jnp.min(mn, axis=0, keepdims=True)
    mx = jnp.max(mx, axis=0, keepdims=True)
    hi0 = mx + jnp.abs(mx) * 1e-3 + 1e-30
    n_valid = (i * t + 1 + lax.broadcasted_iota(jnp.int32, (1, t), 1)).astype(F32)
    want = jnp.minimum(n_valid, float(n_sel))

    def count_ge(thr):
        def body(kt, c):
            return c + fold8(jnp.where(isc_ref[kt] >= thr, 1.0, 0.0), jnp.sum)
        c = lax.fori_loop(0, i + 1, body, jnp.zeros((8, t), F32))
        return jnp.sum(c, axis=0, keepdims=True)

    def bisect_cond(carry):
        it, _, _, c_lo, _ = carry
        return (it < BISECT_ITERS) & (jnp.max(c_lo - want) > 0.0)

    def bisect_body(carry):
        it, lo, hi, c_lo, c_hi = carry
        mid = 0.5 * (lo + hi)
        c_mid = count_ge(mid)
        up = c_mid >= want
        return (it + 1, jnp.where(up, mid, lo), jnp.where(up, hi, mid),
                jnp.where(up, c_mid, c_lo), jnp.where(up, c_hi, c_mid))

    _, lo, hi, c_lo, c_hi = lax.while_loop(bisect_cond, bisect_body,
                                           (0, lo0, hi0, n_valid, jnp.zeros((1, t), F32)))
    need = want - c_hi
    banded = jnp.max(c_lo - want) > 0.0
    bc_ref[...] = jnp.zeros(bc_ref.shape, F32)

    def select_mask(kt):
        x = isc_ref[kt]

        def plain():
            return jnp.where(x >= lo, 1.0, 0.0)

        def with_band():
            band = jnp.where((x >= lo) & (x < hi), 1.0, 0.0)
            before = bc_ref[...] + _dot(tri_ref[...], band.astype(BF16))
            bc_ref[...] = bc_ref[...] + jnp.sum(band, axis=0, keepdims=True)
            return jnp.where((x >= hi) | ((band > 0.0) & (before < need)), 1.0, 0.0)

        return lax.cond(banded, with_band, plain) > 0.0

    def tile(kt, kind):
        r0 = pl.multiple_of(kt * t, t)
        mask = select_mask(kt)

        def logits(h):
            s = _dot_nt(k_ref[pl.ds(r0, t), _slot(h)], q_ref[:, _slot(h)])
            if kind == "diag":
                s = s + d0_ref[h]
            elif kind == "near":
                s = s + d1_ref[h]
            return jnp.where(mask, s, NEG)

        _heads_pipelined(logits, lambda h: vt_ref[kt, _slot(h), :], m_ref, acc_ref)

    def far_body(kt, carry):
        tile(kt, "far")
        return carry

    lax.fori_loop(0, jnp.maximum(i - 1, 0), far_body, 0)

    @pl.when(i >= 1)
    def _():
        tile(i - 1, "near")

    tile(i, "diag")
    _attn_finish(o_ref, acc_ref)


def _dsa(q, k, vt, qi, ki2, wit, rel_bias, *, bsz):
    n = q.shape[0]
    seq = n // bsz
    t = ATTN_TILE
    assert seq % t == 0
    nq = seq // t
    n_sel = min(DSA_TOPK_MAX, seq // 4)
    tri = (jnp.arange(t)[None, :] < jnp.arange(t)[:, None]).astype(BF16)
    r3 = lambda a: a.reshape(bsz, seq, a.shape[-1])
    tile_spec = lambda width: pl.BlockSpec((None, t, width), lambda b, i: (b, i, 0))
    seq_spec = lambda width: pl.BlockSpec((None, seq, width), lambda b, i: (b, 0, 0),
                                          pipeline_mode=pl.Buffered(1))
    out = pl.pallas_call(
        functools.partial(_dsa_kernel, n_sel=n_sel),
        grid=(bsz, nq),
        in_specs=[_smem_spec(), tile_spec(SLOT_WIDTH), seq_spec(SLOT_WIDTH),
                  pl.BlockSpec((nq, SLOT_WIDTH, t), lambda b, i: (b, 0, 0), pipeline_mode=pl.Buffered(1)),
                  tile_spec(ATTN_WIDTH), seq_spec(128),
                  pl.BlockSpec((IDX_HEADS, t), lambda b, i: (0, b * nq + i)),
                  _const_spec((2, t, t)), _const_spec((t, t))],
        out_specs=tile_spec(ATTN_WIDTH),
        out_shape=jax.ShapeDtypeStruct((bsz, seq, ATTN_WIDTH), BF16),
        scratch_shapes=[pltpu.VMEM((N_HEADS, SLOT, t), F32), pltpu.VMEM((N_HEADS, 1, t), F32),
                        pltpu.VMEM((nq, t, t), F32), pltpu.VMEM((1, t), F32),
                        pltpu.VMEM((N_HEADS, t, t), F32), pltpu.VMEM((N_HEADS, t, t), F32)],
        compiler_params=_cparams(("arbitrary", "arbitrary")),
        name="dsa_attention",
    )(rel_bias, r3(q), r3(k), vt, r3(qi), r3(ki2), wit, jnp.asarray(_bucket_tiles(t)), tri)
    return out.reshape(n, ATTN_WIDTH)


def _mixout_kernel(x_ref, ya_ref, yb_ref, wa_ref, wb_ref, o_ref):
    o_ref[...] = x_ref[...] + _dot(ya_ref[...], wa_ref[...]) + _dot(yb_ref[...], wb_ref[...])


def _mixout(x, ya, yb, w_out, *, tm=512):
    n, d = x.shape
    w = ya.shape[1]
    assert n % tm == 0
    row = lambda i: (i, 0)
    return pl.pallas_call(
        _mixout_kernel,
        grid=(n // tm,),
        in_specs=[pl.BlockSpec((tm, d), row), pl.BlockSpec((tm, w), row), pl.BlockSpec((tm, w), row),
                  _const_spec((w, d)), _const_spec((w, d))],
        out_specs=pl.BlockSpec((tm, d), row),
        out_shape=jax.ShapeDtypeStruct((n, d), F32),
        compiler_params=_cparams(("parallel",)),
        name="mixout",
    )(x, ya, yb, w_out[:w].astype(BF16), w_out[w:].astype(BF16))


def kernel(x, rel_bias, ffn1_norm, ffn1_w_gate, ffn1_w_up, ffn1_w_down, mix_norm, ffn2_norm, ffn2_w_gate, ffn2_w_up, ffn2_w_down, ev_w_in, ev_conv_w, ev_conv_b, ev_ra_w, ev_ra_b, ev_ix_w, ev_ix_b, ev_lambda, ev_q_norm, ev_k_norm, ev_w_out, od_w_in, od_dw_w, od_dw_b, od_ln_g, od_ln_b, od_q_norm, od_k_norm, od_w_out):
    bsz, seq, d = x.shape
    depth = ffn1_norm.shape[0]
    h = x.reshape(bsz * seq, d)
    for i in range(depth):
        h = _ffn(h, ffn1_norm[i], ffn1_w_gate[i], ffn1_w_up[i], ffn1_w_down[i])
        j = i // 2
        if i % 2 == 0:
            gate, xr, q, k, vt, kmean = _mixin_even(h, mix_norm[i], ev_w_in[j], ev_q_norm[j], ev_k_norm[j],
                                                    n_blocks=seq // MOBA_BLOCK)
            ya = _lru(gate, xr, ev_conv_w[j], ev_conv_b[j], ev_ra_w[j], ev_ra_b[j],
                      ev_ix_w[j], ev_ix_b[j], ev_lambda[j], bsz=bsz)
            yb = _moba(q, k, vt, kmean, rel_bias, bsz=bsz)
            h = _mixout(h, ya, yb, ev_w_out[j])
        else:
            c, q, k, vt, qi, ki2, wit = _mixin_odd(h, mix_norm[i], od_w_in[j], od_q_norm[j], od_k_norm[j])
            yc = _conf(c, od_dw_w[j], od_dw_b[j], od_ln_g[j], od_ln_b[j], bsz=bsz)
            yd = _dsa(q, k, vt, qi, ki2, wit, rel_bias, bsz=bsz)
            h = _mixout(h, yc, yd, od_w_out[j])
        h = _ffn(h, ffn2_norm[i], ffn2_w_gate[i], ffn2_w_up[i], ffn2_w_down[i])
    return h.reshape(bsz, seq, d)
```

```python
import functools
import math

import numpy as np
import jax
import jax.numpy as jnp
from jax import lax
from jax.experimental import pallas as pl
from jax.experimental.pallas import tpu as pltpu

F32 = jnp.float32
BF16 = jnp.bfloat16

N_HEADS = 8
HEAD_DIM = 64
ATTN_WIDTH = N_HEADS * HEAD_DIM
LRU_C = 8.0
MOBA_BLOCK = 256
MOBA_TOPK = 3
IDX_HEADS = 8
IDX_DIM = 64
DSA_TOPK_MAX = 256
REL_BUCKETS = 32
REL_MAX_EXACT = REL_BUCKETS // 2
REL_MAX_DIST = 128
EPS = 1e-6
NEG = -1e30
M_INIT = -1e29
ATTN_TILE = 256
SUBLANES = 8
CONV_ROWS = 64
SLOT = 128
SLOT_WIDTH = N_HEADS * SLOT
FEAT0 = HEAD_DIM
HEADS_AHEAD = 8
BISECT_ITERS = 32
BISECT_CHECK_EVERY = 4
LOG2E = math.log2(math.e)
Q_SCALE = HEAD_DIM ** -0.5 * LOG2E
VMEM_LIMIT = 56 * 1024 * 1024


def _cparams(sem):
    return pltpu.CompilerParams(dimension_semantics=sem, vmem_limit_bytes=VMEM_LIMIT)


def _dot(a, b):
    return jnp.dot(a, b, preferred_element_type=F32)


def _dot_nt(a, b):
    return lax.dot_general(a, b, (((1,), (1,)), ((), ())), preferred_element_type=F32)


def _split_bf16(x):
    hi = x.astype(BF16)
    lo = (x - hi.astype(F32)).astype(BF16)
    return hi, lo


def _rms_rows(x, g):
    return x * lax.rsqrt(jnp.mean(x * x, axis=-1, keepdims=True) + EPS) * g


def _head_rms(z, gmat, g):
    hi, lo = _split_bf16(z * z)
    ms = _dot(hi, gmat) + _dot(lo, gmat)
    return z * lax.rsqrt(ms + EPS) * g


def _const_spec(shape):
    nd = len(shape)
    return pl.BlockSpec(shape, lambda *_: (0,) * nd, pipeline_mode=pl.Buffered(1))


def _slot(h):
    return slice(h * SLOT, (h + 1) * SLOT)


def _to_slots(x):
    low = lax.broadcasted_iota(jnp.int32, (1, SLOT), 1) < HEAD_DIM
    slots = []
    for p in range(N_HEADS // 2):
        chunk = x[:, p * SLOT:(p + 1) * SLOT]
        slots.append(jnp.where(low, chunk, 0.0))
        slots.append(jnp.where(low, pltpu.roll(chunk, HEAD_DIM, 1), 0.0))
    return slots


def _ffn_kernel(*refs, n_chunks, has_mix):
    if has_mix:
        (x_ref, ya_ref, yb_ref, wo_ref, g_ref, wg_ref, wu_ref, wd_ref, o_ref,
         wg_s, wu_s, wd_s, wo_s) = refs
    else:
        x_ref, g_ref, wg_ref, wu_ref, wd_ref, o_ref, wg_s, wu_s, wd_s = refs
    step = pl.program_id(0)

    @pl.when(step < n_chunks)
    def _():
        wg_s[step] = wg_ref[...].astype(BF16)
        wu_s[step] = wu_ref[...].astype(BF16)
        wd_s[step] = wd_ref[...].astype(BF16)

    if has_mix:
        @pl.when(step == 0)
        def _():
            wo_s[...] = wo_ref[...].astype(BF16)

    @pl.when(step >= n_chunks)
    def _():
        x = x_ref[...]
        if has_mix:
            w = ya_ref.shape[1]
            x = x + _dot(ya_ref[...], wo_s[0:w, :]) + _dot(yb_ref[...], wo_s[w:2 * w, :])
        hn = _rms_rows(x, g_ref[...]).astype(BF16)
        acc = jnp.zeros(x.shape, F32)
        for c in range(n_chunks):
            gt = _dot(hn, wg_s[c])
            ut = _dot(hn, wu_s[c])
            a = (gt * jax.nn.sigmoid(gt) * ut).astype(BF16)
            acc = acc + _dot(a, wd_s[c])
        o_ref[...] = x + 0.5 * acc


def _ffn(x, g, wg, wu, wd, mix=None, *, tm=512, ff_chunk=256):
    n, d = x.shape
    d_ff = wg.shape[1]
    assert n % tm == 0 and d_ff % ff_chunk == 0
    nc = d_ff // ff_chunk
    tile = lambda s: (jnp.maximum(s - nc, 0), 0)
    chunk_col = lambda s: (0, jnp.minimum(s, nc - 1))
    chunk_row = lambda s: (jnp.minimum(s, nc - 1), 0)
    in_specs = [pl.BlockSpec((tm, d), tile)]
    args = [x]
    scratch = [pltpu.VMEM((nc, d, ff_chunk), BF16), pltpu.VMEM((nc, d, ff_chunk), BF16),
               pltpu.VMEM((nc, ff_chunk, d), BF16)]
    if mix is not None:
        ya, yb, w_out = mix
        w = ya.shape[1]
        assert w_out.shape == (2 * w, d)
        in_specs += [pl.BlockSpec((tm, w), tile), pl.BlockSpec((tm, w), tile), _const_spec((2 * w, d))]
        args += [ya, yb, w_out]
        scratch.append(pltpu.VMEM((2 * w, d), BF16))
    in_specs += [_const_spec((1, d)), pl.BlockSpec((d, ff_chunk), chunk_col),
                 pl.BlockSpec((d, ff_chunk), chunk_col), pl.BlockSpec((ff_chunk, d), chunk_row)]
    args += [g.reshape(1, d), wg, wu, wd]
    return pl.pallas_call(
        functools.partial(_ffn_kernel, n_chunks=nc, has_mix=mix is not None),
        grid=(nc + n // tm,),
        in_specs=in_specs,
        out_specs=pl.BlockSpec((tm, d), tile),
        out_shape=jax.ShapeDtypeStruct((n, d), F32),
        scratch_shapes=scratch,
        compiler_params=_cparams(("arbitrary",)),
        name="ffn_mix" if mix is not None else "ffn",
    )(*args)


def _store_qkv_slots(hn, w_ref, gmat_ref, qn_ref, kn_ref, q_o, k_o, v_o, col0, k_feat):
    w = ATTN_WIDTH
    gmat = gmat_ref[...]
    q = _head_rms(_dot(hn, w_ref[:, col0:col0 + w]), gmat, qn_ref[...]) * Q_SCALE
    k = _head_rms(_dot(hn, w_ref[:, col0 + w:col0 + 2 * w]), gmat, kn_ref[...])
    v = _dot(hn, w_ref[:, col0 + 2 * w:col0 + 3 * w])
    lane = lax.broadcasted_iota(jnp.int32, (1, SLOT), 1)
    k_one = k_feat(lane)
    for h, (qs, ks, vs) in enumerate(zip(_to_slots(q), _to_slots(k), _to_slots(v))):
        q_o[:, _slot(h)] = qs.astype(BF16)
        if k_one is not None:
            ks = jnp.where(k_one, 1.0, ks)
        k_o[:, _slot(h)] = ks.astype(BF16)
        vs = jnp.where(lane == FEAT0, 1.0, vs)
        for r in range(vs.shape[0] // ATTN_TILE):
            v_o[r, _slot(h), :] = vs[r * ATTN_TILE:(r + 1) * ATTN_TILE].T.astype(BF16)
    return k


def _mixin_even_kernel(x_ref, g_ref, w_ref, gmat_ref, qn_ref, kn_ref,
                       gate_o, xr_o, q_o, k_o, v_o, km_o, *, n_blocks):
    tm = x_ref.shape[0]
    hn = _rms_rows(x_ref[...], g_ref[...]).astype(BF16)
    w = ATTN_WIDTH
    gate_o[...] = _dot(hn, w_ref[:, 0:w])
    xr_o[...] = _dot(hn, w_ref[:, w:2 * w])
    blocks_per_tile = tm // MOBA_BLOCK
    row_block = lax.broadcasted_iota(jnp.int32, (tm, 1), 0) // MOBA_BLOCK
    block = (pl.program_id(0) * blocks_per_tile + row_block) % n_blocks
    k = _store_qkv_slots(hn, w_ref, gmat_ref, qn_ref, kn_ref, q_o, k_o, v_o, 2 * w,
                         lambda lane: lane == FEAT0 + block)
    for r in range(blocks_per_tile):
        km_o[r] = jnp.mean(k[r * MOBA_BLOCK:(r + 1) * MOBA_BLOCK], axis=0, keepdims=True)


def _vt_spec(tm):
    return pl.BlockSpec((tm // ATTN_TILE, SLOT_WIDTH, ATTN_TILE), lambda i: (i, 0, 0))


def _vt_shape(n):
    return jax.ShapeDtypeStruct((n // ATTN_TILE, SLOT_WIDTH, ATTN_TILE), BF16)


def _group_mean_matrix():
    idx = jnp.arange(ATTN_WIDTH) // HEAD_DIM
    return (idx[:, None] == idx[None, :]).astype(BF16) * (1.0 / HEAD_DIM)


def _mixin_even(x, g, w_in, q_norm, k_norm, *, n_blocks, tm=512):
    n, d = x.shape
    w = ATTN_WIDTH
    assert n % tm == 0 and tm % MOBA_BLOCK == 0 and tm % ATTN_TILE == 0 and w_in.shape[1] == 5 * w
    assert FEAT0 + n_blocks <= SLOT
    row = lambda i: (i, 0)
    blk = tm // MOBA_BLOCK
    slot_shape = jax.ShapeDtypeStruct((n, SLOT_WIDTH), BF16)
    return pl.pallas_call(
        functools.partial(_mixin_even_kernel, n_blocks=n_blocks),
        grid=(n // tm,),
        in_specs=[pl.BlockSpec((tm, d), row), _const_spec((1, d)), _const_spec((d, 5 * w)),
                  _const_spec((w, w)), _const_spec((1, w)), _const_spec((1, w))],
        out_specs=[pl.BlockSpec((tm, w), row)] * 2 + [pl.BlockSpec((tm, SLOT_WIDTH), row)] * 2
        + [_vt_spec(tm), pl.BlockSpec((blk, 1, w), lambda i: (i, 0, 0))],
        out_shape=[jax.ShapeDtypeStruct((n, w), F32), jax.ShapeDtypeStruct((n, w), F32),
                   slot_shape, slot_shape, _vt_shape(n),
                   jax.ShapeDtypeStruct((n // MOBA_BLOCK, 1, w), F32)],
        compiler_params=_cparams(("parallel",)),
        name="mixin_even",
    )(x, g.reshape(1, d), w_in.astype(BF16), _group_mean_matrix(),
      jnp.tile(q_norm, N_HEADS).reshape(1, w), jnp.tile(k_norm, N_HEADS).reshape(1, w))


def _mixin_odd_kernel(x_ref, g_ref, w_ref, ws_ref, gmat_ref, qn_ref, kn_ref,
                      c_o, q_o, k_o, v_o, qi_o, ki_o, wi_o):
    hn = _rms_rows(x_ref[...], g_ref[...]).astype(BF16)
    w = ATTN_WIDTH
    ca = _dot(hn, w_ref[:, 0:w])
    cg = _dot(hn, w_ref[:, w:2 * w])
    c_o[...] = ca * jax.nn.sigmoid(cg)
    _store_qkv_slots(hn, w_ref, gmat_ref, qn_ref, kn_ref, q_o, k_o, v_o, 2 * w, lambda lane: None)
    qi_o[...] = _dot(hn, w_ref[:, 5 * w:6 * w]).astype(BF16)
    small = _dot(hn, ws_ref[...])
    ki_o[...] = small[:, 0:128].astype(BF16)
    wi = small[:, 128:256] * (IDX_DIM ** -0.5 * IDX_HEADS ** -0.5)
    wi_o[...] = wi.T[0:IDX_HEADS, :]


def _mixin_odd(x, g, w_in, q_norm, k_norm, *, tm=512):
    n, d = x.shape
    w = ATTN_WIDTH
    assert n % tm == 0 and tm % ATTN_TILE == 0 and w_in.shape[1] == 6 * w + IDX_DIM + IDX_HEADS
    w_main = w_in[:, :6 * w].astype(BF16)
    w_ki = w_in[:, 6 * w:6 * w + IDX_DIM]
    w_wi = jnp.pad(w_in[:, 6 * w + IDX_DIM:], ((0, 0), (0, 128 - IDX_HEADS)))
    w_small = jnp.concatenate([w_ki, w_ki, w_wi], axis=1).astype(BF16)
    row = lambda i: (i, 0)
    slot_shape = jax.ShapeDtypeStruct((n, SLOT_WIDTH), BF16)
    return pl.pallas_call(
        _mixin_odd_kernel,
        grid=(n // tm,),
        in_specs=[pl.BlockSpec((tm, d), row), _const_spec((1, d)), _const_spec((d, 6 * w)),
                  _const_spec((d, 256)), _const_spec((w, w)), _const_spec((1, w)), _const_spec((1, w))],
        out_specs=[pl.BlockSpec((tm, w), row)] + [pl.BlockSpec((tm, SLOT_WIDTH), row)] * 2
        + [_vt_spec(tm), pl.BlockSpec((tm, w), row), pl.BlockSpec((tm, 128), row),
           pl.BlockSpec((IDX_HEADS, tm), lambda i: (0, i))],
        out_shape=[jax.ShapeDtypeStruct((n, w), F32), slot_shape, slot_shape, _vt_shape(n),
                   jax.ShapeDtypeStruct((n, w), BF16),
                   jax.ShapeDtypeStruct((n, 128), BF16), jax.ShapeDtypeStruct((IDX_HEADS, n), F32)],
        compiler_params=_cparams(("parallel",)),
        name="mixin_odd",
    )(x, g.reshape(1, d), w_main, w_small, _group_mean_matrix(),
      jnp.tile(q_norm, N_HEADS).reshape(1, w), jnp.tile(k_norm, N_HEADS).reshape(1, w))


def _lru_kernel(gate_ref, xr_ref, cw_ref, cb_ref, wa_ref, ba_ref, wx_ref, bx_ref, sp_ref,
                o_ref, xbuf, a_s, u_s, h_s, hc, *, ts):
    j = pl.program_id(1)

    @pl.when(j == 0)
    def _():
        xbuf[0:8, :] = jnp.zeros((8, xbuf.shape[1]), F32)
        hc[...] = jnp.zeros(hc.shape, F32)

    xbuf[8:8 + ts, :] = xr_ref[...]
    xc = cb_ref[...] + cw_ref[0:1, :] * xbuf[5:5 + ts, :]
    for k in range(1, 4):
        xc = xc + cw_ref[k:k + 1, :] * xbuf[5 + k:5 + k + ts, :]
    xbuf[0:8, :] = xbuf[ts:ts + 8, :]

    xcb = xc.astype(BF16)
    r = jax.nn.sigmoid(_dot(xcb, wa_ref[...]) + ba_ref[...])
    ig = jax.nn.sigmoid(_dot(xcb, wx_ref[...]) + bx_ref[...])
    log_a = -LRU_C * r * sp_ref[...]
    a = jnp.exp(log_a)
    a_s[...] = a
    u_s[...] = jnp.sqrt(-jnp.tanh(log_a) * (a * a + 1.0)) * (ig * xc)

    row = lax.broadcasted_iota(jnp.int32, (8, a_s.shape[1]), 0)

    def body(g, carry):
        r0 = pl.multiple_of(g * 8, 8)
        a = a_s[pl.ds(r0, 8), :]
        u = u_s[pl.ds(r0, 8), :]
        for s in (1, 2, 4):
            ok = row >= s
            a_sh = jnp.where(ok, pltpu.roll(a, s, 0), 1.0)
            u_sh = jnp.where(ok, pltpu.roll(u, s, 0), 0.0)
            u = a * u_sh + u
            a = a * a_sh
        h = a * carry + u
        h_s[pl.ds(r0, 8), :] = h
        return h[7:8, :]

    hc[...] = lax.fori_loop(0, ts // 8, body, hc[...], unroll=4)
    o_ref[...] = (h_s[...] * jax.nn.gelu(gate_ref[...])).astype(BF16)


def _block_diag(wb):
    nb, bs, _ = wb.shape
    eye = jnp.eye(nb, dtype=wb.dtype)
    return (eye[:, None, :, None] * wb[:, :, None, :]).reshape(nb * bs, nb * bs)


def _lru(gate, xr, conv_w, conv_b, ra_w, ra_b, ix_w, ix_b, lam, *, bsz, ts=256):
    n, w = xr.shape
    seq = n // bsz
    assert seq % ts == 0
    nt = seq // ts
    row = lambda b, j: (b * nt + j, 0)
    vec = lambda v: v.reshape(1, w).astype(F32)
    return pl.pallas_call(
        functools.partial(_lru_kernel, ts=ts),
        grid=(bsz, nt),
        in_specs=[pl.BlockSpec((ts, w), row), pl.BlockSpec((ts, w), row),
                  _const_spec((conv_w.shape[0], w)), _const_spec((1, w)),
                  _const_spec((w, w)), _const_spec((1, w)), _const_spec((w, w)), _const_spec((1, w)),
                  _const_spec((1, w))],
        out_specs=pl.BlockSpec((ts, w), row),
        out_shape=jax.ShapeDtypeStruct((n, w), BF16),
        scratch_shapes=[pltpu.VMEM((ts + 8, w), F32), pltpu.VMEM((ts, w), F32),
                        pltpu.VMEM((ts, w), F32), pltpu.VMEM((ts, w), F32), pltpu.VMEM((1, w), F32)],
        compiler_params=_cparams(("arbitrary", "arbitrary")),
        name="rg_lru",
    )(gate, xr, conv_w, vec(conv_b), _block_diag(ra_w).astype(BF16), vec(ra_b),
      _block_diag(ix_w).astype(BF16), vec(ix_b), vec(jax.nn.softplus(-lam)))


def _conf_kernel(c_ref, w_ref, b_ref, g_ref, beta_ref, o_ref, cbuf, sh_ref, *, ts, halo, width):
    j = pl.program_id(1)

    @pl.when(j == 0)
    def _():
        cbuf[0:halo, :] = jnp.zeros((halo, cbuf.shape[1]), F32)

    cbuf[halo:halo + ts, :] = c_ref[...]
    base = halo - (width - 1)
    span = sh_ref.shape[1]
    for r in range(1, SUBLANES):
        sh_ref[r - 1] = cbuf[r:r + span, :]

    for c0 in range(0, ts, CONV_ROWS):
        y = jnp.broadcast_to(b_ref[...], (CONV_ROWS, cbuf.shape[1]))
        for k in range(width):
            r = (base + k) % SUBLANES
            u0 = c0 + base + k - r
            win = cbuf[u0:u0 + CONV_ROWS, :] if r == 0 else sh_ref[r - 1, u0:u0 + CONV_ROWS, :]
            y = y + w_ref[k:k + 1, :] * win
        mu = jnp.mean(y, axis=-1, keepdims=True)
        yc = y - mu
        var = jnp.mean(yc * yc, axis=-1, keepdims=True)
        z = yc * lax.rsqrt(var + EPS) * g_ref[...] + beta_ref[...]
        o_ref[c0:c0 + CONV_ROWS, :] = (z * jax.nn.sigmoid(z)).astype(BF16)

    cbuf[0:halo, :] = cbuf[ts:ts + halo, :]


def _conf(c, dw_w, dw_b, ln_g, ln_b, *, bsz, ts=256, halo=32):
    n, w = c.shape
    seq = n // bsz
    width = dw_w.shape[0]
    assert seq % ts == 0 and width - 1 <= halo <= ts and halo % SUBLANES == 0 and ts % CONV_ROWS == 0
    nt = seq // ts
    row = lambda b, j: (b * nt + j, 0)
    vec = lambda v: v.reshape(1, w).astype(F32)
    return pl.pallas_call(
        functools.partial(_conf_kernel, ts=ts, halo=halo, width=width),
        grid=(bsz, nt),
        in_specs=[pl.BlockSpec((ts, w), row), _const_spec((width, w)),
                  _const_spec((1, w)), _const_spec((1, w)), _const_spec((1, w))],
        out_specs=pl.BlockSpec((ts, w), row),
        out_shape=jax.ShapeDtypeStruct((n, w), BF16),
        scratch_shapes=[pltpu.VMEM((ts + halo, w), F32),
                        pltpu.VMEM((SUBLANES - 1, ts + halo - SUBLANES, w), F32)],
        compiler_params=_cparams(("arbitrary", "arbitrary")),
        name="conformer_conv",
    )(c, dw_w, vec(dw_b), vec(ln_g), vec(ln_b))


def _bucket_tiles(t):
    assert t > REL_MAX_DIST
    n = np.arange(2 * t)
    nf = np.maximum(n, 1).astype(np.float32)
    large = REL_MAX_EXACT + (np.log(nf / np.float32(REL_MAX_EXACT))
                             / np.float32(math.log(REL_MAX_DIST / REL_MAX_EXACT))
                             * np.float32(REL_BUCKETS - REL_MAX_EXACT)).astype(np.int32)
    bucket = np.where(n < REL_MAX_EXACT, n, np.minimum(large, REL_BUCKETS - 1)).astype(np.int32)
    qry = np.arange(t)[None, :]
    key = np.arange(t)[:, None]
    return np.stack([bucket[np.maximum(qry - key, 0)], bucket[t + qry - key]])


def _build_bias(idx_ref, rb_ref, d0_ref, d1_ref):
    t = idx_ref.shape[1]
    causal = (lax.broadcasted_iota(jnp.int32, (t, t), 0) <= lax.broadcasted_iota(jnp.int32, (t, t), 1))
    for which, dst in ((0, d0_ref), (1, d1_ref)):
        idx = idx_ref[which]
        for h in range(N_HEADS):
            far = rb_ref[REL_BUCKETS - 1, h]
            acc = jnp.zeros((t, t), F32)
            for b in range(REL_BUCKETS - 1):
                acc = jnp.where(idx == b, (rb_ref[b, h] - far) * LOG2E, acc)
            dst[h] = jnp.where(causal, acc, NEG) if which == 0 else acc


def _attn_init(m_ref, acc_ref):
    m_ref[...] = jnp.full(m_ref.shape, M_INIT, F32)
    acc_ref[...] = jnp.zeros(acc_ref.shape, F32)


def _softmax_step(s, h, vt_h, m_ref, acc_ref):
    m_old = m_ref[h]
    m_new = jnp.maximum(m_old, jnp.max(s, axis=0, keepdims=True))
    p = jnp.exp2(s - m_new)
    m_ref[h] = m_new
    acc_ref[h] = jnp.exp2(m_old - m_new) * acc_ref[h] + _dot(vt_h, p.astype(BF16))


def _heads_pipelined(logits, vt_slot, m_ref, acc_ref):
    ahead = [logits(h) for h in range(HEADS_AHEAD)]
    for h in range(N_HEADS):
        if h + HEADS_AHEAD < N_HEADS:
            ahead.append(logits(h + HEADS_AHEAD))
        _softmax_step(ahead[h], h, vt_slot(h), m_ref, acc_ref)


def _attn_finish(o_ref, acc_ref):
    parts = []
    for h in range(N_HEADS):
        acc = acc_ref[h]
        parts.append(acc[0:HEAD_DIM] * (1.0 / acc[FEAT0:FEAT0 + 1]))
    o_ref[...] = jnp.concatenate(parts, axis=0).T.astype(BF16)


def _smem_spec():
    return pl.BlockSpec(memory_space=pltpu.SMEM)


def _moba_kernel(rb_ref, q_ref, k_ref, vt_ref, km_ref, idx_ref, o_ref,
                 acc_ref, m_ref, qs_ref, d0_ref, d1_ref, *, n_blocks):
    i = pl.program_id(1)
    t = ATTN_TILE
    nbp = -(-n_blocks // 8) * 8

    @pl.when((pl.program_id(0) == 0) & (i == 0))
    def _():
        _build_bias(idx_ref, rb_ref, d0_ref, d1_ref)

    _attn_init(m_ref, acc_ref)

    blk = lax.broadcasted_iota(jnp.int32, (nbp, t), 0)
    past = blk < i
    for h in range(N_HEADS):
        qh = q_ref[:, _slot(h)]
        km_hi, km_lo = _split_bf16(km_ref[:, _slot(h)])
        gate_t = _dot_nt(km_hi, qh) + _dot_nt(km_lo, qh)
        g = jnp.where(past, gate_t[FEAT0:FEAT0 + nbp, :], NEG)
        rank = jnp.zeros((nbp, t), F32)
        for j in range(n_blocks):
            gj = g[j:j + 1, :]
            beats = (gj > g) | ((gj == g) & (blk > j))
            rank = rank + jnp.where(beats, 1.0, 0.0)
        flag = jnp.where(past & (rank >= MOBA_TOPK), NEG, 0.0)
        flag_t = jnp.concatenate([jnp.zeros((FEAT0, t), F32), flag,
                                  jnp.zeros((SLOT - FEAT0 - nbp, t), F32)], axis=0)
        qs_ref[:, _slot(h)] = (qh.astype(F32) + flag_t.T).astype(BF16)

    def tile(kt, kind):
        r0 = pl.multiple_of(kt * t, t)

        def logits(h):
            s = _dot_nt(k_ref[pl.ds(r0, t), _slot(h)], qs_ref[:, _slot(h)])
            if kind == "diag":
                s = s + d0_ref[h]
            elif kind == "near":
                s = s + d1_ref[h]
            return s

        _heads_pipelined(logits, lambda h: vt_ref[kt, _slot(h), :], m_ref, acc_ref)

    def far_body(kt, carry):
        tile(kt, "far")
        return carry

    lax.fori_loop(0, jnp.maximum(i - 1, 0), far_body, 0)

    @pl.when(i >= 1)
    def _():
        tile(i - 1, "near")

    tile(i, "diag")
    _attn_finish(o_ref, acc_ref)


def _moba(q, k, vt, kmean, rel_bias, *, bsz):
    n = q.shape[0]
    seq = n // bsz
    t = ATTN_TILE
    assert seq % t == 0 and t == MOBA_BLOCK
    nq = seq // t
    assert FEAT0 + nq <= SLOT
    km = kmean.reshape(bsz, nq, N_HEADS, HEAD_DIM)
    km = jnp.pad(km, ((0, 0), (FEAT0, SLOT - FEAT0 - nq), (0, 0), (0, SLOT - HEAD_DIM)))
    km = km.reshape(bsz, SLOT, SLOT_WIDTH)
    r3 = lambda a: a.reshape(bsz, seq, SLOT_WIDTH)
    seq_spec = pl.BlockSpec((None, seq, SLOT_WIDTH), lambda b, i: (b, 0, 0), pipeline_mode=pl.Buffered(1))
    out = pl.pallas_call(
        functools.partial(_moba_kernel, n_blocks=nq),
        grid=(bsz, nq),
        in_specs=[_smem_spec(),
                  pl.BlockSpec((None, t, SLOT_WIDTH), lambda b, i: (b, i, 0)), seq_spec,
                  pl.BlockSpec((nq, SLOT_WIDTH, t), lambda b, i: (b, 0, 0), pipeline_mode=pl.Buffered(1)),
                  pl.BlockSpec((None, SLOT, SLOT_WIDTH), lambda b, i: (b, 0, 0)),
                  _const_spec((2, t, t))],
        out_specs=pl.BlockSpec((None, t, ATTN_WIDTH), lambda b, i: (b, i, 0)),
        out_shape=jax.ShapeDtypeStruct((bsz, seq, ATTN_WIDTH), BF16),
        scratch_shapes=[pltpu.VMEM((N_HEADS, SLOT, t), F32), pltpu.VMEM((N_HEADS, 1, t), F32),
                        pltpu.VMEM((t, SLOT_WIDTH), BF16),
                        pltpu.VMEM((N_HEADS, t, t), F32), pltpu.VMEM((N_HEADS, t, t), F32)],
        compiler_params=_cparams(("arbitrary", "arbitrary")),
        name="moba_attention",
    )(rel_bias, r3(q), r3(k), vt, km, jnp.asarray(_bucket_tiles(t)))
    return out.reshape(n, ATTN_WIDTH)


def _dsa_kernel(rb_ref, q_ref, k_ref, vt_ref, qi_ref, ki_ref, wit_ref, idx_ref, tri_ref, o_ref,
                acc_ref, m_ref, isc_ref, bc_ref, d0_ref, d1_ref, *, n_sel):
    i = pl.program_id(1)
    t = ATTN_TILE
    key = lax.broadcasted_iota(jnp.int32, (t, t), 0)
    qry = lax.broadcasted_iota(jnp.int32, (t, t), 1)
    lane128 = lax.broadcasted_iota(jnp.int32, (1, 128), 1)

    @pl.when((pl.program_id(0) == 0) & (i == 0))
    def _():
        _build_bias(idx_ref, rb_ref, d0_ref, d1_ref)

    _attn_init(m_ref, acc_ref)

    def index_tile(kt, diag):
        r0 = pl.multiple_of(kt * t, t)
        ki2 = ki_ref[pl.ds(r0, t), :]
        acc = jnp.zeros((t, t), F32)
        for pr in range(IDX_HEADS // 2):
            q2 = qi_ref[:, pr * 128:(pr + 1) * 128]
            for half in range(2):
                h = 2 * pr + half
                hm = (lane128 >= half * IDX_DIM) & (lane128 < (half + 1) * IDX_DIM)
                s = _dot_nt(ki2, jnp.where(hm, q2, jnp.zeros((), BF16)))
                acc = acc + jnp.maximum(s, 0.0) * wit_ref[h:h + 1, :]
        if diag:
            acc = jnp.where(key <= qry, acc, -jnp.inf)
        isc_ref[kt] = acc

    def index_body(kt, carry):
        index_tile(kt, False)
        return carry

    lax.fori_loop(0, i, index_body, 0)
    index_tile(i, True)

    def fold8(x, op):
        return op(x.reshape(t // 8, 8, t), axis=0)

    def minmax_body(kt, carry):
        mn, mx = carry
        x = isc_ref[kt]
        mx = jnp.maximum(mx, fold8(x, jnp.max))
        mn = jnp.minimum(mn, fold8(jnp.where(x == -jnp.inf, jnp.inf, x), jnp.min))
        return mn, mx

    mn, mx = lax.fori_loop(0, i + 1, minmax_body,
                           (jnp.full((8, t), jnp.inf, F32), jnp.full((8, t), -jnp.inf, F32)))
    lo0 = jnp.min(mn, axis=0, keepdims=True)
    mx = jnp.max(mx, axis=0, keepdims=True)
    hi0 = mx + jnp.abs(mx) * 1e-3 + 1e-30
    n_valid = (i * t + 1 + lax.broadcasted_iota(jnp.int32, (1, t), 1)).astype(F32)
    want = jnp.minimum(n_valid, float(n_sel))

    def count_ge(thr):
        def one(kt, c):
            return c + fold8(jnp.where(isc_ref[kt] >= thr, 1.0, 0.0), jnp.sum)

        def pair(j, c):
            return one(2 * j + 1, one(2 * j, c))

        c = lax.fori_loop(0, (i + 1) // 2, pair, jnp.zeros((8, t), F32))
        c = lax.cond((i + 1) % 2 == 1, lambda c: one(i, c), lambda c: c, c)
        return jnp.sum(c, axis=0, keepdims=True)

    def bisect_step(_, carry):
        lo, hi, c_lo, c_hi = carry
        mid = 0.5 * (lo + hi)
        c_mid = count_ge(mid)
        up = c_mid >= want
        return (jnp.where(up, mid, lo), jnp.where(up, hi, mid),
                jnp.where(up, c_mid, c_lo), jnp.where(up, c_hi, c_mid))

    def bisect_cond(carry):
        it, _, _, c_lo, _ = carry
        return (it < BISECT_ITERS) & (jnp.max(c_lo - want) > 0.0)

    def bisect_body(carry):
        return (carry[0] + BISECT_CHECK_EVERY,) + lax.fori_loop(0, BISECT_CHECK_EVERY, bisect_step, carry[1:])

    _, lo, hi, c_lo, c_hi = lax.while_loop(bisect_cond, bisect_body,
                                           (0, lo0, hi0, n_valid, jnp.zeros((1, t), F32)))
    need = want - c_hi
    banded = jnp.max(c_lo - want) > 0.0
    bc_ref[...] = jnp.zeros(bc_ref.shape, F32)

    def select_mask(kt):
        x = isc_ref[kt]

        def plain():
            return jnp.where(x >= lo, 0.0, NEG)

        def with_band():
            band = jnp.where((x >= lo) & (x < hi), 1.0, 0.0)
            before = bc_ref[...] + _dot(tri_ref[...], band.astype(BF16))
            bc_ref[...] = bc_ref[...] + jnp.sum(band, axis=0, keepdims=True)
            return jnp.where((x >= hi) | ((band > 0.0) & (before < need)), 0.0, NEG)

        return lax.cond(banded, with_band, plain)

    def tile(kt, kind):
        r0 = pl.multiple_of(kt * t, t)
        mask_add = select_mask(kt)

        def logits(h):
            s = _dot_nt(k_ref[pl.ds(r0, t), _slot(h)], q_ref[:, _slot(h)])
            if kind == "diag":
                s = s + d0_ref[h]
            elif kind == "near":
                s = s + d1_ref[h]
            return s + mask_add

        _heads_pipelined(logits, lambda h: vt_ref[kt, _slot(h), :], m_ref, acc_ref)

    def far_body(kt, carry):
        tile(kt, "far")
        return carry

    lax.fori_loop(0, jnp.maximum(i - 1, 0), far_body, 0)

    @pl.when(i >= 1)
    def _():
        tile(i - 1, "near")

    tile(i, "diag")
    _attn_finish(o_ref, acc_ref)


def _dsa(q, k, vt, qi, ki2, wit, rel_bias, *, bsz):
    n = q.shape[0]
    seq = n // bsz
    t = ATTN_TILE
    assert seq % t == 0
    nq = seq // t
    n_sel = min(DSA_TOPK_MAX, seq // 4)
    tri = (jnp.arange(t)[None, :] < jnp.arange(t)[:, None]).astype(BF16)
    r3 = lambda a: a.reshape(bsz, seq, a.shape[-1])
    tile_spec = lambda width: pl.BlockSpec((None, t, width), lambda b, i: (b, i, 0))
    seq_spec = lambda width: pl.BlockSpec((None, seq, width), lambda b, i: (b, 0, 0),
                                          pipeline_mode=pl.Buffered(1))
    out = pl.pallas_call(
        functools.partial(_dsa_kernel, n_sel=n_sel),
        grid=(bsz, nq),
        in_specs=[_smem_spec(), tile_spec(SLOT_WIDTH), seq_spec(SLOT_WIDTH),
                  pl.BlockSpec((nq, SLOT_WIDTH, t), lambda b, i: (b, 0, 0), pipeline_mode=pl.Buffered(1)),
                  tile_spec(ATTN_WIDTH), seq_spec(128),
                  pl.BlockSpec((IDX_HEADS, t), lambda b, i: (0, b * nq + i)),
                  _const_spec((2, t, t)), _const_spec((t, t))],
        out_specs=tile_spec(ATTN_WIDTH),
        out_shape=jax.ShapeDtypeStruct((bsz, seq, ATTN_WIDTH), BF16),
        scratch_shapes=[pltpu.VMEM((N_HEADS, SLOT, t), F32), pltpu.VMEM((N_HEADS, 1, t), F32),
                        pltpu.VMEM((nq, t, t), F32), pltpu.VMEM((1, t), F32),
                        pltpu.VMEM((N_HEADS, t, t), F32), pltpu.VMEM((N_HEADS, t, t), F32)],
        compiler_params=_cparams(("arbitrary", "arbitrary")),
        name="dsa_attention",
    )(rel_bias, r3(q), r3(k), vt, r3(qi), r3(ki2), wit, jnp.asarray(_bucket_tiles(t)), tri)
    return out.reshape(n, ATTN_WIDTH)


def kernel(x, rel_bias, ffn1_norm, ffn1_w_gate, ffn1_w_up, ffn1_w_down, mix_norm, ffn2_norm, ffn2_w_gate, ffn2_w_up, ffn2_w_down, ev_w_in, ev_conv_w, ev_conv_b, ev_ra_w, ev_ra_b, ev_ix_w, ev_ix_b, ev_lambda, ev_q_norm, ev_k_norm, ev_w_out, od_w_in, od_dw_w, od_dw_b, od_ln_g, od_ln_b, od_q_norm, od_k_norm, od_w_out):
    bsz, seq, d = x.shape
    depth = ffn1_norm.shape[0]
    h = x.reshape(bsz * seq, d)
    for i in range(depth):
        h = _ffn(h, ffn1_norm[i], ffn1_w_gate[i], ffn1_w_up[i], ffn1_w_down[i])
        j = i // 2
        if i % 2 == 0:
            gate, xr, q, k, vt, kmean = _mixin_even(h, mix_norm[i], ev_w_in[j], ev_q_norm[j], ev_k_norm[j],
                                                    n_blocks=seq // MOBA_BLOCK)
            ya = _lru(gate, xr, ev_conv_w[j], ev_conv_b[j], ev_ra_w[j], ev_ra_b[j],
                      ev_ix_w[j], ev_ix_b[j], ev_lambda[j], bsz=bsz)
            yb = _moba(q, k, vt, kmean, rel_bias, bsz=bsz)
            mix = (ya, yb, ev_w_out[j])
        else:
            c, q, k, vt, qi, ki2, wit = _mixin_odd(h, mix_norm[i], od_w_in[j], od_q_norm[j], od_k_norm[j])
            yc = _conf(c, od_dw_w[j], od_dw_b[j], od_ln_g[j], od_ln_b[j], bsz=bsz)
            yd = _dsa(q, k, vt, qi, ki2, wit, rel_bias, bsz=bsz)
            mix = (yc, yd, od_w_out[j])
        h = _ffn(h, ffn2_norm[i], ffn2_w_gate[i], ffn2_w_up[i], ffn2_w_down[i], mix)
    return h.reshape(bsz, seq, d)
```

```python
import functools
import math

import numpy as np
import jax
import jax.numpy as jnp
from jax import lax
from jax.experimental import pallas as pl
from jax.experimental.pallas import tpu as pltpu

F32 = jnp.float32
BF16 = jnp.bfloat16

N_HEADS = 8
HEAD_DIM = 64
ATTN_WIDTH = N_HEADS * HEAD_DIM
LRU_C = 8.0
MOBA_BLOCK = 256
MOBA_TOPK = 3
IDX_HEADS = 8
IDX_DIM = 64
DSA_TOPK_MAX = 256
REL_BUCKETS = 32
REL_MAX_EXACT = REL_BUCKETS // 2
REL_MAX_DIST = 128
EPS = 1e-6
NEG = -1e30
M_INIT = -1e29
ATTN_TILE = 256
SUBLANES = 8
CONV_ROWS = 64
SLOT = 128
SLOT_WIDTH = N_HEADS * SLOT
FEAT0 = HEAD_DIM
HEADS_AHEAD = 4
BISECT_ITERS = 32
BISECT_CHECK_EVERY = 4
LOG2E = math.log2(math.e)
Q_SCALE = HEAD_DIM ** -0.5 * LOG2E
VMEM_LIMIT = 56 * 1024 * 1024


def _cparams(sem):
    return pltpu.CompilerParams(dimension_semantics=sem, vmem_limit_bytes=VMEM_LIMIT)


def _dot(a, b):
    return jnp.dot(a, b, preferred_element_type=F32)


def _dot_nt(a, b):
    return lax.dot_general(a, b, (((1,), (1,)), ((), ())), preferred_element_type=F32)


def _split_bf16(x):
    hi = x.astype(BF16)
    lo = (x - hi.astype(F32)).astype(BF16)
    return hi, lo


def _rms_rows(x, g):
    return x * lax.rsqrt(jnp.mean(x * x, axis=-1, keepdims=True) + EPS) * g


def _head_rms(z, gmat, g):
    hi, lo = _split_bf16(z * z)
    ms = _dot(hi, gmat) + _dot(lo, gmat)
    return z * lax.rsqrt(ms + EPS) * g


def _const_spec(shape):
    nd = len(shape)
    return pl.BlockSpec(shape, lambda *_: (0,) * nd, pipeline_mode=pl.Buffered(1))


def _slot(h):
    return slice(h * SLOT, (h + 1) * SLOT)


def _to_slots(x):
    low = lax.broadcasted_iota(jnp.int32, (1, SLOT), 1) < HEAD_DIM
    slots = []
    for p in range(N_HEADS // 2):
        chunk = x[:, p * SLOT:(p + 1) * SLOT]
        slots.append(jnp.where(low, chunk, 0.0))
        slots.append(jnp.where(low, pltpu.roll(chunk, HEAD_DIM, 1), 0.0))
    return slots


def _ffn_kernel(*refs, n_chunks, has_mix):
    if has_mix:
        (x_ref, ya_ref, yb_ref, wo_ref, g_ref, wg_ref, wu_ref, wd_ref, o_ref,
         wg_s, wu_s, wd_s, wo_s) = refs
    else:
        x_ref, g_ref, wg_ref, wu_ref, wd_ref, o_ref, wg_s, wu_s, wd_s = refs
    step = pl.program_id(0)

    @pl.when(step < n_chunks)
    def _():
        wg_s[step] = wg_ref[...].astype(BF16)
        wu_s[step] = wu_ref[...].astype(BF16)
        wd_s[step] = wd_ref[...].astype(BF16)

    if has_mix:
        @pl.when(step == 0)
        def _():
            wo_s[...] = wo_ref[...].astype(BF16)

    @pl.when(step >= n_chunks)
    def _():
        x = x_ref[...]
        if has_mix:
            w = ya_ref.shape[1]
            x = x + _dot(ya_ref[...], wo_s[0:w, :]) + _dot(yb_ref[...], wo_s[w:2 * w, :])
        hn = _rms_rows(x, g_ref[...]).astype(BF16)
        acc = jnp.zeros(x.shape, F32)
        for c in range(n_chunks):
            gt = _dot(hn, wg_s[c])
            ut = _dot(hn, wu_s[c])
            a = (gt * jax.nn.sigmoid(gt) * ut).astype(BF16)
            acc = acc + _dot(a, wd_s[c])
        o_ref[...] = x + 0.5 * acc


def _ffn(x, g, wg, wu, wd, layer, mix=None, *, tm=512, ff_chunk=256):
    n, d = x.shape
    d_ff = wg.shape[2]
    assert n % tm == 0 and d_ff % ff_chunk == 0
    nc = d_ff // ff_chunk
    tile = lambda s: (jnp.maximum(s - nc, 0), 0)
    chunk_col = lambda s: (layer, 0, jnp.minimum(s, nc - 1))
    chunk_row = lambda s: (layer, jnp.minimum(s, nc - 1), 0)
    in_specs = [pl.BlockSpec((tm, d), tile)]
    args = [x]
    scratch = [pltpu.VMEM((nc, d, ff_chunk), BF16), pltpu.VMEM((nc, d, ff_chunk), BF16),
               pltpu.VMEM((nc, ff_chunk, d), BF16)]
    if mix is not None:
        ya, yb, w_out, j = mix
        w = ya.shape[1]
        assert w_out.shape[1:] == (2 * w, d)
        in_specs += [pl.BlockSpec((tm, w), tile), pl.BlockSpec((tm, w), tile),
                     pl.BlockSpec((None, 2 * w, d), lambda s: (j, 0, 0), pipeline_mode=pl.Buffered(1))]
        args += [ya, yb, w_out]
        scratch.append(pltpu.VMEM((2 * w, d), BF16))
    in_specs += [_const_spec((1, d)), pl.BlockSpec((None, d, ff_chunk), chunk_col),
                 pl.BlockSpec((None, d, ff_chunk), chunk_col), pl.BlockSpec((None, ff_chunk, d), chunk_row)]
    args += [g[layer].reshape(1, d), wg, wu, wd]
    return pl.pallas_call(
        functools.partial(_ffn_kernel, n_chunks=nc, has_mix=mix is not None),
        grid=(nc + n // tm,),
        in_specs=in_specs,
        out_specs=pl.BlockSpec((tm, d), tile),
        out_shape=jax.ShapeDtypeStruct((n, d), F32),
        scratch_shapes=scratch,
        compiler_params=_cparams(("arbitrary",)),
        name="ffn_mix" if mix is not None else "ffn",
    )(*args)


def _store_qkv_slots(hn, w_ref, gmat_ref, qn_ref, kn_ref, q_o, k_o, v_o, col0, k_feat):
    w = ATTN_WIDTH
    gmat = gmat_ref[...]
    q = _head_rms(_dot(hn, w_ref[:, col0:col0 + w]), gmat, qn_ref[...]) * Q_SCALE
    k = _head_rms(_dot(hn, w_ref[:, col0 + w:col0 + 2 * w]), gmat, kn_ref[...])
    v = _dot(hn, w_ref[:, col0 + 2 * w:col0 + 3 * w])
    lane = lax.broadcasted_iota(jnp.int32, (1, SLOT), 1)
    k_one = k_feat(lane)
    for h, (qs, ks, vs) in enumerate(zip(_to_slots(q), _to_slots(k), _to_slots(v))):
        q_o[:, _slot(h)] = qs.astype(BF16)
        if k_one is not None:
            ks = jnp.where(k_one, 1.0, ks)
        k_o[:, _slot(h)] = ks.astype(BF16)
        vs = jnp.where(lane == FEAT0, 1.0, vs)
        for r in range(vs.shape[0] // ATTN_TILE):
            v_o[r, _slot(h), :] = vs[r * ATTN_TILE:(r + 1) * ATTN_TILE].T.astype(BF16)
    return k


def _mixin_even_kernel(x_ref, g_ref, w_ref, gmat_ref, qn_ref, kn_ref,
                       gate_o, xr_o, q_o, k_o, v_o, km_o, *, n_blocks):
    tm = x_ref.shape[0]
    hn = _rms_rows(x_ref[...], g_ref[...]).astype(BF16)
    w = ATTN_WIDTH
    gate_o[...] = _dot(hn, w_ref[:, 0:w])
    xr_o[...] = _dot(hn, w_ref[:, w:2 * w])
    blocks_per_tile = tm // MOBA_BLOCK
    row_block = lax.broadcasted_iota(jnp.int32, (tm, 1), 0) // MOBA_BLOCK
    block = (pl.program_id(0) * blocks_per_tile + row_block) % n_blocks
    k = _store_qkv_slots(hn, w_ref, gmat_ref, qn_ref, kn_ref, q_o, k_o, v_o, 2 * w,
                         lambda lane: lane == FEAT0 + block)
    for r in range(blocks_per_tile):
        km_o[r] = jnp.mean(k[r * MOBA_BLOCK:(r + 1) * MOBA_BLOCK], axis=0, keepdims=True)


def _vt_spec(tm):
    return pl.BlockSpec((tm // ATTN_TILE, SLOT_WIDTH, ATTN_TILE), lambda i: (i, 0, 0))


def _vt_shape(n):
    return jax.ShapeDtypeStruct((n // ATTN_TILE, SLOT_WIDTH, ATTN_TILE), BF16)


def _group_mean_matrix():
    idx = jnp.arange(ATTN_WIDTH) // HEAD_DIM
    return (idx[:, None] == idx[None, :]).astype(BF16) * (1.0 / HEAD_DIM)


def _mixin_even(x, g, w_in, q_norm, k_norm, *, n_blocks, tm=512):
    n, d = x.shape
    w = ATTN_WIDTH
    assert n % tm == 0 and tm % MOBA_BLOCK == 0 and tm % ATTN_TILE == 0 and w_in.shape[1] == 5 * w
    assert FEAT0 + n_blocks <= SLOT
    row = lambda i: (i, 0)
    blk = tm // MOBA_BLOCK
    slot_shape = jax.ShapeDtypeStruct((n, SLOT_WIDTH), BF16)
    return pl.pallas_call(
        functools.partial(_mixin_even_kernel, n_blocks=n_blocks),
        grid=(n // tm,),
        in_specs=[pl.BlockSpec((tm, d), row), _const_spec((1, d)), _const_spec((d, 5 * w)),
                  _const_spec((w, w)), _const_spec((1, w)), _const_spec((1, w))],
        out_specs=[pl.BlockSpec((tm, w), row)] * 2 + [pl.BlockSpec((tm, SLOT_WIDTH), row)] * 2
        + [_vt_spec(tm), pl.BlockSpec((blk, 1, w), lambda i: (i, 0, 0))],
        out_shape=[jax.ShapeDtypeStruct((n, w), F32), jax.ShapeDtypeStruct((n, w), F32),
                   slot_shape, slot_shape, _vt_shape(n),
                   jax.ShapeDtypeStruct((n // MOBA_BLOCK, 1, w), F32)],
        compiler_params=_cparams(("parallel",)),
        name="mixin_even",
    )(x, g.reshape(1, d), w_in.astype(BF16), _group_mean_matrix(),
      jnp.tile(q_norm, N_HEADS).reshape(1, w), jnp.tile(k_norm, N_HEADS).reshape(1, w))


def _mixin_odd_kernel(x_ref, g_ref, w_ref, ws_ref, gmat_ref, qn_ref, kn_ref,
                      c_o, q_o, k_o, v_o, qi_o, ki_o, wi_o):
    hn = _rms_rows(x_ref[...], g_ref[...]).astype(BF16)
    w = ATTN_WIDTH
    ca = _dot(hn, w_ref[:, 0:w])
    cg = _dot(hn, w_ref[:, w:2 * w])
    c_o[...] = ca * jax.nn.sigmoid(cg)
    _store_qkv_slots(hn, w_ref, gmat_ref, qn_ref, kn_ref, q_o, k_o, v_o, 2 * w, lambda lane: None)
    qi_o[...] = _dot(hn, w_ref[:, 5 * w:6 * w]).astype(BF16)
    small = _dot(hn, ws_ref[...])
    ki_o[...] = small[:, 0:128].astype(BF16)
    wi = small[:, 128:256] * (IDX_DIM ** -0.5 * IDX_HEADS ** -0.5)
    wi_o[...] = wi.T[0:IDX_HEADS, :]


def _mixin_odd(x, g, w_in, q_norm, k_norm, *, tm=512):
    n, d = x.shape
    w = ATTN_WIDTH
    assert n % tm == 0 and tm % ATTN_TILE == 0 and w_in.shape[1] == 6 * w + IDX_DIM + IDX_HEADS
    w_main = w_in[:, :6 * w].astype(BF16)
    w_ki = w_in[:, 6 * w:6 * w + IDX_DIM]
    w_wi = jnp.pad(w_in[:, 6 * w + IDX_DIM:], ((0, 0), (0, 128 - IDX_HEADS)))
    w_small = jnp.concatenate([w_ki, w_ki, w_wi], axis=1).astype(BF16)
    row = lambda i: (i, 0)
    slot_shape = jax.ShapeDtypeStruct((n, SLOT_WIDTH), BF16)
    return pl.pallas_call(
        _mixin_odd_kernel,
        grid=(n // tm,),
        in_specs=[pl.BlockSpec((tm, d), row), _const_spec((1, d)), _const_spec((d, 6 * w)),
                  _const_spec((d, 256)), _const_spec((w, w)), _const_spec((1, w)), _const_spec((1, w))],
        out_specs=[pl.BlockSpec((tm, w), row)] + [pl.BlockSpec((tm, SLOT_WIDTH), row)] * 2
        + [_vt_spec(tm), pl.BlockSpec((tm, w), row), pl.BlockSpec((tm, 128), row),
           pl.BlockSpec((IDX_HEADS, tm), lambda i: (0, i))],
        out_shape=[jax.ShapeDtypeStruct((n, w), F32), slot_shape, slot_shape, _vt_shape(n),
                   jax.ShapeDtypeStruct((n, w), BF16),
                   jax.ShapeDtypeStruct((n, 128), BF16), jax.ShapeDtypeStruct((IDX_HEADS, n), F32)],
        compiler_params=_cparams(("parallel",)),
        name="mixin_odd",
    )(x, g.reshape(1, d), w_main, w_small, _group_mean_matrix(),
      jnp.tile(q_norm, N_HEADS).reshape(1, w), jnp.tile(k_norm, N_HEADS).reshape(1, w))


def _lru_kernel(gate_ref, xr_ref, cw_ref, cb_ref, wa_ref, ba_ref, wx_ref, bx_ref, sp_ref,
                o_ref, xbuf, a_s, u_s, h_s, hc, *, ts):
    j = pl.program_id(1)

    @pl.when(j == 0)
    def _():
        xbuf[0:8, :] = jnp.zeros((8, xbuf.shape[1]), F32)
        hc[...] = jnp.zeros(hc.shape, F32)

    xbuf[8:8 + ts, :] = xr_ref[...]
    xc = cb_ref[...] + cw_ref[0:1, :] * xbuf[5:5 + ts, :]
    for k in range(1, 4):
        xc = xc + cw_ref[k:k + 1, :] * xbuf[5 + k:5 + k + ts, :]
    xbuf[0:8, :] = xbuf[ts:ts + 8, :]

    xcb = xc.astype(BF16)
    r = jax.nn.sigmoid(_dot(xcb, wa_ref[...]) + ba_ref[...])
    ig = jax.nn.sigmoid(_dot(xcb, wx_ref[...]) + bx_ref[...])
    log_a = -LRU_C * r * sp_ref[...]
    a = jnp.exp(log_a)
    a_s[...] = a
    u_s[...] = jnp.sqrt(-jnp.tanh(log_a) * (a * a + 1.0)) * (ig * xc)

    row = lax.broadcasted_iota(jnp.int32, (8, a_s.shape[1]), 0)

    def body(g, carry):
        r0 = pl.multiple_of(g * 8, 8)
        a = a_s[pl.ds(r0, 8), :]
        u = u_s[pl.ds(r0, 8), :]
        for s in (1, 2, 4):
            ok = row >= s
            a_sh = jnp.where(ok, pltpu.roll(a, s, 0), 1.0)
            u_sh = jnp.where(ok, pltpu.roll(u, s, 0), 0.0)
            u = a * u_sh + u
            a = a * a_sh
        h = a * carry + u
        h_s[pl.ds(r0, 8), :] = h
        return h[7:8, :]

    hc[...] = lax.fori_loop(0, ts // 8, body, hc[...], unroll=4)
    o_ref[...] = (h_s[...] * jax.nn.gelu(gate_ref[...])).astype(BF16)


def _block_diag(wb):
    nb, bs, _ = wb.shape
    eye = jnp.eye(nb, dtype=wb.dtype)
    return (eye[:, None, :, None] * wb[:, :, None, :]).reshape(nb * bs, nb * bs)


def _lru(gate, xr, conv_w, conv_b, ra_w, ra_b, ix_w, ix_b, lam, *, bsz, ts=256):
    n, w = xr.shape
    seq = n // bsz
    assert seq % ts == 0
    nt = seq // ts
    row = lambda b, j: (b * nt + j, 0)
    vec = lambda v: v.reshape(1, w).astype(F32)
    return pl.pallas_call(
        functools.partial(_lru_kernel, ts=ts),
        grid=(bsz, nt),
        in_specs=[pl.BlockSpec((ts, w), row), pl.BlockSpec((ts, w), row),
                  _const_spec((conv_w.shape[0], w)), _const_spec((1, w)),
                  _const_spec((w, w)), _const_spec((1, w)), _const_spec((w, w)), _const_spec((1, w)),
                  _const_spec((1, w))],
        out_specs=pl.BlockSpec((ts, w), row),
        out_shape=jax.ShapeDtypeStruct((n, w), BF16),
        scratch_shapes=[pltpu.VMEM((ts + 8, w), F32), pltpu.VMEM((ts, w), F32),
                        pltpu.VMEM((ts, w), F32), pltpu.VMEM((ts, w), F32), pltpu.VMEM((1, w), F32)],
        compiler_params=_cparams(("arbitrary", "arbitrary")),
        name="rg_lru",
    )(gate, xr, conv_w, vec(conv_b), _block_diag(ra_w).astype(BF16), vec(ra_b),
      _block_diag(ix_w).astype(BF16), vec(ix_b), vec(jax.nn.softplus(-lam)))


def _conf_kernel(c_ref, w_ref, b_ref, g_ref, beta_ref, o_ref, cbuf, sh_ref, *, ts, halo, width):
    j = pl.program_id(1)

    @pl.when(j == 0)
    def _():
        cbuf[0:halo, :] = jnp.zeros((halo, cbuf.shape[1]), F32)

    cbuf[halo:halo + ts, :] = c_ref[...]
    base = halo - (width - 1)
    span = sh_ref.shape[1]
    for r in range(1, SUBLANES):
        sh_ref[r - 1] = cbuf[r:r + span, :]

    for c0 in range(0, ts, CONV_ROWS):
        y = jnp.broadcast_to(b_ref[...], (CONV_ROWS, cbuf.shape[1]))
        for k in range(width):
            r = (base + k) % SUBLANES
            u0 = c0 + base + k - r
            win = cbuf[u0:u0 + CONV_ROWS, :] if r == 0 else sh_ref[r - 1, u0:u0 + CONV_ROWS, :]
            y = y + w_ref[k:k + 1, :] * win
        mu = jnp.mean(y, axis=-1, keepdims=True)
        yc = y - mu
        var = jnp.mean(yc * yc, axis=-1, keepdims=True)
        z = yc * lax.rsqrt(var + EPS) * g_ref[...] + beta_ref[...]
        o_ref[c0:c0 + CONV_ROWS, :] = (z * jax.nn.sigmoid(z)).astype(BF16)

    cbuf[0:halo, :] = cbuf[ts:ts + halo, :]


def _conf(c, dw_w, dw_b, ln_g, ln_b, *, bsz, ts=256, halo=32):
    n, w = c.shape
    seq = n // bsz
    width = dw_w.shape[0]
    assert seq % ts == 0 and width - 1 <= halo <= ts and halo % SUBLANES == 0 and ts % CONV_ROWS == 0
    nt = seq // ts
    row = lambda b, j: (b * nt + j, 0)
    vec = lambda v: v.reshape(1, w).astype(F32)
    return pl.pallas_call(
        functools.partial(_conf_kernel, ts=ts, halo=halo, width=width),
        grid=(bsz, nt),
        in_specs=[pl.BlockSpec((ts, w), row), _const_spec((width, w)),
                  _const_spec((1, w)), _const_spec((1, w)), _const_spec((1, w))],
        out_specs=pl.BlockSpec((ts, w), row),
        out_shape=jax.ShapeDtypeStruct((n, w), BF16),
        scratch_shapes=[pltpu.VMEM((ts + halo, w), F32),
                        pltpu.VMEM((SUBLANES - 1, ts + halo - SUBLANES, w), F32)],
        compiler_params=_cparams(("arbitrary", "arbitrary")),
        name="conformer_conv",
    )(c, dw_w, vec(dw_b), vec(ln_g), vec(ln_b))


def _bucket_tiles(t):
    assert t > REL_MAX_DIST
    n = np.arange(2 * t)
    nf = np.maximum(n, 1).astype(np.float32)
    large = REL_MAX_EXACT + (np.log(nf / np.float32(REL_MAX_EXACT))
                             / np.float32(math.log(REL_MAX_DIST / REL_MAX_EXACT))
                             * np.float32(REL_BUCKETS - REL_MAX_EXACT)).astype(np.int32)
    bucket = np.where(n < REL_MAX_EXACT, n, np.minimum(large, REL_BUCKETS - 1)).astype(np.int32)
    qry = np.arange(t)[None, :]
    key = np.arange(t)[:, None]
    return np.stack([bucket[np.maximum(qry - key, 0)], bucket[t + qry - key]])


def _build_bias(idx_ref, rb_ref, bias_ref):
    t = idx_ref.shape[1]
    causal = (lax.broadcasted_iota(jnp.int32, (t, t), 0) <= lax.broadcasted_iota(jnp.int32, (t, t), 1))
    for h in range(N_HEADS):
        far = rb_ref[REL_BUCKETS - 1, h]
        for which in (0, 1):
            idx = idx_ref[which]
            acc = jnp.zeros((t, t), F32)
            for b in range(REL_BUCKETS - 1):
                acc = jnp.where(idx == b, (rb_ref[b, h] - far) * LOG2E, acc)
            if which == 0:
                bias_ref[h, t:2 * t, :] = jnp.where(causal, acc, NEG)
            else:
                bias_ref[h, 0:t, :] = acc


def _attn_init(m_ref, acc_ref):
    m_ref[...] = jnp.full(m_ref.shape, M_INIT, F32)
    acc_ref[...] = jnp.zeros(acc_ref.shape, F32)


def _softmax_step(s, h, vt_h, m_ref, acc_ref):
    m_old = m_ref[h]
    m_new = jnp.maximum(m_old, jnp.max(s, axis=0, keepdims=True))
    p = jnp.exp2(s - m_new)
    m_ref[h] = m_new
    acc_ref[h] = jnp.exp2(m_old - m_new) * acc_ref[h] + _dot(vt_h, p.astype(BF16))


def _heads_pipelined(logits, vt_slot, m_ref, acc_ref):
    ahead = [logits(h) for h in range(HEADS_AHEAD)]
    for h in range(N_HEADS):
        if h + HEADS_AHEAD < N_HEADS:
            ahead.append(logits(h + HEADS_AHEAD))
        _softmax_step(ahead[h], h, vt_slot(h), m_ref, acc_ref)


FAR_PAIR = ("far", "far")
FAR_ONE = ("far",)
NEAR_DIAG = ("near", "diag")
DIAG_ONE = ("diag",)


def _attend_tiles(kt, kinds, q_slot, k_ref, vt_ref, bias_ref, mask_add, m_ref, acc_ref):
    t = ATTN_TILE
    n = len(kinds)
    rows = pl.ds(pl.multiple_of(kt * t, t), n * t)

    def logits(h):
        s = _dot_nt(k_ref[rows, _slot(h)], q_slot(h))
        if kinds == NEAR_DIAG:
            s = s + bias_ref[h]
        elif kinds == DIAG_ONE:
            s = s + bias_ref[h, t:2 * t, :]
        if mask_add is not None:
            s = s + mask_add
        return s

    def vt_slot(h):
        parts = [vt_ref[kt + a, _slot(h), :] for a in range(n)]
        return parts[0] if n == 1 else jnp.concatenate(parts, axis=1)

    _heads_pipelined(logits, vt_slot, m_ref, acc_ref)


def _for_causal_tiles(i, tiles):
    n_far = jnp.maximum(i - 1, 0)

    def pair(j, carry):
        tiles(2 * j, FAR_PAIR)
        return carry

    lax.fori_loop(0, n_far // 2, pair, 0)

    @pl.when(n_far % 2 == 1)
    def _():
        tiles(n_far - 1, FAR_ONE)

    @pl.when(i >= 1)
    def _():
        tiles(i - 1, NEAR_DIAG)

    @pl.when(i == 0)
    def _():
        tiles(i, DIAG_ONE)


def _attn_finish(o_ref, acc_ref):
    parts = []
    for h in range(N_HEADS):
        acc = acc_ref[h]
        parts.append(acc[0:HEAD_DIM] * (1.0 / acc[FEAT0:FEAT0 + 1]))
    o_ref[...] = jnp.concatenate(parts, axis=0).T.astype(BF16)


def _smem_spec():
    return pl.BlockSpec(memory_space=pltpu.SMEM)


def _moba_kernel(rb_ref, q_ref, k_ref, vt_ref, km_ref, idx_ref, o_ref,
                 acc_ref, m_ref, qs_ref, bias_ref, *, n_blocks):
    i = pl.program_id(1)
    t = ATTN_TILE
    nbp = -(-n_blocks // 8) * 8

    @pl.when((pl.program_id(0) == 0) & (i == 0))
    def _():
        _build_bias(idx_ref, rb_ref, bias_ref)

    _attn_init(m_ref, acc_ref)

    blk = lax.broadcasted_iota(jnp.int32, (nbp, t), 0)
    past = blk < i
    for h in range(N_HEADS):
        qh = q_ref[:, _slot(h)]
        km_hi, km_lo = _split_bf16(km_ref[:, _slot(h)])
        gate_t = _dot_nt(km_hi, qh) + _dot_nt(km_lo, qh)
        g = jnp.where(past, gate_t[FEAT0:FEAT0 + nbp, :], NEG)
        rank = jnp.zeros((nbp, t), F32)
        for j in range(n_blocks):
            gj = g[j:j + 1, :]
            beats = (gj > g) | ((gj == g) & (blk > j))
            rank = rank + jnp.where(beats, 1.0, 0.0)
        flag = jnp.where(past & (rank >= MOBA_TOPK), NEG, 0.0)
        flag_t = jnp.concatenate([jnp.zeros((FEAT0, t), F32), flag,
                                  jnp.zeros((SLOT - FEAT0 - nbp, t), F32)], axis=0)
        qs_ref[:, _slot(h)] = (qh.astype(F32) + flag_t.T).astype(BF16)

    def tiles(kt, kinds):
        _attend_tiles(kt, kinds, lambda h: qs_ref[:, _slot(h)], k_ref, vt_ref, bias_ref, None,
                      m_ref, acc_ref)

    _for_causal_tiles(i, tiles)
    _attn_finish(o_ref, acc_ref)


def _moba(q, k, vt, kmean, rel_bias, *, bsz):
    n = q.shape[0]
    seq = n // bsz
    t = ATTN_TILE
    assert seq % t == 0 and t == MOBA_BLOCK
    nq = seq // t
    assert FEAT0 + nq <= SLOT
    km = kmean.reshape(bsz, nq, N_HEADS, HEAD_DIM)
    km = jnp.pad(km, ((0, 0), (FEAT0, SLOT - FEAT0 - nq), (0, 0), (0, SLOT - HEAD_DIM)))
    km = km.reshape(bsz, SLOT, SLOT_WIDTH)
    r3 = lambda a: a.reshape(bsz, seq, SLOT_WIDTH)
    seq_spec = pl.BlockSpec((None, seq, SLOT_WIDTH), lambda b, i: (b, 0, 0), pipeline_mode=pl.Buffered(1))
    out = pl.pallas_call(
        functools.partial(_moba_kernel, n_blocks=nq),
        grid=(bsz, nq),
        in_specs=[_smem_spec(),
                  pl.BlockSpec((None, t, SLOT_WIDTH), lambda b, i: (b, i, 0)), seq_spec,
                  pl.BlockSpec((nq, SLOT_WIDTH, t), lambda b, i: (b, 0, 0), pipeline_mode=pl.Buffered(1)),
                  pl.BlockSpec((None, SLOT, SLOT_WIDTH), lambda b, i: (b, 0, 0)),
                  _const_spec((2, t, t))],
        out_specs=pl.BlockSpec((None, t, ATTN_WIDTH), lambda b, i: (b, i, 0)),
        out_shape=jax.ShapeDtypeStruct((bsz, seq, ATTN_WIDTH), BF16),
        scratch_shapes=[pltpu.VMEM((N_HEADS, SLOT, t), F32), pltpu.VMEM((N_HEADS, 1, t), F32),
                        pltpu.VMEM((t, SLOT_WIDTH), BF16),
                        pltpu.VMEM((N_HEADS, 2 * t, t), F32)],
        compiler_params=_cparams(("arbitrary", "arbitrary")),
        name="moba_attention",
    )(rel_bias, r3(q), r3(k), vt, km, jnp.asarray(_bucket_tiles(t)))
    return out.reshape(n, ATTN_WIDTH)


def _dsa_kernel(rb_ref, q_ref, k_ref, vt_ref, qi_ref, ki_ref, wit_ref, idx_ref, tri_ref, o_ref,
                acc_ref, m_ref, isc_ref, bc_ref, bias_ref, *, n_sel):
    i = pl.program_id(1)
    t = ATTN_TILE
    key = lax.broadcasted_iota(jnp.int32, (t, t), 0)
    qry = lax.broadcasted_iota(jnp.int32, (t, t), 1)
    lane128 = lax.broadcasted_iota(jnp.int32, (1, 128), 1)

    @pl.when((pl.program_id(0) == 0) & (i == 0))
    def _():
        _build_bias(idx_ref, rb_ref, bias_ref)

    _attn_init(m_ref, acc_ref)

    def index_tiles(kt, n, diag):
        rows = pl.ds(pl.multiple_of(kt * t, t), n * t)
        ki2 = ki_ref[rows, :]
        acc = jnp.zeros((n * t, t), F32)
        for pr in range(IDX_HEADS // 2):
            q2 = qi_ref[:, pr * 128:(pr + 1) * 128]
            for half in range(2):
                h = 2 * pr + half
                hm = (lane128 >= half * IDX_DIM) & (lane128 < (half + 1) * IDX_DIM)
                s = _dot_nt(ki2, jnp.where(hm, q2, jnp.zeros((), BF16)))
                acc = acc + jnp.maximum(s, 0.0) * wit_ref[h:h + 1, :]
        if diag:
            acc = jnp.where(key <= qry, acc, -jnp.inf)
        for a in range(n):
            isc_ref[kt + a] = acc[a * t:(a + 1) * t]

    def index_pair(j, carry):
        index_tiles(2 * j, 2, False)
        return carry

    lax.fori_loop(0, i // 2, index_pair, 0)

    @pl.when(i % 2 == 1)
    def _():
        index_tiles(i - 1, 1, False)

    index_tiles(i, 1, True)

    def fold8(x, op):
        return op(x.reshape(t // 8, 8, t), axis=0)

    def minmax_body(kt, carry):
        mn, mx = carry
        x = isc_ref[kt]
        mx = jnp.maximum(mx, fold8(x, jnp.max))
        mn = jnp.minimum(mn, fold8(jnp.where(x == -jnp.inf, jnp.inf, x), jnp.min))
        return mn, mx

    mn, mx = lax.fori_loop(0, i + 1, minmax_body,
                           (jnp.full((8, t), jnp.inf, F32), jnp.full((8, t), -jnp.inf, F32)))
    lo0 = jnp.min(mn, axis=0, keepdims=True)
    mx = jnp.max(mx, axis=0, keepdims=True)
    hi0 = mx + jnp.abs(mx) * 1e-3 + 1e-30
    n_valid = (i * t + 1 + lax.broadcasted_iota(jnp.int32, (1, t), 1)).astype(F32)
    want = jnp.minimum(n_valid, float(n_sel))

    def count_ge(thr):
        def one(kt, c):
            return c + fold8(jnp.where(isc_ref[kt] >= thr, 1.0, 0.0), jnp.sum)

        def pair(j, c):
            return one(2 * j + 1, one(2 * j, c))

        c = lax.fori_loop(0, (i + 1) // 2, pair, jnp.zeros((8, t), F32))
        c = lax.cond((i + 1) % 2 == 1, lambda c: one(i, c), lambda c: c, c)
        return jnp.sum(c, axis=0, keepdims=True)

    def bisect_step(_, carry):
        lo, hi, c_lo, c_hi = carry
        mid = 0.5 * (lo + hi)
        c_mid = count_ge(mid)
        up = c_mid >= want
        return (jnp.where(up, mid, lo), jnp.where(up, hi, mid),
                jnp.where(up, c_mid, c_lo), jnp.where(up, c_hi, c_mid))

    def bisect_cond(carry):
        it, _, _, c_lo, _ = carry
        return (it < BISECT_ITERS) & (jnp.max(c_lo - want) > 0.0)

    def bisect_body(carry):
        return (carry[0] + BISECT_CHECK_EVERY,) + lax.fori_loop(0, BISECT_CHECK_EVERY, bisect_step, carry[1:])

    _, lo, hi, c_lo, c_hi = lax.while_loop(bisect_cond, bisect_body,
                                           (0, lo0, hi0, n_valid, jnp.zeros((1, t), F32)))
    need = want - c_hi
    banded = jnp.max(c_lo - want) > 0.0
    bc_ref[...] = jnp.zeros(bc_ref.shape, F32)

    def select_mask(kt):
        x = isc_ref[kt]

        def plain():
            return jnp.where(x >= lo, 0.0, NEG)

        def with_band():
            band = jnp.where((x >= lo) & (x < hi), 1.0, 0.0)
            before = bc_ref[...] + _dot(tri_ref[...], band.astype(BF16))
            bc_ref[...] = bc_ref[...] + jnp.sum(band, axis=0, keepdims=True)
            return jnp.where((x >= hi) | ((band > 0.0) & (before < need)), 0.0, NEG)

        return lax.cond(banded, with_band, plain)

    def tiles(kt, kinds):
        masks = [select_mask(kt + a) for a in range(len(kinds))]
        mask_add = masks[0] if len(masks) == 1 else jnp.concatenate(masks, axis=0)
        _attend_tiles(kt, kinds, lambda h: q_ref[:, _slot(h)], k_ref, vt_ref, bias_ref, mask_add,
                      m_ref, acc_ref)

    _for_causal_tiles(i, tiles)
    _attn_finish(o_ref, acc_ref)


def _dsa(q, k, vt, qi, ki2, wit, rel_bias, *, bsz):
    n = q.shape[0]
    seq = n // bsz
    t = ATTN_TILE
    assert seq % t == 0
    nq = seq // t
    n_sel = min(DSA_TOPK_MAX, seq // 4)
    tri = (jnp.arange(t)[None, :] < jnp.arange(t)[:, None]).astype(BF16)
    r3 = lambda a: a.reshape(bsz, seq, a.shape[-1])
    tile_spec = lambda width: pl.BlockSpec((None, t, width), lambda b, i: (b, i, 0))
    seq_spec = lambda width: pl.BlockSpec((None, seq, width), lambda b, i: (b, 0, 0),
                                          pipeline_mode=pl.Buffered(1))
    out = pl.pallas_call(
        functools.partial(_dsa_kernel, n_sel=n_sel),
        grid=(bsz, nq),
        in_specs=[_smem_spec(), tile_spec(SLOT_WIDTH), seq_spec(SLOT_WIDTH),
                  pl.BlockSpec((nq, SLOT_WIDTH, t), lambda b, i: (b, 0, 0), pipeline_mode=pl.Buffered(1)),
                  tile_spec(ATTN_WIDTH), seq_spec(128),
                  pl.BlockSpec((IDX_HEADS, t), lambda b, i: (0, b * nq + i)),
                  _const_spec((2, t, t)), _const_spec((t, t))],
        out_specs=tile_spec(ATTN_WIDTH),
        out_shape=jax.ShapeDtypeStruct((bsz, seq, ATTN_WIDTH), BF16),
        scratch_shapes=[pltpu.VMEM((N_HEADS, SLOT, t), F32), pltpu.VMEM((N_HEADS, 1, t), F32),
                        pltpu.VMEM((nq, t, t), F32), pltpu.VMEM((1, t), F32),
                        pltpu.VMEM((N_HEADS, 2 * t, t), F32)],
        compiler_params=_cparams(("arbitrary", "arbitrary")),
        name="dsa_attention",
    )(rel_bias, r3(q), r3(k), vt, r3(qi), r3(ki2), wit, jnp.asarray(_bucket_tiles(t)), tri)
    return out.reshape(n, ATTN_WIDTH)


def kernel(x, rel_bias, ffn1_norm, ffn1_w_gate, ffn1_w_up, ffn1_w_down, mix_norm, ffn2_norm, ffn2_w_gate, ffn2_w_up, ffn2_w_down, ev_w_in, ev_conv_w, ev_conv_b, ev_ra_w, ev_ra_b, ev_ix_w, ev_ix_b, ev_lambda, ev_q_norm, ev_k_norm, ev_w_out, od_w_in, od_dw_w, od_dw_b, od_ln_g, od_ln_b, od_q_norm, od_k_norm, od_w_out):
    bsz, seq, d = x.shape
    depth = ffn1_norm.shape[0]
    h = x.reshape(bsz * seq, d)
    for i in range(depth):
        h = _ffn(h, ffn1_norm, ffn1_w_gate, ffn1_w_up, ffn1_w_down, i)
        j = i // 2
        if i % 2 == 0:
            gate, xr, q, k, vt, kmean = _mixin_even(h, mix_norm[i], ev_w_in[j], ev_q_norm[j], ev_k_norm[j],
                                                    n_blocks=seq // MOBA_BLOCK)
            ya = _lru(gate, xr, ev_conv_w[j], ev_conv_b[j], ev_ra_w[j], ev_ra_b[j],
                      ev_ix_w[j], ev_ix_b[j], ev_lambda[j], bsz=bsz)
            yb = _moba(q, k, vt, kmean, rel_bias, bsz=bsz)
            mix = (ya, yb, ev_w_out, j)
        else:
            c, q, k, vt, qi, ki2, wit = _mixin_odd(h, mix_norm[i], od_w_in[j], od_q_norm[j], od_k_norm[j])
            yc = _conf(c, od_dw_w[j], od_dw_b[j], od_ln_g[j], od_ln_b[j], bsz=bsz)
            yd = _dsa(q, k, vt, qi, ki2, wit, rel_bias, bsz=bsz)
            mix = (yc, yd, od_w_out, j)
        h = _ffn(h, ffn2_norm, ffn2_w_gate, ffn2_w_up, ffn2_w_down, i, mix)
    return h.reshape(bsz, seq, d)
```

```python
import functools
import math

import numpy as np
import jax
import jax.numpy as jnp
from jax import lax
from jax.experimental import pallas as pl
from jax.experimental.pallas import tpu as pltpu

F32 = jnp.float32
BF16 = jnp.bfloat16

N_HEADS = 8
HEAD_DIM = 64
ATTN_WIDTH = N_HEADS * HEAD_DIM
LRU_C = 8.0
MOBA_BLOCK = 256
MOBA_TOPK = 3
IDX_HEADS = 8
IDX_DIM = 64
DSA_TOPK_MAX = 256
REL_BUCKETS = 32
REL_MAX_EXACT = REL_BUCKETS // 2
REL_MAX_DIST = 128
EPS = 1e-6
NEG = -1e30
M_INIT = -1e29
ATTN_TILE = 256
SUBLANES = 8
CONV_ROWS = 64
SLOT = 128
SLOT_WIDTH = N_HEADS * SLOT
FEAT0 = HEAD_DIM
HEADS_AHEAD = 4
BISECT_ITERS = 32
BISECT_CHECK_EVERY = 4
LOG2E = math.log2(math.e)
Q_SCALE = HEAD_DIM ** -0.5 * LOG2E
VMEM_LIMIT = 56 * 1024 * 1024


def _cparams(sem):
    return pltpu.CompilerParams(dimension_semantics=sem, vmem_limit_bytes=VMEM_LIMIT)


def _dot(a, b):
    return jnp.dot(a, b, preferred_element_type=F32)


def _dot_nt(a, b):
    return lax.dot_general(a, b, (((1,), (1,)), ((), ())), preferred_element_type=F32)


def _split_bf16(x):
    hi = x.astype(BF16)
    lo = (x - hi.astype(F32)).astype(BF16)
    return hi, lo


def _rms_rows(x, g):
    return x * lax.rsqrt(jnp.mean(x * x, axis=-1, keepdims=True) + EPS) * g


def _const_spec(shape):
    nd = len(shape)
    return pl.BlockSpec(shape, lambda *_: (0,) * nd, pipeline_mode=pl.Buffered(1))


def _slot(h):
    return slice(h * SLOT, (h + 1) * SLOT)


def _to_slots(x):
    low = lax.broadcasted_iota(jnp.int32, (1, SLOT), 1) < HEAD_DIM
    slots = []
    for p in range(N_HEADS // 2):
        chunk = x[:, p * SLOT:(p + 1) * SLOT]
        slots.append(jnp.where(low, chunk, 0.0))
        slots.append(jnp.where(low, pltpu.roll(chunk, HEAD_DIM, 1), 0.0))
    return slots


def _ffn_kernel(*refs, n_chunks, has_mix):
    if has_mix:
        (x_ref, ya_ref, yb_ref, wo_ref, g_ref, wg_ref, wu_ref, wd_ref, o_ref,
         wg_s, wu_s, wd_s, wo_s) = refs
    else:
        x_ref, g_ref, wg_ref, wu_ref, wd_ref, o_ref, wg_s, wu_s, wd_s = refs
    step = pl.program_id(0)

    @pl.when(step < n_chunks)
    def _():
        wg_s[step] = wg_ref[...].astype(BF16)
        wu_s[step] = wu_ref[...].astype(BF16)
        wd_s[step] = wd_ref[...].astype(BF16)

    if has_mix:
        @pl.when(step == 0)
        def _():
            wo_s[...] = wo_ref[...].astype(BF16)

    @pl.when(step >= n_chunks)
    def _():
        x = x_ref[...]
        if has_mix:
            w = ya_ref.shape[1]
            x = x + _dot(ya_ref[...], wo_s[0:w, :]) + _dot(yb_ref[...], wo_s[w:2 * w, :])
        hn = _rms_rows(x, g_ref[...]).astype(BF16)
        acc = jnp.zeros(x.shape, F32)
        for c in range(n_chunks):
            gt = _dot(hn, wg_s[c])
            ut = _dot(hn, wu_s[c])
            a = (gt * jax.nn.sigmoid(gt) * ut).astype(BF16)
            acc = acc + _dot(a, wd_s[c])
        o_ref[...] = x + 0.5 * acc


def _ffn(x, g, wg, wu, wd, layer, mix=None, *, tm=512, ff_chunk=256):
    n, d = x.shape
    d_ff = wg.shape[2]
    assert n % tm == 0 and d_ff % ff_chunk == 0
    nc = d_ff // ff_chunk
    tile = lambda s: (jnp.maximum(s - nc, 0), 0)
    chunk_col = lambda s: (layer, 0, jnp.minimum(s, nc - 1))
    chunk_row = lambda s: (layer, jnp.minimum(s, nc - 1), 0)
    in_specs = [pl.BlockSpec((tm, d), tile)]
    args = [x]
    scratch = [pltpu.VMEM((nc, d, ff_chunk), BF16), pltpu.VMEM((nc, d, ff_chunk), BF16),
               pltpu.VMEM((nc, ff_chunk, d), BF16)]
    if mix is not None:
        ya, yb, w_out, j = mix
        w = ya.shape[1]
        assert w_out.shape[1:] == (2 * w, d)
        in_specs += [pl.BlockSpec((tm, w), tile), pl.BlockSpec((tm, w), tile),
                     pl.BlockSpec((None, 2 * w, d), lambda s: (j, 0, 0), pipeline_mode=pl.Buffered(1))]
        args += [ya, yb, w_out]
        scratch.append(pltpu.VMEM((2 * w, d), BF16))
    in_specs += [_const_spec((1, d)), pl.BlockSpec((None, d, ff_chunk), chunk_col),
                 pl.BlockSpec((None, d, ff_chunk), chunk_col), pl.BlockSpec((None, ff_chunk, d), chunk_row)]
    args += [g[layer].reshape(1, d), wg, wu, wd]
    return pl.pallas_call(
        functools.partial(_ffn_kernel, n_chunks=nc, has_mix=mix is not None),
        grid=(nc + n // tm,),
        in_specs=in_specs,
        out_specs=pl.BlockSpec((tm, d), tile),
        out_shape=jax.ShapeDtypeStruct((n, d), F32),
        scratch_shapes=scratch,
        compiler_params=_cparams(("arbitrary",)),
        name="ffn_mix" if mix is not None else "ffn",
    )(*args)


def _slot_rms(xs, g):
    ms = jnp.sum(xs * xs, axis=-1, keepdims=True) * (1.0 / HEAD_DIM)
    return xs * lax.rsqrt(ms + EPS) * g


def _store_q_slots(z, gain_ref, q_o):
    for h, zs in enumerate(_to_slots(z)):
        q_o[:, _slot(h)] = (_slot_rms(zs, gain_ref[...]) * Q_SCALE).astype(BF16)


def _store_k_slots(z, gain_ref, k_o, one_lanes):
    slots = [_slot_rms(zs, gain_ref[...]) for zs in _to_slots(z)]
    for h, ks in enumerate(slots):
        k_o[:, _slot(h)] = (ks if one_lanes is None else jnp.where(one_lanes, 1.0, ks)).astype(BF16)
    return slots


def _store_vt_slots(z, v_o):
    lane = lax.broadcasted_iota(jnp.int32, (1, SLOT), 1)
    for h, vs in enumerate(_to_slots(z)):
        vs = jnp.where(lane == FEAT0, 1.0, vs)
        for r in range(vs.shape[0] // ATTN_TILE):
            v_o[r, _slot(h), :] = vs[r * ATTN_TILE:(r + 1) * ATTN_TILE].T.astype(BF16)


def _pipelined(stages):
    nxt = stages[0][0]()
    for j, (_, consume) in enumerate(stages):
        cur = nxt
        if j + 1 < len(stages):
            nxt = stages[j + 1][0]()
        consume(cur)


def _mixin_even_kernel(x_ref, g_ref, w_ref, qn_ref, kn_ref,
                       gate_o, xr_o, q_o, k_o, v_o, km_o, *, n_blocks):
    tm = x_ref.shape[0]
    hn = _rms_rows(x_ref[...], g_ref[...]).astype(BF16)
    w = ATTN_WIDTH
    proj = lambda c: (lambda: _dot(hn, w_ref[:, c * w:(c + 1) * w]))
    blocks_per_tile = tm // MOBA_BLOCK
    row_block = lax.broadcasted_iota(jnp.int32, (tm, 1), 0) // MOBA_BLOCK
    block = (pl.program_id(0) * blocks_per_tile + row_block) % n_blocks
    block_lane = lax.broadcasted_iota(jnp.int32, (1, SLOT), 1) == FEAT0 + block

    def store_k(z):
        for h, ks in enumerate(_store_k_slots(z, kn_ref, k_o, block_lane)):
            for r in range(blocks_per_tile):
                km_o[r, :, _slot(h)] = jnp.mean(ks[r * MOBA_BLOCK:(r + 1) * MOBA_BLOCK], axis=0, keepdims=True)

    def store(ref):
        def consume(z):
            ref[...] = z
        return consume

    _pipelined([(proj(2), lambda z: _store_q_slots(z, qn_ref, q_o)), (proj(3), store_k),
                (proj(4), lambda z: _store_vt_slots(z, v_o)), (proj(0), store(gate_o)), (proj(1), store(xr_o))])


def _vt_spec(tm):
    return pl.BlockSpec((tm // ATTN_TILE, SLOT_WIDTH, ATTN_TILE), lambda i: (i, 0, 0))


def _vt_shape(n):
    return jax.ShapeDtypeStruct((n // ATTN_TILE, SLOT_WIDTH, ATTN_TILE), BF16)


def _slot_gain(g):
    return jnp.pad(g.astype(F32), (0, SLOT - HEAD_DIM)).reshape(1, SLOT)


def _mixin_even(x, g, w_in, q_norm, k_norm, *, n_blocks, tm=512):
    n, d = x.shape
    w = ATTN_WIDTH
    assert n % tm == 0 and tm % MOBA_BLOCK == 0 and tm % ATTN_TILE == 0 and w_in.shape[1] == 5 * w
    assert FEAT0 + n_blocks <= SLOT
    row = lambda i: (i, 0)
    blk = tm // MOBA_BLOCK
    slot_shape = jax.ShapeDtypeStruct((n, SLOT_WIDTH), BF16)
    return pl.pallas_call(
        functools.partial(_mixin_even_kernel, n_blocks=n_blocks),
        grid=(n // tm,),
        in_specs=[pl.BlockSpec((tm, d), row), _const_spec((1, d)), _const_spec((d, 5 * w)),
                  _const_spec((1, SLOT)), _const_spec((1, SLOT))],
        out_specs=[pl.BlockSpec((tm, w), row)] * 2 + [pl.BlockSpec((tm, SLOT_WIDTH), row)] * 2
        + [_vt_spec(tm), pl.BlockSpec((blk, 1, SLOT_WIDTH), lambda i: (i, 0, 0))],
        out_shape=[jax.ShapeDtypeStruct((n, w), F32), jax.ShapeDtypeStruct((n, w), F32),
                   slot_shape, slot_shape, _vt_shape(n),
                   jax.ShapeDtypeStruct((n // MOBA_BLOCK, 1, SLOT_WIDTH), F32)],
        compiler_params=_cparams(("parallel",)),
        name="mixin_even",
    )(x, g.reshape(1, d), w_in.astype(BF16), _slot_gain(q_norm), _slot_gain(k_norm))


def _mixin_odd_kernel(x_ref, g_ref, w_ref, ws_ref, qn_ref, kn_ref,
                      c_o, q_o, k_o, v_o, qi_o, ki_o, wi_o):
    hn = _rms_rows(x_ref[...], g_ref[...]).astype(BF16)
    w = ATTN_WIDTH
    proj = lambda c, n=1: (lambda: _dot(hn, w_ref[:, c * w:(c + n) * w]))

    def store_glu(z):
        c_o[...] = z[:, 0:w] * jax.nn.sigmoid(z[:, w:2 * w])

    def store_qi(z):
        qi_o[...] = z.astype(BF16)

    def store_small(z):
        ki_o[...] = z[:, 0:128].astype(BF16)
        wi = z[:, 128:256] * (IDX_DIM ** -0.5 * IDX_HEADS ** -0.5)
        wi_o[...] = wi.T[0:IDX_HEADS, :]

    _pipelined([(proj(2), lambda z: _store_q_slots(z, qn_ref, q_o)),
                (proj(3), lambda z: _store_k_slots(z, kn_ref, k_o, None)),
                (proj(4), lambda z: _store_vt_slots(z, v_o)), (proj(0, 2), store_glu),
                (proj(5), store_qi), (lambda: _dot(hn, ws_ref[...]), store_small)])


def _mixin_odd(x, g, w_in, q_norm, k_norm, *, tm=512):
    n, d = x.shape
    w = ATTN_WIDTH
    assert n % tm == 0 and tm % ATTN_TILE == 0 and w_in.shape[1] == 6 * w + IDX_DIM + IDX_HEADS
    w_main = w_in[:, :6 * w].astype(BF16)
    w_ki = w_in[:, 6 * w:6 * w + IDX_DIM]
    w_wi = jnp.pad(w_in[:, 6 * w + IDX_DIM:], ((0, 0), (0, 128 - IDX_HEADS)))
    w_small = jnp.concatenate([w_ki, w_ki, w_wi], axis=1).astype(BF16)
    row = lambda i: (i, 0)
    slot_shape = jax.ShapeDtypeStruct((n, SLOT_WIDTH), BF16)
    return pl.pallas_call(
        _mixin_odd_kernel,
        grid=(n // tm,),
        in_specs=[pl.BlockSpec((tm, d), row), _const_spec((1, d)), _const_spec((d, 6 * w)),
                  _const_spec((d, 256)), _const_spec((1, SLOT)), _const_spec((1, SLOT))],
        out_specs=[pl.BlockSpec((tm, w), row)] + [pl.BlockSpec((tm, SLOT_WIDTH), row)] * 2
        + [_vt_spec(tm), pl.BlockSpec((tm, w), row), pl.BlockSpec((tm, 128), row),
           pl.BlockSpec((IDX_HEADS, tm), lambda i: (0, i))],
        out_shape=[jax.ShapeDtypeStruct((n, w), F32), slot_shape, slot_shape, _vt_shape(n),
                   jax.ShapeDtypeStruct((n, w), BF16),
                   jax.ShapeDtypeStruct((n, 128), BF16), jax.ShapeDtypeStruct((IDX_HEADS, n), F32)],
        compiler_params=_cparams(("parallel",)),
        name="mixin_odd",
    )(x, g.reshape(1, d), w_main, w_small, _slot_gain(q_norm), _slot_gain(k_norm))


def _lru_kernel(gate_ref, xr_ref, cw_ref, cb_ref, wa_ref, ba_ref, wx_ref, bx_ref, sp_ref,
                o_ref, xbuf, a_s, u_s, h_s, hc, *, ts):
    j = pl.program_id(1)

    @pl.when(j == 0)
    def _():
        xbuf[0:8, :] = jnp.zeros((8, xbuf.shape[1]), F32)
        hc[...] = jnp.zeros(hc.shape, F32)

    xbuf[8:8 + ts, :] = xr_ref[...]
    xc = cb_ref[...] + cw_ref[0:1, :] * xbuf[5:5 + ts, :]
    for k in range(1, 4):
        xc = xc + cw_ref[k:k + 1, :] * xbuf[5 + k:5 + k + ts, :]
    xbuf[0:8, :] = xbuf[ts:ts + 8, :]

    xcb = xc.astype(BF16)
    r = jax.nn.sigmoid(_dot(xcb, wa_ref[...]) + ba_ref[...])
    ig = jax.nn.sigmoid(_dot(xcb, wx_ref[...]) + bx_ref[...])
    log_a = -LRU_C * r * sp_ref[...]
    a = jnp.exp(log_a)
    a_s[...] = a
    u_s[...] = jnp.sqrt(-jnp.tanh(log_a) * (a * a + 1.0)) * (ig * xc)

    row = lax.broadcasted_iota(jnp.int32, (8, a_s.shape[1]), 0)

    def body(g, carry):
        r0 = pl.multiple_of(g * 8, 8)
        a = a_s[pl.ds(r0, 8), :]
        u = u_s[pl.ds(r0, 8), :]
        for s in (1, 2, 4):
            ok = row >= s
            a_sh = jnp.where(ok, pltpu.roll(a, s, 0), 1.0)
            u_sh = jnp.where(ok, pltpu.roll(u, s, 0), 0.0)
            u = a * u_sh + u
            a = a * a_sh
        h = a * carry + u
        h_s[pl.ds(r0, 8), :] = h
        return h[7:8, :]

    hc[...] = lax.fori_loop(0, ts // 8, body, hc[...], unroll=4)
    o_ref[...] = (h_s[...] * jax.nn.gelu(gate_ref[...])).astype(BF16)


def _block_diag(wb):
    nb, bs, _ = wb.shape
    eye = jnp.eye(nb, dtype=wb.dtype)
    return (eye[:, None, :, None] * wb[:, :, None, :]).reshape(nb * bs, nb * bs)


def _lru(gate, xr, conv_w, conv_b, ra_w, ra_b, ix_w, ix_b, lam, *, bsz, ts=256):
    n, w = xr.shape
    seq = n // bsz
    assert seq % ts == 0
    nt = seq // ts
    row = lambda b, j: (b * nt + j, 0)
    vec = lambda v: v.reshape(1, w).astype(F32)
    return pl.pallas_call(
        functools.partial(_lru_kernel, ts=ts),
        grid=(bsz, nt),
        in_specs=[pl.BlockSpec((ts, w), row), pl.BlockSpec((ts, w), row),
                  _const_spec((conv_w.shape[0], w)), _const_spec((1, w)),
                  _const_spec((w, w)), _const_spec((1, w)), _const_spec((w, w)), _const_spec((1, w)),
                  _const_spec((1, w))],
        out_specs=pl.BlockSpec((ts, w), row),
        out_shape=jax.ShapeDtypeStruct((n, w), BF16),
        scratch_shapes=[pltpu.VMEM((ts + 8, w), F32), pltpu.VMEM((ts, w), F32),
                        pltpu.VMEM((ts, w), F32), pltpu.VMEM((ts, w), F32), pltpu.VMEM((1, w), F32)],
        compiler_params=_cparams(("arbitrary", "arbitrary")),
        name="rg_lru",
    )(gate, xr, conv_w, vec(conv_b), _block_diag(ra_w).astype(BF16), vec(ra_b),
      _block_diag(ix_w).astype(BF16), vec(ix_b), vec(jax.nn.softplus(-lam)))


def _conf_kernel(c_ref, w_ref, b_ref, g_ref, beta_ref, o_ref, cbuf, sh_ref, *, ts, halo, width):
    j = pl.program_id(1)

    @pl.when(j == 0)
    def _():
        cbuf[0:halo, :] = jnp.zeros((halo, cbuf.shape[1]), F32)

    cbuf[halo:halo + ts, :] = c_ref[...]
    base = halo - (width - 1)
    span = sh_ref.shape[1]
    for r in range(1, SUBLANES):
        sh_ref[r - 1] = cbuf[r:r + span, :]

    for c0 in range(0, ts, CONV_ROWS):
        y = jnp.broadcast_to(b_ref[...], (CONV_ROWS, cbuf.shape[1]))
        for k in range(width):
            r = (base + k) % SUBLANES
            u0 = c0 + base + k - r
            win = cbuf[u0:u0 + CONV_ROWS, :] if r == 0 else sh_ref[r - 1, u0:u0 + CONV_ROWS, :]
            y = y + w_ref[k:k + 1, :] * win
        mu = jnp.mean(y, axis=-1, keepdims=True)
        yc = y - mu
        var = jnp.mean(yc * yc, axis=-1, keepdims=True)
        z = yc * lax.rsqrt(var + EPS) * g_ref[...] + beta_ref[...]
        o_ref[c0:c0 + CONV_ROWS, :] = (z * jax.nn.sigmoid(z)).astype(BF16)

    cbuf[0:halo, :] = cbuf[ts:ts + halo, :]


def _conf(c, dw_w, dw_b, ln_g, ln_b, *, bsz, ts=256, halo=32):
    n, w = c.shape
    seq = n // bsz
    width = dw_w.shape[0]
    assert seq % ts == 0 and width - 1 <= halo <= ts and halo % SUBLANES == 0 and ts % CONV_ROWS == 0
    nt = seq // ts
    row = lambda b, j: (b * nt + j, 0)
    vec = lambda v: v.reshape(1, w).astype(F32)
    return pl.pallas_call(
        functools.partial(_conf_kernel, ts=ts, halo=halo, width=width),
        grid=(bsz, nt),
        in_specs=[pl.BlockSpec((ts, w), row), _const_spec((width, w)),
                  _const_spec((1, w)), _const_spec((1, w)), _const_spec((1, w))],
        out_specs=pl.BlockSpec((ts, w), row),
        out_shape=jax.ShapeDtypeStruct((n, w), BF16),
        scratch_shapes=[pltpu.VMEM((ts + halo, w), F32),
                        pltpu.VMEM((SUBLANES - 1, ts + halo - SUBLANES, w), F32)],
        compiler_params=_cparams(("arbitrary", "arbitrary")),
        name="conformer_conv",
    )(c, dw_w, vec(dw_b), vec(ln_g), vec(ln_b))


def _bucket_tiles(t):
    assert t > REL_MAX_DIST
    n = np.arange(2 * t)
    nf = np.maximum(n, 1).astype(np.float32)
    large = REL_MAX_EXACT + (np.log(nf / np.float32(REL_MAX_EXACT))
                             / np.float32(math.log(REL_MAX_DIST / REL_MAX_EXACT))
                             * np.float32(REL_BUCKETS - REL_MAX_EXACT)).astype(np.int32)
    bucket = np.where(n < REL_MAX_EXACT, n, np.minimum(large, REL_BUCKETS - 1)).astype(np.int32)
    qry = np.arange(t)[None, :]
    key = np.arange(t)[:, None]
    return np.stack([bucket[np.maximum(qry - key, 0)], bucket[t + qry - key]])


def _build_bias(idx_ref, rb_ref, bias_ref):
    t = idx_ref.shape[1]
    causal = (lax.broadcasted_iota(jnp.int32, (t, t), 0) <= lax.broadcasted_iota(jnp.int32, (t, t), 1))
    for h in range(N_HEADS):
        far = rb_ref[REL_BUCKETS - 1, h]
        for which in (0, 1):
            idx = idx_ref[which]
            acc = jnp.zeros((t, t), F32)
            for b in range(REL_BUCKETS - 1):
                acc = jnp.where(idx == b, (rb_ref[b, h] - far) * LOG2E, acc)
            if which == 0:
                bias_ref[h, t:2 * t, :] = jnp.where(causal, acc, NEG)
            else:
                bias_ref[h, 0:t, :] = acc


def _attn_init(m_ref, acc_ref):
    m_ref[...] = jnp.full(m_ref.shape, M_INIT, F32)
    acc_ref[...] = jnp.zeros(acc_ref.shape, F32)


def _softmax_step(s, h, vt_h, m_ref, acc_ref):
    m_old = m_ref[h]
    m_new = jnp.maximum(m_old, jnp.max(s, axis=0, keepdims=True))
    p = jnp.exp2(s - m_new)
    m_ref[h] = m_new
    acc_ref[h] = jnp.exp2(m_old - m_new) * acc_ref[h] + _dot(vt_h, p.astype(BF16))


def _heads_pipelined(logits, vt_slot, m_ref, acc_ref):
    ahead = [logits(h) for h in range(HEADS_AHEAD)]
    for h in range(N_HEADS):
        if h + HEADS_AHEAD < N_HEADS:
            ahead.append(logits(h + HEADS_AHEAD))
        _softmax_step(ahead[h], h, vt_slot(h), m_ref, acc_ref)


FAR_PAIR = ("far", "far")
FAR_ONE = ("far",)
NEAR_DIAG = ("near", "diag")
DIAG_ONE = ("diag",)


def _attend_tiles(kt, kinds, q_slot, k_ref, vt_ref, bias_ref, mask_add, m_ref, acc_ref):
    t = ATTN_TILE
    n = len(kinds)
    rows = pl.ds(pl.multiple_of(kt * t, t), n * t)

    def logits(h):
        s = _dot_nt(k_ref[rows, _slot(h)], q_slot(h))
        if kinds == NEAR_DIAG:
            s = s + bias_ref[h]
        elif kinds == DIAG_ONE:
            s = s + bias_ref[h, t:2 * t, :]
        if mask_add is not None:
            s = s + mask_add
        return s

    def vt_slot(h):
        parts = [vt_ref[kt + a, _slot(h), :] for a in range(n)]
        return parts[0] if n == 1 else jnp.concatenate(parts, axis=1)

    _heads_pipelined(logits, vt_slot, m_ref, acc_ref)


def _for_causal_tiles(i, tiles):
    n_far = jnp.maximum(i - 1, 0)

    def pair(j, carry):
        tiles(2 * j, FAR_PAIR)
        return carry

    lax.fori_loop(0, n_far // 2, pair, 0)

    @pl.when(n_far % 2 == 1)
    def _():
        tiles(n_far - 1, FAR_ONE)

    @pl.when(i >= 1)
    def _():
        tiles(i - 1, NEAR_DIAG)

    @pl.when(i == 0)
    def _():
        tiles(i, DIAG_ONE)


def _attn_finish(o_ref, acc_ref):
    parts = []
    for h in range(N_HEADS):
        acc = acc_ref[h]
        parts.append(acc[0:HEAD_DIM] * (1.0 / acc[FEAT0:FEAT0 + 1]))
    o_ref[...] = jnp.concatenate(parts, axis=0).T.astype(BF16)


def _smem_spec():
    return pl.BlockSpec(memory_space=pltpu.SMEM)


def _moba_kernel(rb_ref, q_ref, k_ref, vt_ref, km_ref, idx_ref, o_ref,
                 acc_ref, m_ref, qs_ref, bias_ref, *, n_blocks):
    i = pl.program_id(1)
    t = ATTN_TILE
    nbp = -(-n_blocks // 8) * 8

    @pl.when((pl.program_id(0) == 0) & (i == 0))
    def _():
        _build_bias(idx_ref, rb_ref, bias_ref)

    _attn_init(m_ref, acc_ref)

    blk = lax.broadcasted_iota(jnp.int32, (nbp, t), 0)
    past = blk < i
    for h in range(N_HEADS):
        qh = q_ref[:, _slot(h)]
        km_hi, km_lo = _split_bf16(km_ref[:, _slot(h)])
        gate_t = _dot_nt(km_hi, qh) + _dot_nt(km_lo, qh)
        g = jnp.where(past, gate_t[FEAT0:FEAT0 + nbp, :], NEG)
        rank = jnp.zeros((nbp, t), F32)
        for j in range(n_blocks):
            gj = g[j:j + 1, :]
            beats = (gj > g) | ((gj == g) & (blk > j))
            rank = rank + jnp.where(beats, 1.0, 0.0)
        flag = jnp.where(past & (rank >= MOBA_TOPK), NEG, 0.0)
        flag_t = jnp.concatenate([jnp.zeros((FEAT0, t), F32), flag,
                                  jnp.zeros((SLOT - FEAT0 - nbp, t), F32)], axis=0)
        qs_ref[:, _slot(h)] = (qh.astype(F32) + flag_t.T).astype(BF16)

    def tiles(kt, kinds):
        _attend_tiles(kt, kinds, lambda h: qs_ref[:, _slot(h)], k_ref, vt_ref, bias_ref, None,
                      m_ref, acc_ref)

    _for_causal_tiles(i, tiles)
    _attn_finish(o_ref, acc_ref)


def _moba(q, k, vt, kmean, rel_bias, *, bsz):
    n = q.shape[0]
    seq = n // bsz
    t = ATTN_TILE
    assert seq % t == 0 and t == MOBA_BLOCK
    nq = seq // t
    assert FEAT0 + nq <= SLOT
    km = jnp.pad(kmean.reshape(bsz, nq, SLOT_WIDTH), ((0, 0), (FEAT0, SLOT - FEAT0 - nq), (0, 0)))
    r3 = lambda a: a.reshape(bsz, seq, SLOT_WIDTH)
    seq_spec = pl.BlockSpec((None, seq, SLOT_WIDTH), lambda b, i: (b, 0, 0), pipeline_mode=pl.Buffered(1))
    out = pl.pallas_call(
        functools.partial(_moba_kernel, n_blocks=nq),
        grid=(bsz, nq),
        in_specs=[_smem_spec(),
                  pl.BlockSpec((None, t, SLOT_WIDTH), lambda b, i: (b, i, 0)), seq_spec,
                  pl.BlockSpec((nq, SLOT_WIDTH, t), lambda b, i: (b, 0, 0), pipeline_mode=pl.Buffered(1)),
                  pl.BlockSpec((None, SLOT, SLOT_WIDTH), lambda b, i: (b, 0, 0)),
                  _const_spec((2, t, t))],
        out_specs=pl.BlockSpec((None, t, ATTN_WIDTH), lambda b, i: (b, i, 0)),
        out_shape=jax.ShapeDtypeStruct((bsz, seq, ATTN_WIDTH), BF16),
        scratch_shapes=[pltpu.VMEM((N_HEADS, SLOT, t), F32), pltpu.VMEM((N_HEADS, 1, t), F32),
                        pltpu.VMEM((t, SLOT_WIDTH), BF16),
                        pltpu.VMEM((N_HEADS, 2 * t, t), F32)],
        compiler_params=_cparams(("arbitrary", "arbitrary")),
        name="moba_attention",
    )(rel_bias, r3(q), r3(k), vt, km, jnp.asarray(_bucket_tiles(t)))
    return out.reshape(n, ATTN_WIDTH)


def _dsa_kernel(rb_ref, q_ref, k_ref, vt_ref, qi_ref, ki_ref, wit_ref, idx_ref, tri_ref, o_ref,
                acc_ref, m_ref, isc_ref, bc_ref, bias_ref, *, n_sel):
    i = pl.program_id(1)
    t = ATTN_TILE
    key = lax.broadcasted_iota(jnp.int32, (t, t), 0)
    qry = lax.broadcasted_iota(jnp.int32, (t, t), 1)
    lane128 = lax.broadcasted_iota(jnp.int32, (1, 128), 1)

    @pl.when((pl.program_id(0) == 0) & (i == 0))
    def _():
        _build_bias(idx_ref, rb_ref, bias_ref)

    _attn_init(m_ref, acc_ref)

    def index_tiles(kt, n, diag):
        rows = pl.ds(pl.multiple_of(kt * t, t), n * t)
        ki2 = ki_ref[rows, :]
        acc = jnp.zeros((n * t, t), F32)
        for pr in range(IDX_HEADS // 2):
            q2 = qi_ref[:, pr * 128:(pr + 1) * 128]
            for half in range(2):
                h = 2 * pr + half
                hm = (lane128 >= half * IDX_DIM) & (lane128 < (half + 1) * IDX_DIM)
                s = _dot_nt(ki2, jnp.where(hm, q2, jnp.zeros((), BF16)))
                acc = acc + jnp.maximum(s, 0.0) * wit_ref[h:h + 1, :]
        if diag:
            acc = jnp.where(key <= qry, acc, -jnp.inf)
        for a in range(n):
            isc_ref[kt + a] = acc[a * t:(a + 1) * t]

    def index_pair(j, carry):
        index_tiles(2 * j, 2, False)
        return carry

    lax.fori_loop(0, i // 2, index_pair, 0)

    @pl.when(i % 2 == 1)
    def _():
        index_tiles(i - 1, 1, False)

    index_tiles(i, 1, True)

    def fold8(x, op):
        return op(x.reshape(t // 8, 8, t), axis=0)

    def minmax_body(kt, carry):
        mn, mx = carry
        x = isc_ref[kt]
        mx = jnp.maximum(mx, fold8(x, jnp.max))
        mn = jnp.minimum(mn, fold8(jnp.where(x == -jnp.inf, jnp.inf, x), jnp.min))
        return mn, mx

    mn, mx = lax.fori_loop(0, i + 1, minmax_body,
                           (jnp.full((8, t), jnp.inf, F32), jnp.full((8, t), -jnp.inf, F32)))
    lo0 = jnp.min(mn, axis=0, keepdims=True)
    mx = jnp.max(mx, axis=0, keepdims=True)
    hi0 = mx + jnp.abs(mx) * 1e-3 + 1e-30
    n_valid = (i * t + 1 + lax.broadcasted_iota(jnp.int32, (1, t), 1)).astype(F32)
    want = jnp.minimum(n_valid, float(n_sel))

    def count_ge(thr):
        def one(kt, c):
            return c + fold8(jnp.where(isc_ref[kt] >= thr, 1.0, 0.0), jnp.sum)

        def pair(j, c):
            return one(2 * j + 1, one(2 * j, c))

        c = lax.fori_loop(0, (i + 1) // 2, pair, jnp.zeros((8, t), F32))
        c = lax.cond((i + 1) % 2 == 1, lambda c: one(i, c), lambda c: c, c)
        return jnp.sum(c, axis=0, keepdims=True)

    def bisect_step(_, carry):
        lo, hi, c_lo, c_hi = carry
        mid = 0.5 * (lo + hi)
        c_mid = count_ge(mid)
        up = c_mid >= want
        return (jnp.where(up, mid, lo), jnp.where(up, hi, mid),
                jnp.where(up, c_mid, c_lo), jnp.where(up, c_hi, c_mid))

    def bisect_cond(carry):
        it, _, _, c_lo, _ = carry
        return (it < BISECT_ITERS) & (jnp.max(c_lo - want) > 0.0)

    def bisect_body(carry):
        return (carry[0] + BISECT_CHECK_EVERY,) + lax.fori_loop(0, BISECT_CHECK_EVERY, bisect_step, carry[1:])

    _, lo, hi, c_lo, c_hi = lax.while_loop(bisect_cond, bisect_body,
                                           (0, lo0, hi0, n_valid, jnp.zeros((1, t), F32)))
    need = want - c_hi
    banded = jnp.max(c_lo - want) > 0.0
    bc_ref[...] = jnp.zeros(bc_ref.shape, F32)

    def select_mask(kt):
        x = isc_ref[kt]

        def plain():
            return jnp.where(x >= lo, 0.0, NEG)

        def with_band():
            band = jnp.where((x >= lo) & (x < hi), 1.0, 0.0)
            before = bc_ref[...] + _dot(tri_ref[...], band.astype(BF16))
            bc_ref[...] = bc_ref[...] + jnp.sum(band, axis=0, keepdims=True)
            return jnp.where((x >= hi) | ((band > 0.0) & (before < need)), 0.0, NEG)

        return lax.cond(banded, with_band, plain)

    def tiles(kt, kinds):
        masks = [select_mask(kt + a) for a in range(len(kinds))]
        mask_add = masks[0] if len(masks) == 1 else jnp.concatenate(masks, axis=0)
        _attend_tiles(kt, kinds, lambda h: q_ref[:, _slot(h)], k_ref, vt_ref, bias_ref, mask_add,
                      m_ref, acc_ref)

    _for_causal_tiles(i, tiles)
    _attn_finish(o_ref, acc_ref)


def _dsa(q, k, vt, qi, ki2, wit, rel_bias, *, bsz):
    n = q.shape[0]
    seq = n // bsz
    t = ATTN_TILE
    assert seq % t == 0
    nq = seq // t
    n_sel = min(DSA_TOPK_MAX, seq // 4)
    tri = (jnp.arange(t)[None, :] < jnp.arange(t)[:, None]).astype(BF16)
    r3 = lambda a: a.reshape(bsz, seq, a.shape[-1])
    tile_spec = lambda width: pl.BlockSpec((None, t, width), lambda b, i: (b, i, 0))
    seq_spec = lambda width: pl.BlockSpec((None, seq, width), lambda b, i: (b, 0, 0),
                                          pipeline_mode=pl.Buffered(1))
    out = pl.pallas_call(
        functools.partial(_dsa_kernel, n_sel=n_sel),
        grid=(bsz, nq),
        in_specs=[_smem_spec(), tile_spec(SLOT_WIDTH), seq_spec(SLOT_WIDTH),
                  pl.BlockSpec((nq, SLOT_WIDTH, t), lambda b, i: (b, 0, 0), pipeline_mode=pl.Buffered(1)),
                  tile_spec(ATTN_WIDTH), seq_spec(128),
                  pl.BlockSpec((IDX_HEADS, t), lambda b, i: (0, b * nq + i)),
                  _const_spec((2, t, t)), _const_spec((t, t))],
        out_specs=tile_spec(ATTN_WIDTH),
        out_shape=jax.ShapeDtypeStruct((bsz, seq, ATTN_WIDTH), BF16),
        scratch_shapes=[pltpu.VMEM((N_HEADS, SLOT, t), F32), pltpu.VMEM((N_HEADS, 1, t), F32),
                        pltpu.VMEM((nq, t, t), F32), pltpu.VMEM((1, t), F32),
                        pltpu.VMEM((N_HEADS, 2 * t, t), F32)],
        compiler_params=_cparams(("arbitrary", "arbitrary")),
        name="dsa_attention",
    )(rel_bias, r3(q), r3(k), vt, r3(qi), r3(ki2), wit, jnp.asarray(_bucket_tiles(t)), tri)
    return out.reshape(n, ATTN_WIDTH)


def kernel(x, rel_bias, ffn1_norm, ffn1_w_gate, ffn1_w_up, ffn1_w_down, mix_norm, ffn2_norm, ffn2_w_gate, ffn2_w_up, ffn2_w_down, ev_w_in, ev_conv_w, ev_conv_b, ev_ra_w, ev_ra_b, ev_ix_w, ev_ix_b, ev_lambda, ev_q_norm, ev_k_norm, ev_w_out, od_w_in, od_dw_w, od_dw_b, od_ln_g, od_ln_b, od_q_norm, od_k_norm, od_w_out):
    bsz, seq, d = x.shape
    depth = ffn1_norm.shape[0]
    h = x.reshape(bsz * seq, d)
    for i in range(depth):
        h = _ffn(h, ffn1_norm, ffn1_w_gate, ffn1_w_up, ffn1_w_down, i)
        j = i // 2
        if i % 2 == 0:
            gate, xr, q, k, vt, kmean = _mixin_even(h, mix_norm[i], ev_w_in[j], ev_q_norm[j], ev_k_norm[j],
                                                    n_blocks=seq // MOBA_BLOCK)
            ya = _lru(gate, xr, ev_conv_w[j], ev_conv_b[j], ev_ra_w[j], ev_ra_b[j],
                      ev_ix_w[j], ev_ix_b[j], ev_lambda[j], bsz=bsz)
            yb = _moba(q, k, vt, kmean, rel_bias, bsz=bsz)
            mix = (ya, yb, ev_w_out, j)
        else:
            c, q, k, vt, qi, ki2, wit = _mixin_odd(h, mix_norm[i], od_w_in[j], od_q_norm[j], od_k_norm[j])
            yc = _conf(c, od_dw_w[j], od_dw_b[j], od_ln_g[j], od_ln_b[j], bsz=bsz)
            yd = _dsa(q, k, vt, qi, ki2, wit, rel_bias, bsz=bsz)
            mix = (yc, yd, od_w_out, j)
        h = _ffn(h, ffn2_norm, ffn2_w_gate, ffn2_w_up, ffn2_w_down, i, mix)
    return h.reshape(bsz, seq, d)
```

```python
import functools
import math

import numpy as np
import jax
import jax.numpy as jnp
from jax import lax
from jax.experimental import pallas as pl
from jax.experimental.pallas import tpu as pltpu

F32 = jnp.float32
BF16 = jnp.bfloat16

N_HEADS = 8
HEAD_DIM = 64
ATTN_WIDTH = N_HEADS * HEAD_DIM
LRU_C = 8.0
MOBA_BLOCK = 256
MOBA_TOPK = 3
IDX_HEADS = 8
IDX_DIM = 64
DSA_TOPK_MAX = 256
REL_BUCKETS = 32
REL_MAX_EXACT = REL_BUCKETS // 2
REL_MAX_DIST = 128
EPS = 1e-6
NEG = -1e30
M_INIT = -1e29
ATTN_TILE = 256
SUBLANES = 8
CONV_ROWS = 64
SLOT = 128
SLOT_WIDTH = N_HEADS * SLOT
FEAT0 = HEAD_DIM
HEADS_AHEAD = 4
BISECT_ITERS = 32
BISECT_CHECK_EVERY = 4
F32_TINY = float(np.finfo(np.float32).tiny)
LOG2E = math.log2(math.e)
Q_SCALE = HEAD_DIM ** -0.5 * LOG2E
VMEM_LIMIT = 56 * 1024 * 1024


def _cparams(sem):
    return pltpu.CompilerParams(dimension_semantics=sem, vmem_limit_bytes=VMEM_LIMIT)


def _dot(a, b):
    return jnp.dot(a, b, preferred_element_type=F32)


def _dot_nt(a, b):
    return lax.dot_general(a, b, (((1,), (1,)), ((), ())), preferred_element_type=F32)


def _split_bf16(x):
    hi = x.astype(BF16)
    lo = (x - hi.astype(F32)).astype(BF16)
    return hi, lo


def _rms_rows(x, g):
    return x * lax.rsqrt(jnp.mean(x * x, axis=-1, keepdims=True) + EPS) * g


def _const_spec(shape):
    nd = len(shape)
    return pl.BlockSpec(shape, lambda *_: (0,) * nd, pipeline_mode=pl.Buffered(1))


def _slot(h):
    return slice(h * SLOT, (h + 1) * SLOT)


def _to_slots(x):
    low = lax.broadcasted_iota(jnp.int32, (1, SLOT), 1) < HEAD_DIM
    slots = []
    for p in range(N_HEADS // 2):
        chunk = x[:, p * SLOT:(p + 1) * SLOT]
        slots.append(jnp.where(low, chunk, 0.0))
        slots.append(jnp.where(low, pltpu.roll(chunk, HEAD_DIM, 1), 0.0))
    return slots


def _ffn_kernel(*refs, n_chunks, has_mix):
    if has_mix:
        (x_ref, ya_ref, yb_ref, wo_ref, g_ref, wg_ref, wu_ref, wd_ref, o_ref,
         wg_s, wu_s, wd_s, wo_s) = refs
    else:
        x_ref, g_ref, wg_ref, wu_ref, wd_ref, o_ref, wg_s, wu_s, wd_s = refs
    step = pl.program_id(0)

    @pl.when(step < n_chunks)
    def _():
        wg_s[step] = wg_ref[...].astype(BF16)
        wu_s[step] = wu_ref[...].astype(BF16)
        wd_s[step] = wd_ref[...].astype(BF16)

    if has_mix:
        @pl.when(step == 0)
        def _():
            wo_s[...] = wo_ref[...].astype(BF16)

    @pl.when(step >= n_chunks)
    def _():
        x = x_ref[...]
        if has_mix:
            w = ya_ref.shape[1]
            x = x + _dot(ya_ref[...], wo_s[0:w, :]) + _dot(yb_ref[...], wo_s[w:2 * w, :])
        hn = _rms_rows(x, g_ref[...]).astype(BF16)
        acc = jnp.zeros(x.shape, F32)
        for c in range(n_chunks):
            gt = _dot(hn, wg_s[c])
            ut = _dot(hn, wu_s[c])
            a = (gt * jax.nn.sigmoid(gt) * ut).astype(BF16)
            acc = acc + _dot(a, wd_s[c])
        o_ref[...] = x + 0.5 * acc


def _ffn(x, g, wg, wu, wd, layer, mix=None, *, tm=512, ff_chunk=256):
    n, d = x.shape
    d_ff = wg.shape[2]
    assert n % tm == 0 and d_ff % ff_chunk == 0
    nc = d_ff // ff_chunk
    tile = lambda s: (jnp.maximum(s - nc, 0), 0)
    chunk_col = lambda s: (layer, 0, jnp.minimum(s, nc - 1))
    chunk_row = lambda s: (layer, jnp.minimum(s, nc - 1), 0)
    in_specs = [pl.BlockSpec((tm, d), tile)]
    args = [x]
    scratch = [pltpu.VMEM((nc, d, ff_chunk), BF16), pltpu.VMEM((nc, d, ff_chunk), BF16),
               pltpu.VMEM((nc, ff_chunk, d), BF16)]
    if mix is not None:
        ya, yb, w_out, j = mix
        w = ya.shape[1]
        assert w_out.shape[1:] == (2 * w, d)
        in_specs += [pl.BlockSpec((tm, w), tile), pl.BlockSpec((tm, w), tile),
                     pl.BlockSpec((None, 2 * w, d), lambda s: (j, 0, 0), pipeline_mode=pl.Buffered(1))]
        args += [ya, yb, w_out]
        scratch.append(pltpu.VMEM((2 * w, d), BF16))
    in_specs += [_const_spec((1, d)), pl.BlockSpec((None, d, ff_chunk), chunk_col),
                 pl.BlockSpec((None, d, ff_chunk), chunk_col), pl.BlockSpec((None, ff_chunk, d), chunk_row)]
    args += [g[layer].reshape(1, d), wg, wu, wd]
    return pl.pallas_call(
        functools.partial(_ffn_kernel, n_chunks=nc, has_mix=mix is not None),
        grid=(nc + n // tm,),
        in_specs=in_specs,
        out_specs=pl.BlockSpec((tm, d), tile),
        out_shape=jax.ShapeDtypeStruct((n, d), F32),
        scratch_shapes=scratch,
        compiler_params=_cparams(("arbitrary",)),
        name="ffn_mix" if mix is not None else "ffn",
    )(*args)


def _slot_rms(xs, g):
    ms = jnp.sum(xs * xs, axis=-1, keepdims=True) * (1.0 / HEAD_DIM)
    return xs * lax.rsqrt(ms + EPS) * g


def _store_q_slots(z, gain_ref, q_o):
    for h, zs in enumerate(_to_slots(z)):
        q_o[:, _slot(h)] = (_slot_rms(zs, gain_ref[...]) * Q_SCALE).astype(BF16)


def _store_k_slots(z, gain_ref, k_o, one_lanes):
    slots = [_slot_rms(zs, gain_ref[...]) for zs in _to_slots(z)]
    for h, ks in enumerate(slots):
        k_o[:, _slot(h)] = (ks if one_lanes is None else jnp.where(one_lanes, 1.0, ks)).astype(BF16)
    return slots


def _store_vt_slots(z, v_o):
    lane = lax.broadcasted_iota(jnp.int32, (1, SLOT), 1)
    for h, vs in enumerate(_to_slots(z)):
        vs = jnp.where(lane == FEAT0, 1.0, vs)
        for r in range(vs.shape[0] // ATTN_TILE):
            v_o[r, _slot(h), :] = vs[r * ATTN_TILE:(r + 1) * ATTN_TILE].T.astype(BF16)


def _pipelined(stages):
    nxt = stages[0][0]()
    for j, (_, consume) in enumerate(stages):
        cur = nxt
        if j + 1 < len(stages):
            nxt = stages[j + 1][0]()
        consume(cur)


def _mixin_even_kernel(x_ref, g_ref, w_ref, qn_ref, kn_ref,
                       gate_o, xr_o, q_o, k_o, v_o, km_o, *, n_blocks):
    tm = x_ref.shape[0]
    hn = _rms_rows(x_ref[...], g_ref[...]).astype(BF16)
    w = ATTN_WIDTH
    proj = lambda c: (lambda: _dot(hn, w_ref[:, c * w:(c + 1) * w]))
    blocks_per_tile = tm // MOBA_BLOCK
    row_block = lax.broadcasted_iota(jnp.int32, (tm, 1), 0) // MOBA_BLOCK
    block = (pl.program_id(0) * blocks_per_tile + row_block) % n_blocks
    block_lane = lax.broadcasted_iota(jnp.int32, (1, SLOT), 1) == FEAT0 + block

    def store_k(z):
        for h, ks in enumerate(_store_k_slots(z, kn_ref, k_o, block_lane)):
            for r in range(blocks_per_tile):
                km_o[r, :, _slot(h)] = jnp.mean(ks[r * MOBA_BLOCK:(r + 1) * MOBA_BLOCK], axis=0, keepdims=True)

    def store(ref):
        def consume(z):
            ref[...] = z
        return consume

    _pipelined([(proj(2), lambda z: _store_q_slots(z, qn_ref, q_o)), (proj(3), store_k),
                (proj(4), lambda z: _store_vt_slots(z, v_o)), (proj(0), store(gate_o)), (proj(1), store(xr_o))])


def _vt_spec(tm):
    return pl.BlockSpec((tm // ATTN_TILE, SLOT_WIDTH, ATTN_TILE), lambda i: (i, 0, 0))


def _vt_shape(n):
    return jax.ShapeDtypeStruct((n // ATTN_TILE, SLOT_WIDTH, ATTN_TILE), BF16)


def _slot_gain(g):
    return jnp.pad(g.astype(F32), (0, SLOT - HEAD_DIM)).reshape(1, SLOT)


def _mixin_even(x, g, w_in, q_norm, k_norm, *, n_blocks, tm=512):
    n, d = x.shape
    w = ATTN_WIDTH
    assert n % tm == 0 and tm % MOBA_BLOCK == 0 and tm % ATTN_TILE == 0 and w_in.shape[1] == 5 * w
    assert FEAT0 + n_blocks <= SLOT
    row = lambda i: (i, 0)
    blk = tm // MOBA_BLOCK
    slot_shape = jax.ShapeDtypeStruct((n, SLOT_WIDTH), BF16)
    return pl.pallas_call(
        functools.partial(_mixin_even_kernel, n_blocks=n_blocks),
        grid=(n // tm,),
        in_specs=[pl.BlockSpec((tm, d), row), _const_spec((1, d)), _const_spec((d, 5 * w)),
                  _const_spec((1, SLOT)), _const_spec((1, SLOT))],
        out_specs=[pl.BlockSpec((tm, w), row)] * 2 + [pl.BlockSpec((tm, SLOT_WIDTH), row)] * 2
        + [_vt_spec(tm), pl.BlockSpec((blk, 1, SLOT_WIDTH), lambda i: (i, 0, 0))],
        out_shape=[jax.ShapeDtypeStruct((n, w), F32), jax.ShapeDtypeStruct((n, w), F32),
                   slot_shape, slot_shape, _vt_shape(n),
                   jax.ShapeDtypeStruct((n // MOBA_BLOCK, 1, SLOT_WIDTH), F32)],
        compiler_params=_cparams(("parallel",)),
        name="mixin_even",
    )(x, g.reshape(1, d), w_in.astype(BF16), _slot_gain(q_norm), _slot_gain(k_norm))


def _mixin_odd_kernel(x_ref, g_ref, w_ref, ws_ref, qn_ref, kn_ref,
                      c_o, q_o, k_o, v_o, qi_o, ki_o, wi_o):
    hn = _rms_rows(x_ref[...], g_ref[...]).astype(BF16)
    w = ATTN_WIDTH
    proj = lambda c, n=1: (lambda: _dot(hn, w_ref[:, c * w:(c + n) * w]))

    def store_glu(z):
        c_o[...] = z[:, 0:w] * jax.nn.sigmoid(z[:, w:2 * w])

    def store_qi(z):
        qi_o[...] = z.astype(BF16)

    def store_small(z):
        ki_o[...] = z[:, 0:128].astype(BF16)
        wi = z[:, 128:256] * (IDX_DIM ** -0.5 * IDX_HEADS ** -0.5)
        wi_o[...] = wi.T[0:IDX_HEADS, :]

    _pipelined([(proj(2), lambda z: _store_q_slots(z, qn_ref, q_o)),
                (proj(3), lambda z: _store_k_slots(z, kn_ref, k_o, None)),
                (proj(4), lambda z: _store_vt_slots(z, v_o)), (proj(0, 2), store_glu),
                (proj(5), store_qi), (lambda: _dot(hn, ws_ref[...]), store_small)])


def _mixin_odd(x, g, w_in, q_norm, k_norm, *, tm=512):
    n, d = x.shape
    w = ATTN_WIDTH
    assert n % tm == 0 and tm % ATTN_TILE == 0 and w_in.shape[1] == 6 * w + IDX_DIM + IDX_HEADS
    w_main = w_in[:, :6 * w].astype(BF16)
    w_ki = w_in[:, 6 * w:6 * w + IDX_DIM]
    w_wi = jnp.pad(w_in[:, 6 * w + IDX_DIM:], ((0, 0), (0, 128 - IDX_HEADS)))
    w_small = jnp.concatenate([w_ki, w_ki, w_wi], axis=1).astype(BF16)
    row = lambda i: (i, 0)
    slot_shape = jax.ShapeDtypeStruct((n, SLOT_WIDTH), BF16)
    return pl.pallas_call(
        _mixin_odd_kernel,
        grid=(n // tm,),
        in_specs=[pl.BlockSpec((tm, d), row), _const_spec((1, d)), _const_spec((d, 6 * w)),
                  _const_spec((d, 256)), _const_spec((1, SLOT)), _const_spec((1, SLOT))],
        out_specs=[pl.BlockSpec((tm, w), row)] + [pl.BlockSpec((tm, SLOT_WIDTH), row)] * 2
        + [_vt_spec(tm), pl.BlockSpec((tm, w), row), pl.BlockSpec((tm, 128), row),
           pl.BlockSpec((IDX_HEADS, tm), lambda i: (0, i))],
        out_shape=[jax.ShapeDtypeStruct((n, w), F32), slot_shape, slot_shape, _vt_shape(n),
                   jax.ShapeDtypeStruct((n, w), BF16),
                   jax.ShapeDtypeStruct((n, 128), BF16), jax.ShapeDtypeStruct((IDX_HEADS, n), F32)],
        compiler_params=_cparams(("parallel",)),
        name="mixin_odd",
    )(x, g.reshape(1, d), w_main, w_small, _slot_gain(q_norm), _slot_gain(k_norm))


def _lru_kernel(gate_ref, xr_ref, cw_ref, cb_ref, wa_ref, ba_ref, wx_ref, bx_ref, sp_ref,
                o_ref, xbuf, a_s, u_s, h_s, hc, *, ts):
    j = pl.program_id(1)

    @pl.when(j == 0)
    def _():
        xbuf[0:8, :] = jnp.zeros((8, xbuf.shape[1]), F32)
        hc[...] = jnp.zeros(hc.shape, F32)

    xbuf[8:8 + ts, :] = xr_ref[...]
    xc = cb_ref[...] + cw_ref[0:1, :] * xbuf[5:5 + ts, :]
    for k in range(1, 4):
        xc = xc + cw_ref[k:k + 1, :] * xbuf[5 + k:5 + k + ts, :]
    xbuf[0:8, :] = xbuf[ts:ts + 8, :]

    xcb = xc.astype(BF16)
    r = jax.nn.sigmoid(_dot(xcb, wa_ref[...]) + ba_ref[...])
    ig = jax.nn.sigmoid(_dot(xcb, wx_ref[...]) + bx_ref[...])
    log_a = -LRU_C * r * sp_ref[...]
    a = jnp.exp(log_a)
    a_s[...] = a
    u_s[...] = jnp.sqrt(-jnp.tanh(log_a) * (a * a + 1.0)) * (ig * xc)

    row = lax.broadcasted_iota(jnp.int32, (8, a_s.shape[1]), 0)

    def body(g, carry):
        r0 = pl.multiple_of(g * 8, 8)
        a = a_s[pl.ds(r0, 8), :]
        u = u_s[pl.ds(r0, 8), :]
        for s in (1, 2, 4):
            ok = row >= s
            a_sh = jnp.where(ok, pltpu.roll(a, s, 0), 1.0)
            u_sh = jnp.where(ok, pltpu.roll(u, s, 0), 0.0)
            u = a * u_sh + u
            a = a * a_sh
        h = a * carry + u
        h_s[pl.ds(r0, 8), :] = h
        return h[7:8, :]

    hc[...] = lax.fori_loop(0, ts // 8, body, hc[...], unroll=4)
    o_ref[...] = (h_s[...] * jax.nn.gelu(gate_ref[...])).astype(BF16)


def _block_diag(wb):
    nb, bs, _ = wb.shape
    eye = jnp.eye(nb, dtype=wb.dtype)
    return (eye[:, None, :, None] * wb[:, :, None, :]).reshape(nb * bs, nb * bs)


def _lru(gate, xr, conv_w, conv_b, ra_w, ra_b, ix_w, ix_b, lam, *, bsz, ts=256):
    n, w = xr.shape
    seq = n // bsz
    assert seq % ts == 0
    nt = seq // ts
    row = lambda b, j: (b * nt + j, 0)
    vec = lambda v: v.reshape(1, w).astype(F32)
    return pl.pallas_call(
        functools.partial(_lru_kernel, ts=ts),
        grid=(bsz, nt),
        in_specs=[pl.BlockSpec((ts, w), row), pl.BlockSpec((ts, w), row),
                  _const_spec((conv_w.shape[0], w)), _const_spec((1, w)),
                  _const_spec((w, w)), _const_spec((1, w)), _const_spec((w, w)), _const_spec((1, w)),
                  _const_spec((1, w))],
        out_specs=pl.BlockSpec((ts, w), row),
        out_shape=jax.ShapeDtypeStruct((n, w), BF16),
        scratch_shapes=[pltpu.VMEM((ts + 8, w), F32), pltpu.VMEM((ts, w), F32),
                        pltpu.VMEM((ts, w), F32), pltpu.VMEM((ts, w), F32), pltpu.VMEM((1, w), F32)],
        compiler_params=_cparams(("arbitrary", "arbitrary")),
        name="rg_lru",
    )(gate, xr, conv_w, vec(conv_b), _block_diag(ra_w).astype(BF16), vec(ra_b),
      _block_diag(ix_w).astype(BF16), vec(ix_b), vec(jax.nn.softplus(-lam)))


def _conf_kernel(c_ref, w_ref, b_ref, g_ref, beta_ref, o_ref, cbuf, sh_ref, *, ts, halo, width):
    j = pl.program_id(1)

    @pl.when(j == 0)
    def _():
        cbuf[0:halo, :] = jnp.zeros((halo, cbuf.shape[1]), F32)

    cbuf[halo:halo + ts, :] = c_ref[...]
    base = halo - (width - 1)
    span = sh_ref.shape[1]
    for r in range(1, SUBLANES):
        sh_ref[r - 1] = cbuf[r:r + span, :]

    for c0 in range(0, ts, CONV_ROWS):
        y = jnp.broadcast_to(b_ref[...], (CONV_ROWS, cbuf.shape[1]))
        for k in range(width):
            r = (base + k) % SUBLANES
            u0 = c0 + base + k - r
            win = cbuf[u0:u0 + CONV_ROWS, :] if r == 0 else sh_ref[r - 1, u0:u0 + CONV_ROWS, :]
            y = y + w_ref[k:k + 1, :] * win
        mu = jnp.mean(y, axis=-1, keepdims=True)
        yc = y - mu
        var = jnp.mean(yc * yc, axis=-1, keepdims=True)
        z = yc * lax.rsqrt(var + EPS) * g_ref[...] + beta_ref[...]
        o_ref[c0:c0 + CONV_ROWS, :] = (z * jax.nn.sigmoid(z)).astype(BF16)

    cbuf[0:halo, :] = cbuf[ts:ts + halo, :]


def _conf(c, dw_w, dw_b, ln_g, ln_b, *, bsz, ts=256, halo=32):
    n, w = c.shape
    seq = n // bsz
    width = dw_w.shape[0]
    assert seq % ts == 0 and width - 1 <= halo <= ts and halo % SUBLANES == 0 and ts % CONV_ROWS == 0
    nt = seq // ts
    row = lambda b, j: (b * nt + j, 0)
    vec = lambda v: v.reshape(1, w).astype(F32)
    return pl.pallas_call(
        functools.partial(_conf_kernel, ts=ts, halo=halo, width=width),
        grid=(bsz, nt),
        in_specs=[pl.BlockSpec((ts, w), row), _const_spec((width, w)),
                  _const_spec((1, w)), _const_spec((1, w)), _const_spec((1, w))],
        out_specs=pl.BlockSpec((ts, w), row),
        out_shape=jax.ShapeDtypeStruct((n, w), BF16),
        scratch_shapes=[pltpu.VMEM((ts + halo, w), F32),
                        pltpu.VMEM((SUBLANES - 1, ts + halo - SUBLANES, w), F32)],
        compiler_params=_cparams(("arbitrary", "arbitrary")),
        name="conformer_conv",
    )(c, dw_w, vec(dw_b), vec(ln_g), vec(ln_b))


def _bucket_tiles(t):
    assert t > REL_MAX_DIST
    n = np.arange(2 * t)
    nf = np.maximum(n, 1).astype(np.float32)
    large = REL_MAX_EXACT + (np.log(nf / np.float32(REL_MAX_EXACT))
                             / np.float32(math.log(REL_MAX_DIST / REL_MAX_EXACT))
                             * np.float32(REL_BUCKETS - REL_MAX_EXACT)).astype(np.int32)
    bucket = np.where(n < REL_MAX_EXACT, n, np.minimum(large, REL_BUCKETS - 1)).astype(np.int32)
    qry = np.arange(t)[None, :]
    key = np.arange(t)[:, None]
    return np.stack([bucket[np.maximum(qry - key, 0)], bucket[t + qry - key]])


def _build_bias(idx_ref, rb_ref, bias_ref):
    t = idx_ref.shape[1]
    causal = (lax.broadcasted_iota(jnp.int32, (t, t), 0) <= lax.broadcasted_iota(jnp.int32, (t, t), 1))
    for h in range(N_HEADS):
        far = rb_ref[REL_BUCKETS - 1, h]
        for which in (0, 1):
            idx = idx_ref[which]
            acc = jnp.zeros((t, t), F32)
            for b in range(REL_BUCKETS - 1):
                acc = jnp.where(idx == b, (rb_ref[b, h] - far) * LOG2E, acc)
            if which == 0:
                bias_ref[h, t:2 * t, :] = jnp.where(causal, acc, NEG)
            else:
                bias_ref[h, 0:t, :] = acc


def _attn_init(m_ref, acc_ref):
    m_ref[...] = jnp.full(m_ref.shape, M_INIT, F32)
    acc_ref[...] = jnp.zeros(acc_ref.shape, F32)


def _softmax_step(s, h, vt_h, m_ref, acc_ref):
    m_old = m_ref[h]
    m_new = jnp.maximum(m_old, jnp.max(s, axis=0, keepdims=True))
    p = jnp.exp2(s - m_new)
    m_ref[h] = m_new
    acc_ref[h] = jnp.exp2(m_old - m_new) * acc_ref[h] + _dot(vt_h, p.astype(BF16))


def _heads_pipelined(logits, vt_slot, m_ref, acc_ref):
    ahead = [logits(h) for h in range(HEADS_AHEAD)]
    for h in range(N_HEADS):
        if h + HEADS_AHEAD < N_HEADS:
            ahead.append(logits(h + HEADS_AHEAD))
        _softmax_step(ahead[h], h, vt_slot(h), m_ref, acc_ref)


FAR_PAIR = ("far", "far")
FAR_ONE = ("far",)
NEAR_DIAG = ("near", "diag")
DIAG_ONE = ("diag",)


def _attend_tiles(kt, kinds, q_slot, k_ref, vt_ref, bias_ref, mask_add, m_ref, acc_ref):
    t = ATTN_TILE
    n = len(kinds)
    rows = pl.ds(pl.multiple_of(kt * t, t), n * t)

    def logits(h):
        s = _dot_nt(k_ref[rows, _slot(h)], q_slot(h))
        if kinds == NEAR_DIAG:
            s = s + bias_ref[h]
        elif kinds == DIAG_ONE:
            s = s + bias_ref[h, t:2 * t, :]
        if mask_add is not None:
            s = s + mask_add
        return s

    def vt_slot(h):
        parts = [vt_ref[kt + a, _slot(h), :] for a in range(n)]
        return parts[0] if n == 1 else jnp.concatenate(parts, axis=1)

    _heads_pipelined(logits, vt_slot, m_ref, acc_ref)


def _for_causal_tiles(i, tiles):
    n_far = jnp.maximum(i - 1, 0)

    def pair(j, carry):
        tiles(2 * j, FAR_PAIR)
        return carry

    lax.fori_loop(0, n_far // 2, pair, 0)

    @pl.when(n_far % 2 == 1)
    def _():
        tiles(n_far - 1, FAR_ONE)

    @pl.when(i >= 1)
    def _():
        tiles(i - 1, NEAR_DIAG)

    @pl.when(i == 0)
    def _():
        tiles(i, DIAG_ONE)


def _attn_finish(o_ref, acc_ref):
    parts = []
    for h in range(N_HEADS):
        acc = acc_ref[h]
        parts.append(acc[0:HEAD_DIM] * (1.0 / acc[FEAT0:FEAT0 + 1]))
    o_ref[...] = jnp.concatenate(parts, axis=0).T.astype(BF16)


def _smem_spec():
    return pl.BlockSpec(memory_space=pltpu.SMEM)


def _moba_kernel(rb_ref, q_ref, k_ref, vt_ref, km_ref, idx_ref, o_ref,
                 acc_ref, m_ref, qs_ref, bias_ref, *, n_blocks):
    i = pl.program_id(1)
    t = ATTN_TILE
    nbp = -(-n_blocks // 8) * 8

    @pl.when((pl.program_id(0) == 0) & (i == 0))
    def _():
        _build_bias(idx_ref, rb_ref, bias_ref)

    _attn_init(m_ref, acc_ref)

    blk = lax.broadcasted_iota(jnp.int32, (nbp, t), 0)
    past = blk < i
    for h in range(N_HEADS):
        qh = q_ref[:, _slot(h)]
        km_hi, km_lo = _split_bf16(km_ref[:, _slot(h)])
        gate_t = _dot_nt(km_hi, qh) + _dot_nt(km_lo, qh)
        g = jnp.where(past, gate_t[FEAT0:FEAT0 + nbp, :], NEG)
        rank = jnp.zeros((nbp, t), F32)
        for j in range(n_blocks):
            gj = g[j:j + 1, :]
            beats = (gj > g) | ((gj == g) & (blk > j))
            rank = rank + jnp.where(beats, 1.0, 0.0)
        flag = jnp.where(past & (rank >= MOBA_TOPK), NEG, 0.0)
        flag_t = jnp.concatenate([jnp.zeros((FEAT0, t), F32), flag,
                                  jnp.zeros((SLOT - FEAT0 - nbp, t), F32)], axis=0)
        qs_ref[:, _slot(h)] = (qh.astype(F32) + flag_t.T).astype(BF16)

    def tiles(kt, kinds):
        _attend_tiles(kt, kinds, lambda h: qs_ref[:, _slot(h)], k_ref, vt_ref, bias_ref, None,
                      m_ref, acc_ref)

    _for_causal_tiles(i, tiles)
    _attn_finish(o_ref, acc_ref)


def _moba(q, k, vt, kmean, rel_bias, *, bsz):
    n = q.shape[0]
    seq = n // bsz
    t = ATTN_TILE
    assert seq % t == 0 and t == MOBA_BLOCK
    nq = seq // t
    assert FEAT0 + nq <= SLOT
    km = jnp.pad(kmean.reshape(bsz, nq, SLOT_WIDTH), ((0, 0), (FEAT0, SLOT - FEAT0 - nq), (0, 0)))
    r3 = lambda a: a.reshape(bsz, seq, SLOT_WIDTH)
    seq_spec = pl.BlockSpec((None, seq, SLOT_WIDTH), lambda b, i: (b, 0, 0), pipeline_mode=pl.Buffered(1))
    out = pl.pallas_call(
        functools.partial(_moba_kernel, n_blocks=nq),
        grid=(bsz, nq),
        in_specs=[_smem_spec(),
                  pl.BlockSpec((None, t, SLOT_WIDTH), lambda b, i: (b, i, 0)), seq_spec,
                  pl.BlockSpec((nq, SLOT_WIDTH, t), lambda b, i: (b, 0, 0), pipeline_mode=pl.Buffered(1)),
                  pl.BlockSpec((None, SLOT, SLOT_WIDTH), lambda b, i: (b, 0, 0)),
                  _const_spec((2, t, t))],
        out_specs=pl.BlockSpec((None, t, ATTN_WIDTH), lambda b, i: (b, i, 0)),
        out_shape=jax.ShapeDtypeStruct((bsz, seq, ATTN_WIDTH), BF16),
        scratch_shapes=[pltpu.VMEM((N_HEADS, SLOT, t), F32), pltpu.VMEM((N_HEADS, 1, t), F32),
                        pltpu.VMEM((t, SLOT_WIDTH), BF16),
                        pltpu.VMEM((N_HEADS, 2 * t, t), F32)],
        compiler_params=_cparams(("arbitrary", "arbitrary")),
        name="moba_attention",
    )(rel_bias, r3(q), r3(k), vt, km, jnp.asarray(_bucket_tiles(t)))
    return out.reshape(n, ATTN_WIDTH)


def _dsa_kernel(rb_ref, q_ref, k_ref, vt_ref, qi_ref, ki_ref, wit_ref, idx_ref, tri_ref, o_ref,
                acc_ref, m_ref, isc_ref, bc_ref, bias_ref, *, n_sel):
    i = pl.program_id(1)
    t = ATTN_TILE
    key = lax.broadcasted_iota(jnp.int32, (t, t), 0)
    qry = lax.broadcasted_iota(jnp.int32, (t, t), 1)
    lane128 = lax.broadcasted_iota(jnp.int32, (1, 128), 1)

    @pl.when((pl.program_id(0) == 0) & (i == 0))
    def _():
        _build_bias(idx_ref, rb_ref, bias_ref)

    _attn_init(m_ref, acc_ref)

    def index_tiles(kt, n, diag):
        rows = pl.ds(pl.multiple_of(kt * t, t), n * t)
        ki2 = ki_ref[rows, :]
        acc = jnp.zeros((n * t, t), F32)
        for pr in range(IDX_HEADS // 2):
            q2 = qi_ref[:, pr * 128:(pr + 1) * 128]
            for half in range(2):
                h = 2 * pr + half
                hm = (lane128 >= half * IDX_DIM) & (lane128 < (half + 1) * IDX_DIM)
                s = _dot_nt(ki2, jnp.where(hm, q2, jnp.zeros((), BF16)))
                acc = acc + jnp.maximum(s, 0.0) * wit_ref[h:h + 1, :]
        if diag:
            acc = jnp.where(key <= qry, acc, -jnp.inf)
        for a in range(n):
            isc_ref[kt + a] = acc[a * t:(a + 1) * t]

    def index_pair(j, carry):
        index_tiles(2 * j, 2, False)
        return carry

    lax.fori_loop(0, i // 2, index_pair, 0)

    @pl.when(i % 2 == 1)
    def _():
        index_tiles(i - 1, 1, False)

    index_tiles(i, 1, True)

    def fold8(x, op):
        return op(x.reshape(t // 8, 8, t), axis=0)

    def minmax_body(kt, carry):
        mn, mx = carry
        x = isc_ref[kt]
        mx = jnp.maximum(mx, fold8(x, jnp.max))
        mn = jnp.minimum(mn, fold8(jnp.where(x == -jnp.inf, jnp.inf, x), jnp.min))
        return mn, mx

    mn, mx = lax.fori_loop(0, i + 1, minmax_body,
                           (jnp.full((8, t), jnp.inf, F32), jnp.full((8, t), -jnp.inf, F32)))
    lo0 = jnp.min(mn, axis=0, keepdims=True)
    mx = jnp.max(mx, axis=0, keepdims=True)
    hi0 = mx + jnp.abs(mx) * 1e-3 + 1e-30
    n_valid = (i * t + 1 + lax.broadcasted_iota(jnp.int32, (1, t), 1)).astype(F32)
    want = jnp.minimum(n_valid, float(n_sel))

    def count_ge(thr):
        def one(kt, c):
            return c + fold8(jnp.where(isc_ref[kt] >= thr, 1.0, 0.0), jnp.sum)

        def pair(j, c):
            return one(2 * j + 1, one(2 * j, c))

        c = lax.fori_loop(0, (i + 1) // 2, pair, jnp.zeros((8, t), F32))
        c = lax.cond((i + 1) % 2 == 1, lambda c: one(i, c), lambda c: c, c)
        return jnp.sum(c, axis=0, keepdims=True)

    c_nonneg = count_ge(0.0)
    c_pos = count_ge(F32_TINY)
    above = want <= c_pos
    zero_tie = (want > c_pos) & (want <= c_nonneg)
    lo1 = jnp.where(above, F32_TINY, jnp.where(zero_tie, 0.0, lo0))
    hi1 = jnp.where(above, hi0, jnp.where(zero_tie, F32_TINY, 0.0))
    c_lo1 = jnp.where(above, c_pos, jnp.where(zero_tie, c_nonneg, n_valid))
    c_hi1 = jnp.where(above, 0.0, jnp.where(zero_tie, c_pos, c_nonneg))

    def bisect_step(_, carry):
        lo, hi, c_lo, c_hi = carry
        mid = 0.5 * (lo + hi)
        c_mid = count_ge(mid)
        up = c_mid >= want
        return (jnp.where(up, mid, lo), jnp.where(up, hi, mid),
                jnp.where(up, c_mid, c_lo), jnp.where(up, c_hi, c_mid))

    def bisect_cond(carry):
        it, _, _, c_lo, _ = carry
        return (it < BISECT_ITERS) & (jnp.max(jnp.where(zero_tie, 0.0, c_lo - want)) > 0.0)

    def bisect_body(carry):
        return (carry[0] + BISECT_CHECK_EVERY,) + lax.fori_loop(0, BISECT_CHECK_EVERY, bisect_step, carry[1:])

    _, lo, hi, c_lo, c_hi = lax.while_loop(bisect_cond, bisect_body, (0, lo1, hi1, c_lo1, c_hi1))
    need = want - c_hi
    banded = jnp.max(c_lo - want) > 0.0
    bc_ref[...] = jnp.zeros(bc_ref.shape, F32)

    def select_mask(kt):
        x = isc_ref[kt]

        def plain():
            return jnp.where(x >= lo, 0.0, NEG)

        def with_band():
            band = jnp.where((x >= lo) & (x < hi), 1.0, 0.0)
            before = bc_ref[...] + _dot(tri_ref[...], band.astype(BF16))
            bc_ref[...] = bc_ref[...] + jnp.sum(band, axis=0, keepdims=True)
            return jnp.where((x >= hi) | ((band > 0.0) & (before < need)), 0.0, NEG)

        return lax.cond(banded, with_band, plain)

    def tiles(kt, kinds):
        masks = [select_mask(kt + a) for a in range(len(kinds))]
        mask_add = masks[0] if len(masks) == 1 else jnp.concatenate(masks, axis=0)
        _attend_tiles(kt, kinds, lambda h: q_ref[:, _slot(h)], k_ref, vt_ref, bias_ref, mask_add,
                      m_ref, acc_ref)

    _for_causal_tiles(i, tiles)
    _attn_finish(o_ref, acc_ref)


def _dsa(q, k, vt, qi, ki2, wit, rel_bias, *, bsz):
    n = q.shape[0]
    seq = n // bsz
    t = ATTN_TILE
    assert seq % t == 0
    nq = seq // t
    n_sel = min(DSA_TOPK_MAX, seq // 4)
    tri = (jnp.arange(t)[None, :] < jnp.arange(t)[:, None]).astype(BF16)
    r3 = lambda a: a.reshape(bsz, seq, a.shape[-1])
    tile_spec = lambda width: pl.BlockSpec((None, t, width), lambda b, i: (b, i, 0))
    seq_spec = lambda width: pl.BlockSpec((None, seq, width), lambda b, i: (b, 0, 0),
                                          pipeline_mode=pl.Buffered(1))
    out = pl.pallas_call(
        functools.partial(_dsa_kernel, n_sel=n_sel),
        grid=(bsz, nq),
        in_specs=[_smem_spec(), tile_spec(SLOT_WIDTH), seq_spec(SLOT_WIDTH),
                  pl.BlockSpec((nq, SLOT_WIDTH, t), lambda b, i: (b, 0, 0), pipeline_mode=pl.Buffered(1)),
                  tile_spec(ATTN_WIDTH), seq_spec(128),
                  pl.BlockSpec((IDX_HEADS, t), lambda b, i: (0, b * nq + i)),
                  _const_spec((2, t, t)), _const_spec((t, t))],
        out_specs=tile_spec(ATTN_WIDTH),
        out_shape=jax.ShapeDtypeStruct((bsz, seq, ATTN_WIDTH), BF16),
        scratch_shapes=[pltpu.VMEM((N_HEADS, SLOT, t), F32), pltpu.VMEM((N_HEADS, 1, t), F32),
                        pltpu.VMEM((nq, t, t), F32), pltpu.VMEM((1, t), F32),
                        pltpu.VMEM((N_HEADS, 2 * t, t), F32)],
        compiler_params=_cparams(("arbitrary", "arbitrary")),
        name="dsa_attention",
    )(rel_bias, r3(q), r3(k), vt, r3(qi), r3(ki2), wit, jnp.asarray(_bucket_tiles(t)), tri)
    return out.reshape(n, ATTN_WIDTH)


def kernel(x, rel_bias, ffn1_norm, ffn1_w_gate, ffn1_w_up, ffn1_w_down, mix_norm, ffn2_norm, ffn2_w_gate, ffn2_w_up, ffn2_w_down, ev_w_in, ev_conv_w, ev_conv_b, ev_ra_w, ev_ra_b, ev_ix_w, ev_ix_b, ev_lambda, ev_q_norm, ev_k_norm, ev_w_out, od_w_in, od_dw_w, od_dw_b, od_ln_g, od_ln_b, od_q_norm, od_k_norm, od_w_out):
    bsz, seq, d = x.shape
    depth = ffn1_norm.shape[0]
    h = x.reshape(bsz * seq, d)
    for i in range(depth):
        h = _ffn(h, ffn1_norm, ffn1_w_gate, ffn1_w_up, ffn1_w_down, i)
        j = i // 2
        if i % 2 == 0:
            gate, xr, q, k, vt, kmean = _mixin_even(h, mix_norm[i], ev_w_in[j], ev_q_norm[j], ev_k_norm[j],
                                                    n_blocks=seq // MOBA_BLOCK)
            ya = _lru(gate, xr, ev_conv_w[j], ev_conv_b[j], ev_ra_w[j], ev_ra_b[j],
                      ev_ix_w[j], ev_ix_b[j], ev_lambda[j], bsz=bsz)
            yb = _moba(q, k, vt, kmean, rel_bias, bsz=bsz)
            mix = (ya, yb, ev_w_out, j)
        else:
            c, q, k, vt, qi, ki2, wit = _mixin_odd(h, mix_norm[i], od_w_in[j], od_q_norm[j], od_k_norm[j])
            yc = _conf(c, od_dw_w[j], od_dw_b[j], od_ln_g[j], od_ln_b[j], bsz=bsz)
            yd = _dsa(q, k, vt, qi, ki2, wit, rel_bias, bsz=bsz)
            mix = (yc, yd, od_w_out, j)
        h = _ffn(h, ffn2_norm, ffn2_w_gate, ffn2_w_up, ffn2_w_down, i, mix)
    return h.reshape(bsz, seq, d)
```

```python
import functools
import math

import numpy as np
import jax
import jax.numpy as jnp
from jax import lax
from jax.experimental import pallas as pl
from jax.experimental.pallas import tpu as pltpu

F32 = jnp.float32
BF16 = jnp.bfloat16

N_HEADS = 8
HEAD_DIM = 64
ATTN_WIDTH = N_HEADS * HEAD_DIM
LRU_C = 8.0
MOBA_BLOCK = 256
MOBA_TOPK = 3
IDX_HEADS = 8
IDX_DIM = 64
DSA_TOPK_MAX = 256
REL_BUCKETS = 32
REL_MAX_EXACT = REL_BUCKETS // 2
REL_MAX_DIST = 128
EPS = 1e-6
NEG = -1e30
M_INIT = -1e29
ATTN_TILE = 256
SUBLANES = 8
CONV_ROWS = 64
SLOT = 128
SLOT_WIDTH = N_HEADS * SLOT
FEAT0 = HEAD_DIM
VT_ROWS = 80
HEADS_AHEAD = 4
BISECT_ITERS = 32
BISECT_CHECK_EVERY = 4
F32_TINY = float(np.finfo(np.float32).tiny)
LOG2E = math.log2(math.e)
Q_SCALE = HEAD_DIM ** -0.5 * LOG2E
VMEM_LIMIT = 56 * 1024 * 1024


def _cparams(sem):
    return pltpu.CompilerParams(dimension_semantics=sem, vmem_limit_bytes=VMEM_LIMIT)


def _dot(a, b):
    return jnp.dot(a, b, preferred_element_type=F32)


def _dot_nt(a, b):
    return lax.dot_general(a, b, (((1,), (1,)), ((), ())), preferred_element_type=F32)


def _split_bf16(x):
    hi = x.astype(BF16)
    lo = (x - hi.astype(F32)).astype(BF16)
    return hi, lo


def _rms_rows(x, g):
    return x * lax.rsqrt(jnp.mean(x * x, axis=-1, keepdims=True) + EPS) * g


def _const_spec(shape):
    nd = len(shape)
    return pl.BlockSpec(shape, lambda *_: (0,) * nd, pipeline_mode=pl.Buffered(1))


def _slot(h):
    return slice(h * SLOT, (h + 1) * SLOT)


def _vt_rows(h):
    return slice(h * VT_ROWS, (h + 1) * VT_ROWS)


def _to_slots(x):
    low = lax.broadcasted_iota(jnp.int32, (1, SLOT), 1) < HEAD_DIM
    slots = []
    for p in range(N_HEADS // 2):
        chunk = x[:, p * SLOT:(p + 1) * SLOT]
        slots.append(jnp.where(low, chunk, 0.0))
        slots.append(jnp.where(low, pltpu.roll(chunk, HEAD_DIM, 1), 0.0))
    return slots


def _ffn_kernel(*refs, n_chunks, has_mix):
    if has_mix:
        (x_ref, ya_ref, yb_ref, wo_ref, g_ref, wg_ref, wu_ref, wd_ref, o_ref,
         wg_s, wu_s, wd_s, wo_s) = refs
    else:
        x_ref, g_ref, wg_ref, wu_ref, wd_ref, o_ref, wg_s, wu_s, wd_s = refs
    step = pl.program_id(0)

    @pl.when(step < n_chunks)
    def _():
        wg_s[step] = wg_ref[...].astype(BF16)
        wu_s[step] = wu_ref[...].astype(BF16)
        wd_s[step] = wd_ref[...].astype(BF16)

    if has_mix:
        @pl.when(step == 0)
        def _():
            wo_s[...] = wo_ref[...].astype(BF16)

    @pl.when(step >= n_chunks)
    def _():
        x = x_ref[...]
        if has_mix:
            w = ya_ref.shape[1]
            x = x + _dot(ya_ref[...], wo_s[0:w, :]) + _dot(yb_ref[...], wo_s[w:2 * w, :])
        hn = _rms_rows(x, g_ref[...]).astype(BF16)
        acc = jnp.zeros(x.shape, F32)
        for c in range(n_chunks):
            gt = _dot(hn, wg_s[c])
            ut = _dot(hn, wu_s[c])
            a = (gt * jax.nn.sigmoid(gt) * ut).astype(BF16)
            acc = acc + _dot(a, wd_s[c])
        o_ref[...] = x + 0.5 * acc


def _ffn(x, g, wg, wu, wd, layer, mix=None, *, tm=512, ff_chunk=256):
    n, d = x.shape
    d_ff = wg.shape[2]
    assert n % tm == 0 and d_ff % ff_chunk == 0
    nc = d_ff // ff_chunk
    tile = lambda s: (jnp.maximum(s - nc, 0), 0)
    chunk_col = lambda s: (layer, 0, jnp.minimum(s, nc - 1))
    chunk_row = lambda s: (layer, jnp.minimum(s, nc - 1), 0)
    in_specs = [pl.BlockSpec((tm, d), tile)]
    args = [x]
    scratch = [pltpu.VMEM((nc, d, ff_chunk), BF16), pltpu.VMEM((nc, d, ff_chunk), BF16),
               pltpu.VMEM((nc, ff_chunk, d), BF16)]
    if mix is not None:
        ya, yb, w_out, j = mix
        w = ya.shape[1]
        assert w_out.shape[1:] == (2 * w, d)
        in_specs += [pl.BlockSpec((tm, w), tile), pl.BlockSpec((tm, w), tile),
                     pl.BlockSpec((None, 2 * w, d), lambda s: (j, 0, 0), pipeline_mode=pl.Buffered(1))]
        args += [ya, yb, w_out]
        scratch.append(pltpu.VMEM((2 * w, d), BF16))
    in_specs += [_const_spec((1, d)), pl.BlockSpec((None, d, ff_chunk), chunk_col),
                 pl.BlockSpec((None, d, ff_chunk), chunk_col), pl.BlockSpec((None, ff_chunk, d), chunk_row)]
    args += [g[layer].reshape(1, d), wg, wu, wd]
    return pl.pallas_call(
        functools.partial(_ffn_kernel, n_chunks=nc, has_mix=mix is not None),
        grid=(nc + n // tm,),
        in_specs=in_specs,
        out_specs=pl.BlockSpec((tm, d), tile),
        out_shape=jax.ShapeDtypeStruct((n, d), F32),
        scratch_shapes=scratch,
        compiler_params=_cparams(("arbitrary",)),
        name="ffn_mix" if mix is not None else "ffn",
    )(*args)


def _slot_rms(xs, g):
    ms = jnp.sum(xs * xs, axis=-1, keepdims=True) * (1.0 / HEAD_DIM)
    return xs * lax.rsqrt(ms + EPS) * g


def _store_q_slots(z, gain_ref, q_o):
    for h, zs in enumerate(_to_slots(z)):
        q_o[:, _slot(h)] = (_slot_rms(zs, gain_ref[...]) * Q_SCALE).astype(BF16)


def _store_k_slots(z, gain_ref, k_o, one_lanes):
    slots = [_slot_rms(zs, gain_ref[...]) for zs in _to_slots(z)]
    for h, ks in enumerate(slots):
        k_o[:, _slot(h)] = (ks if one_lanes is None else jnp.where(one_lanes, 1.0, ks)).astype(BF16)
    return slots


def _store_vt_slots(z, v_o):
    lane = lax.broadcasted_iota(jnp.int32, (1, SLOT), 1)
    for h, vs in enumerate(_to_slots(z)):
        vs = jnp.where(lane == FEAT0, 1.0, vs)
        for r in range(vs.shape[0] // ATTN_TILE):
            vt = vs[r * ATTN_TILE:(r + 1) * ATTN_TILE].T
            v_o[r, _vt_rows(h), :] = vt[0:VT_ROWS].astype(BF16)


def _pipelined(stages):
    nxt = stages[0][0]()
    for j, (_, consume) in enumerate(stages):
        cur = nxt
        if j + 1 < len(stages):
            nxt = stages[j + 1][0]()
        consume(cur)


def _mixin_even_kernel(x_ref, g_ref, w_ref, qn_ref, kn_ref,
                       gate_o, xr_o, q_o, k_o, v_o, km_o, *, n_blocks):
    tm = x_ref.shape[0]
    hn = _rms_rows(x_ref[...], g_ref[...]).astype(BF16)
    w = ATTN_WIDTH
    proj = lambda c: (lambda: _dot(hn, w_ref[:, c * w:(c + 1) * w]))
    blocks_per_tile = tm // MOBA_BLOCK
    row_block = lax.broadcasted_iota(jnp.int32, (tm, 1), 0) // MOBA_BLOCK
    block = (pl.program_id(0) * blocks_per_tile + row_block) % n_blocks
    block_lane = lax.broadcasted_iota(jnp.int32, (1, SLOT), 1) == FEAT0 + block

    def store_k(z):
        for h, ks in enumerate(_store_k_slots(z, kn_ref, k_o, block_lane)):
            for r in range(blocks_per_tile):
                km_o[r, :, _slot(h)] = jnp.mean(ks[r * MOBA_BLOCK:(r + 1) * MOBA_BLOCK], axis=0, keepdims=True)

    def store(ref):
        def consume(z):
            ref[...] = z
        return consume

    _pipelined([(proj(2), lambda z: _store_q_slots(z, qn_ref, q_o)), (proj(3), store_k),
                (proj(4), lambda z: _store_vt_slots(z, v_o)), (proj(0), store(gate_o)), (proj(1), store(xr_o))])


def _vt_spec(tm):
    return pl.BlockSpec((tm // ATTN_TILE, N_HEADS * VT_ROWS, ATTN_TILE), lambda i: (i, 0, 0))


def _vt_shape(n):
    return jax.ShapeDtypeStruct((n // ATTN_TILE, N_HEADS * VT_ROWS, ATTN_TILE), BF16)


def _slot_gain(g):
    return jnp.pad(g.astype(F32), (0, SLOT - HEAD_DIM)).reshape(1, SLOT)


def _mixin_even(x, g, w_in, q_norm, k_norm, *, n_blocks, tm=512):
    n, d = x.shape
    w = ATTN_WIDTH
    assert n % tm == 0 and tm % MOBA_BLOCK == 0 and tm % ATTN_TILE == 0 and w_in.shape[1] == 5 * w
    assert FEAT0 + n_blocks <= SLOT
    row = lambda i: (i, 0)
    blk = tm // MOBA_BLOCK
    slot_shape = jax.ShapeDtypeStruct((n, SLOT_WIDTH), BF16)
    return pl.pallas_call(
        functools.partial(_mixin_even_kernel, n_blocks=n_blocks),
        grid=(n // tm,),
        in_specs=[pl.BlockSpec((tm, d), row), _const_spec((1, d)), _const_spec((d, 5 * w)),
                  _const_spec((1, SLOT)), _const_spec((1, SLOT))],
        out_specs=[pl.BlockSpec((tm, w), row)] * 2 + [pl.BlockSpec((tm, SLOT_WIDTH), row)] * 2
        + [_vt_spec(tm), pl.BlockSpec((blk, 1, SLOT_WIDTH), lambda i: (i, 0, 0))],
        out_shape=[jax.ShapeDtypeStruct((n, w), F32), jax.ShapeDtypeStruct((n, w), F32),
                   slot_shape, slot_shape, _vt_shape(n),
                   jax.ShapeDtypeStruct((n // MOBA_BLOCK, 1, SLOT_WIDTH), F32)],
        compiler_params=_cparams(("parallel",)),
        name="mixin_even",
    )(x, g.reshape(1, d), w_in.astype(BF16), _slot_gain(q_norm), _slot_gain(k_norm))


def _mixin_odd_kernel(x_ref, g_ref, w_ref, ws_ref, qn_ref, kn_ref,
                      c_o, q_o, k_o, v_o, qi_o, ki_o, wi_o):
    hn = _rms_rows(x_ref[...], g_ref[...]).astype(BF16)
    w = ATTN_WIDTH
    proj = lambda c, n=1: (lambda: _dot(hn, w_ref[:, c * w:(c + n) * w]))

    def store_glu(z):
        c_o[...] = z[:, 0:w] * jax.nn.sigmoid(z[:, w:2 * w])

    def store_qi(z):
        qi_o[...] = z.astype(BF16)

    def store_small(z):
        ki_o[...] = z[:, 0:128].astype(BF16)
        wi = z[:, 128:256] * (IDX_DIM ** -0.5 * IDX_HEADS ** -0.5)
        wi_o[...] = wi.T[0:IDX_HEADS, :]

    _pipelined([(proj(2), lambda z: _store_q_slots(z, qn_ref, q_o)),
                (proj(3), lambda z: _store_k_slots(z, kn_ref, k_o, None)),
                (proj(4), lambda z: _store_vt_slots(z, v_o)), (proj(0, 2), store_glu),
                (proj(5), store_qi), (lambda: _dot(hn, ws_ref[...]), store_small)])


def _mixin_odd(x, g, w_in, q_norm, k_norm, *, tm=512):
    n, d = x.shape
    w = ATTN_WIDTH
    assert n % tm == 0 and tm % ATTN_TILE == 0 and w_in.shape[1] == 6 * w + IDX_DIM + IDX_HEADS
    w_main = w_in[:, :6 * w].astype(BF16)
    w_ki = w_in[:, 6 * w:6 * w + IDX_DIM]
    w_wi = jnp.pad(w_in[:, 6 * w + IDX_DIM:], ((0, 0), (0, 128 - IDX_HEADS)))
    w_small = jnp.concatenate([w_ki, w_ki, w_wi], axis=1).astype(BF16)
    row = lambda i: (i, 0)
    slot_shape = jax.ShapeDtypeStruct((n, SLOT_WIDTH), BF16)
    return pl.pallas_call(
        _mixin_odd_kernel,
        grid=(n // tm,),
        in_specs=[pl.BlockSpec((tm, d), row), _const_spec((1, d)), _const_spec((d, 6 * w)),
                  _const_spec((d, 256)), _const_spec((1, SLOT)), _const_spec((1, SLOT))],
        out_specs=[pl.BlockSpec((tm, w), row)] + [pl.BlockSpec((tm, SLOT_WIDTH), row)] * 2
        + [_vt_spec(tm), pl.BlockSpec((tm, w), row), pl.BlockSpec((tm, 128), row),
           pl.BlockSpec((IDX_HEADS, tm), lambda i: (0, i))],
        out_shape=[jax.ShapeDtypeStruct((n, w), F32), slot_shape, slot_shape, _vt_shape(n),
                   jax.ShapeDtypeStruct((n, w), BF16),
                   jax.ShapeDtypeStruct((n, 128), BF16), jax.ShapeDtypeStruct((IDX_HEADS, n), F32)],
        compiler_params=_cparams(("parallel",)),
        name="mixin_odd",
    )(x, g.reshape(1, d), w_main, w_small, _slot_gain(q_norm), _slot_gain(k_norm))


def _lru_kernel(gate_ref, xr_ref, cw_ref, cb_ref, wa_ref, ba_ref, wx_ref, bx_ref, sp_ref,
                o_ref, xbuf, a_s, u_s, h_s, hc, *, ts):
    j = pl.program_id(1)

    @pl.when(j == 0)
    def _():
        xbuf[0:8, :] = jnp.zeros((8, xbuf.shape[1]), F32)
        hc[...] = jnp.zeros(hc.shape, F32)

    xbuf[8:8 + ts, :] = xr_ref[...]
    xc = cb_ref[...] + cw_ref[0:1, :] * xbuf[5:5 + ts, :]
    for k in range(1, 4):
        xc = xc + cw_ref[k:k + 1, :] * xbuf[5 + k:5 + k + ts, :]
    xbuf[0:8, :] = xbuf[ts:ts + 8, :]

    xcb = xc.astype(BF16)
    r = jax.nn.sigmoid(_dot(xcb, wa_ref[...]) + ba_ref[...])
    ig = jax.nn.sigmoid(_dot(xcb, wx_ref[...]) + bx_ref[...])
    log_a = -LRU_C * r * sp_ref[...]
    a = jnp.exp(log_a)
    a_s[...] = a
    u_s[...] = jnp.sqrt(-jnp.tanh(log_a) * (a * a + 1.0)) * (ig * xc)

    row = lax.broadcasted_iota(jnp.int32, (8, a_s.shape[1]), 0)

    def body(g, carry):
        r0 = pl.multiple_of(g * 8, 8)
        a = a_s[pl.ds(r0, 8), :]
        u = u_s[pl.ds(r0, 8), :]
        for s in (1, 2, 4):
            ok = row >= s
            a_sh = jnp.where(ok, pltpu.roll(a, s, 0), 1.0)
            u_sh = jnp.where(ok, pltpu.roll(u, s, 0), 0.0)
            u = a * u_sh + u
            a = a * a_sh
        h = a * carry + u
        h_s[pl.ds(r0, 8), :] = h
        return h[7:8, :]

    hc[...] = lax.fori_loop(0, ts // 8, body, hc[...], unroll=4)
    o_ref[...] = (h_s[...] * jax.nn.gelu(gate_ref[...])).astype(BF16)


def _block_diag(wb):
    nb, bs, _ = wb.shape
    eye = jnp.eye(nb, dtype=wb.dtype)
    return (eye[:, None, :, None] * wb[:, :, None, :]).reshape(nb * bs, nb * bs)


def _lru(gate, xr, conv_w, conv_b, ra_w, ra_b, ix_w, ix_b, lam, *, bsz, ts=256):
    n, w = xr.shape
    seq = n // bsz
    assert seq % ts == 0
    nt = seq // ts
    row = lambda b, j: (b * nt + j, 0)
    vec = lambda v: v.reshape(1, w).astype(F32)
    return pl.pallas_call(
        functools.partial(_lru_kernel, ts=ts),
        grid=(bsz, nt),
        in_specs=[pl.BlockSpec((ts, w), row), pl.BlockSpec((ts, w), row),
                  _const_spec((conv_w.shape[0], w)), _const_spec((1, w)),
                  _const_spec((w, w)), _const_spec((1, w)), _const_spec((w, w)), _const_spec((1, w)),
                  _const_spec((1, w))],
        out_specs=pl.BlockSpec((ts, w), row),
        out_shape=jax.ShapeDtypeStruct((n, w), BF16),
        scratch_shapes=[pltpu.VMEM((ts + 8, w), F32), pltpu.VMEM((ts, w), F32),
                        pltpu.VMEM((ts, w), F32), pltpu.VMEM((ts, w), F32), pltpu.VMEM((1, w), F32)],
        compiler_params=_cparams(("arbitrary", "arbitrary")),
        name="rg_lru",
    )(gate, xr, conv_w, vec(conv_b), _block_diag(ra_w).astype(BF16), vec(ra_b),
      _block_diag(ix_w).astype(BF16), vec(ix_b), vec(jax.nn.softplus(-lam)))


def _conf_kernel(c_ref, w_ref, b_ref, g_ref, beta_ref, o_ref, cbuf, sh_ref, *, ts, halo, width):
    j = pl.program_id(1)

    @pl.when(j == 0)
    def _():
        cbuf[0:halo, :] = jnp.zeros((halo, cbuf.shape[1]), F32)

    cbuf[halo:halo + ts, :] = c_ref[...]
    base = halo - (width - 1)
    span = sh_ref.shape[1]
    for r in range(1, SUBLANES):
        sh_ref[r - 1] = cbuf[r:r + span, :]

    for c0 in range(0, ts, CONV_ROWS):
        y = jnp.broadcast_to(b_ref[...], (CONV_ROWS, cbuf.shape[1]))
        for k in range(width):
            r = (base + k) % SUBLANES
            u0 = c0 + base + k - r
            win = cbuf[u0:u0 + CONV_ROWS, :] if r == 0 else sh_ref[r - 1, u0:u0 + CONV_ROWS, :]
            y = y + w_ref[k:k + 1, :] * win
        mu = jnp.mean(y, axis=-1, keepdims=True)
        yc = y - mu
        var = jnp.mean(yc * yc, axis=-1, keepdims=True)
        z = yc * lax.rsqrt(var + EPS) * g_ref[...] + beta_ref[...]
        o_ref[c0:c0 + CONV_ROWS, :] = (z * jax.nn.sigmoid(z)).astype(BF16)

    cbuf[0:halo, :] = cbuf[ts:ts + halo, :]


def _conf(c, dw_w, dw_b, ln_g, ln_b, *, bsz, ts=256, halo=32):
    n, w = c.shape
    seq = n // bsz
    width = dw_w.shape[0]
    assert seq % ts == 0 and width - 1 <= halo <= ts and halo % SUBLANES == 0 and ts % CONV_ROWS == 0
    nt = seq // ts
    row = lambda b, j: (b * nt + j, 0)
    vec = lambda v: v.reshape(1, w).astype(F32)
    return pl.pallas_call(
        functools.partial(_conf_kernel, ts=ts, halo=halo, width=width),
        grid=(bsz, nt),
        in_specs=[pl.BlockSpec((ts, w), row), _const_spec((width, w)),
                  _const_spec((1, w)), _const_spec((1, w)), _const_spec((1, w))],
        out_specs=pl.BlockSpec((ts, w), row),
        out_shape=jax.ShapeDtypeStruct((n, w), BF16),
        scratch_shapes=[pltpu.VMEM((ts + halo, w), F32),
                        pltpu.VMEM((SUBLANES - 1, ts + halo - SUBLANES, w), F32)],
        compiler_params=_cparams(("arbitrary", "arbitrary")),
        name="conformer_conv",
    )(c, dw_w, vec(dw_b), vec(ln_g), vec(ln_b))


def _bucket_tiles(t):
    assert t > REL_MAX_DIST
    n = np.arange(2 * t)
    nf = np.maximum(n, 1).astype(np.float32)
    large = REL_MAX_EXACT + (np.log(nf / np.float32(REL_MAX_EXACT))
                             / np.float32(math.log(REL_MAX_DIST / REL_MAX_EXACT))
                             * np.float32(REL_BUCKETS - REL_MAX_EXACT)).astype(np.int32)
    bucket = np.where(n < REL_MAX_EXACT, n, np.minimum(large, REL_BUCKETS - 1)).astype(np.int32)
    qry = np.arange(t)[None, :]
    key = np.arange(t)[:, None]
    return np.stack([bucket[np.maximum(qry - key, 0)], bucket[t + qry - key]])


def _build_bias(idx_ref, rb_ref, bias_ref):
    t = idx_ref.shape[1]
    causal = (lax.broadcasted_iota(jnp.int32, (t, t), 0) <= lax.broadcasted_iota(jnp.int32, (t, t), 1))
    for h in range(N_HEADS):
        far = rb_ref[REL_BUCKETS - 1, h]
        for which in (0, 1):
            idx = idx_ref[which]
            acc = jnp.zeros((t, t), F32)
            for b in range(REL_BUCKETS - 1):
                acc = jnp.where(idx == b, (rb_ref[b, h] - far) * LOG2E, acc)
            if which == 0:
                bias_ref[h, t:2 * t, :] = jnp.where(causal, acc, NEG)
            else:
                bias_ref[h, 0:t, :] = acc


def _attn_init(m_ref, acc_ref):
    m_ref[...] = jnp.full(m_ref.shape, M_INIT, F32)
    acc_ref[...] = jnp.zeros(acc_ref.shape, F32)


def _softmax_step(s, h, vt_h, m_ref, acc_ref):
    m_old = m_ref[h]
    m_new = jnp.maximum(m_old, jnp.max(s, axis=0, keepdims=True))
    p = jnp.exp2(s - m_new)
    m_ref[h] = m_new
    acc_ref[h] = jnp.exp2(m_old - m_new) * acc_ref[h] + _dot(vt_h, p.astype(BF16))


def _heads_pipelined(logits, vt_slot, m_ref, acc_ref):
    ahead = [logits(h) for h in range(HEADS_AHEAD)]
    for h in range(N_HEADS):
        if h + HEADS_AHEAD < N_HEADS:
            ahead.append(logits(h + HEADS_AHEAD))
        _softmax_step(ahead[h], h, vt_slot(h), m_ref, acc_ref)


FAR_PAIR = ("far", "far")
FAR_ONE = ("far",)
NEAR_DIAG = ("near", "diag")
DIAG_ONE = ("diag",)


def _attend_tiles(kt, kinds, q_slot, k_ref, vt_ref, bias_ref, mask_add, m_ref, acc_ref):
    t = ATTN_TILE
    n = len(kinds)
    rows = pl.ds(pl.multiple_of(kt * t, t), n * t)

    def logits(h):
        s = _dot_nt(k_ref[rows, _slot(h)], q_slot(h))
        if kinds == NEAR_DIAG:
            s = s + bias_ref[h]
        elif kinds == DIAG_ONE:
            s = s + bias_ref[h, t:2 * t, :]
        if mask_add is not None:
            s = s + mask_add
        return s

    def vt_slot(h):
        parts = [vt_ref[kt + a, _vt_rows(h), :] for a in range(n)]
        return parts[0] if n == 1 else jnp.concatenate(parts, axis=1)

    _heads_pipelined(logits, vt_slot, m_ref, acc_ref)


def _for_causal_tiles(i, tiles):
    n_far = jnp.maximum(i - 1, 0)

    def pair(j, carry):
        tiles(2 * j, FAR_PAIR)
        return carry

    lax.fori_loop(0, n_far // 2, pair, 0)

    @pl.when(n_far % 2 == 1)
    def _():
        tiles(n_far - 1, FAR_ONE)

    @pl.when(i >= 1)
    def _():
        tiles(i - 1, NEAR_DIAG)

    @pl.when(i == 0)
    def _():
        tiles(i, DIAG_ONE)


def _attn_finish(o_ref, acc_ref):
    parts = []
    for h in range(N_HEADS):
        acc = acc_ref[h]
        parts.append(acc[0:HEAD_DIM] * (1.0 / acc[FEAT0:FEAT0 + 1]))
    o_ref[...] = jnp.concatenate(parts, axis=0).T.astype(BF16)


def _smem_spec():
    return pl.BlockSpec(memory_space=pltpu.SMEM)


def _moba_kernel(rb_ref, q_ref, k_ref, vt_ref, km_ref, idx_ref, o_ref,
                 acc_ref, m_ref, qs_ref, bias_ref, *, n_blocks):
    i = pl.program_id(1)
    t = ATTN_TILE
    nbp = -(-n_blocks // 8) * 8

    @pl.when((pl.program_id(0) == 0) & (i == 0))
    def _():
        _build_bias(idx_ref, rb_ref, bias_ref)

    _attn_init(m_ref, acc_ref)

    blk = lax.broadcasted_iota(jnp.int32, (nbp, t), 0)
    past = blk < i
    for h in range(N_HEADS):
        qh = q_ref[:, _slot(h)]
        km_hi, km_lo = _split_bf16(km_ref[:, _slot(h)])
        gate_t = _dot_nt(km_hi, qh) + _dot_nt(km_lo, qh)
        g = jnp.where(past, gate_t[FEAT0:FEAT0 + nbp, :], NEG)
        rank = jnp.zeros((nbp, t), F32)
        for j in range(n_blocks):
            gj = g[j:j + 1, :]
            beats = (gj > g) | ((gj == g) & (blk > j))
            rank = rank + jnp.where(beats, 1.0, 0.0)
        flag = jnp.where(past & (rank >= MOBA_TOPK), NEG, 0.0)
        flag_t = jnp.concatenate([jnp.zeros((FEAT0, t), F32), flag,
                                  jnp.zeros((SLOT - FEAT0 - nbp, t), F32)], axis=0)
        qs_ref[:, _slot(h)] = (qh.astype(F32) + flag_t.T).astype(BF16)

    def tiles(kt, kinds):
        _attend_tiles(kt, kinds, lambda h: qs_ref[:, _slot(h)], k_ref, vt_ref, bias_ref, None,
                      m_ref, acc_ref)

    _for_causal_tiles(i, tiles)
    _attn_finish(o_ref, acc_ref)


def _moba(q, k, vt, kmean, rel_bias, *, bsz):
    n = q.shape[0]
    seq = n // bsz
    t = ATTN_TILE
    assert seq % t == 0 and t == MOBA_BLOCK
    nq = seq // t
    assert FEAT0 + nq <= SLOT
    km = jnp.pad(kmean.reshape(bsz, nq, SLOT_WIDTH), ((0, 0), (FEAT0, SLOT - FEAT0 - nq), (0, 0)))
    r3 = lambda a: a.reshape(bsz, seq, SLOT_WIDTH)
    seq_spec = pl.BlockSpec((None, seq, SLOT_WIDTH), lambda b, i: (b, 0, 0))
    out = pl.pallas_call(
        functools.partial(_moba_kernel, n_blocks=nq),
        grid=(bsz, nq),
        in_specs=[_smem_spec(),
                  pl.BlockSpec((None, t, SLOT_WIDTH), lambda b, i: (b, i, 0)), seq_spec,
                  pl.BlockSpec((nq, N_HEADS * VT_ROWS, t), lambda b, i: (b, 0, 0)),
                  pl.BlockSpec((None, SLOT, SLOT_WIDTH), lambda b, i: (b, 0, 0)),
                  _const_spec((2, t, t))],
        out_specs=pl.BlockSpec((None, t, ATTN_WIDTH), lambda b, i: (b, i, 0)),
        out_shape=jax.ShapeDtypeStruct((bsz, seq, ATTN_WIDTH), BF16),
        scratch_shapes=[pltpu.VMEM((N_HEADS, VT_ROWS, t), F32), pltpu.VMEM((N_HEADS, 1, t), F32),
                        pltpu.VMEM((t, SLOT_WIDTH), BF16),
                        pltpu.VMEM((N_HEADS, 2 * t, t), F32)],
        compiler_params=_cparams(("arbitrary", "arbitrary")),
        name="moba_attention",
    )(rel_bias, r3(q), r3(k), vt, km, jnp.asarray(_bucket_tiles(t)))
    return out.reshape(n, ATTN_WIDTH)


def _dsa_kernel(rb_ref, q_ref, k_ref, vt_ref, qi_ref, ki_ref, wit_ref, idx_ref, tri_ref, o_ref,
                acc_ref, m_ref, isc_ref, bc_ref, bias_ref, *, n_sel):
    i = pl.program_id(1)
    t = ATTN_TILE
    key = lax.broadcasted_iota(jnp.int32, (t, t), 0)
    qry = lax.broadcasted_iota(jnp.int32, (t, t), 1)
    lane128 = lax.broadcasted_iota(jnp.int32, (1, 128), 1)

    @pl.when((pl.program_id(0) == 0) & (i == 0))
    def _():
        _build_bias(idx_ref, rb_ref, bias_ref)

    _attn_init(m_ref, acc_ref)

    def index_tiles(kt, n, diag):
        rows = pl.ds(pl.multiple_of(kt * t, t), n * t)
        ki2 = ki_ref[rows, :]
        acc = jnp.zeros((n * t, t), F32)
        for pr in range(IDX_HEADS // 2):
            q2 = qi_ref[:, pr * 128:(pr + 1) * 128]
            for half in range(2):
                h = 2 * pr + half
                hm = (lane128 >= half * IDX_DIM) & (lane128 < (half + 1) * IDX_DIM)
                s = _dot_nt(ki2, jnp.where(hm, q2, jnp.zeros((), BF16)))
                acc = acc + jnp.maximum(s, 0.0) * wit_ref[h:h + 1, :]
        if diag:
            acc = jnp.where(key <= qry, acc, -jnp.inf)
        for a in range(n):
            isc_ref[kt + a] = acc[a * t:(a + 1) * t]

    def index_pair(j, carry):
        index_tiles(2 * j, 2, False)
        return carry

    lax.fori_loop(0, i // 2, index_pair, 0)

    @pl.when(i % 2 == 1)
    def _():
        index_tiles(i - 1, 1, False)

    index_tiles(i, 1, True)

    def fold8(x, op):
        return op(x.reshape(t // 8, 8, t), axis=0)

    def minmax_body(kt, carry):
        mn, mx = carry
        x = isc_ref[kt]
        mx = jnp.maximum(mx, fold8(x, jnp.max))
        mn = jnp.minimum(mn, fold8(jnp.where(x == -jnp.inf, jnp.inf, x), jnp.min))
        return mn, mx

    mn, mx = lax.fori_loop(0, i + 1, minmax_body,
                           (jnp.full((8, t), jnp.inf, F32), jnp.full((8, t), -jnp.inf, F32)))
    lo0 = jnp.min(mn, axis=0, keepdims=True)
    mx = jnp.max(mx, axis=0, keepdims=True)
    hi0 = mx + jnp.abs(mx) * 1e-3 + 1e-30
    n_valid = (i * t + 1 + lax.broadcasted_iota(jnp.int32, (1, t), 1)).astype(F32)
    want = jnp.minimum(n_valid, float(n_sel))

    def count_ge(thr):
        def one(kt, c):
            return c + fold8(jnp.where(isc_ref[kt] >= thr, 1.0, 0.0), jnp.sum)

        def pair(j, c):
            return one(2 * j + 1, one(2 * j, c))

        c = lax.fori_loop(0, (i + 1) // 2, pair, jnp.zeros((8, t), F32))
        c = lax.cond((i + 1) % 2 == 1, lambda c: one(i, c), lambda c: c, c)
        return jnp.sum(c, axis=0, keepdims=True)

    c_nonneg = count_ge(0.0)
    c_pos = count_ge(F32_TINY)
    above = want <= c_pos
    zero_tie = (want > c_pos) & (want <= c_nonneg)
    lo1 = jnp.where(above, F32_TINY, jnp.where(zero_tie, 0.0, lo0))
    hi1 = jnp.where(above, hi0, jnp.where(zero_tie, F32_TINY, 0.0))
    c_lo1 = jnp.where(above, c_pos, jnp.where(zero_tie, c_nonneg, n_valid))
    c_hi1 = jnp.where(above, 0.0, jnp.where(zero_tie, c_pos, c_nonneg))

    def bisect_step(_, carry):
        lo, hi, c_lo, c_hi = carry
        mid = 0.5 * (lo + hi)
        c_mid = count_ge(mid)
        up = c_mid >= want
        return (jnp.where(up, mid, lo), jnp.where(up, hi, mid),
                jnp.where(up, c_mid, c_lo), jnp.where(up, c_hi, c_mid))

    def bisect_cond(carry):
        it, _, _, c_lo, _ = carry
        return (it < BISECT_ITERS) & (jnp.max(jnp.where(zero_tie, 0.0, c_lo - want)) > 0.0)

    def bisect_body(carry):
        return (carry[0] + BISECT_CHECK_EVERY,) + lax.fori_loop(0, BISECT_CHECK_EVERY, bisect_step, carry[1:])

    _, lo, hi, c_lo, c_hi = lax.while_loop(bisect_cond, bisect_body, (0, lo1, hi1, c_lo1, c_hi1))
    need = want - c_hi
    banded = jnp.max(c_lo - want) > 0.0
    bc_ref[...] = jnp.zeros(bc_ref.shape, F32)

    def select_mask(kt):
        x = isc_ref[kt]

        def plain():
            return jnp.where(x >= lo, 0.0, NEG)

        def with_band():
            band = jnp.where((x >= lo) & (x < hi), 1.0, 0.0)
            before = bc_ref[...] + _dot(tri_ref[...], band.astype(BF16))
            bc_ref[...] = bc_ref[...] + jnp.sum(band, axis=0, keepdims=True)
            return jnp.where((x >= hi) | ((band > 0.0) & (before < need)), 0.0, NEG)

        return lax.cond(banded, with_band, plain)

    def tiles(kt, kinds):
        masks = [select_mask(kt + a) for a in range(len(kinds))]
        mask_add = masks[0] if len(masks) == 1 else jnp.concatenate(masks, axis=0)
        _attend_tiles(kt, kinds, lambda h: q_ref[:, _slot(h)], k_ref, vt_ref, bias_ref, mask_add,
                      m_ref, acc_ref)

    _for_causal_tiles(i, tiles)
    _attn_finish(o_ref, acc_ref)


def _dsa(q, k, vt, qi, ki2, wit, rel_bias, *, bsz):
    n = q.shape[0]
    seq = n // bsz
    t = ATTN_TILE
    assert seq % t == 0
    nq = seq // t
    n_sel = min(DSA_TOPK_MAX, seq // 4)
    tri = (jnp.arange(t)[None, :] < jnp.arange(t)[:, None]).astype(BF16)
    r3 = lambda a: a.reshape(bsz, seq, a.shape[-1])
    tile_spec = lambda width: pl.BlockSpec((None, t, width), lambda b, i: (b, i, 0))
    seq_spec = lambda width: pl.BlockSpec((None, seq, width), lambda b, i: (b, 0, 0))
    out = pl.pallas_call(
        functools.partial(_dsa_kernel, n_sel=n_sel),
        grid=(bsz, nq),
        in_specs=[_smem_spec(), tile_spec(SLOT_WIDTH), seq_spec(SLOT_WIDTH),
                  pl.BlockSpec((nq, N_HEADS * VT_ROWS, t), lambda b, i: (b, 0, 0)),
                  tile_spec(ATTN_WIDTH), seq_spec(128),
                  pl.BlockSpec((IDX_HEADS, t), lambda b, i: (0, b * nq + i)),
                  _const_spec((2, t, t)), _const_spec((t, t))],
        out_specs=tile_spec(ATTN_WIDTH),
        out_shape=jax.ShapeDtypeStruct((bsz, seq, ATTN_WIDTH), BF16),
        scratch_shapes=[pltpu.VMEM((N_HEADS, VT_ROWS, t), F32), pltpu.VMEM((N_HEADS, 1, t), F32),
                        pltpu.VMEM((nq, t, t), F32), pltpu.VMEM((1, t), F32),
                        pltpu.VMEM((N_HEADS, 2 * t, t), F32)],
        compiler_params=_cparams(("arbitrary", "arbitrary")),
        name="dsa_attention",
    )(rel_bias, r3(q), r3(k), vt, r3(qi), r3(ki2), wit, jnp.asarray(_bucket_tiles(t)), tri)
    return out.reshape(n, ATTN_WIDTH)


def kernel(x, rel_bias, ffn1_norm, ffn1_w_gate, ffn1_w_up, ffn1_w_down, mix_norm, ffn2_norm, ffn2_w_gate, ffn2_w_up, ffn2_w_down, ev_w_in, ev_conv_w, ev_conv_b, ev_ra_w, ev_ra_b, ev_ix_w, ev_ix_b, ev_lambda, ev_q_norm, ev_k_norm, ev_w_out, od_w_in, od_dw_w, od_dw_b, od_ln_g, od_ln_b, od_q_norm, od_k_norm, od_w_out):
    bsz, seq, d = x.shape
    depth = ffn1_norm.shape[0]
    h = x.reshape(bsz * seq, d)
    for i in range(depth):
        h = _ffn(h, ffn1_norm, ffn1_w_gate, ffn1_w_up, ffn1_w_down, i)
        j = i // 2
        if i % 2 == 0:
            gate, xr, q, k, vt, kmean = _mixin_even(h, mix_norm[i], ev_w_in[j], ev_q_norm[j], ev_k_norm[j],
                                                    n_blocks=seq // MOBA_BLOCK)
            ya = _lru(gate, xr, ev_conv_w[j], ev_conv_b[j], ev_ra_w[j], ev_ra_b[j],
                      ev_ix_w[j], ev_ix_b[j], ev_lambda[j], bsz=bsz)
            yb = _moba(q, k, vt, kmean, rel_bias, bsz=bsz)
            mix = (ya, yb, ev_w_out, j)
        else:
            c, q, k, vt, qi, ki2, wit = _mixin_odd(h, mix_norm[i], od_w_in[j], od_q_norm[j], od_k_norm[j])
            yc = _conf(c, od_dw_w[j], od_dw_b[j], od_ln_g[j], od_ln_b[j], bsz=bsz)
            yd = _dsa(q, k, vt, qi, ki2, wit, rel_bias, bsz=bsz)
            mix = (yc, yd, od_w_out, j)
        h = _ffn(h, ffn2_norm, ffn2_w_gate, ffn2_w_up, ffn2_w_down, i, mix)
    return h.reshape(bsz, seq, d)
```

```python
import functools
import math

import numpy as np
import jax
import jax.numpy as jnp
from jax import lax
from jax.experimental import pallas as pl
from jax.experimental.pallas import tpu as pltpu

F32 = jnp.float32
BF16 = jnp.bfloat16

N_HEADS = 8
HEAD_DIM = 64
ATTN_WIDTH = N_HEADS * HEAD_DIM
LRU_C = 8.0
MOBA_BLOCK = 256
MOBA_TOPK = 3
IDX_HEADS = 8
IDX_DIM = 64
DSA_TOPK_MAX = 256
REL_BUCKETS = 32
REL_MAX_EXACT = REL_BUCKETS // 2
REL_MAX_DIST = 128
EPS = 1e-6
NEG = -1e30
M_INIT = -1e29
ATTN_TILE = 256
SUBLANES = 8
CONV_ROWS = 64
SLOT = 128
SLOT_WIDTH = N_HEADS * SLOT
FEAT0 = HEAD_DIM
VT_ROWS = 80
HEADS_AHEAD = 4
BISECT_ITERS = 32
BISECT_CHECK_EVERY = 4
F32_TINY = float(np.finfo(np.float32).tiny)
LOG2E = math.log2(math.e)
Q_SCALE = HEAD_DIM ** -0.5 * LOG2E
VMEM_LIMIT = 56 * 1024 * 1024
FFN_VMEM_LIMIT = 60 * 1024 * 1024


def _cparams(sem):
    return pltpu.CompilerParams(dimension_semantics=sem, vmem_limit_bytes=VMEM_LIMIT)


def _dot(a, b):
    return jnp.dot(a, b, preferred_element_type=F32)


def _dot_nt(a, b):
    return lax.dot_general(a, b, (((1,), (1,)), ((), ())), preferred_element_type=F32)


def _split_bf16(x):
    hi = x.astype(BF16)
    lo = (x - hi.astype(F32)).astype(BF16)
    return hi, lo


def _rms_rows(x, g):
    return x * lax.rsqrt(jnp.mean(x * x, axis=-1, keepdims=True) + EPS) * g


def _const_spec(shape):
    nd = len(shape)
    return pl.BlockSpec(shape, lambda *_: (0,) * nd, pipeline_mode=pl.Buffered(1))


def _slot(h):
    return slice(h * SLOT, (h + 1) * SLOT)


def _vt_rows(h):
    return slice(h * VT_ROWS, (h + 1) * VT_ROWS)


def _to_slots(x):
    low = lax.broadcasted_iota(jnp.int32, (1, SLOT), 1) < HEAD_DIM
    slots = []
    for p in range(N_HEADS // 2):
        chunk = x[:, p * SLOT:(p + 1) * SLOT]
        slots.append(jnp.where(low, chunk, 0.0))
        slots.append(jnp.where(low, pltpu.roll(chunk, HEAD_DIM, 1), 0.0))
    return slots


def _ffn_kernel(*refs, n_chunks, has_mix):
    if has_mix:
        x_ref, ya_ref, yb_ref, wo_ref, g_ref, wg_ref, wu_ref, wd_ref, o_ref, wg_s, wu_s, wd_s = refs
    else:
        x_ref, g_ref, wg_ref, wu_ref, wd_ref, o_ref, wg_s, wu_s, wd_s = refs
    step = pl.program_id(0)

    @pl.when(step < n_chunks)
    def _():
        wg_s[step] = wg_ref[...].astype(BF16)
        wu_s[step] = wu_ref[...].astype(BF16)
        wd_s[step] = wd_ref[...].astype(BF16)

    @pl.when(step >= n_chunks)
    def _():
        x = x_ref[...]
        if has_mix:
            w = ya_ref.shape[1]
            x = x + _dot(ya_ref[...], wo_ref[0:w, :]) + _dot(yb_ref[...], wo_ref[w:2 * w, :])
        hn = _rms_rows(x, g_ref[...]).astype(BF16)
        acc = jnp.zeros(x.shape, F32)
        for c in range(n_chunks):
            gt = _dot(hn, wg_s[c])
            ut = _dot(hn, wu_s[c])
            a = (gt * jax.nn.sigmoid(gt) * ut).astype(BF16)
            acc = acc + _dot(a, wd_s[c])
        o_ref[...] = x + 0.5 * acc


def _ffn(x, g, wg, wu, wd, layer, mix=None, *, tm=1024, ff_chunk=256):
    n, d = x.shape
    d_ff = wg.shape[2]
    assert n % tm == 0 and d_ff % ff_chunk == 0
    nc = d_ff // ff_chunk
    tile = lambda s: (jnp.maximum(s - nc, 0), 0)
    chunk_col = lambda s: (layer, 0, jnp.minimum(s, nc - 1))
    chunk_row = lambda s: (layer, jnp.minimum(s, nc - 1), 0)
    in_specs = [pl.BlockSpec((tm, d), tile)]
    args = [x]
    scratch = [pltpu.VMEM((nc, d, ff_chunk), BF16), pltpu.VMEM((nc, d, ff_chunk), BF16),
               pltpu.VMEM((nc, ff_chunk, d), BF16)]
    if mix is not None:
        ya, yb, w_out = mix
        w = ya.shape[1]
        assert w_out.shape == (2 * w, d)
        in_specs += [pl.BlockSpec((tm, w), tile), pl.BlockSpec((tm, w), tile), _const_spec((2 * w, d))]
        args += [ya, yb, w_out.astype(BF16)]
    stage = lambda shape, index: pl.BlockSpec(shape, index, pipeline_mode=pl.Buffered(1))
    in_specs += [_const_spec((1, d)), stage((None, d, ff_chunk), chunk_col),
                 stage((None, d, ff_chunk), chunk_col), stage((None, ff_chunk, d), chunk_row)]
    args += [g[layer].reshape(1, d), wg, wu, wd]
    return pl.pallas_call(
        functools.partial(_ffn_kernel, n_chunks=nc, has_mix=mix is not None),
        grid=(nc + n // tm,),
        in_specs=in_specs,
        out_specs=pl.BlockSpec((tm, d), tile),
        out_shape=jax.ShapeDtypeStruct((n, d), F32),
        scratch_shapes=scratch,
        compiler_params=pltpu.CompilerParams(dimension_semantics=("arbitrary",),
                                             vmem_limit_bytes=FFN_VMEM_LIMIT),
        name="ffn_mix" if mix is not None else "ffn",
    )(*args)


def _slot_rms(xs, g):
    ms = jnp.sum(xs * xs, axis=-1, keepdims=True) * (1.0 / HEAD_DIM)
    return xs * lax.rsqrt(ms + EPS) * g


def _store_q_slots(z, gain_ref, q_o):
    for h, zs in enumerate(_to_slots(z)):
        q_o[:, _slot(h)] = (_slot_rms(zs, gain_ref[...]) * Q_SCALE).astype(BF16)


def _store_k_slots(z, gain_ref, k_o, one_lanes):
    slots = [_slot_rms(zs, gain_ref[...]) for zs in _to_slots(z)]
    for h, ks in enumerate(slots):
        k_o[:, _slot(h)] = (ks if one_lanes is None else jnp.where(one_lanes, 1.0, ks)).astype(BF16)
    return slots


def _store_vt_slots(z, v_o):
    lane = lax.broadcasted_iota(jnp.int32, (1, SLOT), 1)
    for h, vs in enumerate(_to_slots(z)):
        vs = jnp.where(lane == FEAT0, 1.0, vs)
        for r in range(vs.shape[0] // ATTN_TILE):
            vt = vs[r * ATTN_TILE:(r + 1) * ATTN_TILE].T
            v_o[r, _vt_rows(h), :] = vt[0:VT_ROWS].astype(BF16)


def _pipelined(stages):
    nxt = stages[0][0]()
    for j, (_, consume) in enumerate(stages):
        cur = nxt
        if j + 1 < len(stages):
            nxt = stages[j + 1][0]()
        consume(cur)


def _mixin_even_kernel(x_ref, g_ref, w_ref, qn_ref, kn_ref,
                       gate_o, xr_o, q_o, k_o, v_o, km_o, *, n_blocks):
    tm = x_ref.shape[0]
    hn = _rms_rows(x_ref[...], g_ref[...]).astype(BF16)
    w = ATTN_WIDTH
    proj = lambda c: (lambda: _dot(hn, w_ref[:, c * w:(c + 1) * w]))
    blocks_per_tile = tm // MOBA_BLOCK
    row_block = lax.broadcasted_iota(jnp.int32, (tm, 1), 0) // MOBA_BLOCK
    block = (pl.program_id(0) * blocks_per_tile + row_block) % n_blocks
    block_lane = lax.broadcasted_iota(jnp.int32, (1, SLOT), 1) == FEAT0 + block

    def store_k(z):
        for h, ks in enumerate(_store_k_slots(z, kn_ref, k_o, block_lane)):
            for r in range(blocks_per_tile):
                km_o[r, :, _slot(h)] = jnp.mean(ks[r * MOBA_BLOCK:(r + 1) * MOBA_BLOCK], axis=0, keepdims=True)

    def store(ref):
        def consume(z):
            ref[...] = z
        return consume

    _pipelined([(proj(2), lambda z: _store_q_slots(z, qn_ref, q_o)), (proj(3), store_k),
                (proj(4), lambda z: _store_vt_slots(z, v_o)), (proj(0), store(gate_o)), (proj(1), store(xr_o))])


def _vt_spec(tm):
    return pl.BlockSpec((tm // ATTN_TILE, N_HEADS * VT_ROWS, ATTN_TILE), lambda i: (i, 0, 0))


def _vt_shape(n):
    return jax.ShapeDtypeStruct((n // ATTN_TILE, N_HEADS * VT_ROWS, ATTN_TILE), BF16)


def _slot_gain(g):
    return jnp.pad(g.astype(F32), (0, SLOT - HEAD_DIM)).reshape(1, SLOT)


def _mixin_even(x, g, w_in, q_norm, k_norm, *, n_blocks, tm=512):
    n, d = x.shape
    w = ATTN_WIDTH
    assert n % tm == 0 and tm % MOBA_BLOCK == 0 and tm % ATTN_TILE == 0 and w_in.shape[1] == 5 * w
    assert FEAT0 + n_blocks <= SLOT
    row = lambda i: (i, 0)
    blk = tm // MOBA_BLOCK
    slot_shape = jax.ShapeDtypeStruct((n, SLOT_WIDTH), BF16)
    return pl.pallas_call(
        functools.partial(_mixin_even_kernel, n_blocks=n_blocks),
        grid=(n // tm,),
        in_specs=[pl.BlockSpec((tm, d), row), _const_spec((1, d)), _const_spec((d, 5 * w)),
                  _const_spec((1, SLOT)), _const_spec((1, SLOT))],
        out_specs=[pl.BlockSpec((tm, w), row)] * 2 + [pl.BlockSpec((tm, SLOT_WIDTH), row)] * 2
        + [_vt_spec(tm), pl.BlockSpec((blk, 1, SLOT_WIDTH), lambda i: (i, 0, 0))],
        out_shape=[jax.ShapeDtypeStruct((n, w), F32), jax.ShapeDtypeStruct((n, w), F32),
                   slot_shape, slot_shape, _vt_shape(n),
                   jax.ShapeDtypeStruct((n // MOBA_BLOCK, 1, SLOT_WIDTH), F32)],
        compiler_params=_cparams(("parallel",)),
        name="mixin_even",
    )(x, g.reshape(1, d), w_in.astype(BF16), _slot_gain(q_norm), _slot_gain(k_norm))


def _mixin_odd_kernel(x_ref, g_ref, w_ref, ws_ref, qn_ref, kn_ref,
                      c_o, q_o, k_o, v_o, qi_o, ki_o, wi_o):
    hn = _rms_rows(x_ref[...], g_ref[...]).astype(BF16)
    w = ATTN_WIDTH
    proj = lambda c, n=1: (lambda: _dot(hn, w_ref[:, c * w:(c + n) * w]))

    def store_glu(z):
        c_o[...] = z[:, 0:w] * jax.nn.sigmoid(z[:, w:2 * w])

    def store_qi(z):
        qi_o[...] = z.astype(BF16)

    def store_small(z):
        ki_o[...] = z[:, 0:128].astype(BF16)
        wi = z[:, 128:256] * (IDX_DIM ** -0.5 * IDX_HEADS ** -0.5)
        wi_o[...] = wi.T[0:IDX_HEADS, :]

    _pipelined([(proj(2), lambda z: _store_q_slots(z, qn_ref, q_o)),
                (proj(3), lambda z: _store_k_slots(z, kn_ref, k_o, None)),
                (proj(4), lambda z: _store_vt_slots(z, v_o)), (proj(0, 2), store_glu),
                (proj(5), store_qi), (lambda: _dot(hn, ws_ref[...]), store_small)])


def _mixin_odd(x, g, w_in, q_norm, k_norm, *, tm=512):
    n, d = x.shape
    w = ATTN_WIDTH
    assert n % tm == 0 and tm % ATTN_TILE == 0 and w_in.shape[1] == 6 * w + IDX_DIM + IDX_HEADS
    w_main = w_in[:, :6 * w].astype(BF16)
    w_ki = w_in[:, 6 * w:6 * w + IDX_DIM]
    w_wi = jnp.pad(w_in[:, 6 * w + IDX_DIM:], ((0, 0), (0, 128 - IDX_HEADS)))
    w_small = jnp.concatenate([w_ki, w_ki, w_wi], axis=1).astype(BF16)
    row = lambda i: (i, 0)
    slot_shape = jax.ShapeDtypeStruct((n, SLOT_WIDTH), BF16)
    return pl.pallas_call(
        _mixin_odd_kernel,
        grid=(n // tm,),
        in_specs=[pl.BlockSpec((tm, d), row), _const_spec((1, d)), _const_spec((d, 6 * w)),
                  _const_spec((d, 256)), _const_spec((1, SLOT)), _const_spec((1, SLOT))],
        out_specs=[pl.BlockSpec((tm, w), row)] + [pl.BlockSpec((tm, SLOT_WIDTH), row)] * 2
        + [_vt_spec(tm), pl.BlockSpec((tm, w), row), pl.BlockSpec((tm, 128), row),
           pl.BlockSpec((IDX_HEADS, tm), lambda i: (0, i))],
        out_shape=[jax.ShapeDtypeStruct((n, w), F32), slot_shape, slot_shape, _vt_shape(n),
                   jax.ShapeDtypeStruct((n, w), BF16),
                   jax.ShapeDtypeStruct((n, 128), BF16), jax.ShapeDtypeStruct((IDX_HEADS, n), F32)],
        compiler_params=_cparams(("parallel",)),
        name="mixin_odd",
    )(x, g.reshape(1, d), w_main, w_small, _slot_gain(q_norm), _slot_gain(k_norm))


def _lru_kernel(gate_ref, xr_ref, cw_ref, cb_ref, wa_ref, ba_ref, wx_ref, bx_ref, sp_ref,
                o_ref, xbuf, a_s, u_s, h_s, hc, *, ts):
    j = pl.program_id(1)

    @pl.when(j == 0)
    def _():
        xbuf[0:8, :] = jnp.zeros((8, xbuf.shape[1]), F32)
        hc[...] = jnp.zeros(hc.shape, F32)

    xbuf[8:8 + ts, :] = xr_ref[...]
    xc = cb_ref[...] + cw_ref[0:1, :] * xbuf[5:5 + ts, :]
    for k in range(1, 4):
        xc = xc + cw_ref[k:k + 1, :] * xbuf[5 + k:5 + k + ts, :]
    xbuf[0:8, :] = xbuf[ts:ts + 8, :]

    xcb = xc.astype(BF16)
    r = jax.nn.sigmoid(_dot(xcb, wa_ref[...]) + ba_ref[...])
    ig = jax.nn.sigmoid(_dot(xcb, wx_ref[...]) + bx_ref[...])
    log_a = -LRU_C * r * sp_ref[...]
    a = jnp.exp(log_a)
    a_s[...] = a
    u_s[...] = jnp.sqrt(-jnp.tanh(log_a) * (a * a + 1.0)) * (ig * xc)

    row = lax.broadcasted_iota(jnp.int32, (8, a_s.shape[1]), 0)

    def body(g, carry):
        r0 = pl.multiple_of(g * 8, 8)
        a = a_s[pl.ds(r0, 8), :]
        u = u_s[pl.ds(r0, 8), :]
        for s in (1, 2, 4):
            ok = row >= s
            a_sh = jnp.where(ok, pltpu.roll(a, s, 0), 1.0)
            u_sh = jnp.where(ok, pltpu.roll(u, s, 0), 0.0)
            u = a * u_sh + u
            a = a * a_sh
        h = a * carry + u
        h_s[pl.ds(r0, 8), :] = h
        return h[7:8, :]

    hc[...] = lax.fori_loop(0, ts // 8, body, hc[...], unroll=4)
    o_ref[...] = (h_s[...] * jax.nn.gelu(gate_ref[...])).astype(BF16)


def _block_diag(wb):
    nb, bs, _ = wb.shape
    eye = jnp.eye(nb, dtype=wb.dtype)
    return (eye[:, None, :, None] * wb[:, :, None, :]).reshape(nb * bs, nb * bs)


def _lru(gate, xr, conv_w, conv_b, ra_w, ra_b, ix_w, ix_b, lam, *, bsz, ts=256):
    n, w = xr.shape
    seq = n // bsz
    assert seq % ts == 0
    nt = seq // ts
    row = lambda b, j: (b * nt + j, 0)
    vec = lambda v: v.reshape(1, w).astype(F32)
    return pl.pallas_call(
        functools.partial(_lru_kernel, ts=ts),
        grid=(bsz, nt),
        in_specs=[pl.BlockSpec((ts, w), row), pl.BlockSpec((ts, w), row),
                  _const_spec((conv_w.shape[0], w)), _const_spec((1, w)),
                  _const_spec((w, w)), _const_spec((1, w)), _const_spec((w, w)), _const_spec((1, w)),
                  _const_spec((1, w))],
        out_specs=pl.BlockSpec((ts, w), row),
        out_shape=jax.ShapeDtypeStruct((n, w), BF16),
        scratch_shapes=[pltpu.VMEM((ts + 8, w), F32), pltpu.VMEM((ts, w), F32),
                        pltpu.VMEM((ts, w), F32), pltpu.VMEM((ts, w), F32), pltpu.VMEM((1, w), F32)],
        compiler_params=_cparams(("arbitrary", "arbitrary")),
        name="rg_lru",
    )(gate, xr, conv_w, vec(conv_b), _block_diag(ra_w).astype(BF16), vec(ra_b),
      _block_diag(ix_w).astype(BF16), vec(ix_b), vec(jax.nn.softplus(-lam)))


def _conf_kernel(c_ref, w_ref, b_ref, g_ref, beta_ref, o_ref, cbuf, sh_ref, *, ts, halo, width):
    j = pl.program_id(1)

    @pl.when(j == 0)
    def _():
        cbuf[0:halo, :] = jnp.zeros((halo, cbuf.shape[1]), F32)

    cbuf[halo:halo + ts, :] = c_ref[...]
    base = halo - (width - 1)
    span = sh_ref.shape[1]
    for r in range(1, SUBLANES):
        sh_ref[r - 1] = cbuf[r:r + span, :]

    for c0 in range(0, ts, CONV_ROWS):
        y = jnp.broadcast_to(b_ref[...], (CONV_ROWS, cbuf.shape[1]))
        for k in range(width):
            r = (base + k) % SUBLANES
            u0 = c0 + base + k - r
            win = cbuf[u0:u0 + CONV_ROWS, :] if r == 0 else sh_ref[r - 1, u0:u0 + CONV_ROWS, :]
            y = y + w_ref[k:k + 1, :] * win
        mu = jnp.mean(y, axis=-1, keepdims=True)
        yc = y - mu
        var = jnp.mean(yc * yc, axis=-1, keepdims=True)
        z = yc * lax.rsqrt(var + EPS) * g_ref[...] + beta_ref[...]
        o_ref[c0:c0 + CONV_ROWS, :] = (z * jax.nn.sigmoid(z)).astype(BF16)

    cbuf[0:halo, :] = cbuf[ts:ts + halo, :]


def _conf(c, dw_w, dw_b, ln_g, ln_b, *, bsz, ts=256, halo=32):
    n, w = c.shape
    seq = n // bsz
    width = dw_w.shape[0]
    assert seq % ts == 0 and width - 1 <= halo <= ts and halo % SUBLANES == 0 and ts % CONV_ROWS == 0
    nt = seq // ts
    row = lambda b, j: (b * nt + j, 0)
    vec = lambda v: v.reshape(1, w).astype(F32)
    return pl.pallas_call(
        functools.partial(_conf_kernel, ts=ts, halo=halo, width=width),
        grid=(bsz, nt),
        in_specs=[pl.BlockSpec((ts, w), row), _const_spec((width, w)),
                  _const_spec((1, w)), _const_spec((1, w)), _const_spec((1, w))],
        out_specs=pl.BlockSpec((ts, w), row),
        out_shape=jax.ShapeDtypeStruct((n, w), BF16),
        scratch_shapes=[pltpu.VMEM((ts + halo, w), F32),
                        pltpu.VMEM((SUBLANES - 1, ts + halo - SUBLANES, w), F32)],
        compiler_params=_cparams(("arbitrary", "arbitrary")),
        name="conformer_conv",
    )(c, dw_w, vec(dw_b), vec(ln_g), vec(ln_b))


def _bucket_tiles(t):
    assert t > REL_MAX_DIST
    n = np.arange(2 * t)
    nf = np.maximum(n, 1).astype(np.float32)
    large = REL_MAX_EXACT + (np.log(nf / np.float32(REL_MAX_EXACT))
                             / np.float32(math.log(REL_MAX_DIST / REL_MAX_EXACT))
                             * np.float32(REL_BUCKETS - REL_MAX_EXACT)).astype(np.int32)
    bucket = np.where(n < REL_MAX_EXACT, n, np.minimum(large, REL_BUCKETS - 1)).astype(np.int32)
    qry = np.arange(t)[None, :]
    key = np.arange(t)[:, None]
    return np.stack([bucket[np.maximum(qry - key, 0)], bucket[t + qry - key]])


def _build_bias(idx_ref, rb_ref, bias_ref):
    t = idx_ref.shape[1]
    causal = (lax.broadcasted_iota(jnp.int32, (t, t), 0) <= lax.broadcasted_iota(jnp.int32, (t, t), 1))
    for h in range(N_HEADS):
        far = rb_ref[REL_BUCKETS - 1, h]
        for which in (0, 1):
            idx = idx_ref[which]
            acc = jnp.zeros((t, t), F32)
            for b in range(REL_BUCKETS - 1):
                acc = jnp.where(idx == b, (rb_ref[b, h] - far) * LOG2E, acc)
            if which == 0:
                bias_ref[h, t:2 * t, :] = jnp.where(causal, acc, NEG)
            else:
                bias_ref[h, 0:t, :] = acc


def _attn_init(m_ref, acc_ref):
    m_ref[...] = jnp.full(m_ref.shape, M_INIT, F32)
    acc_ref[...] = jnp.zeros(acc_ref.shape, F32)


def _softmax_step(s, h, vt_h, m_ref, acc_ref):
    m_old = m_ref[h]
    m_new = jnp.maximum(m_old, jnp.max(s, axis=0, keepdims=True))
    p = jnp.exp2(s - m_new)
    m_ref[h] = m_new
    acc_ref[h] = jnp.exp2(m_old - m_new) * acc_ref[h] + _dot(vt_h, p.astype(BF16))


def _heads_pipelined(logits, vt_slot, m_ref, acc_ref):
    ahead = [logits(h) for h in range(HEADS_AHEAD)]
    for h in range(N_HEADS):
        if h + HEADS_AHEAD < N_HEADS:
            ahead.append(logits(h + HEADS_AHEAD))
        _softmax_step(ahead[h], h, vt_slot(h), m_ref, acc_ref)


FAR_PAIR = ("far", "far")
FAR_ONE = ("far",)
NEAR_DIAG = ("near", "diag")
DIAG_ONE = ("diag",)


def _attend_tiles(kt, kinds, q_slot, k_ref, vt_ref, bias_ref, mask_add, m_ref, acc_ref):
    t = ATTN_TILE
    n = len(kinds)
    rows = pl.ds(pl.multiple_of(kt * t, t), n * t)

    def logits(h):
        s = _dot_nt(k_ref[rows, _slot(h)], q_slot(h))
        if kinds == NEAR_DIAG:
            s = s + bias_ref[h]
        elif kinds == DIAG_ONE:
            s = s + bias_ref[h, t:2 * t, :]
        if mask_add is not None:
            s = s + mask_add
        return s

    def vt_slot(h):
        parts = [vt_ref[kt + a, _vt_rows(h), :] for a in range(n)]
        return parts[0] if n == 1 else jnp.concatenate(parts, axis=1)

    _heads_pipelined(logits, vt_slot, m_ref, acc_ref)


def _for_causal_tiles(i, tiles):
    n_far = jnp.maximum(i - 1, 0)

    def pair(j, carry):
        tiles(2 * j, FAR_PAIR)
        return carry

    lax.fori_loop(0, n_far // 2, pair, 0)

    @pl.when(n_far % 2 == 1)
    def _():
        tiles(n_far - 1, FAR_ONE)

    @pl.when(i >= 1)
    def _():
        tiles(i - 1, NEAR_DIAG)

    @pl.when(i == 0)
    def _():
        tiles(i, DIAG_ONE)


def _attn_finish(o_ref, acc_ref):
    parts = []
    for h in range(N_HEADS):
        acc = acc_ref[h]
        parts.append(acc[0:HEAD_DIM] * (1.0 / acc[FEAT0:FEAT0 + 1]))
    o_ref[...] = jnp.concatenate(parts, axis=0).T.astype(BF16)


def _smem_spec():
    return pl.BlockSpec(memory_space=pltpu.SMEM)


def _moba_kernel(rb_ref, q_ref, k_ref, vt_ref, km_ref, idx_ref, o_ref,
                 acc_ref, m_ref, qs_ref, bias_ref, *, n_blocks):
    i = pl.program_id(1)
    t = ATTN_TILE
    nbp = -(-n_blocks // 8) * 8

    @pl.when((pl.program_id(0) == 0) & (i == 0))
    def _():
        _build_bias(idx_ref, rb_ref, bias_ref)

    _attn_init(m_ref, acc_ref)

    blk = lax.broadcasted_iota(jnp.int32, (nbp, t), 0)
    past = blk < i
    for h in range(N_HEADS):
        qh = q_ref[:, _slot(h)]
        km_hi, km_lo = _split_bf16(km_ref[:, _slot(h)])
        gate_t = _dot_nt(km_hi, qh) + _dot_nt(km_lo, qh)
        g = jnp.where(past, gate_t[FEAT0:FEAT0 + nbp, :], NEG)
        rank = jnp.zeros((nbp, t), F32)
        for j in range(n_blocks):
            gj = g[j:j + 1, :]
            beats = (gj > g) | ((gj == g) & (blk > j))
            rank = rank + jnp.where(beats, 1.0, 0.0)
        flag = jnp.where(past & (rank >= MOBA_TOPK), NEG, 0.0)
        flag_t = jnp.concatenate([jnp.zeros((FEAT0, t), F32), flag,
                                  jnp.zeros((SLOT - FEAT0 - nbp, t), F32)], axis=0)
        qs_ref[:, _slot(h)] = (qh.astype(F32) + flag_t.T).astype(BF16)

    def tiles(kt, kinds):
        _attend_tiles(kt, kinds, lambda h: qs_ref[:, _slot(h)], k_ref, vt_ref, bias_ref, None,
                      m_ref, acc_ref)

    _for_causal_tiles(i, tiles)
    _attn_finish(o_ref, acc_ref)


def _moba(q, k, vt, kmean, rel_bias, *, bsz):
    n = q.shape[0]
    seq = n // bsz
    t = ATTN_TILE
    assert seq % t == 0 and t == MOBA_BLOCK
    nq = seq // t
    assert FEAT0 + nq <= SLOT
    km = jnp.pad(kmean.reshape(bsz, nq, SLOT_WIDTH), ((0, 0), (FEAT0, SLOT - FEAT0 - nq), (0, 0)))
    r3 = lambda a: a.reshape(bsz, seq, SLOT_WIDTH)
    seq_spec = pl.BlockSpec((None, seq, SLOT_WIDTH), lambda b, i: (b, 0, 0))
    out = pl.pallas_call(
        functools.partial(_moba_kernel, n_blocks=nq),
        grid=(bsz, nq),
        in_specs=[_smem_spec(),
                  pl.BlockSpec((None, t, SLOT_WIDTH), lambda b, i: (b, i, 0)), seq_spec,
                  pl.BlockSpec((nq, N_HEADS * VT_ROWS, t), lambda b, i: (b, 0, 0)),
                  pl.BlockSpec((None, SLOT, SLOT_WIDTH), lambda b, i: (b, 0, 0)),
                  _const_spec((2, t, t))],
        out_specs=pl.BlockSpec((None, t, ATTN_WIDTH), lambda b, i: (b, i, 0)),
        out_shape=jax.ShapeDtypeStruct((bsz, seq, ATTN_WIDTH), BF16),
        scratch_shapes=[pltpu.VMEM((N_HEADS, VT_ROWS, t), F32), pltpu.VMEM((N_HEADS, 1, t), F32),
                        pltpu.VMEM((t, SLOT_WIDTH), BF16),
                        pltpu.VMEM((N_HEADS, 2 * t, t), F32)],
        compiler_params=_cparams(("arbitrary", "arbitrary")),
        name="moba_attention",
    )(rel_bias, r3(q), r3(k), vt, km, jnp.asarray(_bucket_tiles(t)))
    return out.reshape(n, ATTN_WIDTH)


def _dsa_kernel(rb_ref, q_ref, k_ref, vt_ref, qi_ref, ki_ref, wit_ref, idx_ref, tri_ref, o_ref,
                acc_ref, m_ref, isc_ref, bc_ref, bias_ref, *, n_sel):
    i = pl.program_id(1)
    t = ATTN_TILE
    key = lax.broadcasted_iota(jnp.int32, (t, t), 0)
    qry = lax.broadcasted_iota(jnp.int32, (t, t), 1)
    lane128 = lax.broadcasted_iota(jnp.int32, (1, 128), 1)

    @pl.when((pl.program_id(0) == 0) & (i == 0))
    def _():
        _build_bias(idx_ref, rb_ref, bias_ref)

    _attn_init(m_ref, acc_ref)

    def index_tiles(kt, n, diag):
        rows = pl.ds(pl.multiple_of(kt * t, t), n * t)
        ki2 = ki_ref[rows, :]
        acc = jnp.zeros((n * t, t), F32)
        for pr in range(IDX_HEADS // 2):
            q2 = qi_ref[:, pr * 128:(pr + 1) * 128]
            for half in range(2):
                h = 2 * pr + half
                hm = (lane128 >= half * IDX_DIM) & (lane128 < (half + 1) * IDX_DIM)
                s = _dot_nt(ki2, jnp.where(hm, q2, jnp.zeros((), BF16)))
                acc = acc + jnp.maximum(s, 0.0) * wit_ref[h:h + 1, :]
        if diag:
            acc = jnp.where(key <= qry, acc, -jnp.inf)
        for a in range(n):
            isc_ref[kt + a] = acc[a * t:(a + 1) * t]

    def index_pair(j, carry):
        index_tiles(2 * j, 2, False)
        return carry

    lax.fori_loop(0, i // 2, index_pair, 0)

    @pl.when(i % 2 == 1)
    def _():
        index_tiles(i - 1, 1, False)

    index_tiles(i, 1, True)

    def fold8(x, op):
        return op(x.reshape(t // 8, 8, t), axis=0)

    def minmax_body(kt, carry):
        mn, mx = carry
        x = isc_ref[kt]
        mx = jnp.maximum(mx, fold8(x, jnp.max))
        mn = jnp.minimum(mn, fold8(jnp.where(x == -jnp.inf, jnp.inf, x), jnp.min))
        return mn, mx

    mn, mx = lax.fori_loop(0, i + 1, minmax_body,
                           (jnp.full((8, t), jnp.inf, F32), jnp.full((8, t), -jnp.inf, F32)))
    lo0 = jnp.min(mn, axis=0, keepdims=True)
    mx = jnp.max(mx, axis=0, keepdims=True)
    hi0 = mx + jnp.abs(mx) * 1e-3 + 1e-30
    n_valid = (i * t + 1 + lax.broadcasted_iota(jnp.int32, (1, t), 1)).astype(F32)
    want = jnp.minimum(n_valid, float(n_sel))

    def count_ge(thr):
        def one(kt, c):
            return c + fold8(jnp.where(isc_ref[kt] >= thr, 1.0, 0.0), jnp.sum)

        def pair(j, c):
            return one(2 * j + 1, one(2 * j, c))

        c = lax.fori_loop(0, (i + 1) // 2, pair, jnp.zeros((8, t), F32))
        c = lax.cond((i + 1) % 2 == 1, lambda c: one(i, c), lambda c: c, c)
        return jnp.sum(c, axis=0, keepdims=True)

    c_nonneg = count_ge(0.0)
    c_pos = count_ge(F32_TINY)
    above = want <= c_pos
    zero_tie = (want > c_pos) & (want <= c_nonneg)
    lo1 = jnp.where(above, F32_TINY, jnp.where(zero_tie, 0.0, lo0))
    hi1 = jnp.where(above, hi0, jnp.where(zero_tie, F32_TINY, 0.0))
    c_lo1 = jnp.where(above, c_pos, jnp.where(zero_tie, c_nonneg, n_valid))
    c_hi1 = jnp.where(above, 0.0, jnp.where(zero_tie, c_pos, c_nonneg))

    def bisect_step(_, carry):
        lo, hi, c_lo, c_hi = carry
        mid = 0.5 * (lo + hi)
        c_mid = count_ge(mid)
        up = c_mid >= want
        return (jnp.where(up, mid, lo), jnp.where(up, hi, mid),
                jnp.where(up, c_mid, c_lo), jnp.where(up, c_hi, c_mid))

    def bisect_cond(carry):
        it, _, _, c_lo, _ = carry
        return (it < BISECT_ITERS) & (jnp.max(jnp.where(zero_tie, 0.0, c_lo - want)) > 0.0)

    def bisect_body(carry):
        return (carry[0] + BISECT_CHECK_EVERY,) + lax.fori_loop(0, BISECT_CHECK_EVERY, bisect_step, carry[1:])

    _, lo, hi, c_lo, c_hi = lax.while_loop(bisect_cond, bisect_body, (0, lo1, hi1, c_lo1, c_hi1))
    need = want - c_hi
    banded = jnp.max(c_lo - want) > 0.0
    bc_ref[...] = jnp.zeros(bc_ref.shape, F32)

    def select_mask(kt):
        x = isc_ref[kt]

        def plain():
            return jnp.where(x >= lo, 0.0, NEG)

        def with_band():
            band = jnp.where((x >= lo) & (x < hi), 1.0, 0.0)
            before = bc_ref[...] + _dot(tri_ref[...], band.astype(BF16))
            bc_ref[...] = bc_ref[...] + jnp.sum(band, axis=0, keepdims=True)
            return jnp.where((x >= hi) | ((band > 0.0) & (before < need)), 0.0, NEG)

        return lax.cond(banded, with_band, plain)

    def tiles(kt, kinds):
        masks = [select_mask(kt + a) for a in range(len(kinds))]
        mask_add = masks[0] if len(masks) == 1 else jnp.concatenate(masks, axis=0)
        _attend_tiles(kt, kinds, lambda h: q_ref[:, _slot(h)], k_ref, vt_ref, bias_ref, mask_add,
                      m_ref, acc_ref)

    _for_causal_tiles(i, tiles)
    _attn_finish(o_ref, acc_ref)


def _dsa(q, k, vt, qi, ki2, wit, rel_bias, *, bsz):
    n = q.shape[0]
    seq = n // bsz
    t = ATTN_TILE
    assert seq % t == 0
    nq = seq // t
    n_sel = min(DSA_TOPK_MAX, seq // 4)
    tri = (jnp.arange(t)[None, :] < jnp.arange(t)[:, None]).astype(BF16)
    r3 = lambda a: a.reshape(bsz, seq, a.shape[-1])
    tile_spec = lambda width: pl.BlockSpec((None, t, width), lambda b, i: (b, i, 0))
    seq_spec = lambda width: pl.BlockSpec((None, seq, width), lambda b, i: (b, 0, 0))
    out = pl.pallas_call(
        functools.partial(_dsa_kernel, n_sel=n_sel),
        grid=(bsz, nq),
        in_specs=[_smem_spec(), tile_spec(SLOT_WIDTH), seq_spec(SLOT_WIDTH),
                  pl.BlockSpec((nq, N_HEADS * VT_ROWS, t), lambda b, i: (b, 0, 0)),
                  tile_spec(ATTN_WIDTH), seq_spec(128),
                  pl.BlockSpec((IDX_HEADS, t), lambda b, i: (0, b * nq + i)),
                  _const_spec((2, t, t)), _const_spec((t, t))],
        out_specs=tile_spec(ATTN_WIDTH),
        out_shape=jax.ShapeDtypeStruct((bsz, seq, ATTN_WIDTH), BF16),
        scratch_shapes=[pltpu.VMEM((N_HEADS, VT_ROWS, t), F32), pltpu.VMEM((N_HEADS, 1, t), F32),
                        pltpu.VMEM((nq, t, t), F32), pltpu.VMEM((1, t), F32),
                        pltpu.VMEM((N_HEADS, 2 * t, t), F32)],
        compiler_params=_cparams(("arbitrary", "arbitrary")),
        name="dsa_attention",
    )(rel_bias, r3(q), r3(k), vt, r3(qi), r3(ki2), wit, jnp.asarray(_bucket_tiles(t)), tri)
    return out.reshape(n, ATTN_WIDTH)


def kernel(x, rel_bias, ffn1_norm, ffn1_w_gate, ffn1_w_up, ffn1_w_down, mix_norm, ffn2_norm, ffn2_w_gate, ffn2_w_up, ffn2_w_down, ev_w_in, ev_conv_w, ev_conv_b, ev_ra_w, ev_ra_b, ev_ix_w, ev_ix_b, ev_lambda, ev_q_norm, ev_k_norm, ev_w_out, od_w_in, od_dw_w, od_dw_b, od_ln_g, od_ln_b, od_q_norm, od_k_norm, od_w_out):
    bsz, seq, d = x.shape
    depth = ffn1_norm.shape[0]
    h = x.reshape(bsz * seq, d)
    for i in range(depth):
        h = _ffn(h, ffn1_norm, ffn1_w_gate, ffn1_w_up, ffn1_w_down, i)
        j = i // 2
        if i % 2 == 0:
            gate, xr, q, k, vt, kmean = _mixin_even(h, mix_norm[i], ev_w_in[j], ev_q_norm[j], ev_k_norm[j],
                                                    n_blocks=seq // MOBA_BLOCK)
            ya = _lru(gate, xr, ev_conv_w[j], ev_conv_b[j], ev_ra_w[j], ev_ra_b[j],
                      ev_ix_w[j], ev_ix_b[j], ev_lambda[j], bsz=bsz)
            yb = _moba(q, k, vt, kmean, rel_bias, bsz=bsz)
            mix = (ya, yb, ev_w_out[j])
        else:
            c, q, k, vt, qi, ki2, wit = _mixin_odd(h, mix_norm[i], od_w_in[j], od_q_norm[j], od_k_norm[j])
            yc = _conf(c, od_dw_w[j], od_dw_b[j], od_ln_g[j], od_ln_b[j], bsz=bsz)
            yd = _dsa(q, k, vt, qi, ki2, wit, rel_bias, bsz=bsz)
            mix = (yc, yd, od_w_out[j])
        h = _ffn(h, ffn2_norm, ffn2_w_gate, ffn2_w_up, ffn2_w_down, i, mix)
    return h.reshape(bsz, seq, d)
```

```python
import functools
import math

import numpy as np
import jax
import jax.numpy as jnp
from jax import lax
from jax.experimental import pallas as pl
from jax.experimental.pallas import tpu as pltpu

F32 = jnp.float32
BF16 = jnp.bfloat16

N_HEADS = 8
HEAD_DIM = 64
ATTN_WIDTH = N_HEADS * HEAD_DIM
LRU_C = 8.0
MOBA_BLOCK = 256
MOBA_TOPK = 3
IDX_HEADS = 8
IDX_DIM = 64
DSA_TOPK_MAX = 256
REL_BUCKETS = 32
REL_MAX_EXACT = REL_BUCKETS // 2
REL_MAX_DIST = 128
EPS = 1e-6
NEG = -1e30
M_INIT = -1e29
ATTN_TILE = 256
SUBLANES = 8
CONV_ROWS = 64
SLOT = 128
SLOT_WIDTH = N_HEADS * SLOT
FEAT0 = HEAD_DIM
VT_ROWS = 80
HEADS_AHEAD = 4
BISECT_ITERS = 32
BISECT_CHECK_EVERY = 4
F32_TINY = float(np.finfo(np.float32).tiny)
LOG2E = math.log2(math.e)
Q_SCALE = HEAD_DIM ** -0.5 * LOG2E
VMEM_LIMIT = 56 * 1024 * 1024
FFN_VMEM_LIMIT = 60 * 1024 * 1024


def _cparams(sem):
    return pltpu.CompilerParams(dimension_semantics=sem, vmem_limit_bytes=VMEM_LIMIT)


def _dot(a, b):
    return jnp.dot(a, b, preferred_element_type=F32)


def _dot_nt(a, b):
    return lax.dot_general(a, b, (((1,), (1,)), ((), ())), preferred_element_type=F32)


def _split_bf16(x):
    hi = x.astype(BF16)
    lo = (x - hi.astype(F32)).astype(BF16)
    return hi, lo


def _rms_rows(x, g):
    return x * lax.rsqrt(jnp.mean(x * x, axis=-1, keepdims=True) + EPS) * g


def _const_spec(shape):
    nd = len(shape)
    return pl.BlockSpec(shape, lambda *_: (0,) * nd, pipeline_mode=pl.Buffered(1))


def _slot(h):
    return slice(h * SLOT, (h + 1) * SLOT)


def _vt_rows(h):
    return slice(h * VT_ROWS, (h + 1) * VT_ROWS)


def _to_slots(x):
    low = lax.broadcasted_iota(jnp.int32, (1, SLOT), 1) < HEAD_DIM
    slots = []
    for p in range(N_HEADS // 2):
        chunk = x[:, p * SLOT:(p + 1) * SLOT]
        slots.append(jnp.where(low, chunk, 0.0))
        slots.append(jnp.where(low, pltpu.roll(chunk, HEAD_DIM, 1), 0.0))
    return slots


def _ffn_kernel(*refs, n_chunks, has_mix):
    if has_mix:
        x_ref, ya_ref, yb_ref, wo_ref, g_ref, wg_ref, wu_ref, wd_ref, o_ref, wg_s, wu_s, wd_s = refs
    else:
        x_ref, g_ref, wg_ref, wu_ref, wd_ref, o_ref, wg_s, wu_s, wd_s = refs
    step = pl.program_id(0)

    @pl.when(step < n_chunks)
    def _():
        wg_s[step] = wg_ref[...].astype(BF16)
        wu_s[step] = wu_ref[...].astype(BF16)
        wd_s[step] = wd_ref[...].astype(BF16)

    @pl.when(step >= n_chunks)
    def _():
        x = x_ref[...]
        if has_mix:
            w = ya_ref.shape[1]
            x = x + _dot(ya_ref[...], wo_ref[0:w, :]) + _dot(yb_ref[...], wo_ref[w:2 * w, :])
        hn = _rms_rows(x, g_ref[...]).astype(BF16)
        acc = jnp.zeros(x.shape, F32)
        for c in range(n_chunks):
            gt = _dot(hn, wg_s[c])
            ut = _dot(hn, wu_s[c])
            a = (gt * jax.nn.sigmoid(gt) * ut).astype(BF16)
            acc = acc + _dot(a, wd_s[c])
        o_ref[...] = x + 0.5 * acc


def _ffn(x, g, wg, wu, wd, layer, mix=None, *, tm=1024, ff_chunk=256):
    n, d = x.shape
    d_ff = wg.shape[2]
    assert n % tm == 0 and d_ff % ff_chunk == 0
    nc = d_ff // ff_chunk
    tile = lambda s: (jnp.maximum(s - nc, 0), 0)
    chunk_col = lambda s: (layer, 0, jnp.minimum(s, nc - 1))
    chunk_row = lambda s: (layer, jnp.minimum(s, nc - 1), 0)
    in_specs = [pl.BlockSpec((tm, d), tile)]
    args = [x]
    scratch = [pltpu.VMEM((nc, d, ff_chunk), BF16), pltpu.VMEM((nc, d, ff_chunk), BF16),
               pltpu.VMEM((nc, ff_chunk, d), BF16)]
    if mix is not None:
        ya, yb, w_out = mix
        w = ya.shape[1]
        assert w_out.shape == (2 * w, d)
        in_specs += [pl.BlockSpec((tm, w), tile), pl.BlockSpec((tm, w), tile), _const_spec((2 * w, d))]
        args += [ya, yb, w_out.astype(BF16)]
    in_specs += [_const_spec((1, d)), pl.BlockSpec((None, d, ff_chunk), chunk_col),
                 pl.BlockSpec((None, d, ff_chunk), chunk_col), pl.BlockSpec((None, ff_chunk, d), chunk_row)]
    args += [g[layer].reshape(1, d), wg, wu, wd]
    return pl.pallas_call(
        functools.partial(_ffn_kernel, n_chunks=nc, has_mix=mix is not None),
        grid=(nc + n // tm,),
        in_specs=in_specs,
        out_specs=pl.BlockSpec((tm, d), tile),
        out_shape=jax.ShapeDtypeStruct((n, d), F32),
        scratch_shapes=scratch,
        compiler_params=pltpu.CompilerParams(dimension_semantics=("arbitrary",),
                                             vmem_limit_bytes=FFN_VMEM_LIMIT),
        name="ffn_mix" if mix is not None else "ffn",
    )(*args)


def _slot_rms(xs, g):
    ms = jnp.sum(xs * xs, axis=-1, keepdims=True) * (1.0 / HEAD_DIM)
    return xs * lax.rsqrt(ms + EPS) * g


def _store_q_slots(z, gain_ref, q_o):
    for h, zs in enumerate(_to_slots(z)):
        q_o[:, _slot(h)] = (_slot_rms(zs, gain_ref[...]) * Q_SCALE).astype(BF16)


def _store_k_slots(z, gain_ref, k_o, one_lanes):
    slots = [_slot_rms(zs, gain_ref[...]) for zs in _to_slots(z)]
    for h, ks in enumerate(slots):
        k_o[:, _slot(h)] = (ks if one_lanes is None else jnp.where(one_lanes, 1.0, ks)).astype(BF16)
    return slots


def _store_vt_slots(z, v_o):
    lane = lax.broadcasted_iota(jnp.int32, (1, SLOT), 1)
    for h, vs in enumerate(_to_slots(z)):
        vs = jnp.where(lane == FEAT0, 1.0, vs)
        for r in range(vs.shape[0] // ATTN_TILE):
            vt = vs[r * ATTN_TILE:(r + 1) * ATTN_TILE].T
            v_o[r, _vt_rows(h), :] = vt[0:VT_ROWS].astype(BF16)


def _pipelined(stages):
    nxt = stages[0][0]()
    for j, (_, consume) in enumerate(stages):
        cur = nxt
        if j + 1 < len(stages):
            nxt = stages[j + 1][0]()
        consume(cur)


def _mixin_even_kernel(x_ref, g_ref, w_ref, qn_ref, kn_ref,
                       gate_o, xr_o, q_o, k_o, v_o, km_o, *, n_blocks):
    tm = x_ref.shape[0]
    hn = _rms_rows(x_ref[...], g_ref[...]).astype(BF16)
    w = ATTN_WIDTH
    proj = lambda c: (lambda: _dot(hn, w_ref[:, c * w:(c + 1) * w]))
    blocks_per_tile = tm // MOBA_BLOCK
    row_block = lax.broadcasted_iota(jnp.int32, (tm, 1), 0) // MOBA_BLOCK
    block = (pl.program_id(0) * blocks_per_tile + row_block) % n_blocks
    block_lane = lax.broadcasted_iota(jnp.int32, (1, SLOT), 1) == FEAT0 + block

    def store_k(z):
        for h, ks in enumerate(_store_k_slots(z, kn_ref, k_o, block_lane)):
            for r in range(blocks_per_tile):
                km_o[r, :, _slot(h)] = jnp.mean(ks[r * MOBA_BLOCK:(r + 1) * MOBA_BLOCK], axis=0, keepdims=True)

    def store(ref):
        def consume(z):
            ref[...] = z
        return consume

    _pipelined([(proj(2), lambda z: _store_q_slots(z, qn_ref, q_o)), (proj(3), store_k),
                (proj(4), lambda z: _store_vt_slots(z, v_o)), (proj(0), store(gate_o)), (proj(1), store(xr_o))])


def _vt_spec(tm):
    return pl.BlockSpec((tm // ATTN_TILE, N_HEADS * VT_ROWS, ATTN_TILE), lambda i: (i, 0, 0))


def _vt_shape(n):
    return jax.ShapeDtypeStruct((n // ATTN_TILE, N_HEADS * VT_ROWS, ATTN_TILE), BF16)


def _slot_gain(g):
    return jnp.pad(g.astype(F32), (0, SLOT - HEAD_DIM)).reshape(1, SLOT)


def _mixin_even(x, g, w_in, q_norm, k_norm, *, n_blocks, tm=512):
    n, d = x.shape
    w = ATTN_WIDTH
    assert n % tm == 0 and tm % MOBA_BLOCK == 0 and tm % ATTN_TILE == 0 and w_in.shape[1] == 5 * w
    assert FEAT0 + n_blocks <= SLOT
    row = lambda i: (i, 0)
    blk = tm // MOBA_BLOCK
    slot_shape = jax.ShapeDtypeStruct((n, SLOT_WIDTH), BF16)
    return pl.pallas_call(
        functools.partial(_mixin_even_kernel, n_blocks=n_blocks),
        grid=(n // tm,),
        in_specs=[pl.BlockSpec((tm, d), row), _const_spec((1, d)), _const_spec((d, 5 * w)),
                  _const_spec((1, SLOT)), _const_spec((1, SLOT))],
        out_specs=[pl.BlockSpec((tm, w), row)] * 2 + [pl.BlockSpec((tm, SLOT_WIDTH), row)] * 2
        + [_vt_spec(tm), pl.BlockSpec((blk, 1, SLOT_WIDTH), lambda i: (i, 0, 0))],
        out_shape=[jax.ShapeDtypeStruct((n, w), F32), jax.ShapeDtypeStruct((n, w), F32),
                   slot_shape, slot_shape, _vt_shape(n),
                   jax.ShapeDtypeStruct((n // MOBA_BLOCK, 1, SLOT_WIDTH), F32)],
        compiler_params=_cparams(("parallel",)),
        name="mixin_even",
    )(x, g.reshape(1, d), w_in.astype(BF16), _slot_gain(q_norm), _slot_gain(k_norm))


def _mixin_odd_kernel(x_ref, g_ref, w_ref, ws_ref, qn_ref, kn_ref,
                      c_o, q_o, k_o, v_o, qi_o, ki_o, wi_o):
    hn = _rms_rows(x_ref[...], g_ref[...]).astype(BF16)
    w = ATTN_WIDTH
    proj = lambda c, n=1: (lambda: _dot(hn, w_ref[:, c * w:(c + n) * w]))

    def store_glu(z):
        c_o[...] = z[:, 0:w] * jax.nn.sigmoid(z[:, w:2 * w])

    def store_qi(z):
        qi_o[...] = z.astype(BF16)

    def store_small(z):
        ki_o[...] = z[:, 0:128].astype(BF16)
        wi = z[:, 128:256] * (IDX_DIM ** -0.5 * IDX_HEADS ** -0.5)
        wi_o[...] = wi.T[0:IDX_HEADS, :]

    _pipelined([(proj(2), lambda z: _store_q_slots(z, qn_ref, q_o)),
                (proj(3), lambda z: _store_k_slots(z, kn_ref, k_o, None)),
                (proj(4), lambda z: _store_vt_slots(z, v_o)), (proj(0, 2), store_glu),
                (proj(5), store_qi), (lambda: _dot(hn, ws_ref[...]), store_small)])


def _mixin_odd(x, g, w_in, q_norm, k_norm, *, tm=512):
    n, d = x.shape
    w = ATTN_WIDTH
    assert n % tm == 0 and tm % ATTN_TILE == 0 and w_in.shape[1] == 6 * w + IDX_DIM + IDX_HEADS
    w_main = w_in[:, :6 * w].astype(BF16)
    w_ki = w_in[:, 6 * w:6 * w + IDX_DIM]
    w_wi = jnp.pad(w_in[:, 6 * w + IDX_DIM:], ((0, 0), (0, 128 - IDX_HEADS)))
    w_small = jnp.concatenate([w_ki, w_ki, w_wi], axis=1).astype(BF16)
    row = lambda i: (i, 0)
    slot_shape = jax.ShapeDtypeStruct((n, SLOT_WIDTH), BF16)
    return pl.pallas_call(
        _mixin_odd_kernel,
        grid=(n // tm,),
        in_specs=[pl.BlockSpec((tm, d), row), _const_spec((1, d)), _const_spec((d, 6 * w)),
                  _const_spec((d, 256)), _const_spec((1, SLOT)), _const_spec((1, SLOT))],
        out_specs=[pl.BlockSpec((tm, w), row)] + [pl.BlockSpec((tm, SLOT_WIDTH), row)] * 2
        + [_vt_spec(tm), pl.BlockSpec((tm, w), row), pl.BlockSpec((tm, 128), row),
           pl.BlockSpec((IDX_HEADS, tm), lambda i: (0, i))],
        out_shape=[jax.ShapeDtypeStruct((n, w), F32), slot_shape, slot_shape, _vt_shape(n),
                   jax.ShapeDtypeStruct((n, w), BF16),
                   jax.ShapeDtypeStruct((n, 128), BF16), jax.ShapeDtypeStruct((IDX_HEADS, n), F32)],
        compiler_params=_cparams(("parallel",)),
        name="mixin_odd",
    )(x, g.reshape(1, d), w_main, w_small, _slot_gain(q_norm), _slot_gain(k_norm))


def _lru_kernel(gate_ref, xr_ref, cw_ref, cb_ref, wa_ref, ba_ref, wx_ref, bx_ref, sp_ref,
                o_ref, xbuf, a_s, u_s, h_s, hc, *, ts):
    j = pl.program_id(1)

    @pl.when(j == 0)
    def _():
        xbuf[0:8, :] = jnp.zeros((8, xbuf.shape[1]), F32)
        hc[...] = jnp.zeros(hc.shape, F32)

    xbuf[8:8 + ts, :] = xr_ref[...]
    xc = cb_ref[...] + cw_ref[0:1, :] * xbuf[5:5 + ts, :]
    for k in range(1, 4):
        xc = xc + cw_ref[k:k + 1, :] * xbuf[5 + k:5 + k + ts, :]
    xbuf[0:8, :] = xbuf[ts:ts + 8, :]

    xcb = xc.astype(BF16)
    r = jax.nn.sigmoid(_dot(xcb, wa_ref[...]) + ba_ref[...])
    ig = jax.nn.sigmoid(_dot(xcb, wx_ref[...]) + bx_ref[...])
    log_a = -LRU_C * r * sp_ref[...]
    a = jnp.exp(log_a)
    a_s[...] = a
    u_s[...] = jnp.sqrt(-jnp.tanh(log_a) * (a * a + 1.0)) * (ig * xc)

    row = lax.broadcasted_iota(jnp.int32, (8, a_s.shape[1]), 0)

    def body(g, carry):
        r0 = pl.multiple_of(g * 8, 8)
        a = a_s[pl.ds(r0, 8), :]
        u = u_s[pl.ds(r0, 8), :]
        for s in (1, 2, 4):
            ok = row >= s
            a_sh = jnp.where(ok, pltpu.roll(a, s, 0), 1.0)
            u_sh = jnp.where(ok, pltpu.roll(u, s, 0), 0.0)
            u = a * u_sh + u
            a = a * a_sh
        h = a * carry + u
        h_s[pl.ds(r0, 8), :] = h
        return h[7:8, :]

    hc[...] = lax.fori_loop(0, ts // 8, body, hc[...], unroll=4)
    o_ref[...] = (h_s[...] * jax.nn.gelu(gate_ref[...])).astype(BF16)


def _block_diag(wb):
    nb, bs, _ = wb.shape
    eye = jnp.eye(nb, dtype=wb.dtype)
    return (eye[:, None, :, None] * wb[:, :, None, :]).reshape(nb * bs, nb * bs)


def _lru(gate, xr, conv_w, conv_b, ra_w, ra_b, ix_w, ix_b, lam, *, bsz, ts=256):
    n, w = xr.shape
    seq = n // bsz
    assert seq % ts == 0
    nt = seq // ts
    row = lambda b, j: (b * nt + j, 0)
    vec = lambda v: v.reshape(1, w).astype(F32)
    return pl.pallas_call(
        functools.partial(_lru_kernel, ts=ts),
        grid=(bsz, nt),
        in_specs=[pl.BlockSpec((ts, w), row), pl.BlockSpec((ts, w), row),
                  _const_spec((conv_w.shape[0], w)), _const_spec((1, w)),
                  _const_spec((w, w)), _const_spec((1, w)), _const_spec((w, w)), _const_spec((1, w)),
                  _const_spec((1, w))],
        out_specs=pl.BlockSpec((ts, w), row),
        out_shape=jax.ShapeDtypeStruct((n, w), BF16),
        scratch_shapes=[pltpu.VMEM((ts + 8, w), F32), pltpu.VMEM((ts, w), F32),
                        pltpu.VMEM((ts, w), F32), pltpu.VMEM((ts, w), F32), pltpu.VMEM((1, w), F32)],
        compiler_params=_cparams(("arbitrary", "arbitrary")),
        name="rg_lru",
    )(gate, xr, conv_w, vec(conv_b), _block_diag(ra_w).astype(BF16), vec(ra_b),
      _block_diag(ix_w).astype(BF16), vec(ix_b), vec(jax.nn.softplus(-lam)))


def _conf_kernel(c_ref, w_ref, b_ref, g_ref, beta_ref, o_ref, cbuf, sh_ref, *, ts, halo, width):
    j = pl.program_id(1)

    @pl.when(j == 0)
    def _():
        cbuf[0:halo, :] = jnp.zeros((halo, cbuf.shape[1]), F32)

    cbuf[halo:halo + ts, :] = c_ref[...]
    base = halo - (width - 1)
    span = sh_ref.shape[1]
    for r in range(1, SUBLANES):
        sh_ref[r - 1] = cbuf[r:r + span, :]

    for c0 in range(0, ts, CONV_ROWS):
        y = jnp.broadcast_to(b_ref[...], (CONV_ROWS, cbuf.shape[1]))
        for k in range(width):
            r = (base + k) % SUBLANES
            u0 = c0 + base + k - r
            win = cbuf[u0:u0 + CONV_ROWS, :] if r == 0 else sh_ref[r - 1, u0:u0 + CONV_ROWS, :]
            y = y + w_ref[k:k + 1, :] * win
        mu = jnp.mean(y, axis=-1, keepdims=True)
        yc = y - mu
        var = jnp.mean(yc * yc, axis=-1, keepdims=True)
        z = yc * lax.rsqrt(var + EPS) * g_ref[...] + beta_ref[...]
        o_ref[c0:c0 + CONV_ROWS, :] = (z * jax.nn.sigmoid(z)).astype(BF16)

    cbuf[0:halo, :] = cbuf[ts:ts + halo, :]


def _conf(c, dw_w, dw_b, ln_g, ln_b, *, bsz, ts=256, halo=32):
    n, w = c.shape
    seq = n // bsz
    width = dw_w.shape[0]
    assert seq % ts == 0 and width - 1 <= halo <= ts and halo % SUBLANES == 0 and ts % CONV_ROWS == 0
    nt = seq // ts
    row = lambda b, j: (b * nt + j, 0)
    vec = lambda v: v.reshape(1, w).astype(F32)
    return pl.pallas_call(
        functools.partial(_conf_kernel, ts=ts, halo=halo, width=width),
        grid=(bsz, nt),
        in_specs=[pl.BlockSpec((ts, w), row), _const_spec((width, w)),
                  _const_spec((1, w)), _const_spec((1, w)), _const_spec((1, w))],
        out_specs=pl.BlockSpec((ts, w), row),
        out_shape=jax.ShapeDtypeStruct((n, w), BF16),
        scratch_shapes=[pltpu.VMEM((ts + halo, w), F32),
                        pltpu.VMEM((SUBLANES - 1, ts + halo - SUBLANES, w), F32)],
        compiler_params=_cparams(("arbitrary", "arbitrary")),
        name="conformer_conv",
    )(c, dw_w, vec(dw_b), vec(ln_g), vec(ln_b))


def _bucket_tiles(t):
    assert t > REL_MAX_DIST
    n = np.arange(2 * t)
    nf = np.maximum(n, 1).astype(np.float32)
    large = REL_MAX_EXACT + (np.log(nf / np.float32(REL_MAX_EXACT))
                             / np.float32(math.log(REL_MAX_DIST / REL_MAX_EXACT))
                             * np.float32(REL_BUCKETS - REL_MAX_EXACT)).astype(np.int32)
    bucket = np.where(n < REL_MAX_EXACT, n, np.minimum(large, REL_BUCKETS - 1)).astype(np.int32)
    qry = np.arange(t)[None, :]
    key = np.arange(t)[:, None]
    return np.stack([bucket[np.maximum(qry - key, 0)], bucket[t + qry - key]])


def _build_bias(idx_ref, rb_ref, bias_ref):
    t = idx_ref.shape[1]
    causal = (lax.broadcasted_iota(jnp.int32, (t, t), 0) <= lax.broadcasted_iota(jnp.int32, (t, t), 1))
    for h in range(N_HEADS):
        far = rb_ref[REL_BUCKETS - 1, h]
        for which in (0, 1):
            idx = idx_ref[which]
            acc = jnp.zeros((t, t), F32)
            for b in range(REL_BUCKETS - 1):
                acc = jnp.where(idx == b, (rb_ref[b, h] - far) * LOG2E, acc)
            if which == 0:
                bias_ref[h, t:2 * t, :] = jnp.where(causal, acc, NEG)
            else:
                bias_ref[h, 0:t, :] = acc


def _attn_init(m_ref, acc_ref):
    m_ref[...] = jnp.full(m_ref.shape, M_INIT, F32)
    acc_ref[...] = jnp.zeros(acc_ref.shape, F32)


def _softmax_step(s, h, vt_h, m_ref, acc_ref):
    m_old = m_ref[h]
    m_new = jnp.maximum(m_old, jnp.max(s, axis=0, keepdims=True))
    p = jnp.exp2(s - m_new)
    m_ref[h] = m_new
    acc_ref[h] = jnp.exp2(m_old - m_new) * acc_ref[h] + _dot(vt_h, p.astype(BF16))


def _heads_pipelined(logits, vt_slot, m_ref, acc_ref):
    ahead = [logits(h) for h in range(HEADS_AHEAD)]
    for h in range(N_HEADS):
        if h + HEADS_AHEAD < N_HEADS:
            ahead.append(logits(h + HEADS_AHEAD))
        _softmax_step(ahead[h], h, vt_slot(h), m_ref, acc_ref)


FAR_PAIR = ("far", "far")
FAR_ONE = ("far",)
NEAR_DIAG = ("near", "diag")
DIAG_ONE = ("diag",)


def _attend_tiles(kt, kinds, q_slot, k_ref, vt_ref, bias_ref, mask_add, m_ref, acc_ref):
    t = ATTN_TILE
    n = len(kinds)
    rows = pl.ds(pl.multiple_of(kt * t, t), n * t)

    def logits(h):
        s = _dot_nt(k_ref[rows, _slot(h)], q_slot(h))
        if kinds == NEAR_DIAG:
            s = s + bias_ref[h]
        elif kinds == DIAG_ONE:
            s = s + bias_ref[h, t:2 * t, :]
        if mask_add is not None:
            s = s + mask_add
        return s

    def vt_slot(h):
        parts = [vt_ref[kt + a, _vt_rows(h), :] for a in range(n)]
        return parts[0] if n == 1 else jnp.concatenate(parts, axis=1)

    _heads_pipelined(logits, vt_slot, m_ref, acc_ref)


def _for_causal_tiles(i, tiles):
    n_far = jnp.maximum(i - 1, 0)

    def pair(j, carry):
        tiles(2 * j, FAR_PAIR)
        return carry

    lax.fori_loop(0, n_far // 2, pair, 0)

    @pl.when(n_far % 2 == 1)
    def _():
        tiles(n_far - 1, FAR_ONE)

    @pl.when(i >= 1)
    def _():
        tiles(i - 1, NEAR_DIAG)

    @pl.when(i == 0)
    def _():
        tiles(i, DIAG_ONE)


def _attn_finish(o_ref, acc_ref):
    parts = []
    for h in range(N_HEADS):
        acc = acc_ref[h]
        parts.append(acc[0:HEAD_DIM] * (1.0 / acc[FEAT0:FEAT0 + 1]))
    o_ref[...] = jnp.concatenate(parts, axis=0).T.astype(BF16)


def _smem_spec():
    return pl.BlockSpec(memory_space=pltpu.SMEM)


def _moba_kernel(rb_ref, q_ref, k_ref, vt_ref, km_ref, idx_ref, o_ref,
                 acc_ref, m_ref, qs_ref, bias_ref, *, n_blocks):
    i = pl.program_id(1)
    t = ATTN_TILE
    nbp = -(-n_blocks // 8) * 8

    @pl.when((pl.program_id(0) == 0) & (i == 0))
    def _():
        _build_bias(idx_ref, rb_ref, bias_ref)

    _attn_init(m_ref, acc_ref)

    blk = lax.broadcasted_iota(jnp.int32, (nbp, t), 0)
    past = blk < i
    for h in range(N_HEADS):
        qh = q_ref[:, _slot(h)]
        km_hi, km_lo = _split_bf16(km_ref[:, _slot(h)])
        gate_t = _dot_nt(km_hi, qh) + _dot_nt(km_lo, qh)
        g = jnp.where(past, gate_t[FEAT0:FEAT0 + nbp, :], NEG)
        rank = jnp.zeros((nbp, t), F32)
        for j in range(n_blocks):
            gj = g[j:j + 1, :]
            beats = (gj > g) | ((gj == g) & (blk > j))
            rank = rank + jnp.where(beats, 1.0, 0.0)
        flag = jnp.where(past & (rank >= MOBA_TOPK), NEG, 0.0)
        flag_t = jnp.concatenate([jnp.zeros((FEAT0, t), F32), flag,
                                  jnp.zeros((SLOT - FEAT0 - nbp, t), F32)], axis=0)
        qs_ref[:, _slot(h)] = (qh.astype(F32) + flag_t.T).astype(BF16)

    def tiles(kt, kinds):
        _attend_tiles(kt, kinds, lambda h: qs_ref[:, _slot(h)], k_ref, vt_ref, bias_ref, None,
                      m_ref, acc_ref)

    _for_causal_tiles(i, tiles)
    _attn_finish(o_ref, acc_ref)


def _moba(q, k, vt, kmean, rel_bias, *, bsz):
    n = q.shape[0]
    seq = n // bsz
    t = ATTN_TILE
    assert seq % t == 0 and t == MOBA_BLOCK
    nq = seq // t
    assert FEAT0 + nq <= SLOT
    km = jnp.pad(kmean.reshape(bsz, nq, SLOT_WIDTH), ((0, 0), (FEAT0, SLOT - FEAT0 - nq), (0, 0)))
    r3 = lambda a: a.reshape(bsz, seq, SLOT_WIDTH)
    seq_spec = pl.BlockSpec((None, seq, SLOT_WIDTH), lambda b, i: (b, 0, 0))
    out = pl.pallas_call(
        functools.partial(_moba_kernel, n_blocks=nq),
        grid=(bsz, nq),
        in_specs=[_smem_spec(),
                  pl.BlockSpec((None, t, SLOT_WIDTH), lambda b, i: (b, i, 0)), seq_spec,
                  pl.BlockSpec((nq, N_HEADS * VT_ROWS, t), lambda b, i: (b, 0, 0)),
                  pl.BlockSpec((None, SLOT, SLOT_WIDTH), lambda b, i: (b, 0, 0)),
                  _const_spec((2, t, t))],
        out_specs=pl.BlockSpec((None, t, ATTN_WIDTH), lambda b, i: (b, i, 0)),
        out_shape=jax.ShapeDtypeStruct((bsz, seq, ATTN_WIDTH), BF16),
        scratch_shapes=[pltpu.VMEM((N_HEADS, VT_ROWS, t), F32), pltpu.VMEM((N_HEADS, 1, t), F32),
                        pltpu.VMEM((t, SLOT_WIDTH), BF16),
                        pltpu.VMEM((N_HEADS, 2 * t, t), F32)],
        compiler_params=_cparams(("arbitrary", "arbitrary")),
        name="moba_attention",
    )(rel_bias, r3(q), r3(k), vt, km, jnp.asarray(_bucket_tiles(t)))
    return out.reshape(n, ATTN_WIDTH)


def _dsa_kernel(rb_ref, q_ref, k_ref, vt_ref, qi_ref, ki_ref, wit_ref, idx_ref, tri_ref, o_ref,
                acc_ref, m_ref, isc_ref, bc_ref, bias_ref, *, n_sel):
    i = pl.program_id(1)
    t = ATTN_TILE
    key = lax.broadcasted_iota(jnp.int32, (t, t), 0)
    qry = lax.broadcasted_iota(jnp.int32, (t, t), 1)
    lane128 = lax.broadcasted_iota(jnp.int32, (1, 128), 1)

    @pl.when((pl.program_id(0) == 0) & (i == 0))
    def _():
        _build_bias(idx_ref, rb_ref, bias_ref)

    _attn_init(m_ref, acc_ref)

    def index_tiles(kt, n, diag):
        rows = pl.ds(pl.multiple_of(kt * t, t), n * t)
        ki2 = ki_ref[rows, :]
        acc = jnp.zeros((n * t, t), F32)
        for pr in range(IDX_HEADS // 2):
            q2 = qi_ref[:, pr * 128:(pr + 1) * 128]
            for half in range(2):
                h = 2 * pr + half
                hm = (lane128 >= half * IDX_DIM) & (lane128 < (half + 1) * IDX_DIM)
                s = _dot_nt(ki2, jnp.where(hm, q2, jnp.zeros((), BF16)))
                acc = acc + jnp.maximum(s, 0.0) * wit_ref[h:h + 1, :]
        if diag:
            acc = jnp.where(key <= qry, acc, -jnp.inf)
        for a in range(n):
            isc_ref[kt + a] = acc[a * t:(a + 1) * t]

    def index_pair(j, carry):
        index_tiles(2 * j, 2, False)
        return carry

    lax.fori_loop(0, i // 2, index_pair, 0)

    @pl.when(i % 2 == 1)
    def _():
        index_tiles(i - 1, 1, False)

    index_tiles(i, 1, True)

    def fold8(x, op):
        return op(x.reshape(t // 8, 8, t), axis=0)

    def minmax_body(kt, carry):
        mn, mx = carry
        x = isc_ref[kt]
        mx = jnp.maximum(mx, fold8(x, jnp.max))
        mn = jnp.minimum(mn, fold8(jnp.where(x == -jnp.inf, jnp.inf, x), jnp.min))
        return mn, mx

    mn, mx = lax.fori_loop(0, i + 1, minmax_body,
                           (jnp.full((8, t), jnp.inf, F32), jnp.full((8, t), -jnp.inf, F32)))
    lo0 = jnp.min(mn, axis=0, keepdims=True)
    mx = jnp.max(mx, axis=0, keepdims=True)
    hi0 = mx + jnp.abs(mx) * 1e-3 + 1e-30
    n_valid = (i * t + 1 + lax.broadcasted_iota(jnp.int32, (1, t), 1)).astype(F32)
    want = jnp.minimum(n_valid, float(n_sel))

    def count_ge(thr):
        def one(kt, c):
            return c + fold8(jnp.where(isc_ref[kt] >= thr, 1.0, 0.0), jnp.sum)

        def pair(j, c):
            return one(2 * j + 1, one(2 * j, c))

        c = lax.fori_loop(0, (i + 1) // 2, pair, jnp.zeros((8, t), F32))
        c = lax.cond((i + 1) % 2 == 1, lambda c: one(i, c), lambda c: c, c)
        return jnp.sum(c, axis=0, keepdims=True)

    c_nonneg = count_ge(0.0)
    c_pos = count_ge(F32_TINY)
    above = want <= c_pos
    zero_tie = (want > c_pos) & (want <= c_nonneg)
    lo1 = jnp.where(above, F32_TINY, jnp.where(zero_tie, 0.0, lo0))
    hi1 = jnp.where(above, hi0, jnp.where(zero_tie, F32_TINY, 0.0))
    c_lo1 = jnp.where(above, c_pos, jnp.where(zero_tie, c_nonneg, n_valid))
    c_hi1 = jnp.where(above, 0.0, jnp.where(zero_tie, c_pos, c_nonneg))

    def bisect_step(_, carry):
        lo, hi, c_lo, c_hi = carry
        mid = 0.5 * (lo + hi)
        c_mid = count_ge(mid)
        up = c_mid >= want
        return (jnp.where(up, mid, lo), jnp.where(up, hi, mid),
                jnp.where(up, c_mid, c_lo), jnp.where(up, c_hi, c_mid))

    def bisect_cond(carry):
        it, _, _, c_lo, _ = carry
        return (it < BISECT_ITERS) & (jnp.max(jnp.where(zero_tie, 0.0, c_lo - want)) > 0.0)

    def bisect_body(carry):
        return (carry[0] + BISECT_CHECK_EVERY,) + lax.fori_loop(0, BISECT_CHECK_EVERY, bisect_step, carry[1:])

    _, lo, hi, c_lo, c_hi = lax.while_loop(bisect_cond, bisect_body, (0, lo1, hi1, c_lo1, c_hi1))
    need = want - c_hi
    banded = jnp.max(c_lo - want) > 0.0
    bc_ref[...] = jnp.zeros(bc_ref.shape, F32)

    def select_mask(kt):
        x = isc_ref[kt]

        def plain():
            return jnp.where(x >= lo, 0.0, NEG)

        def with_band():
            band = jnp.where((x >= lo) & (x < hi), 1.0, 0.0)
            before = bc_ref[...] + _dot(tri_ref[...], band.astype(BF16))
            bc_ref[...] = bc_ref[...] + jnp.sum(band, axis=0, keepdims=True)
            return jnp.where((x >= hi) | ((band > 0.0) & (before < need)), 0.0, NEG)

        return lax.cond(banded, with_band, plain)

    def tiles(kt, kinds):
        masks = [select_mask(kt + a) for a in range(len(kinds))]
        mask_add = masks[0] if len(masks) == 1 else jnp.concatenate(masks, axis=0)
        _attend_tiles(kt, kinds, lambda h: q_ref[:, _slot(h)], k_ref, vt_ref, bias_ref, mask_add,
                      m_ref, acc_ref)

    _for_causal_tiles(i, tiles)
    _attn_finish(o_ref, acc_ref)


def _dsa(q, k, vt, qi, ki2, wit, rel_bias, *, bsz):
    n = q.shape[0]
    seq = n // bsz
    t = ATTN_TILE
    assert seq % t == 0
    nq = seq // t
    n_sel = min(DSA_TOPK_MAX, seq // 4)
    tri = (jnp.arange(t)[None, :] < jnp.arange(t)[:, None]).astype(BF16)
    r3 = lambda a: a.reshape(bsz, seq, a.shape[-1])
    tile_spec = lambda width: pl.BlockSpec((None, t, width), lambda b, i: (b, i, 0))
    seq_spec = lambda width: pl.BlockSpec((None, seq, width), lambda b, i: (b, 0, 0))
    out = pl.pallas_call(
        functools.partial(_dsa_kernel, n_sel=n_sel),
        grid=(bsz, nq),
        in_specs=[_smem_spec(), tile_spec(SLOT_WIDTH), seq_spec(SLOT_WIDTH),
                  pl.BlockSpec((nq, N_HEADS * VT_ROWS, t), lambda b, i: (b, 0, 0)),
                  tile_spec(ATTN_WIDTH), seq_spec(128),
                  pl.BlockSpec((IDX_HEADS, t), lambda b, i: (0, b * nq + i)),
                  _const_spec((2, t, t)), _const_spec((t, t))],
        out_specs=tile_spec(ATTN_WIDTH),
        out_shape=jax.ShapeDtypeStruct((bsz, seq, ATTN_WIDTH), BF16),
        scratch_shapes=[pltpu.VMEM((N_HEADS, VT_ROWS, t), F32), pltpu.VMEM((N_HEADS, 1, t), F32),
                        pltpu.VMEM((nq, t, t), F32), pltpu.VMEM((1, t), F32),
                        pltpu.VMEM((N_HEADS, 2 * t, t), F32)],
        compiler_params=_cparams(("arbitrary", "arbitrary")),
        name="dsa_attention",
    )(rel_bias, r3(q), r3(k), vt, r3(qi), r3(ki2), wit, jnp.asarray(_bucket_tiles(t)), tri)
    return out.reshape(n, ATTN_WIDTH)


def kernel(x, rel_bias, ffn1_norm, ffn1_w_gate, ffn1_w_up, ffn1_w_down, mix_norm, ffn2_norm, ffn2_w_gate, ffn2_w_up, ffn2_w_down, ev_w_in, ev_conv_w, ev_conv_b, ev_ra_w, ev_ra_b, ev_ix_w, ev_ix_b, ev_lambda, ev_q_norm, ev_k_norm, ev_w_out, od_w_in, od_dw_w, od_dw_b, od_ln_g, od_ln_b, od_q_norm, od_k_norm, od_w_out):
    bsz, seq, d = x.shape
    depth = ffn1_norm.shape[0]
    h = x.reshape(bsz * seq, d)
    for i in range(depth):
        h = _ffn(h, ffn1_norm, ffn1_w_gate, ffn1_w_up, ffn1_w_down, i)
        j = i // 2
        if i % 2 == 0:
            gate, xr, q, k, vt, kmean = _mixin_even(h, mix_norm[i], ev_w_in[j], ev_q_norm[j], ev_k_norm[j],
                                                    n_blocks=seq // MOBA_BLOCK)
            ya = _lru(gate, xr, ev_conv_w[j], ev_conv_b[j], ev_ra_w[j], ev_ra_b[j],
                      ev_ix_w[j], ev_ix_b[j], ev_lambda[j], bsz=bsz)
            yb = _moba(q, k, vt, kmean, rel_bias, bsz=bsz)
            mix = (ya, yb, ev_w_out[j])
        else:
            c, q, k, vt, qi, ki2, wit = _mixin_odd(h, mix_norm[i], od_w_in[j], od_q_norm[j], od_k_norm[j])
            yc = _conf(c, od_dw_w[j], od_dw_b[j], od_ln_g[j], od_ln_b[j], bsz=bsz)
            yd = _dsa(q, k, vt, qi, ki2, wit, rel_bias, bsz=bsz)
            mix = (yc, yd, od_w_out[j])
        h = _ffn(h, ffn2_norm, ffn2_w_gate, ffn2_w_up, ffn2_w_down, i, mix)
    return h.reshape(bsz, seq, d)
```

```python
import functools
import math

import numpy as np
import jax
import jax.numpy as jnp
from jax import lax
from jax.experimental import pallas as pl
from jax.experimental.pallas import tpu as pltpu

F32 = jnp.float32
BF16 = jnp.bfloat16

N_HEADS = 8
HEAD_DIM = 64
ATTN_WIDTH = N_HEADS * HEAD_DIM
LRU_C = 8.0
MOBA_BLOCK = 256
MOBA_TOPK = 3
IDX_HEADS = 8
IDX_DIM = 64
DSA_TOPK_MAX = 256
REL_BUCKETS = 32
REL_MAX_EXACT = REL_BUCKETS // 2
REL_MAX_DIST = 128
EPS = 1e-6
NEG = -1e30
M_INIT = -1e29
ATTN_TILE = 256
SUBLANES = 8
CONV_ROWS = 64
SLOT = 128
SLOT_WIDTH = N_HEADS * SLOT
FEAT0 = HEAD_DIM
VT_ROWS = 80
HEADS_AHEAD = 4
BISECT_ITERS = 32
BISECT_CHECK_EVERY = 4
F32_TINY = float(np.finfo(np.float32).tiny)
LOG2E = math.log2(math.e)
Q_SCALE = HEAD_DIM ** -0.5 * LOG2E
VMEM_LIMIT = 56 * 1024 * 1024
FFN_VMEM_LIMIT = 60 * 1024 * 1024


def _cparams(sem):
    return pltpu.CompilerParams(dimension_semantics=sem, vmem_limit_bytes=VMEM_LIMIT)


def _dot(a, b):
    return jnp.dot(a, b, preferred_element_type=F32)


def _dot_nt(a, b):
    return lax.dot_general(a, b, (((1,), (1,)), ((), ())), preferred_element_type=F32)


def _split_bf16(x):
    hi = x.astype(BF16)
    lo = (x - hi.astype(F32)).astype(BF16)
    return hi, lo


def _rms_rows(x, g):
    return x * lax.rsqrt(jnp.mean(x * x, axis=-1, keepdims=True) + EPS) * g


def _const_spec(shape):
    nd = len(shape)
    return pl.BlockSpec(shape, lambda *_: (0,) * nd, pipeline_mode=pl.Buffered(1))


def _slot(h):
    return slice(h * SLOT, (h + 1) * SLOT)


def _vt_rows(h):
    return slice(h * VT_ROWS, (h + 1) * VT_ROWS)


def _to_slots(x):
    low = lax.broadcasted_iota(jnp.int32, (1, SLOT), 1) < HEAD_DIM
    slots = []
    for p in range(N_HEADS // 2):
        chunk = x[:, p * SLOT:(p + 1) * SLOT]
        slots.append(jnp.where(low, chunk, 0.0))
        slots.append(jnp.where(low, pltpu.roll(chunk, HEAD_DIM, 1), 0.0))
    return slots


def _ffn_kernel(*refs, n_chunks, has_mix):
    if has_mix:
        x_ref, ya_ref, yb_ref, wo_ref, g_ref, wg_ref, wu_ref, wd_ref, o_ref, wg_s, wu_s, wd_s = refs
    else:
        x_ref, g_ref, wg_ref, wu_ref, wd_ref, o_ref, wg_s, wu_s, wd_s = refs
    step = pl.program_id(0)

    @pl.when(step < n_chunks)
    def _():
        wg_s[step] = wg_ref[...].astype(BF16)
        wu_s[step] = wu_ref[...].astype(BF16)
        wd_s[step] = wd_ref[...].astype(BF16)

    @pl.when(step >= n_chunks)
    def _():
        x = x_ref[...]
        if has_mix:
            w = ya_ref.shape[1]
            x = x + _dot(ya_ref[...], wo_ref[0:w, :]) + _dot(yb_ref[...], wo_ref[w:2 * w, :])
        hn = _rms_rows(x, g_ref[...]).astype(BF16)
        acc = jnp.zeros(x.shape, F32)
        for c in range(n_chunks):
            gt = _dot(hn, wg_s[c])
            ut = _dot(hn, wu_s[c])
            a = (gt * jax.nn.sigmoid(gt) * ut).astype(BF16)
            acc = acc + _dot(a, wd_s[c])
        o_ref[...] = x + 0.5 * acc


def _ffn(x, g, wg, wu, wd, layer, mix=None, *, tm=1024, ff_chunk=256):
    n, d = x.shape
    d_ff = wg.shape[2]
    assert n % tm == 0 and d_ff % ff_chunk == 0
    nc = d_ff // ff_chunk
    tile = lambda s: (jnp.maximum(s - nc, 0), 0)
    chunk_col = lambda s: (layer, 0, jnp.minimum(s, nc - 1))
    chunk_row = lambda s: (layer, jnp.minimum(s, nc - 1), 0)
    in_specs = [pl.BlockSpec((tm, d), tile)]
    args = [x]
    scratch = [pltpu.VMEM((nc, d, ff_chunk), BF16), pltpu.VMEM((nc, d, ff_chunk), BF16),
               pltpu.VMEM((nc, ff_chunk, d), BF16)]
    if mix is not None:
        ya, yb, w_out = mix
        w = ya.shape[1]
        assert w_out.shape == (2 * w, d)
        in_specs += [pl.BlockSpec((tm, w), tile), pl.BlockSpec((tm, w), tile), _const_spec((2 * w, d))]
        args += [ya, yb, w_out.astype(BF16)]
    in_specs += [_const_spec((1, d)), pl.BlockSpec((None, d, ff_chunk), chunk_col),
                 pl.BlockSpec((None, d, ff_chunk), chunk_col), pl.BlockSpec((None, ff_chunk, d), chunk_row)]
    args += [g[layer].reshape(1, d), wg, wu, wd]
    return pl.pallas_call(
        functools.partial(_ffn_kernel, n_chunks=nc, has_mix=mix is not None),
        grid=(nc + n // tm,),
        in_specs=in_specs,
        out_specs=pl.BlockSpec((tm, d), tile),
        out_shape=jax.ShapeDtypeStruct((n, d), F32),
        scratch_shapes=scratch,
        compiler_params=pltpu.CompilerParams(dimension_semantics=("arbitrary",),
                                             vmem_limit_bytes=FFN_VMEM_LIMIT),
        name="ffn_mix" if mix is not None else "ffn",
    )(*args)


def _slot_rms(xs, g):
    ms = jnp.sum(xs * xs, axis=-1, keepdims=True) * (1.0 / HEAD_DIM)
    return xs * lax.rsqrt(ms + EPS) * g


def _store_q_slots(z, gain_ref, q_o):
    for h, zs in enumerate(_to_slots(z)):
        q_o[:, _slot(h)] = (_slot_rms(zs, gain_ref[...]) * Q_SCALE).astype(BF16)


def _store_k_slots(z, gain_ref, k_o, one_lanes):
    slots = [_slot_rms(zs, gain_ref[...]) for zs in _to_slots(z)]
    for h, ks in enumerate(slots):
        k_o[:, _slot(h)] = (ks if one_lanes is None else jnp.where(one_lanes, 1.0, ks)).astype(BF16)
    return slots


def _store_vt_slots(z, v_o):
    lane = lax.broadcasted_iota(jnp.int32, (1, SLOT), 1)
    for h, vs in enumerate(_to_slots(z)):
        vs = jnp.where(lane == FEAT0, 1.0, vs)
        for r in range(vs.shape[0] // ATTN_TILE):
            vt = vs[r * ATTN_TILE:(r + 1) * ATTN_TILE].T
            v_o[r, _vt_rows(h), :] = vt[0:VT_ROWS].astype(BF16)


def _pipelined(stages):
    nxt = stages[0][0]()
    for j, (_, consume) in enumerate(stages):
        cur = nxt
        if j + 1 < len(stages):
            nxt = stages[j + 1][0]()
        consume(cur)


def _mixin_even_kernel(x_ref, g_ref, w_ref, qn_ref, kn_ref,
                       gate_o, xr_o, q_o, k_o, v_o, km_o, *, n_blocks):
    tm = x_ref.shape[0]
    hn = _rms_rows(x_ref[...], g_ref[...]).astype(BF16)
    w = ATTN_WIDTH
    proj = lambda c: (lambda: _dot(hn, w_ref[:, c * w:(c + 1) * w]))
    blocks_per_tile = tm // MOBA_BLOCK
    row_block = lax.broadcasted_iota(jnp.int32, (tm, 1), 0) // MOBA_BLOCK
    block = (pl.program_id(0) * blocks_per_tile + row_block) % n_blocks
    block_lane = lax.broadcasted_iota(jnp.int32, (1, SLOT), 1) == FEAT0 + block

    def store_k(z):
        for h, ks in enumerate(_store_k_slots(z, kn_ref, k_o, block_lane)):
            for r in range(blocks_per_tile):
                km_o[r, :, _slot(h)] = jnp.mean(ks[r * MOBA_BLOCK:(r + 1) * MOBA_BLOCK], axis=0, keepdims=True)

    def store(ref):
        def consume(z):
            ref[...] = z
        return consume

    _pipelined([(proj(2), lambda z: _store_q_slots(z, qn_ref, q_o)), (proj(3), store_k),
                (proj(4), lambda z: _store_vt_slots(z, v_o)), (proj(0), store(gate_o)), (proj(1), store(xr_o))])


def _vt_spec(tm):
    return pl.BlockSpec((tm // ATTN_TILE, N_HEADS * VT_ROWS, ATTN_TILE), lambda i: (i, 0, 0))


def _vt_shape(n):
    return jax.ShapeDtypeStruct((n // ATTN_TILE, N_HEADS * VT_ROWS, ATTN_TILE), BF16)


def _slot_gain(g):
    return jnp.pad(g.astype(F32), (0, SLOT - HEAD_DIM)).reshape(1, SLOT)


def _mixin_even(x, g, w_in, q_norm, k_norm, *, n_blocks, tm=512):
    n, d = x.shape
    w = ATTN_WIDTH
    assert n % tm == 0 and tm % MOBA_BLOCK == 0 and tm % ATTN_TILE == 0 and w_in.shape[1] == 5 * w
    assert FEAT0 + n_blocks <= SLOT
    row = lambda i: (i, 0)
    blk = tm // MOBA_BLOCK
    slot_shape = jax.ShapeDtypeStruct((n, SLOT_WIDTH), BF16)
    return pl.pallas_call(
        functools.partial(_mixin_even_kernel, n_blocks=n_blocks),
        grid=(n // tm,),
        in_specs=[pl.BlockSpec((tm, d), row), _const_spec((1, d)), _const_spec((d, 5 * w)),
                  _const_spec((1, SLOT)), _const_spec((1, SLOT))],
        out_specs=[pl.BlockSpec((tm, w), row)] * 2 + [pl.BlockSpec((tm, SLOT_WIDTH), row)] * 2
        + [_vt_spec(tm), pl.BlockSpec((blk, 1, SLOT_WIDTH), lambda i: (i, 0, 0))],
        out_shape=[jax.ShapeDtypeStruct((n, w), F32), jax.ShapeDtypeStruct((n, w), F32),
                   slot_shape, slot_shape, _vt_shape(n),
                   jax.ShapeDtypeStruct((n // MOBA_BLOCK, 1, SLOT_WIDTH), F32)],
        compiler_params=_cparams(("parallel",)),
        name="mixin_even",
    )(x, g.reshape(1, d), w_in.astype(BF16), _slot_gain(q_norm), _slot_gain(k_norm))


def _conv_ln_silu(c, first, w_ref, b_ref, g_ref, beta_ref, o_ref, cbuf, sh_ref, *, halo):
    ts = c.shape[0]
    width = w_ref.shape[0]

    @pl.when(first)
    def _():
        cbuf[0:halo, :] = jnp.zeros((halo, cbuf.shape[1]), F32)

    cbuf[halo:halo + ts, :] = c
    base = halo - (width - 1)
    span = sh_ref.shape[1]
    for r in range(1, SUBLANES):
        sh_ref[r - 1] = cbuf[r:r + span, :]

    def chunk(c0):
        y = jnp.broadcast_to(b_ref[...], (CONV_ROWS, cbuf.shape[1]))
        for k in range(width):
            r = (base + k) % SUBLANES
            u0 = c0 + base + k - r
            win = cbuf[u0:u0 + CONV_ROWS, :] if r == 0 else sh_ref[r - 1, u0:u0 + CONV_ROWS, :]
            y = y + w_ref[k:k + 1, :] * win
        mu = jnp.mean(y, axis=-1, keepdims=True)
        yc = y - mu
        var = jnp.mean(yc * yc, axis=-1, keepdims=True)
        z = yc * lax.rsqrt(var + EPS) * g_ref[...] + beta_ref[...]
        o_ref[c0:c0 + CONV_ROWS, :] = (z * jax.nn.sigmoid(z)).astype(BF16)

    def close():
        cbuf[0:halo, :] = cbuf[ts:ts + halo, :]

    return [functools.partial(chunk, c0) for c0 in range(0, ts, CONV_ROWS)], close


def _mixin_odd_kernel(x_ref, g_ref, w_ref, ws_ref, qn_ref, kn_ref, dw_ref, db_ref, lg_ref, lb_ref,
                      yc_o, q_o, k_o, v_o, qi_o, ki_o, wi_o, cbuf, sh_ref, *, tiles_per_seq, halo):
    hn = _rms_rows(x_ref[...], g_ref[...]).astype(BF16)
    w = ATTN_WIDTH
    proj = lambda c, n=1: _dot(hn, w_ref[:, c * w:(c + n) * w])
    first = pl.program_id(0) % tiles_per_seq == 0

    z_glu = proj(0, 2)
    z_q = proj(2)
    chunks, close = _conv_ln_silu(z_glu[:, 0:w] * jax.nn.sigmoid(z_glu[:, w:2 * w]), first,
                                  dw_ref, db_ref, lg_ref, lb_ref, yc_o, cbuf, sh_ref, halo=halo)
    per_stage = -(-len(chunks) // 4)
    run_chunks = lambda j: [f() for f in chunks[j * per_stage:(j + 1) * per_stage]]
    z_k = proj(3)
    _store_q_slots(z_q, qn_ref, q_o)
    run_chunks(0)
    z_v = proj(4)
    _store_k_slots(z_k, kn_ref, k_o, None)
    run_chunks(1)
    z_qi = proj(5)
    _store_vt_slots(z_v, v_o)
    run_chunks(2)
    z_small = _dot(hn, ws_ref[...])
    qi_o[...] = z_qi.astype(BF16)
    run_chunks(3)
    close()
    ki_o[...] = z_small[:, 0:128].astype(BF16)
    wi = z_small[:, 128:256] * (IDX_DIM ** -0.5 * IDX_HEADS ** -0.5)
    wi_o[...] = wi.T[0:IDX_HEADS, :]


def _mixin_odd(x, g, w_in, q_norm, k_norm, dw_w, dw_b, ln_g, ln_b, *, bsz, tm=512, halo=32):
    n, d = x.shape
    w = ATTN_WIDTH
    width = dw_w.shape[0]
    assert n % tm == 0 and tm % ATTN_TILE == 0 and w_in.shape[1] == 6 * w + IDX_DIM + IDX_HEADS
    assert (n // bsz) % tm == 0 and width - 1 <= halo <= tm and halo % SUBLANES == 0 and tm % CONV_ROWS == 0
    w_main = w_in[:, :6 * w].astype(BF16)
    w_ki = w_in[:, 6 * w:6 * w + IDX_DIM]
    w_wi = jnp.pad(w_in[:, 6 * w + IDX_DIM:], ((0, 0), (0, 128 - IDX_HEADS)))
    w_small = jnp.concatenate([w_ki, w_ki, w_wi], axis=1).astype(BF16)
    row = lambda i: (i, 0)
    vec = lambda v: v.reshape(1, w).astype(F32)
    slot_shape = jax.ShapeDtypeStruct((n, SLOT_WIDTH), BF16)
    return pl.pallas_call(
        functools.partial(_mixin_odd_kernel, tiles_per_seq=(n // bsz) // tm, halo=halo),
        grid=(n // tm,),
        in_specs=[pl.BlockSpec((tm, d), row), _const_spec((1, d)), _const_spec((d, 6 * w)),
                  _const_spec((d, 256)), _const_spec((1, SLOT)), _const_spec((1, SLOT)),
                  _const_spec((width, w)), _const_spec((1, w)), _const_spec((1, w)), _const_spec((1, w))],
        out_specs=[pl.BlockSpec((tm, w), row)] + [pl.BlockSpec((tm, SLOT_WIDTH), row)] * 2
        + [_vt_spec(tm), pl.BlockSpec((tm, w), row), pl.BlockSpec((tm, 128), row),
           pl.BlockSpec((IDX_HEADS, tm), lambda i: (0, i))],
        out_shape=[jax.ShapeDtypeStruct((n, w), BF16), slot_shape, slot_shape, _vt_shape(n),
                   jax.ShapeDtypeStruct((n, w), BF16),
                   jax.ShapeDtypeStruct((n, 128), BF16), jax.ShapeDtypeStruct((IDX_HEADS, n), F32)],
        scratch_shapes=[pltpu.VMEM((tm + halo, w), F32),
                        pltpu.VMEM((SUBLANES - 1, tm + halo - SUBLANES, w), F32)],
        compiler_params=_cparams(("arbitrary",)),
        name="mixin_odd",
    )(x, g.reshape(1, d), w_main, w_small, _slot_gain(q_norm), _slot_gain(k_norm),
      dw_w, vec(dw_b), vec(ln_g), vec(ln_b))


def _lru_kernel(gate_ref, xr_ref, cw_ref, cb_ref, wa_ref, ba_ref, wx_ref, bx_ref, sp_ref,
                o_ref, xbuf, a_s, u_s, h_s, hc, *, ts):
    j = pl.program_id(1)

    @pl.when(j == 0)
    def _():
        xbuf[0:8, :] = jnp.zeros((8, xbuf.shape[1]), F32)
        hc[...] = jnp.zeros(hc.shape, F32)

    xbuf[8:8 + ts, :] = xr_ref[...]
    xc = cb_ref[...] + cw_ref[0:1, :] * xbuf[5:5 + ts, :]
    for k in range(1, 4):
        xc = xc + cw_ref[k:k + 1, :] * xbuf[5 + k:5 + k + ts, :]
    xbuf[0:8, :] = xbuf[ts:ts + 8, :]

    xcb = xc.astype(BF16)
    r = jax.nn.sigmoid(_dot(xcb, wa_ref[...]) + ba_ref[...])
    ig = jax.nn.sigmoid(_dot(xcb, wx_ref[...]) + bx_ref[...])
    log_a = -LRU_C * r * sp_ref[...]
    a = jnp.exp(log_a)
    a_s[...] = a
    u_s[...] = jnp.sqrt(-jnp.tanh(log_a) * (a * a + 1.0)) * (ig * xc)

    row = lax.broadcasted_iota(jnp.int32, (8, a_s.shape[1]), 0)

    def body(g, carry):
        r0 = pl.multiple_of(g * 8, 8)
        a = a_s[pl.ds(r0, 8), :]
        u = u_s[pl.ds(r0, 8), :]
        for s in (1, 2, 4):
            ok = row >= s
            a_sh = jnp.where(ok, pltpu.roll(a, s, 0), 1.0)
            u_sh = jnp.where(ok, pltpu.roll(u, s, 0), 0.0)
            u = a * u_sh + u
            a = a * a_sh
        h = a * carry + u
        h_s[pl.ds(r0, 8), :] = h
        return h[7:8, :]

    hc[...] = lax.fori_loop(0, ts // 8, body, hc[...], unroll=4)
    o_ref[...] = (h_s[...] * jax.nn.gelu(gate_ref[...])).astype(BF16)


def _block_diag(wb):
    nb, bs, _ = wb.shape
    eye = jnp.eye(nb, dtype=wb.dtype)
    return (eye[:, None, :, None] * wb[:, :, None, :]).reshape(nb * bs, nb * bs)


def _lru(gate, xr, conv_w, conv_b, ra_w, ra_b, ix_w, ix_b, lam, *, bsz, ts=256):
    n, w = xr.shape
    seq = n // bsz
    assert seq % ts == 0
    nt = seq // ts
    row = lambda b, j: (b * nt + j, 0)
    vec = lambda v: v.reshape(1, w).astype(F32)
    return pl.pallas_call(
        functools.partial(_lru_kernel, ts=ts),
        grid=(bsz, nt),
        in_specs=[pl.BlockSpec((ts, w), row), pl.BlockSpec((ts, w), row),
                  _const_spec((conv_w.shape[0], w)), _const_spec((1, w)),
                  _const_spec((w, w)), _const_spec((1, w)), _const_spec((w, w)), _const_spec((1, w)),
                  _const_spec((1, w))],
        out_specs=pl.BlockSpec((ts, w), row),
        out_shape=jax.ShapeDtypeStruct((n, w), BF16),
        scratch_shapes=[pltpu.VMEM((ts + 8, w), F32), pltpu.VMEM((ts, w), F32),
                        pltpu.VMEM((ts, w), F32), pltpu.VMEM((ts, w), F32), pltpu.VMEM((1, w), F32)],
        compiler_params=_cparams(("arbitrary", "arbitrary")),
        name="rg_lru",
    )(gate, xr, conv_w, vec(conv_b), _block_diag(ra_w).astype(BF16), vec(ra_b),
      _block_diag(ix_w).astype(BF16), vec(ix_b), vec(jax.nn.softplus(-lam)))


def _bucket_tiles(t):
    assert t > REL_MAX_DIST
    n = np.arange(2 * t)
    nf = np.maximum(n, 1).astype(np.float32)
    large = REL_MAX_EXACT + (np.log(nf / np.float32(REL_MAX_EXACT))
                             / np.float32(math.log(REL_MAX_DIST / REL_MAX_EXACT))
                             * np.float32(REL_BUCKETS - REL_MAX_EXACT)).astype(np.int32)
    bucket = np.where(n < REL_MAX_EXACT, n, np.minimum(large, REL_BUCKETS - 1)).astype(np.int32)
    qry = np.arange(t)[None, :]
    key = np.arange(t)[:, None]
    return np.stack([bucket[np.maximum(qry - key, 0)], bucket[t + qry - key]])


def _build_bias(idx_ref, rb_ref, bias_ref):
    t = idx_ref.shape[1]
    causal = (lax.broadcasted_iota(jnp.int32, (t, t), 0) <= lax.broadcasted_iota(jnp.int32, (t, t), 1))
    for h in range(N_HEADS):
        far = rb_ref[REL_BUCKETS - 1, h]
        for which in (0, 1):
            idx = idx_ref[which]
            acc = jnp.zeros((t, t), F32)
            for b in range(REL_BUCKETS - 1):
                acc = jnp.where(idx == b, (rb_ref[b, h] - far) * LOG2E, acc)
            if which == 0:
                bias_ref[h, t:2 * t, :] = jnp.where(causal, acc, NEG)
            else:
                bias_ref[h, 0:t, :] = acc


def _attn_init(m_ref, acc_ref):
    m_ref[...] = jnp.full(m_ref.shape, M_INIT, F32)
    acc_ref[...] = jnp.zeros(acc_ref.shape, F32)


def _softmax_step(s, h, vt_h, m_ref, acc_ref):
    m_old = m_ref[h]
    m_new = jnp.maximum(m_old, jnp.max(s, axis=0, keepdims=True))
    p = jnp.exp2(s - m_new)
    m_ref[h] = m_new
    acc_ref[h] = jnp.exp2(m_old - m_new) * acc_ref[h] + _dot(vt_h, p.astype(BF16))


def _heads_pipelined(logits, vt_slot, m_ref, acc_ref):
    ahead = [logits(h) for h in range(HEADS_AHEAD)]
    for h in range(N_HEADS):
        if h + HEADS_AHEAD < N_HEADS:
            ahead.append(logits(h + HEADS_AHEAD))
        _softmax_step(ahead[h], h, vt_slot(h), m_ref, acc_ref)


FAR_PAIR = ("far", "far")
FAR_ONE = ("far",)
NEAR_DIAG = ("near", "diag")
DIAG_ONE = ("diag",)


def _attend_tiles(kt, kinds, q_slot, k_ref, vt_ref, bias_ref, mask_add, m_ref, acc_ref):
    t = ATTN_TILE
    n = len(kinds)
    rows = pl.ds(pl.multiple_of(kt * t, t), n * t)

    def logits(h):
        s = _dot_nt(k_ref[rows, _slot(h)], q_slot(h))
        if kinds == NEAR_DIAG:
            s = s + bias_ref[h]
        elif kinds == DIAG_ONE:
            s = s + bias_ref[h, t:2 * t, :]
        if mask_add is not None:
            s = s + mask_add
        return s

    def vt_slot(h):
        parts = [vt_ref[kt + a, _vt_rows(h), :] for a in range(n)]
        return parts[0] if n == 1 else jnp.concatenate(parts, axis=1)

    _heads_pipelined(logits, vt_slot, m_ref, acc_ref)


def _for_causal_tiles(i, tiles):
    n_far = jnp.maximum(i - 1, 0)

    def pair(j, carry):
        tiles(2 * j, FAR_PAIR)
        return carry

    lax.fori_loop(0, n_far // 2, pair, 0)

    @pl.when(n_far % 2 == 1)
    def _():
        tiles(n_far - 1, FAR_ONE)

    @pl.when(i >= 1)
    def _():
        tiles(i - 1, NEAR_DIAG)

    @pl.when(i == 0)
    def _():
        tiles(i, DIAG_ONE)


def _attn_finish(o_ref, acc_ref):
    parts = []
    for h in range(N_HEADS):
        acc = acc_ref[h]
        parts.append(acc[0:HEAD_DIM] * (1.0 / acc[FEAT0:FEAT0 + 1]))
    o_ref[...] = jnp.concatenate(parts, axis=0).T.astype(BF16)


def _smem_spec():
    return pl.BlockSpec(memory_space=pltpu.SMEM)


def _moba_kernel(rb_ref, q_ref, k_ref, vt_ref, km_ref, idx_ref, o_ref,
                 acc_ref, m_ref, qs_ref, bias_ref, *, n_blocks):
    i = pl.program_id(1)
    t = ATTN_TILE
    nbp = -(-n_blocks // 8) * 8

    @pl.when((pl.program_id(0) == 0) & (i == 0))
    def _():
        _build_bias(idx_ref, rb_ref, bias_ref)

    _attn_init(m_ref, acc_ref)

    blk = lax.broadcasted_iota(jnp.int32, (nbp, t), 0)
    past = blk < i
    for h in range(N_HEADS):
        qh = q_ref[:, _slot(h)]
        km_hi, km_lo = _split_bf16(km_ref[:, _slot(h)])
        gate_t = _dot_nt(km_hi, qh) + _dot_nt(km_lo, qh)
        g = jnp.where(past, gate_t[FEAT0:FEAT0 + nbp, :], NEG)
        rank = jnp.zeros((nbp, t), F32)
        for j in range(n_blocks):
            gj = g[j:j + 1, :]
            beats = (gj > g) | ((gj == g) & (blk > j))
            rank = rank + jnp.where(beats, 1.0, 0.0)
        flag = jnp.where(past & (rank >= MOBA_TOPK), NEG, 0.0)
        flag_t = jnp.concatenate([jnp.zeros((FEAT0, t), F32), flag,
                                  jnp.zeros((SLOT - FEAT0 - nbp, t), F32)], axis=0)
        qs_ref[:, _slot(h)] = (qh.astype(F32) + flag_t.T).astype(BF16)

    def tiles(kt, kinds):
        _attend_tiles(kt, kinds, lambda h: qs_ref[:, _slot(h)], k_ref, vt_ref, bias_ref, None,
                      m_ref, acc_ref)

    _for_causal_tiles(i, tiles)
    _attn_finish(o_ref, acc_ref)


def _moba(q, k, vt, kmean, rel_bias, *, bsz):
    n = q.shape[0]
    seq = n // bsz
    t = ATTN_TILE
    assert seq % t == 0 and t == MOBA_BLOCK
    nq = seq // t
    assert FEAT0 + nq <= SLOT
    km = jnp.pad(kmean.reshape(bsz, nq, SLOT_WIDTH), ((0, 0), (FEAT0, SLOT - FEAT0 - nq), (0, 0)))
    r3 = lambda a: a.reshape(bsz, seq, SLOT_WIDTH)
    seq_spec = pl.BlockSpec((None, seq, SLOT_WIDTH), lambda b, i: (b, 0, 0))
    out = pl.pallas_call(
        functools.partial(_moba_kernel, n_blocks=nq),
        grid=(bsz, nq),
        in_specs=[_smem_spec(),
                  pl.BlockSpec((None, t, SLOT_WIDTH), lambda b, i: (b, i, 0)), seq_spec,
                  pl.BlockSpec((nq, N_HEADS * VT_ROWS, t), lambda b, i: (b, 0, 0)),
                  pl.BlockSpec((None, SLOT, SLOT_WIDTH), lambda b, i: (b, 0, 0)),
                  _const_spec((2, t, t))],
        out_specs=pl.BlockSpec((None, t, ATTN_WIDTH), lambda b, i: (b, i, 0)),
        out_shape=jax.ShapeDtypeStruct((bsz, seq, ATTN_WIDTH), BF16),
        scratch_shapes=[pltpu.VMEM((N_HEADS, VT_ROWS, t), F32), pltpu.VMEM((N_HEADS, 1, t), F32),
                        pltpu.VMEM((t, SLOT_WIDTH), BF16),
                        pltpu.VMEM((N_HEADS, 2 * t, t), F32)],
        compiler_params=_cparams(("arbitrary", "arbitrary")),
        name="moba_attention",
    )(rel_bias, r3(q), r3(k), vt, km, jnp.asarray(_bucket_tiles(t)))
    return out.reshape(n, ATTN_WIDTH)


def _dsa_kernel(rb_ref, q_ref, k_ref, vt_ref, qi_ref, ki_ref, wit_ref, idx_ref, tri_ref, o_ref,
                acc_ref, m_ref, isc_ref, bc_ref, bias_ref, *, n_sel):
    i = pl.program_id(1)
    t = ATTN_TILE
    key = lax.broadcasted_iota(jnp.int32, (t, t), 0)
    qry = lax.broadcasted_iota(jnp.int32, (t, t), 1)
    lane128 = lax.broadcasted_iota(jnp.int32, (1, 128), 1)

    @pl.when((pl.program_id(0) == 0) & (i == 0))
    def _():
        _build_bias(idx_ref, rb_ref, bias_ref)

    _attn_init(m_ref, acc_ref)

    def index_tiles(kt, n, diag):
        rows = pl.ds(pl.multiple_of(kt * t, t), n * t)
        ki2 = ki_ref[rows, :]
        acc = jnp.zeros((n * t, t), F32)
        for pr in range(IDX_HEADS // 2):
            q2 = qi_ref[:, pr * 128:(pr + 1) * 128]
            for half in range(2):
                h = 2 * pr + half
                hm = (lane128 >= half * IDX_DIM) & (lane128 < (half + 1) * IDX_DIM)
                s = _dot_nt(ki2, jnp.where(hm, q2, jnp.zeros((), BF16)))
                acc = acc + jnp.maximum(s, 0.0) * wit_ref[h:h + 1, :]
        if diag:
            acc = jnp.where(key <= qry, acc, -jnp.inf)
        for a in range(n):
            isc_ref[kt + a] = acc[a * t:(a + 1) * t]

    def index_pair(j, carry):
        index_tiles(2 * j, 2, False)
        return carry

    lax.fori_loop(0, i // 2, index_pair, 0)

    @pl.when(i % 2 == 1)
    def _():
        index_tiles(i - 1, 1, False)

    index_tiles(i, 1, True)

    def fold8(x, op):
        return op(x.reshape(t // 8, 8, t), axis=0)

    def minmax_body(kt, carry):
        mn, mx = carry
        x = isc_ref[kt]
        mx = jnp.maximum(mx, fold8(x, jnp.max))
        mn = jnp.minimum(mn, fold8(jnp.where(x == -jnp.inf, jnp.inf, x), jnp.min))
        return mn, mx

    mn, mx = lax.fori_loop(0, i + 1, minmax_body,
                           (jnp.full((8, t), jnp.inf, F32), jnp.full((8, t), -jnp.inf, F32)))
    lo0 = jnp.min(mn, axis=0, keepdims=True)
    mx = jnp.max(mx, axis=0, keepdims=True)
    hi0 = mx + jnp.abs(mx) * 1e-3 + 1e-30
    n_valid = (i * t + 1 + lax.broadcasted_iota(jnp.int32, (1, t), 1)).astype(F32)
    want = jnp.minimum(n_valid, float(n_sel))

    def count_ge(thr):
        def one(kt, c):
            return c + fold8(jnp.where(isc_ref[kt] >= thr, 1.0, 0.0), jnp.sum)

        def pair(j, c):
            return one(2 * j + 1, one(2 * j, c))

        c = lax.fori_loop(0, (i + 1) // 2, pair, jnp.zeros((8, t), F32))
        c = lax.cond((i + 1) % 2 == 1, lambda c: one(i, c), lambda c: c, c)
        return jnp.sum(c, axis=0, keepdims=True)

    c_nonneg = count_ge(0.0)
    c_pos = count_ge(F32_TINY)
    above = want <= c_pos
    zero_tie = (want > c_pos) & (want <= c_nonneg)
    lo1 = jnp.where(above, F32_TINY, jnp.where(zero_tie, 0.0, lo0))
    hi1 = jnp.where(above, hi0, jnp.where(zero_tie, F32_TINY, 0.0))
    c_lo1 = jnp.where(above, c_pos, jnp.where(zero_tie, c_nonneg, n_valid))
    c_hi1 = jnp.where(above, 0.0, jnp.where(zero_tie, c_pos, c_nonneg))

    def bisect_step(_, carry):
        lo, hi, c_lo, c_hi = carry
        mid = 0.5 * (lo + hi)
        c_mid = count_ge(mid)
        up = c_mid >= want
        return (jnp.where(up, mid, lo), jnp.where(up, hi, mid),
                jnp.where(up, c_mid, c_lo), jnp.where(up, c_hi, c_mid))

    def bisect_cond(carry):
        it, _, _, c_lo, _ = carry
        return (it < BISECT_ITERS) & (jnp.max(jnp.where(zero_tie, 0.0, c_lo - want)) > 0.0)

    def bisect_body(carry):
        return (carry[0] + BISECT_CHECK_EVERY,) + lax.fori_loop(0, BISECT_CHECK_EVERY, bisect_step, carry[1:])

    _, lo, hi, c_lo, c_hi = lax.while_loop(bisect_cond, bisect_body, (0, lo1, hi1, c_lo1, c_hi1))
    need = want - c_hi
    banded = jnp.max(c_lo - want) > 0.0
    bc_ref[...] = jnp.zeros(bc_ref.shape, F32)

    def select_mask(kt):
        x = isc_ref[kt]

        def plain():
            return jnp.where(x >= lo, 0.0, NEG)

        def with_band():
            band = jnp.where((x >= lo) & (x < hi), 1.0, 0.0)
            before = bc_ref[...] + _dot(tri_ref[...], band.astype(BF16))
            bc_ref[...] = bc_ref[...] + jnp.sum(band, axis=0, keepdims=True)
            return jnp.where((x >= hi) | ((band > 0.0) & (before < need)), 0.0, NEG)

        return lax.cond(banded, with_band, plain)

    def tiles(kt, kinds):
        masks = [select_mask(kt + a) for a in range(len(kinds))]
        mask_add = masks[0] if len(masks) == 1 else jnp.concatenate(masks, axis=0)
        _attend_tiles(kt, kinds, lambda h: q_ref[:, _slot(h)], k_ref, vt_ref, bias_ref, mask_add,
                      m_ref, acc_ref)

    _for_causal_tiles(i, tiles)
    _attn_finish(o_ref, acc_ref)


def _dsa(q, k, vt, qi, ki2, wit, rel_bias, *, bsz):
    n = q.shape[0]
    seq = n // bsz
    t = ATTN_TILE
    assert seq % t == 0
    nq = seq // t
    n_sel = min(DSA_TOPK_MAX, seq // 4)
    tri = (jnp.arange(t)[None, :] < jnp.arange(t)[:, None]).astype(BF16)
    r3 = lambda a: a.reshape(bsz, seq, a.shape[-1])
    tile_spec = lambda width: pl.BlockSpec((None, t, width), lambda b, i: (b, i, 0))
    seq_spec = lambda width: pl.BlockSpec((None, seq, width), lambda b, i: (b, 0, 0))
    out = pl.pallas_call(
        functools.partial(_dsa_kernel, n_sel=n_sel),
        grid=(bsz, nq),
        in_specs=[_smem_spec(), tile_spec(SLOT_WIDTH), seq_spec(SLOT_WIDTH),
                  pl.BlockSpec((nq, N_HEADS * VT_ROWS, t), lambda b, i: (b, 0, 0)),
                  tile_spec(ATTN_WIDTH), seq_spec(128),
                  pl.BlockSpec((IDX_HEADS, t), lambda b, i: (0, b * nq + i)),
                  _const_spec((2, t, t)), _const_spec((t, t))],
        out_specs=tile_spec(ATTN_WIDTH),
        out_shape=jax.ShapeDtypeStruct((bsz, seq, ATTN_WIDTH), BF16),
        scratch_shapes=[pltpu.VMEM((N_HEADS, VT_ROWS, t), F32), pltpu.VMEM((N_HEADS, 1, t), F32),
                        pltpu.VMEM((nq, t, t), F32), pltpu.VMEM((1, t), F32),
                        pltpu.VMEM((N_HEADS, 2 * t, t), F32)],
        compiler_params=_cparams(("arbitrary", "arbitrary")),
        name="dsa_attention",
    )(rel_bias, r3(q), r3(k), vt, r3(qi), r3(ki2), wit, jnp.asarray(_bucket_tiles(t)), tri)
    return out.reshape(n, ATTN_WIDTH)


def kernel(x, rel_bias, ffn1_norm, ffn1_w_gate, ffn1_w_up, ffn1_w_down, mix_norm, ffn2_norm, ffn2_w_gate, ffn2_w_up, ffn2_w_down, ev_w_in, ev_conv_w, ev_conv_b, ev_ra_w, ev_ra_b, ev_ix_w, ev_ix_b, ev_lambda, ev_q_norm, ev_k_norm, ev_w_out, od_w_in, od_dw_w, od_dw_b, od_ln_g, od_ln_b, od_q_norm, od_k_norm, od_w_out):
    bsz, seq, d = x.shape
    depth = ffn1_norm.shape[0]
    h = x.reshape(bsz * seq, d)
    for i in range(depth):
        h = _ffn(h, ffn1_norm, ffn1_w_gate, ffn1_w_up, ffn1_w_down, i)
        j = i // 2
        if i % 2 == 0:
            gate, xr, q, k, vt, kmean = _mixin_even(h, mix_norm[i], ev_w_in[j], ev_q_norm[j], ev_k_norm[j],
                                                    n_blocks=seq // MOBA_BLOCK)
            ya = _lru(gate, xr, ev_conv_w[j], ev_conv_b[j], ev_ra_w[j], ev_ra_b[j],
                      ev_ix_w[j], ev_ix_b[j], ev_lambda[j], bsz=bsz)
            yb = _moba(q, k, vt, kmean, rel_bias, bsz=bsz)
            mix = (ya, yb, ev_w_out[j])
        else:
            yc, q, k, vt, qi, ki2, wit = _mixin_odd(h, mix_norm[i], od_w_in[j], od_q_norm[j], od_k_norm[j],
                                                    od_dw_w[j], od_dw_b[j], od_ln_g[j], od_ln_b[j], bsz=bsz)
            yd = _dsa(q, k, vt, qi, ki2, wit, rel_bias, bsz=bsz)
            mix = (yc, yd, od_w_out[j])
        h = _ffn(h, ffn2_norm, ffn2_w_gate, ffn2_w_up, ffn2_w_down, i, mix)
    return h.reshape(bsz, seq, d)
```

```python
import functools
import math

import numpy as np
import jax
import jax.numpy as jnp
from jax import lax
from jax.experimental import pallas as pl
from jax.experimental.pallas import tpu as pltpu

F32 = jnp.float32
BF16 = jnp.bfloat16

N_HEADS = 8
HEAD_DIM = 64
ATTN_WIDTH = N_HEADS * HEAD_DIM
LRU_C = 8.0
MOBA_BLOCK = 256
MOBA_TOPK = 3
IDX_HEADS = 8
IDX_DIM = 64
DSA_TOPK_MAX = 256
REL_BUCKETS = 32
REL_MAX_EXACT = REL_BUCKETS // 2
REL_MAX_DIST = 128
EPS = 1e-6
NEG = -1e30
M_INIT = -1e29
ATTN_TILE = 256
SUBLANES = 8
CONV_ROWS = 64
SLOT = 128
SLOT_WIDTH = N_HEADS * SLOT
FEAT0 = HEAD_DIM
VT_ROWS = 80
HEADS_AHEAD = 4
BISECT_ITERS = 32
BISECT_CHECK_EVERY = 4
F32_TINY = float(np.finfo(np.float32).tiny)
LOG2E = math.log2(math.e)
Q_SCALE = HEAD_DIM ** -0.5 * LOG2E
VMEM_LIMIT = 56 * 1024 * 1024
FFN_VMEM_LIMIT = 60 * 1024 * 1024


def _cparams(sem):
    return pltpu.CompilerParams(dimension_semantics=sem, vmem_limit_bytes=VMEM_LIMIT)


def _dot(a, b):
    return jnp.dot(a, b, preferred_element_type=F32)


def _dot_nt(a, b):
    return lax.dot_general(a, b, (((1,), (1,)), ((), ())), preferred_element_type=F32)


def _split_bf16(x):
    hi = x.astype(BF16)
    lo = (x - hi.astype(F32)).astype(BF16)
    return hi, lo


def _rms_rows(x, g):
    return x * lax.rsqrt(jnp.mean(x * x, axis=-1, keepdims=True) + EPS) * g


def _const_spec(shape):
    nd = len(shape)
    return pl.BlockSpec(shape, lambda *_: (0,) * nd, pipeline_mode=pl.Buffered(1))


def _slot(h):
    return slice(h * SLOT, (h + 1) * SLOT)


def _vt_rows(h):
    return slice(h * VT_ROWS, (h + 1) * VT_ROWS)


def _to_slots(x):
    low = lax.broadcasted_iota(jnp.int32, (1, SLOT), 1) < HEAD_DIM
    slots = []
    for p in range(N_HEADS // 2):
        chunk = x[:, p * SLOT:(p + 1) * SLOT]
        slots.append(jnp.where(low, chunk, 0.0))
        slots.append(jnp.where(low, pltpu.roll(chunk, HEAD_DIM, 1), 0.0))
    return slots


def _ffn_kernel(*refs, n_chunks, has_mix):
    if has_mix:
        x_ref, ya_ref, yb_ref, wo_ref, g_ref, wg_ref, wu_ref, wd_ref, o_ref, wg_s, wu_s, wd_s = refs
    else:
        x_ref, g_ref, wg_ref, wu_ref, wd_ref, o_ref, wg_s, wu_s, wd_s = refs
    step = pl.program_id(0)

    @pl.when(step < n_chunks)
    def _():
        wg_s[step] = wg_ref[...].astype(BF16)
        wu_s[step] = wu_ref[...].astype(BF16)
        wd_s[step] = wd_ref[...].astype(BF16)

    @pl.when(step >= n_chunks)
    def _():
        x = x_ref[...]
        if has_mix:
            w = ya_ref.shape[1]
            x = x + _dot(ya_ref[...], wo_ref[0:w, :]) + _dot(yb_ref[...], wo_ref[w:2 * w, :])
        hn = _rms_rows(x, g_ref[...]).astype(BF16)
        acc = jnp.zeros(x.shape, F32)
        for c in range(n_chunks):
            gt = _dot(hn, wg_s[c])
            ut = _dot(hn, wu_s[c])
            a = (gt * jax.nn.sigmoid(gt) * ut).astype(BF16)
            acc = acc + _dot(a, wd_s[c])
        o_ref[...] = x + 0.5 * acc


def _ffn(x, g, wg, wu, wd, layer, mix=None, *, tm=1024, ff_chunk=256):
    n, d = x.shape
    d_ff = wg.shape[2]
    assert n % tm == 0 and d_ff % ff_chunk == 0
    nc = d_ff // ff_chunk
    tile = lambda s: (jnp.maximum(s - nc, 0), 0)
    chunk_col = lambda s: (layer, 0, jnp.minimum(s, nc - 1))
    chunk_row = lambda s: (layer, jnp.minimum(s, nc - 1), 0)
    in_specs = [pl.BlockSpec((tm, d), tile)]
    args = [x]
    scratch = [pltpu.VMEM((nc, d, ff_chunk), BF16), pltpu.VMEM((nc, d, ff_chunk), BF16),
               pltpu.VMEM((nc, ff_chunk, d), BF16)]
    if mix is not None:
        ya, yb, w_out = mix
        w = ya.shape[1]
        assert w_out.shape == (2 * w, d)
        in_specs += [pl.BlockSpec((tm, w), tile), pl.BlockSpec((tm, w), tile), _const_spec((2 * w, d))]
        args += [ya, yb, w_out.astype(BF16)]
    in_specs += [_const_spec((1, d)), pl.BlockSpec((None, d, ff_chunk), chunk_col),
                 pl.BlockSpec((None, d, ff_chunk), chunk_col), pl.BlockSpec((None, ff_chunk, d), chunk_row)]
    args += [g[layer].reshape(1, d), wg, wu, wd]
    return pl.pallas_call(
        functools.partial(_ffn_kernel, n_chunks=nc, has_mix=mix is not None),
        grid=(nc + n // tm,),
        in_specs=in_specs,
        out_specs=pl.BlockSpec((tm, d), tile),
        out_shape=jax.ShapeDtypeStruct((n, d), F32),
        scratch_shapes=scratch,
        compiler_params=pltpu.CompilerParams(dimension_semantics=("arbitrary",),
                                             vmem_limit_bytes=FFN_VMEM_LIMIT),
        name="ffn_mix" if mix is not None else "ffn",
    )(*args)


def _slot_rms(xs, g):
    ms = jnp.sum(xs * xs, axis=-1, keepdims=True) * (1.0 / HEAD_DIM)
    return xs * lax.rsqrt(ms + EPS) * g


def _store_q_slots(z, gain_ref, q_o):
    for h, zs in enumerate(_to_slots(z)):
        q_o[:, _slot(h)] = (_slot_rms(zs, gain_ref[...]) * Q_SCALE).astype(BF16)


def _store_k_slots(z, gain_ref, k_o, one_lanes):
    slots = [_slot_rms(zs, gain_ref[...]) for zs in _to_slots(z)]
    for h, ks in enumerate(slots):
        k_o[:, _slot(h)] = (ks if one_lanes is None else jnp.where(one_lanes, 1.0, ks)).astype(BF16)
    return slots


def _store_vt_slots(zt, v_o):
    pad = VT_ROWS - HEAD_DIM
    tail = (lax.broadcasted_iota(jnp.int32, (pad, ATTN_TILE), 0) == 0).astype(BF16)
    for h in range(N_HEADS):
        for r in range(zt.shape[1] // ATTN_TILE):
            feat = zt[h * HEAD_DIM:(h + 1) * HEAD_DIM, r * ATTN_TILE:(r + 1) * ATTN_TILE]
            v_o[r, h * VT_ROWS:h * VT_ROWS + HEAD_DIM, :] = feat.astype(BF16)
            v_o[r, h * VT_ROWS + HEAD_DIM:(h + 1) * VT_ROWS, :] = tail


def _pipelined(stages):
    nxt = stages[0][0]()
    for j, (_, consume) in enumerate(stages):
        cur = nxt
        if j + 1 < len(stages):
            nxt = stages[j + 1][0]()
        consume(cur)


def _mixin_even_kernel(x_ref, g_ref, w_ref, wvt_ref, qn_ref, kn_ref,
                       gate_o, xr_o, q_o, k_o, v_o, km_o, *, n_blocks):
    tm = x_ref.shape[0]
    hn = _rms_rows(x_ref[...], g_ref[...]).astype(BF16)
    w = ATTN_WIDTH
    proj = lambda c: (lambda: _dot(hn, w_ref[:, c * w:(c + 1) * w]))
    blocks_per_tile = tm // MOBA_BLOCK
    row_block = lax.broadcasted_iota(jnp.int32, (tm, 1), 0) // MOBA_BLOCK
    block = (pl.program_id(0) * blocks_per_tile + row_block) % n_blocks
    block_lane = lax.broadcasted_iota(jnp.int32, (1, SLOT), 1) == FEAT0 + block

    def store_k(z):
        for h, ks in enumerate(_store_k_slots(z, kn_ref, k_o, block_lane)):
            for r in range(blocks_per_tile):
                km_o[r, :, _slot(h)] = jnp.mean(ks[r * MOBA_BLOCK:(r + 1) * MOBA_BLOCK], axis=0, keepdims=True)

    def store(ref):
        def consume(z):
            ref[...] = z
        return consume

    _pipelined([(proj(2), lambda z: _store_q_slots(z, qn_ref, q_o)), (proj(3), store_k),
                (lambda: _dot_nt(wvt_ref[...], hn), lambda zt: _store_vt_slots(zt, v_o)),
                (proj(0), store(gate_o)), (proj(1), store(xr_o))])


def _vt_spec(tm):
    return pl.BlockSpec((tm // ATTN_TILE, N_HEADS * VT_ROWS, ATTN_TILE), lambda i: (i, 0, 0))


def _vt_shape(n):
    return jax.ShapeDtypeStruct((n // ATTN_TILE, N_HEADS * VT_ROWS, ATTN_TILE), BF16)


def _slot_gain(g):
    return jnp.pad(g.astype(F32), (0, SLOT - HEAD_DIM)).reshape(1, SLOT)


def _mixin_even(x, g, w_in, q_norm, k_norm, *, n_blocks, tm=512):
    n, d = x.shape
    w = ATTN_WIDTH
    assert n % tm == 0 and tm % MOBA_BLOCK == 0 and tm % ATTN_TILE == 0 and w_in.shape[1] == 5 * w
    assert FEAT0 + n_blocks <= SLOT
    row = lambda i: (i, 0)
    blk = tm // MOBA_BLOCK
    slot_shape = jax.ShapeDtypeStruct((n, SLOT_WIDTH), BF16)
    return pl.pallas_call(
        functools.partial(_mixin_even_kernel, n_blocks=n_blocks),
        grid=(n // tm,),
        in_specs=[pl.BlockSpec((tm, d), row), _const_spec((1, d)), _const_spec((d, 4 * w)),
                  _const_spec((w, d)), _const_spec((1, SLOT)), _const_spec((1, SLOT))],
        out_specs=[pl.BlockSpec((tm, w), row)] * 2 + [pl.BlockSpec((tm, SLOT_WIDTH), row)] * 2
        + [_vt_spec(tm), pl.BlockSpec((blk, 1, SLOT_WIDTH), lambda i: (i, 0, 0))],
        out_shape=[jax.ShapeDtypeStruct((n, w), F32), jax.ShapeDtypeStruct((n, w), F32),
                   slot_shape, slot_shape, _vt_shape(n),
                   jax.ShapeDtypeStruct((n // MOBA_BLOCK, 1, SLOT_WIDTH), F32)],
        compiler_params=_cparams(("parallel",)),
        name="mixin_even",
    )(x, g.reshape(1, d), w_in[:, :4 * w].astype(BF16), w_in[:, 4 * w:].T.astype(BF16),
      _slot_gain(q_norm), _slot_gain(k_norm))


def _conv_ln_silu(c, first, w_ref, b_ref, g_ref, beta_ref, o_ref, cbuf, sh_ref, *, halo):
    ts = c.shape[0]
    width = w_ref.shape[0]

    @pl.when(first)
    def _():
        cbuf[0:halo, :] = jnp.zeros((halo, cbuf.shape[1]), F32)

    cbuf[halo:halo + ts, :] = c
    base = halo - (width - 1)
    span = sh_ref.shape[1]
    for r in range(1, SUBLANES):
        sh_ref[r - 1] = cbuf[r:r + span, :]

    def chunk(c0):
        y = jnp.broadcast_to(b_ref[...], (CONV_ROWS, cbuf.shape[1]))
        for k in range(width):
            r = (base + k) % SUBLANES
            u0 = c0 + base + k - r
            win = cbuf[u0:u0 + CONV_ROWS, :] if r == 0 else sh_ref[r - 1, u0:u0 + CONV_ROWS, :]
            y = y + w_ref[k:k + 1, :] * win
        mu = jnp.mean(y, axis=-1, keepdims=True)
        yc = y - mu
        var = jnp.mean(yc * yc, axis=-1, keepdims=True)
        z = yc * lax.rsqrt(var + EPS) * g_ref[...] + beta_ref[...]
        o_ref[c0:c0 + CONV_ROWS, :] = (z * jax.nn.sigmoid(z)).astype(BF16)

    def close():
        cbuf[0:halo, :] = cbuf[ts:ts + halo, :]

    return [functools.partial(chunk, c0) for c0 in range(0, ts, CONV_ROWS)], close


def _mixin_odd_kernel(x_ref, g_ref, w_ref, wvt_ref, wki_ref, wwit_ref, qn_ref, kn_ref,
                      dw_ref, db_ref, lg_ref, lb_ref,
                      yc_o, q_o, k_o, v_o, qi_o, ki_o, wi_o, cbuf, sh_ref, *, tiles_per_seq, halo):
    hn = _rms_rows(x_ref[...], g_ref[...]).astype(BF16)
    w = ATTN_WIDTH
    proj = lambda c, n=1: _dot(hn, w_ref[:, c * w:(c + n) * w])
    first = pl.program_id(0) % tiles_per_seq == 0

    z_glu = proj(0, 2)
    z_q = proj(2)
    chunks, close = _conv_ln_silu(z_glu[:, 0:w] * jax.nn.sigmoid(z_glu[:, w:2 * w]), first,
                                  dw_ref, db_ref, lg_ref, lb_ref, yc_o, cbuf, sh_ref, halo=halo)
    per_stage = -(-len(chunks) // 4)
    run_chunks = lambda j: [f() for f in chunks[j * per_stage:(j + 1) * per_stage]]
    z_k = proj(3)
    _store_q_slots(z_q, qn_ref, q_o)
    run_chunks(0)
    z_vt = _dot_nt(wvt_ref[...], hn)
    _store_k_slots(z_k, kn_ref, k_o, None)
    run_chunks(1)
    z_qi = proj(4)
    _store_vt_slots(z_vt, v_o)
    run_chunks(2)
    z_ki = _dot(hn, wki_ref[...])
    z_wit = _dot_nt(wwit_ref[...], hn)
    qi_o[...] = z_qi.astype(BF16)
    run_chunks(3)
    close()
    ki_o[...] = z_ki.astype(BF16)
    wi_o[...] = z_wit * (IDX_DIM ** -0.5 * IDX_HEADS ** -0.5)


def _mixin_odd(x, g, w_in, q_norm, k_norm, dw_w, dw_b, ln_g, ln_b, *, bsz, tm=512, halo=32):
    n, d = x.shape
    w = ATTN_WIDTH
    width = dw_w.shape[0]
    assert n % tm == 0 and tm % ATTN_TILE == 0 and w_in.shape[1] == 6 * w + IDX_DIM + IDX_HEADS
    assert (n // bsz) % tm == 0 and width - 1 <= halo <= tm and halo % SUBLANES == 0 and tm % CONV_ROWS == 0
    w_main = jnp.concatenate([w_in[:, :4 * w], w_in[:, 5 * w:6 * w]], axis=1).astype(BF16)
    w_vt = w_in[:, 4 * w:5 * w].T.astype(BF16)
    w_ki = w_in[:, 6 * w:6 * w + IDX_DIM]
    w_ki2 = jnp.concatenate([w_ki, w_ki], axis=1).astype(BF16)
    w_wit = w_in[:, 6 * w + IDX_DIM:].T.astype(BF16)
    row = lambda i: (i, 0)
    vec = lambda v: v.reshape(1, w).astype(F32)
    slot_shape = jax.ShapeDtypeStruct((n, SLOT_WIDTH), BF16)
    return pl.pallas_call(
        functools.partial(_mixin_odd_kernel, tiles_per_seq=(n // bsz) // tm, halo=halo),
        grid=(n // tm,),
        in_specs=[pl.BlockSpec((tm, d), row), _const_spec((1, d)), _const_spec((d, 5 * w)),
                  _const_spec((w, d)), _const_spec((d, 2 * IDX_DIM)), _const_spec((IDX_HEADS, d)),
                  _const_spec((1, SLOT)), _const_spec((1, SLOT)),
                  _const_spec((width, w)), _const_spec((1, w)), _const_spec((1, w)), _const_spec((1, w))],
        out_specs=[pl.BlockSpec((tm, w), row)] + [pl.BlockSpec((tm, SLOT_WIDTH), row)] * 2
        + [_vt_spec(tm), pl.BlockSpec((tm, w), row), pl.BlockSpec((tm, 128), row),
           pl.BlockSpec((IDX_HEADS, tm), lambda i: (0, i))],
        out_shape=[jax.ShapeDtypeStruct((n, w), BF16), slot_shape, slot_shape, _vt_shape(n),
                   jax.ShapeDtypeStruct((n, w), BF16),
                   jax.ShapeDtypeStruct((n, 128), BF16), jax.ShapeDtypeStruct((IDX_HEADS, n), F32)],
        scratch_shapes=[pltpu.VMEM((tm + halo, w), F32),
                        pltpu.VMEM((SUBLANES - 1, tm + halo - SUBLANES, w), F32)],
        compiler_params=_cparams(("arbitrary",)),
        name="mixin_odd",
    )(x, g.reshape(1, d), w_main, w_vt, w_ki2, w_wit, _slot_gain(q_norm), _slot_gain(k_norm),
      dw_w, vec(dw_b), vec(ln_g), vec(ln_b))


def _lru_kernel(gate_ref, xr_ref, cw_ref, cb_ref, wa_ref, ba_ref, wx_ref, bx_ref, sp_ref,
                o_ref, xbuf, a_s, u_s, h_s, hc, *, ts):
    j = pl.program_id(1)

    @pl.when(j == 0)
    def _():
        xbuf[0:8, :] = jnp.zeros((8, xbuf.shape[1]), F32)
        hc[...] = jnp.zeros(hc.shape, F32)

    xbuf[8:8 + ts, :] = xr_ref[...]
    xc = cb_ref[...] + cw_ref[0:1, :] * xbuf[5:5 + ts, :]
    for k in range(1, 4):
        xc = xc + cw_ref[k:k + 1, :] * xbuf[5 + k:5 + k + ts, :]
    xbuf[0:8, :] = xbuf[ts:ts + 8, :]

    xcb = xc.astype(BF16)
    r = jax.nn.sigmoid(_dot(xcb, wa_ref[...]) + ba_ref[...])
    ig = jax.nn.sigmoid(_dot(xcb, wx_ref[...]) + bx_ref[...])
    log_a = -LRU_C * r * sp_ref[...]
    a = jnp.exp(log_a)
    a_s[...] = a
    u_s[...] = jnp.sqrt(-jnp.tanh(log_a) * (a * a + 1.0)) * (ig * xc)

    row = lax.broadcasted_iota(jnp.int32, (8, a_s.shape[1]), 0)

    def body(g, carry):
        r0 = pl.multiple_of(g * 8, 8)
        a = a_s[pl.ds(r0, 8), :]
        u = u_s[pl.ds(r0, 8), :]
        for s in (1, 2, 4):
            ok = row >= s
            a_sh = jnp.where(ok, pltpu.roll(a, s, 0), 1.0)
            u_sh = jnp.where(ok, pltpu.roll(u, s, 0), 0.0)
            u = a * u_sh + u
            a = a * a_sh
        h = a * carry + u
        h_s[pl.ds(r0, 8), :] = h
        return h[7:8, :]

    hc[...] = lax.fori_loop(0, ts // 8, body, hc[...], unroll=4)
    o_ref[...] = (h_s[...] * jax.nn.gelu(gate_ref[...])).astype(BF16)


def _block_diag(wb):
    nb, bs, _ = wb.shape
    eye = jnp.eye(nb, dtype=wb.dtype)
    return (eye[:, None, :, None] * wb[:, :, None, :]).reshape(nb * bs, nb * bs)


def _lru(gate, xr, conv_w, conv_b, ra_w, ra_b, ix_w, ix_b, lam, *, bsz, ts=256):
    n, w = xr.shape
    seq = n // bsz
    assert seq % ts == 0
    nt = seq // ts
    row = lambda b, j: (b * nt + j, 0)
    vec = lambda v: v.reshape(1, w).astype(F32)
    return pl.pallas_call(
        functools.partial(_lru_kernel, ts=ts),
        grid=(bsz, nt),
        in_specs=[pl.BlockSpec((ts, w), row), pl.BlockSpec((ts, w), row),
                  _const_spec((conv_w.shape[0], w)), _const_spec((1, w)),
                  _const_spec((w, w)), _const_spec((1, w)), _const_spec((w, w)), _const_spec((1, w)),
                  _const_spec((1, w))],
        out_specs=pl.BlockSpec((ts, w), row),
        out_shape=jax.ShapeDtypeStruct((n, w), BF16),
        scratch_shapes=[pltpu.VMEM((ts + 8, w), F32), pltpu.VMEM((ts, w), F32),
                        pltpu.VMEM((ts, w), F32), pltpu.VMEM((ts, w), F32), pltpu.VMEM((1, w), F32)],
        compiler_params=_cparams(("arbitrary", "arbitrary")),
        name="rg_lru",
    )(gate, xr, conv_w, vec(conv_b), _block_diag(ra_w).astype(BF16), vec(ra_b),
      _block_diag(ix_w).astype(BF16), vec(ix_b), vec(jax.nn.softplus(-lam)))


def _bucket_tiles(t):
    assert t > REL_MAX_DIST
    n = np.arange(2 * t)
    nf = np.maximum(n, 1).astype(np.float32)
    large = REL_MAX_EXACT + (np.log(nf / np.float32(REL_MAX_EXACT))
                             / np.float32(math.log(REL_MAX_DIST / REL_MAX_EXACT))
                             * np.float32(REL_BUCKETS - REL_MAX_EXACT)).astype(np.int32)
    bucket = np.where(n < REL_MAX_EXACT, n, np.minimum(large, REL_BUCKETS - 1)).astype(np.int32)
    qry = np.arange(t)[None, :]
    key = np.arange(t)[:, None]
    return np.stack([bucket[np.maximum(qry - key, 0)], bucket[t + qry - key]])


def _build_bias(idx_ref, rb_ref, bias_ref):
    t = idx_ref.shape[1]
    causal = (lax.broadcasted_iota(jnp.int32, (t, t), 0) <= lax.broadcasted_iota(jnp.int32, (t, t), 1))
    for h in range(N_HEADS):
        far = rb_ref[REL_BUCKETS - 1, h]
        for which in (0, 1):
            idx = idx_ref[which]
            acc = jnp.zeros((t, t), F32)
            for b in range(REL_BUCKETS - 1):
                acc = jnp.where(idx == b, (rb_ref[b, h] - far) * LOG2E, acc)
            if which == 0:
                bias_ref[h, t:2 * t, :] = jnp.where(causal, acc, NEG)
            else:
                bias_ref[h, 0:t, :] = acc


def _attn_init(m_ref, acc_ref):
    m_ref[...] = jnp.full(m_ref.shape, M_INIT, F32)
    acc_ref[...] = jnp.zeros(acc_ref.shape, F32)


def _softmax_step(s, h, vt_h, m_ref, acc_ref):
    m_old = m_ref[h]
    m_new = jnp.maximum(m_old, jnp.max(s, axis=0, keepdims=True))
    p = jnp.exp2(s - m_new)
    m_ref[h] = m_new
    acc_ref[h] = jnp.exp2(m_old - m_new) * acc_ref[h] + _dot(vt_h, p.astype(BF16))


def _heads_pipelined(logits, vt_slot, m_ref, acc_ref):
    ahead = [logits(h) for h in range(HEADS_AHEAD)]
    for h in range(N_HEADS):
        if h + HEADS_AHEAD < N_HEADS:
            ahead.append(logits(h + HEADS_AHEAD))
        _softmax_step(ahead[h], h, vt_slot(h), m_ref, acc_ref)


FAR_PAIR = ("far", "far")
FAR_ONE = ("far",)
NEAR_DIAG = ("near", "diag")
DIAG_ONE = ("diag",)


def _attend_tiles(kt, kinds, q_slot, k_ref, vt_ref, bias_ref, mask_add, m_ref, acc_ref):
    t = ATTN_TILE
    n = len(kinds)
    rows = pl.ds(pl.multiple_of(kt * t, t), n * t)

    def logits(h):
        s = _dot_nt(k_ref[rows, _slot(h)], q_slot(h))
        if kinds == NEAR_DIAG:
            s = s + bias_ref[h]
        elif kinds == DIAG_ONE:
            s = s + bias_ref[h, t:2 * t, :]
        if mask_add is not None:
            s = s + mask_add
        return s

    def vt_slot(h):
        parts = [vt_ref[kt + a, _vt_rows(h), :] for a in range(n)]
        return parts[0] if n == 1 else jnp.concatenate(parts, axis=1)

    _heads_pipelined(logits, vt_slot, m_ref, acc_ref)


def _for_causal_tiles(i, tiles):
    n_far = jnp.maximum(i - 1, 0)

    def pair(j, carry):
        tiles(2 * j, FAR_PAIR)
        return carry

    lax.fori_loop(0, n_far // 2, pair, 0)

    @pl.when(n_far % 2 == 1)
    def _():
        tiles(n_far - 1, FAR_ONE)

    @pl.when(i >= 1)
    def _():
        tiles(i - 1, NEAR_DIAG)

    @pl.when(i == 0)
    def _():
        tiles(i, DIAG_ONE)


def _attn_finish(o_ref, acc_ref):
    parts = []
    for h in range(N_HEADS):
        acc = acc_ref[h]
        parts.append(acc[0:HEAD_DIM] * (1.0 / acc[FEAT0:FEAT0 + 1]))
    o_ref[...] = jnp.concatenate(parts, axis=0).T.astype(BF16)


def _smem_spec():
    return pl.BlockSpec(memory_space=pltpu.SMEM)


def _moba_kernel(rb_ref, q_ref, k_ref, vt_ref, km_ref, idx_ref, o_ref,
                 acc_ref, m_ref, qs_ref, bias_ref, *, n_blocks):
    i = pl.program_id(1)
    t = ATTN_TILE
    nbp = -(-n_blocks // 8) * 8

    @pl.when((pl.program_id(0) == 0) & (i == 0))
    def _():
        _build_bias(idx_ref, rb_ref, bias_ref)

    _attn_init(m_ref, acc_ref)

    blk = lax.broadcasted_iota(jnp.int32, (nbp, t), 0)
    past = blk < i
    for h in range(N_HEADS):
        qh = q_ref[:, _slot(h)]
        km_hi, km_lo = _split_bf16(km_ref[:, _slot(h)])
        gate_t = _dot_nt(km_hi, qh) + _dot_nt(km_lo, qh)
        g = jnp.where(past, gate_t[FEAT0:FEAT0 + nbp, :], NEG)
        rank = jnp.zeros((nbp, t), F32)
        for j in range(n_blocks):
            gj = g[j:j + 1, :]
            beats = (gj > g) | ((gj == g) & (blk > j))
            rank = rank + jnp.where(beats, 1.0, 0.0)
        flag = jnp.where(past & (rank >= MOBA_TOPK), NEG, 0.0)
        flag_t = jnp.concatenate([jnp.zeros((FEAT0, t), F32), flag,
                                  jnp.zeros((SLOT - FEAT0 - nbp, t), F32)], axis=0)
        qs_ref[:, _slot(h)] = (qh.astype(F32) + flag_t.T).astype(BF16)

    def tiles(kt, kinds):
        _attend_tiles(kt, kinds, lambda h: qs_ref[:, _slot(h)], k_ref, vt_ref, bias_ref, None,
                      m_ref, acc_ref)

    _for_causal_tiles(i, tiles)
    _attn_finish(o_ref, acc_ref)


def _moba(q, k, vt, kmean, rel_bias, *, bsz):
    n = q.shape[0]
    seq = n // bsz
    t = ATTN_TILE
    assert seq % t == 0 and t == MOBA_BLOCK
    nq = seq // t
    assert FEAT0 + nq <= SLOT
    km = jnp.pad(kmean.reshape(bsz, nq, SLOT_WIDTH), ((0, 0), (FEAT0, SLOT - FEAT0 - nq), (0, 0)))
    r3 = lambda a: a.reshape(bsz, seq, SLOT_WIDTH)
    seq_spec = pl.BlockSpec((None, seq, SLOT_WIDTH), lambda b, i: (b, 0, 0))
    out = pl.pallas_call(
        functools.partial(_moba_kernel, n_blocks=nq),
        grid=(bsz, nq),
        in_specs=[_smem_spec(),
                  pl.BlockSpec((None, t, SLOT_WIDTH), lambda b, i: (b, i, 0)), seq_spec,
                  pl.BlockSpec((nq, N_HEADS * VT_ROWS, t), lambda b, i: (b, 0, 0)),
                  pl.BlockSpec((None, SLOT, SLOT_WIDTH), lambda b, i: (b, 0, 0)),
                  _const_spec((2, t, t))],
        out_specs=pl.BlockSpec((None, t, ATTN_WIDTH), lambda b, i: (b, i, 0)),
        out_shape=jax.ShapeDtypeStruct((bsz, seq, ATTN_WIDTH), BF16),
        scratch_shapes=[pltpu.VMEM((N_HEADS, VT_ROWS, t), F32), pltpu.VMEM((N_HEADS, 1, t), F32),
                        pltpu.VMEM((t, SLOT_WIDTH), BF16),
                        pltpu.VMEM((N_HEADS, 2 * t, t), F32)],
        compiler_params=_cparams(("arbitrary", "arbitrary")),
        name="moba_attention",
    )(rel_bias, r3(q), r3(k), vt, km, jnp.asarray(_bucket_tiles(t)))
    return out.reshape(n, ATTN_WIDTH)


def _dsa_kernel(rb_ref, q_ref, k_ref, vt_ref, qi_ref, ki_ref, wit_ref, idx_ref, tri_ref, o_ref,
                acc_ref, m_ref, isc_ref, bc_ref, bias_ref, *, n_sel):
    i = pl.program_id(1)
    t = ATTN_TILE
    key = lax.broadcasted_iota(jnp.int32, (t, t), 0)
    qry = lax.broadcasted_iota(jnp.int32, (t, t), 1)
    lane128 = lax.broadcasted_iota(jnp.int32, (1, 128), 1)

    @pl.when((pl.program_id(0) == 0) & (i == 0))
    def _():
        _build_bias(idx_ref, rb_ref, bias_ref)

    _attn_init(m_ref, acc_ref)

    def index_tiles(kt, n, diag):
        rows = pl.ds(pl.multiple_of(kt * t, t), n * t)
        ki2 = ki_ref[rows, :]
        acc = jnp.zeros((n * t, t), F32)
        for pr in range(IDX_HEADS // 2):
            q2 = qi_ref[:, pr * 128:(pr + 1) * 128]
            for half in range(2):
                h = 2 * pr + half
                hm = (lane128 >= half * IDX_DIM) & (lane128 < (half + 1) * IDX_DIM)
                s = _dot_nt(ki2, jnp.where(hm, q2, jnp.zeros((), BF16)))
                acc = acc + jnp.maximum(s, 0.0) * wit_ref[h:h + 1, :]
        if diag:
            acc = jnp.where(key <= qry, acc, -jnp.inf)
        for a in range(n):
            isc_ref[kt + a] = acc[a * t:(a + 1) * t]

    def index_pair(j, carry):
        index_tiles(2 * j, 2, False)
        return carry

    lax.fori_loop(0, i // 2, index_pair, 0)

    @pl.when(i % 2 == 1)
    def _():
        index_tiles(i - 1, 1, False)

    index_tiles(i, 1, True)

    def fold8(x, op):
        return op(x.reshape(t // 8, 8, t), axis=0)

    def minmax_body(kt, carry):
        mn, mx = carry
        x = isc_ref[kt]
        mx = jnp.maximum(mx, fold8(x, jnp.max))
        mn = jnp.minimum(mn, fold8(jnp.where(x == -jnp.inf, jnp.inf, x), jnp.min))
        return mn, mx

    mn, mx = lax.fori_loop(0, i + 1, minmax_body,
                           (jnp.full((8, t), jnp.inf, F32), jnp.full((8, t), -jnp.inf, F32)))
    lo0 = jnp.min(mn, axis=0, keepdims=True)
    mx = jnp.max(mx, axis=0, keepdims=True)
    hi0 = mx + jnp.abs(mx) * 1e-3 + 1e-30
    n_valid = (i * t + 1 + lax.broadcasted_iota(jnp.int32, (1, t), 1)).astype(F32)
    want = jnp.minimum(n_valid, float(n_sel))

    def count_ge(thr):
        def one(kt, c):
            return c + fold8(jnp.where(isc_ref[kt] >= thr, 1.0, 0.0), jnp.sum)

        def pair(j, c):
            return one(2 * j + 1, one(2 * j, c))

        c = lax.fori_loop(0, (i + 1) // 2, pair, jnp.zeros((8, t), F32))
        c = lax.cond((i + 1) % 2 == 1, lambda c: one(i, c), lambda c: c, c)
        return jnp.sum(c, axis=0, keepdims=True)

    c_nonneg = count_ge(0.0)
    c_pos = count_ge(F32_TINY)
    above = want <= c_pos
    zero_tie = (want > c_pos) & (want <= c_nonneg)
    lo1 = jnp.where(above, F32_TINY, jnp.where(zero_tie, 0.0, lo0))
    hi1 = jnp.where(above, hi0, jnp.where(zero_tie, F32_TINY, 0.0))
    c_lo1 = jnp.where(above, c_pos, jnp.where(zero_tie, c_nonneg, n_valid))
    c_hi1 = jnp.where(above, 0.0, jnp.where(zero_tie, c_pos, c_nonneg))

    def bisect_step(_, carry):
        lo, hi, c_lo, c_hi = carry
        mid = 0.5 * (lo + hi)
        c_mid = count_ge(mid)
        up = c_mid >= want
        return (jnp.where(up, mid, lo), jnp.where(up, hi, mid),
                jnp.where(up, c_mid, c_lo), jnp.where(up, c_hi, c_mid))

    def bisect_cond(carry):
        it, _, _, c_lo, _ = carry
        return (it < BISECT_ITERS) & (jnp.max(jnp.where(zero_tie, 0.0, c_lo - want)) > 0.0)

    def bisect_body(carry):
        return (carry[0] + BISECT_CHECK_EVERY,) + lax.fori_loop(0, BISECT_CHECK_EVERY, bisect_step, carry[1:])

    _, lo, hi, c_lo, c_hi = lax.while_loop(bisect_cond, bisect_body, (0, lo1, hi1, c_lo1, c_hi1))
    need = want - c_hi
    banded = jnp.max(c_lo - want) > 0.0
    bc_ref[...] = jnp.zeros(bc_ref.shape, F32)

    def select_mask(kt):
        x = isc_ref[kt]

        def plain():
            return jnp.where(x >= lo, 0.0, NEG)

        def with_band():
            band = jnp.where((x >= lo) & (x < hi), 1.0, 0.0)
            before = bc_ref[...] + _dot(tri_ref[...], band.astype(BF16))
            bc_ref[...] = bc_ref[...] + jnp.sum(band, axis=0, keepdims=True)
            return jnp.where((x >= hi) | ((band > 0.0) & (before < need)), 0.0, NEG)

        return lax.cond(banded, with_band, plain)

    def tiles(kt, kinds):
        masks = [select_mask(kt + a) for a in range(len(kinds))]
        mask_add = masks[0] if len(masks) == 1 else jnp.concatenate(masks, axis=0)
        _attend_tiles(kt, kinds, lambda h: q_ref[:, _slot(h)], k_ref, vt_ref, bias_ref, mask_add,
                      m_ref, acc_ref)

    _for_causal_tiles(i, tiles)
    _attn_finish(o_ref, acc_ref)


def _dsa(q, k, vt, qi, ki2, wit, rel_bias, *, bsz):
    n = q.shape[0]
    seq = n // bsz
    t = ATTN_TILE
    assert seq % t == 0
    nq = seq // t
    n_sel = min(DSA_TOPK_MAX, seq // 4)
    tri = (jnp.arange(t)[None, :] < jnp.arange(t)[:, None]).astype(BF16)
    r3 = lambda a: a.reshape(bsz, seq, a.shape[-1])
    tile_spec = lambda width: pl.BlockSpec((None, t, width), lambda b, i: (b, i, 0))
    seq_spec = lambda width: pl.BlockSpec((None, seq, width), lambda b, i: (b, 0, 0))
    out = pl.pallas_call(
        functools.partial(_dsa_kernel, n_sel=n_sel),
        grid=(bsz, nq),
        in_specs=[_smem_spec(), tile_spec(SLOT_WIDTH), seq_spec(SLOT_WIDTH),
                  pl.BlockSpec((nq, N_HEADS * VT_ROWS, t), lambda b, i: (b, 0, 0)),
                  tile_spec(ATTN_WIDTH), seq_spec(128),
                  pl.BlockSpec((IDX_HEADS, t), lambda b, i: (0, b * nq + i)),
                  _const_spec((2, t, t)), _const_spec((t, t))],
        out_specs=tile_spec(ATTN_WIDTH),
        out_shape=jax.ShapeDtypeStruct((bsz, seq, ATTN_WIDTH), BF16),
        scratch_shapes=[pltpu.VMEM((N_HEADS, VT_ROWS, t), F32), pltpu.VMEM((N_HEADS, 1, t), F32),
                        pltpu.VMEM((nq, t, t), F32), pltpu.VMEM((1, t), F32),
                        pltpu.VMEM((N_HEADS, 2 * t, t), F32)],
        compiler_params=_cparams(("arbitrary", "arbitrary")),
        name="dsa_attention",
    )(rel_bias, r3(q), r3(k), vt, r3(qi), r3(ki2), wit, jnp.asarray(_bucket_tiles(t)), tri)
    return out.reshape(n, ATTN_WIDTH)


def kernel(x, rel_bias, ffn1_norm, ffn1_w_gate, ffn1_w_up, ffn1_w_down, mix_norm, ffn2_norm, ffn2_w_gate, ffn2_w_up, ffn2_w_down, ev_w_in, ev_conv_w, ev_conv_b, ev_ra_w, ev_ra_b, ev_ix_w, ev_ix_b, ev_lambda, ev_q_norm, ev_k_norm, ev_w_out, od_w_in, od_dw_w, od_dw_b, od_ln_g, od_ln_b, od_q_norm, od_k_norm, od_w_out):
    bsz, seq, d = x.shape
    depth = ffn1_norm.shape[0]
    h = x.reshape(bsz * seq, d)
    for i in range(depth):
        h = _ffn(h, ffn1_norm, ffn1_w_gate, ffn1_w_up, ffn1_w_down, i)
        j = i // 2
        if i % 2 == 0:
            gate, xr, q, k, vt, kmean = _mixin_even(h, mix_norm[i], ev_w_in[j], ev_q_norm[j], ev_k_norm[j],
                                                    n_blocks=seq // MOBA_BLOCK)
            ya = _lru(gate, xr, ev_conv_w[j], ev_conv_b[j], ev_ra_w[j], ev_ra_b[j],
                      ev_ix_w[j], ev_ix_b[j], ev_lambda[j], bsz=bsz)
            yb = _moba(q, k, vt, kmean, rel_bias, bsz=bsz)
            mix = (ya, yb, ev_w_out[j])
        else:
            yc, q, k, vt, qi, ki2, wit = _mixin_odd(h, mix_norm[i], od_w_in[j], od_q_norm[j], od_k_norm[j],
                                                    od_dw_w[j], od_dw_b[j], od_ln_g[j], od_ln_b[j], bsz=bsz)
            yd = _dsa(q, k, vt, qi, ki2, wit, rel_bias, bsz=bsz)
            mix = (yc, yd, od_w_out[j])
        h = _ffn(h, ffn2_norm, ffn2_w_gate, ffn2_w_up, ffn2_w_down, i, mix)
    return h.reshape(bsz, seq, d)
```

```python
import functools
import math

import numpy as np
import jax
import jax.numpy as jnp
from jax import lax
from jax.experimental import pallas as pl
from jax.experimental.pallas import tpu as pltpu

F32 = jnp.float32
BF16 = jnp.bfloat16

N_HEADS = 8
HEAD_DIM = 64
ATTN_WIDTH = N_HEADS * HEAD_DIM
LRU_C = 8.0
MOBA_BLOCK = 256
MOBA_TOPK = 3
IDX_HEADS = 8
IDX_DIM = 64
DSA_TOPK_MAX = 256
REL_BUCKETS = 32
REL_MAX_EXACT = REL_BUCKETS // 2
REL_MAX_DIST = 128
EPS = 1e-6
NEG = -1e30
M_INIT = -1e29
ATTN_TILE = 256
SUBLANES = 8
CONV_ROWS = 64
SLOT = 128
SLOT_WIDTH = N_HEADS * SLOT
FEAT0 = HEAD_DIM
VT_ROWS = 80
HEADS_AHEAD = 4
BISECT_ITERS = 32
BISECT_CHECK_EVERY = 4
F32_TINY = float(np.finfo(np.float32).tiny)
LOG2E = math.log2(math.e)
Q_SCALE = HEAD_DIM ** -0.5 * LOG2E
VMEM_LIMIT = 56 * 1024 * 1024
FFN_VMEM_LIMIT = 60 * 1024 * 1024


def _cparams(sem):
    return pltpu.CompilerParams(dimension_semantics=sem, vmem_limit_bytes=VMEM_LIMIT)


def _dot(a, b):
    return jnp.dot(a, b, preferred_element_type=F32)


def _dot_nt(a, b):
    return lax.dot_general(a, b, (((1,), (1,)), ((), ())), preferred_element_type=F32)


def _split_bf16(x):
    hi = x.astype(BF16)
    lo = (x - hi.astype(F32)).astype(BF16)
    return hi, lo


def _rms_rows(x, g):
    return x * lax.rsqrt(jnp.mean(x * x, axis=-1, keepdims=True) + EPS) * g


def _const_spec(shape):
    nd = len(shape)
    return pl.BlockSpec(shape, lambda *_: (0,) * nd, pipeline_mode=pl.Buffered(1))


def _slot(h):
    return slice(h * SLOT, (h + 1) * SLOT)


def _vt_rows(h):
    return slice(h * VT_ROWS, (h + 1) * VT_ROWS)


def _to_slots(x):
    low = lax.broadcasted_iota(jnp.int32, (1, SLOT), 1) < HEAD_DIM
    slots = []
    for p in range(N_HEADS // 2):
        chunk = x[:, p * SLOT:(p + 1) * SLOT]
        slots.append(jnp.where(low, chunk, 0.0))
        slots.append(jnp.where(low, pltpu.roll(chunk, HEAD_DIM, 1), 0.0))
    return slots


def _ffn_kernel(*refs, n_chunks, has_mix):
    if has_mix:
        x_ref, ya_ref, yb_ref, wo_ref, g_ref, wg_ref, wu_ref, wd_ref, o_ref, wg_s, wu_s, wd_s = refs
    else:
        x_ref, g_ref, wg_ref, wu_ref, wd_ref, o_ref, wg_s, wu_s, wd_s = refs
    step = pl.program_id(0)

    @pl.when(step < n_chunks)
    def _():
        wg_s[step] = wg_ref[...].astype(BF16)
        wu_s[step] = wu_ref[...].astype(BF16)
        wd_s[step] = wd_ref[...].astype(BF16)

    @pl.when(step >= n_chunks)
    def _():
        x = x_ref[...]
        if has_mix:
            w = ya_ref.shape[1]
            x = x + _dot(ya_ref[...], wo_ref[0:w, :]) + _dot(yb_ref[...], wo_ref[w:2 * w, :])
        hn = _rms_rows(x, g_ref[...]).astype(BF16)
        acc = jnp.zeros(x.shape, F32)
        for c in range(n_chunks):
            gt = _dot(hn, wg_s[c])
            ut = _dot(hn, wu_s[c])
            a = (gt * jax.nn.sigmoid(gt) * ut).astype(BF16)
            acc = acc + _dot(a, wd_s[c])
        o_ref[...] = x + 0.5 * acc


def _ffn(x, g, wg, wu, wd, layer, mix=None, *, tm=1024, ff_chunk=256):
    n, d = x.shape
    d_ff = wg.shape[2]
    assert n % tm == 0 and d_ff % ff_chunk == 0
    nc = d_ff // ff_chunk
    tile = lambda s: (jnp.maximum(s - nc, 0), 0)
    chunk_col = lambda s: (layer, 0, jnp.minimum(s, nc - 1))
    chunk_row = lambda s: (layer, jnp.minimum(s, nc - 1), 0)
    in_specs = [pl.BlockSpec((tm, d), tile)]
    args = [x]
    scratch = [pltpu.VMEM((nc, d, ff_chunk), BF16), pltpu.VMEM((nc, d, ff_chunk), BF16),
               pltpu.VMEM((nc, ff_chunk, d), BF16)]
    if mix is not None:
        ya, yb, w_out = mix
        w = ya.shape[1]
        assert w_out.shape == (2 * w, d)
        in_specs += [pl.BlockSpec((tm, w), tile), pl.BlockSpec((tm, w), tile), _const_spec((2 * w, d))]
        args += [ya, yb, w_out.astype(BF16)]
    in_specs += [_const_spec((1, d)), pl.BlockSpec((None, d, ff_chunk), chunk_col),
                 pl.BlockSpec((None, d, ff_chunk), chunk_col), pl.BlockSpec((None, ff_chunk, d), chunk_row)]
    args += [g[layer].reshape(1, d), wg, wu, wd]
    return pl.pallas_call(
        functools.partial(_ffn_kernel, n_chunks=nc, has_mix=mix is not None),
        grid=(nc + n // tm,),
        in_specs=in_specs,
        out_specs=pl.BlockSpec((tm, d), tile),
        out_shape=jax.ShapeDtypeStruct((n, d), F32),
        scratch_shapes=scratch,
        compiler_params=pltpu.CompilerParams(dimension_semantics=("arbitrary",),
                                             vmem_limit_bytes=FFN_VMEM_LIMIT),
        name="ffn_mix" if mix is not None else "ffn",
    )(*args)


def _slot_rms(xs, g):
    ms = jnp.sum(xs * xs, axis=-1, keepdims=True) * (1.0 / HEAD_DIM)
    return xs * lax.rsqrt(ms + EPS) * g


def _store_q_slots(z, gain_ref, q_o):
    for h, zs in enumerate(_to_slots(z)):
        q_o[:, _slot(h)] = (_slot_rms(zs, gain_ref[...]) * Q_SCALE).astype(BF16)


def _store_k_slots(z, gain_ref, k_o, one_lanes):
    slots = [_slot_rms(zs, gain_ref[...]) for zs in _to_slots(z)]
    for h, ks in enumerate(slots):
        k_o[:, _slot(h)] = (ks if one_lanes is None else jnp.where(one_lanes, 1.0, ks)).astype(BF16)
    return slots


def _store_vt_slots(zt, v_o):
    pad = VT_ROWS - HEAD_DIM
    tail = (lax.broadcasted_iota(jnp.int32, (pad, ATTN_TILE), 0) == 0).astype(BF16)
    for h in range(N_HEADS):
        for r in range(zt.shape[1] // ATTN_TILE):
            feat = zt[h * HEAD_DIM:(h + 1) * HEAD_DIM, r * ATTN_TILE:(r + 1) * ATTN_TILE]
            v_o[r, h * VT_ROWS:h * VT_ROWS + HEAD_DIM, :] = feat.astype(BF16)
            v_o[r, h * VT_ROWS + HEAD_DIM:(h + 1) * VT_ROWS, :] = tail


def _pipelined(stages):
    nxt = stages[0][0]()
    for j, (_, consume) in enumerate(stages):
        cur = nxt
        if j + 1 < len(stages):
            nxt = stages[j + 1][0]()
        consume(cur)


def _mixin_even_kernel(x_ref, g_ref, w_ref, wvt_ref, qn_ref, kn_ref,
                       gate_o, xr_o, q_o, k_o, v_o, km_o, *, n_blocks):
    tm = x_ref.shape[0]
    hn = _rms_rows(x_ref[...], g_ref[...]).astype(BF16)
    w = ATTN_WIDTH
    proj = lambda c: (lambda: _dot(hn, w_ref[:, c * w:(c + 1) * w]))
    blocks_per_tile = tm // MOBA_BLOCK
    row_block = lax.broadcasted_iota(jnp.int32, (tm, 1), 0) // MOBA_BLOCK
    block = (pl.program_id(0) * blocks_per_tile + row_block) % n_blocks
    block_lane = lax.broadcasted_iota(jnp.int32, (1, SLOT), 1) == FEAT0 + block

    def store_k(z):
        for h, ks in enumerate(_store_k_slots(z, kn_ref, k_o, block_lane)):
            for r in range(blocks_per_tile):
                km_o[r, :, _slot(h)] = jnp.mean(ks[r * MOBA_BLOCK:(r + 1) * MOBA_BLOCK], axis=0, keepdims=True)

    def store(ref):
        def consume(z):
            ref[...] = z
        return consume

    _pipelined([(proj(2), lambda z: _store_q_slots(z, qn_ref, q_o)), (proj(3), store_k),
                (lambda: _dot_nt(wvt_ref[...], hn), lambda zt: _store_vt_slots(zt, v_o)),
                (proj(0), store(gate_o)), (proj(1), store(xr_o))])


def _vt_spec(tm):
    return pl.BlockSpec((tm // ATTN_TILE, N_HEADS * VT_ROWS, ATTN_TILE), lambda i: (i, 0, 0))


def _vt_shape(n):
    return jax.ShapeDtypeStruct((n // ATTN_TILE, N_HEADS * VT_ROWS, ATTN_TILE), BF16)


def _slot_gain(g):
    return jnp.pad(g.astype(F32), (0, SLOT - HEAD_DIM)).reshape(1, SLOT)


def _mixin_even(x, g, w_in, q_norm, k_norm, *, n_blocks, tm=512):
    n, d = x.shape
    w = ATTN_WIDTH
    assert n % tm == 0 and tm % MOBA_BLOCK == 0 and tm % ATTN_TILE == 0 and w_in.shape[1] == 5 * w
    assert FEAT0 + n_blocks <= SLOT
    row = lambda i: (i, 0)
    blk = tm // MOBA_BLOCK
    slot_shape = jax.ShapeDtypeStruct((n, SLOT_WIDTH), BF16)
    return pl.pallas_call(
        functools.partial(_mixin_even_kernel, n_blocks=n_blocks),
        grid=(n // tm,),
        in_specs=[pl.BlockSpec((tm, d), row), _const_spec((1, d)), _const_spec((d, 4 * w)),
                  _const_spec((w, d)), _const_spec((1, SLOT)), _const_spec((1, SLOT))],
        out_specs=[pl.BlockSpec((tm, w), row)] * 2 + [pl.BlockSpec((tm, SLOT_WIDTH), row)] * 2
        + [_vt_spec(tm), pl.BlockSpec((blk, 1, SLOT_WIDTH), lambda i: (i, 0, 0))],
        out_shape=[jax.ShapeDtypeStruct((n, w), F32), jax.ShapeDtypeStruct((n, w), F32),
                   slot_shape, slot_shape, _vt_shape(n),
                   jax.ShapeDtypeStruct((n // MOBA_BLOCK, 1, SLOT_WIDTH), F32)],
        compiler_params=_cparams(("parallel",)),
        name="mixin_even",
    )(x, g.reshape(1, d), w_in[:, :4 * w].astype(BF16), w_in[:, 4 * w:].T.astype(BF16),
      _slot_gain(q_norm), _slot_gain(k_norm))


def _conv_ln_silu(c, first, w_ref, b_ref, g_ref, beta_ref, o_ref, cbuf, sh_ref, *, halo):
    ts = c.shape[0]
    width = w_ref.shape[0]

    @pl.when(first)
    def _():
        cbuf[0:halo, :] = jnp.zeros((halo, cbuf.shape[1]), F32)

    cbuf[halo:halo + ts, :] = c
    base = halo - (width - 1)
    span = sh_ref.shape[1]
    for r in range(1, SUBLANES):
        sh_ref[r - 1] = cbuf[r:r + span, :]

    def chunk(c0):
        y = jnp.broadcast_to(b_ref[...], (CONV_ROWS, cbuf.shape[1]))
        for k in range(width):
            r = (base + k) % SUBLANES
            u0 = c0 + base + k - r
            win = cbuf[u0:u0 + CONV_ROWS, :] if r == 0 else sh_ref[r - 1, u0:u0 + CONV_ROWS, :]
            y = y + w_ref[k:k + 1, :] * win
        mu = jnp.mean(y, axis=-1, keepdims=True)
        yc = y - mu
        var = jnp.mean(yc * yc, axis=-1, keepdims=True)
        z = yc * lax.rsqrt(var + EPS) * g_ref[...] + beta_ref[...]
        o_ref[c0:c0 + CONV_ROWS, :] = (z * jax.nn.sigmoid(z)).astype(BF16)

    def close():
        cbuf[0:halo, :] = cbuf[ts:ts + halo, :]

    return [functools.partial(chunk, c0) for c0 in range(0, ts, CONV_ROWS)], close


def _mixin_odd_kernel(x_ref, g_ref, w_ref, wvt_ref, wki_ref, wwit_ref, qn_ref, kn_ref,
                      dw_ref, db_ref, lg_ref, lb_ref,
                      yc_o, q_o, k_o, v_o, qi_o, ki_o, wi_o, cbuf, sh_ref, *, tiles_per_seq, halo):
    hn = _rms_rows(x_ref[...], g_ref[...]).astype(BF16)
    w = ATTN_WIDTH
    proj = lambda c, n=1: _dot(hn, w_ref[:, c * w:(c + n) * w])
    first = pl.program_id(0) % tiles_per_seq == 0

    z_glu = proj(0, 2)
    z_q = proj(2)
    chunks, close = _conv_ln_silu(z_glu[:, 0:w] * jax.nn.sigmoid(z_glu[:, w:2 * w]), first,
                                  dw_ref, db_ref, lg_ref, lb_ref, yc_o, cbuf, sh_ref, halo=halo)
    per_stage = -(-len(chunks) // 4)
    run_chunks = lambda j: [f() for f in chunks[j * per_stage:(j + 1) * per_stage]]
    z_k = proj(3)
    _store_q_slots(z_q, qn_ref, q_o)
    run_chunks(0)
    z_vt = _dot_nt(wvt_ref[...], hn)
    _store_k_slots(z_k, kn_ref, k_o, None)
    run_chunks(1)
    z_qi = proj(4)
    _store_vt_slots(z_vt, v_o)
    run_chunks(2)
    z_ki = _dot(hn, wki_ref[...])
    z_wit = _dot_nt(wwit_ref[...], hn)
    qi_o[...] = z_qi.astype(BF16)
    run_chunks(3)
    close()
    ki_o[...] = z_ki.astype(BF16)
    wi_o[...] = z_wit * (IDX_DIM ** -0.5 * IDX_HEADS ** -0.5)


def _mixin_odd(x, g, w_in, q_norm, k_norm, dw_w, dw_b, ln_g, ln_b, *, bsz, tm=512, halo=32):
    n, d = x.shape
    w = ATTN_WIDTH
    width = dw_w.shape[0]
    assert n % tm == 0 and tm % ATTN_TILE == 0 and w_in.shape[1] == 6 * w + IDX_DIM + IDX_HEADS
    assert (n // bsz) % tm == 0 and width - 1 <= halo <= tm and halo % SUBLANES == 0 and tm % CONV_ROWS == 0
    w_main = jnp.concatenate([w_in[:, :4 * w], w_in[:, 5 * w:6 * w]], axis=1).astype(BF16)
    w_vt = w_in[:, 4 * w:5 * w].T.astype(BF16)
    w_ki = w_in[:, 6 * w:6 * w + IDX_DIM]
    w_ki2 = jnp.concatenate([w_ki, w_ki], axis=1).astype(BF16)
    w_wit = w_in[:, 6 * w + IDX_DIM:].T.astype(BF16)
    row = lambda i: (i, 0)
    vec = lambda v: v.reshape(1, w).astype(F32)
    slot_shape = jax.ShapeDtypeStruct((n, SLOT_WIDTH), BF16)
    return pl.pallas_call(
        functools.partial(_mixin_odd_kernel, tiles_per_seq=(n // bsz) // tm, halo=halo),
        grid=(n // tm,),
        in_specs=[pl.BlockSpec((tm, d), row), _const_spec((1, d)), _const_spec((d, 5 * w)),
                  _const_spec((w, d)), _const_spec((d, 2 * IDX_DIM)), _const_spec((IDX_HEADS, d)),
                  _const_spec((1, SLOT)), _const_spec((1, SLOT)),
                  _const_spec((width, w)), _const_spec((1, w)), _const_spec((1, w)), _const_spec((1, w))],
        out_specs=[pl.BlockSpec((tm, w), row)] + [pl.BlockSpec((tm, SLOT_WIDTH), row)] * 2
        + [_vt_spec(tm), pl.BlockSpec((tm, w), row), pl.BlockSpec((tm, 128), row),
           pl.BlockSpec((IDX_HEADS, tm), lambda i: (0, i))],
        out_shape=[jax.ShapeDtypeStruct((n, w), BF16), slot_shape, slot_shape, _vt_shape(n),
                   jax.ShapeDtypeStruct((n, w), BF16),
                   jax.ShapeDtypeStruct((n, 128), BF16), jax.ShapeDtypeStruct((IDX_HEADS, n), F32)],
        scratch_shapes=[pltpu.VMEM((tm + halo, w), F32),
                        pltpu.VMEM((SUBLANES - 1, tm + halo - SUBLANES, w), F32)],
        compiler_params=_cparams(("arbitrary",)),
        name="mixin_odd",
    )(x, g.reshape(1, d), w_main, w_vt, w_ki2, w_wit, _slot_gain(q_norm), _slot_gain(k_norm),
      dw_w, vec(dw_b), vec(ln_g), vec(ln_b))


def _lru_kernel(gate_ref, xr_ref, cw_ref, cb_ref, wa_ref, ba_ref, wx_ref, bx_ref, sp_ref,
                o_ref, xbuf, a_s, u_s, h_s, hc, *, ts):
    j = pl.program_id(1)

    @pl.when(j == 0)
    def _():
        xbuf[0:8, :] = jnp.zeros((8, xbuf.shape[1]), F32)
        hc[...] = jnp.zeros(hc.shape, F32)

    xbuf[8:8 + ts, :] = xr_ref[...]
    xc = cb_ref[...] + cw_ref[0:1, :] * xbuf[5:5 + ts, :]
    for k in range(1, 4):
        xc = xc + cw_ref[k:k + 1, :] * xbuf[5 + k:5 + k + ts, :]
    xbuf[0:8, :] = xbuf[ts:ts + 8, :]

    xcb = xc.astype(BF16)
    r = jax.nn.sigmoid(_dot(xcb, wa_ref[...]) + ba_ref[...])
    ig = jax.nn.sigmoid(_dot(xcb, wx_ref[...]) + bx_ref[...])
    log_a = -LRU_C * r * sp_ref[...]
    a = jnp.exp(log_a)
    a_s[...] = a
    u_s[...] = jnp.sqrt(-jnp.tanh(log_a) * (a * a + 1.0)) * (ig * xc)

    row = lax.broadcasted_iota(jnp.int32, (8, a_s.shape[1]), 0)

    def body(g, carry):
        r0 = pl.multiple_of(g * 8, 8)
        a = a_s[pl.ds(r0, 8), :]
        u = u_s[pl.ds(r0, 8), :]
        for s in (1, 2, 4):
            ok = row >= s
            a_sh = jnp.where(ok, pltpu.roll(a, s, 0), 1.0)
            u_sh = jnp.where(ok, pltpu.roll(u, s, 0), 0.0)
            u = a * u_sh + u
            a = a * a_sh
        h = a * carry + u
        h_s[pl.ds(r0, 8), :] = h
        return h[7:8, :]

    hc[...] = lax.fori_loop(0, ts // 8, body, hc[...], unroll=4)
    o_ref[...] = (h_s[...] * jax.nn.gelu(gate_ref[...])).astype(BF16)


def _block_diag(wb):
    nb, bs, _ = wb.shape
    eye = jnp.eye(nb, dtype=wb.dtype)
    return (eye[:, None, :, None] * wb[:, :, None, :]).reshape(nb * bs, nb * bs)


def _lru(gate, xr, conv_w, conv_b, ra_w, ra_b, ix_w, ix_b, lam, *, bsz, ts=256):
    n, w = xr.shape
    seq = n // bsz
    assert seq % ts == 0
    nt = seq // ts
    row = lambda b, j: (b * nt + j, 0)
    vec = lambda v: v.reshape(1, w).astype(F32)
    return pl.pallas_call(
        functools.partial(_lru_kernel, ts=ts),
        grid=(bsz, nt),
        in_specs=[pl.BlockSpec((ts, w), row), pl.BlockSpec((ts, w), row),
                  _const_spec((conv_w.shape[0], w)), _const_spec((1, w)),
                  _const_spec((w, w)), _const_spec((1, w)), _const_spec((w, w)), _const_spec((1, w)),
                  _const_spec((1, w))],
        out_specs=pl.BlockSpec((ts, w), row),
        out_shape=jax.ShapeDtypeStruct((n, w), BF16),
        scratch_shapes=[pltpu.VMEM((ts + 8, w), F32), pltpu.VMEM((ts, w), F32),
                        pltpu.VMEM((ts, w), F32), pltpu.VMEM((ts, w), F32), pltpu.VMEM((1, w), F32)],
        compiler_params=_cparams(("arbitrary", "arbitrary")),
        name="rg_lru",
    )(gate, xr, conv_w, vec(conv_b), _block_diag(ra_w).astype(BF16), vec(ra_b),
      _block_diag(ix_w).astype(BF16), vec(ix_b), vec(jax.nn.softplus(-lam)))


def _bucket_tiles(t):
    assert t > REL_MAX_DIST
    n = np.arange(2 * t)
    nf = np.maximum(n, 1).astype(np.float32)
    large = REL_MAX_EXACT + (np.log(nf / np.float32(REL_MAX_EXACT))
                             / np.float32(math.log(REL_MAX_DIST / REL_MAX_EXACT))
                             * np.float32(REL_BUCKETS - REL_MAX_EXACT)).astype(np.int32)
    bucket = np.where(n < REL_MAX_EXACT, n, np.minimum(large, REL_BUCKETS - 1)).astype(np.int32)
    qry = np.arange(t)[None, :]
    key = np.arange(t)[:, None]
    return np.stack([bucket[np.maximum(qry - key, 0)], bucket[t + qry - key]])


def _build_bias(idx_ref, rb_ref, bias_ref):
    t = idx_ref.shape[1]
    causal = (lax.broadcasted_iota(jnp.int32, (t, t), 0) <= lax.broadcasted_iota(jnp.int32, (t, t), 1))
    for h in range(N_HEADS):
        far = rb_ref[REL_BUCKETS - 1, h]
        for which in (0, 1):
            idx = idx_ref[which]
            acc = jnp.zeros((t, t), F32)
            for b in range(REL_BUCKETS - 1):
                acc = jnp.where(idx == b, (rb_ref[b, h] - far) * LOG2E, acc)
            if which == 0:
                bias_ref[h, t:2 * t, :] = jnp.where(causal, acc, NEG)
            else:
                bias_ref[h, 0:t, :] = acc


def _attn_init(m_ref, acc_ref):
    m_ref[...] = jnp.full(m_ref.shape, M_INIT, F32)
    acc_ref[...] = jnp.zeros(acc_ref.shape, F32)


def _softmax_step(s, h, vt_h, m_ref, acc_ref):
    m_old = m_ref[h]
    m_new = jnp.maximum(m_old, jnp.max(s, axis=0, keepdims=True))
    p = jnp.exp2(s - m_new)
    m_ref[h] = m_new
    acc_ref[h] = jnp.exp2(m_old - m_new) * acc_ref[h] + _dot(vt_h, p.astype(BF16))


def _heads_pipelined(logits, vt_slot, m_ref, acc_ref):
    ahead = [logits(h) for h in range(HEADS_AHEAD)]
    for h in range(N_HEADS):
        if h + HEADS_AHEAD < N_HEADS:
            ahead.append(logits(h + HEADS_AHEAD))
        _softmax_step(ahead[h], h, vt_slot(h), m_ref, acc_ref)


FAR_PAIR = ("far", "far")
FAR_ONE = ("far",)
NEAR_DIAG = ("near", "diag")
DIAG_ONE = ("diag",)


def _attend_tiles(kt, kinds, q_slot, k_ref, vt_ref, bias_ref, mask_add, m_ref, acc_ref):
    t = ATTN_TILE
    n = len(kinds)
    rows = pl.ds(pl.multiple_of(kt * t, t), n * t)

    def logits(h):
        s = _dot_nt(k_ref[rows, _slot(h)], q_slot(h))
        if kinds == NEAR_DIAG:
            s = s + bias_ref[h]
        elif kinds == DIAG_ONE:
            s = s + bias_ref[h, t:2 * t, :]
        if mask_add is not None:
            s = s + mask_add
        return s

    def vt_slot(h):
        parts = [vt_ref[kt + a, _vt_rows(h), :] for a in range(n)]
        return parts[0] if n == 1 else jnp.concatenate(parts, axis=1)

    _heads_pipelined(logits, vt_slot, m_ref, acc_ref)


def _for_causal_tiles(i, tiles):
    n_far = jnp.maximum(i - 1, 0)

    def pair(j, carry):
        tiles(2 * j, FAR_PAIR)
        return carry

    lax.fori_loop(0, n_far // 2, pair, 0)

    @pl.when(n_far % 2 == 1)
    def _():
        tiles(n_far - 1, FAR_ONE)

    @pl.when(i >= 1)
    def _():
        tiles(i - 1, NEAR_DIAG)

    @pl.when(i == 0)
    def _():
        tiles(i, DIAG_ONE)


def _attn_finish(o_ref, acc_ref):
    parts = []
    for h in range(N_HEADS):
        acc = acc_ref[h]
        parts.append(acc[0:HEAD_DIM] * (1.0 / acc[FEAT0:FEAT0 + 1]))
    o_ref[...] = jnp.concatenate(parts, axis=0).T.astype(BF16)


def _smem_spec():
    return pl.BlockSpec(memory_space=pltpu.SMEM)


def _moba_kernel(rb_ref, q_ref, k_ref, vt_ref, km_ref, idx_ref, o_ref,
                 acc_ref, m_ref, qs_ref, bias_ref, *, n_blocks):
    i = pl.program_id(1)
    t = ATTN_TILE
    nbp = -(-n_blocks // 8) * 8

    @pl.when((pl.program_id(0) == 0) & (i == 0))
    def _():
        _build_bias(idx_ref, rb_ref, bias_ref)

    _attn_init(m_ref, acc_ref)

    blk = lax.broadcasted_iota(jnp.int32, (nbp, t), 0)
    past = blk < i
    for h in range(N_HEADS):
        qh = q_ref[:, _slot(h)]
        km_hi, km_lo = _split_bf16(km_ref[:, _slot(h)])
        gate_t = _dot_nt(km_hi, qh) + _dot_nt(km_lo, qh)
        g = jnp.where(past, gate_t[FEAT0:FEAT0 + nbp, :], NEG)
        rank = jnp.zeros((nbp, t), F32)
        for j in range(n_blocks):
            gj = g[j:j + 1, :]
            beats = (gj > g) | ((gj == g) & (blk > j))
            rank = rank + jnp.where(beats, 1.0, 0.0)
        flag = jnp.where(past & (rank >= MOBA_TOPK), NEG, 0.0)
        flag_t = jnp.concatenate([jnp.zeros((FEAT0, t), F32), flag,
                                  jnp.zeros((SLOT - FEAT0 - nbp, t), F32)], axis=0)
        qs_ref[:, _slot(h)] = (qh.astype(F32) + flag_t.T).astype(BF16)

    def tiles(kt, kinds):
        _attend_tiles(kt, kinds, lambda h: qs_ref[:, _slot(h)], k_ref, vt_ref, bias_ref, None,
                      m_ref, acc_ref)

    _for_causal_tiles(i, tiles)
    _attn_finish(o_ref, acc_ref)


def _moba(q, k, vt, kmean, rel_bias, *, bsz):
    n = q.shape[0]
    seq = n // bsz
    t = ATTN_TILE
    assert seq % t == 0 and t == MOBA_BLOCK
    nq = seq // t
    assert FEAT0 + nq <= SLOT
    km = jnp.pad(kmean.reshape(bsz, nq, SLOT_WIDTH), ((0, 0), (FEAT0, SLOT - FEAT0 - nq), (0, 0)))
    r3 = lambda a: a.reshape(bsz, seq, SLOT_WIDTH)
    seq_spec = pl.BlockSpec((None, seq, SLOT_WIDTH), lambda b, i: (b, 0, 0))
    out = pl.pallas_call(
        functools.partial(_moba_kernel, n_blocks=nq),
        grid=(bsz, nq),
        in_specs=[_smem_spec(),
                  pl.BlockSpec((None, t, SLOT_WIDTH), lambda b, i: (b, i, 0)), seq_spec,
                  pl.BlockSpec((nq, N_HEADS * VT_ROWS, t), lambda b, i: (b, 0, 0)),
                  pl.BlockSpec((None, SLOT, SLOT_WIDTH), lambda b, i: (b, 0, 0)),
                  _const_spec((2, t, t))],
        out_specs=pl.BlockSpec((None, t, ATTN_WIDTH), lambda b, i: (b, i, 0)),
        out_shape=jax.ShapeDtypeStruct((bsz, seq, ATTN_WIDTH), BF16),
        scratch_shapes=[pltpu.VMEM((N_HEADS, VT_ROWS, t), F32), pltpu.VMEM((N_HEADS, 1, t), F32),
                        pltpu.VMEM((t, SLOT_WIDTH), BF16),
                        pltpu.VMEM((N_HEADS, 2 * t, t), F32)],
        compiler_params=_cparams(("arbitrary", "arbitrary")),
        name="moba_attention",
    )(rel_bias, r3(q), r3(k), vt, km, jnp.asarray(_bucket_tiles(t)))
    return out.reshape(n, ATTN_WIDTH)


def _dsa_kernel(rb_ref, q_ref, k_ref, vt_ref, qi_ref, ki_ref, wit_ref, idx_ref, tri_ref, o_ref,
                acc_ref, m_ref, isc_ref, bc_ref, bias_ref, *, n_sel):
    i = pl.program_id(1)
    t = ATTN_TILE
    key = lax.broadcasted_iota(jnp.int32, (t, t), 0)
    qry = lax.broadcasted_iota(jnp.int32, (t, t), 1)
    lane128 = lax.broadcasted_iota(jnp.int32, (1, 128), 1)

    @pl.when((pl.program_id(0) == 0) & (i == 0))
    def _():
        _build_bias(idx_ref, rb_ref, bias_ref)

    _attn_init(m_ref, acc_ref)

    def index_tiles(kt, n, diag):
        rows = pl.ds(pl.multiple_of(kt * t, t), n * t)
        ki2 = ki_ref[rows, :]
        acc = jnp.zeros((n * t, t), F32)
        for pr in range(IDX_HEADS // 2):
            q2 = qi_ref[:, pr * 128:(pr + 1) * 128]
            for half in range(2):
                h = 2 * pr + half
                hm = (lane128 >= half * IDX_DIM) & (lane128 < (half + 1) * IDX_DIM)
                s = _dot_nt(ki2, jnp.where(hm, q2, jnp.zeros((), BF16)))
                acc = acc + jnp.maximum(s, 0.0) * wit_ref[h:h + 1, :]
        if diag:
            acc = jnp.where(key <= qry, acc, -jnp.inf)
        for a in range(n):
            isc_ref[kt + a] = acc[a * t:(a + 1) * t]

    def index_pair(j, carry):
        index_tiles(2 * j, 2, False)
        return carry

    lax.fori_loop(0, i // 2, index_pair, 0)

    @pl.when(i % 2 == 1)
    def _():
        index_tiles(i - 1, 1, False)

    index_tiles(i, 1, True)

    def fold8(x, op):
        return op(x.reshape(t // 8, 8, t), axis=0)

    def minmax_body(kt, carry):
        mn, mx = carry
        x = isc_ref[kt]
        mx = jnp.maximum(mx, fold8(x, jnp.max))
        mn = jnp.minimum(mn, fold8(jnp.where(x == -jnp.inf, jnp.inf, x), jnp.min))
        return mn, mx

    mn, mx = lax.fori_loop(0, i + 1, minmax_body,
                           (jnp.full((8, t), jnp.inf, F32), jnp.full((8, t), -jnp.inf, F32)))
    lo0 = jnp.min(mn, axis=0, keepdims=True)
    mx = jnp.max(mx, axis=0, keepdims=True)
    hi0 = mx + jnp.abs(mx) * 1e-3 + 1e-30
    n_valid = (i * t + 1 + lax.broadcasted_iota(jnp.int32, (1, t), 1)).astype(F32)
    want = jnp.minimum(n_valid, float(n_sel))

    def count_ge(thr):
        def one(kt, c):
            return c + fold8(jnp.where(isc_ref[kt] >= thr, 1.0, 0.0), jnp.sum)

        def pair(j, c):
            return one(2 * j + 1, one(2 * j, c))

        c = lax.fori_loop(0, (i + 1) // 2, pair, jnp.zeros((8, t), F32))
        c = lax.cond((i + 1) % 2 == 1, lambda c: one(i, c), lambda c: c, c)
        return jnp.sum(c, axis=0, keepdims=True)

    c_nonneg = count_ge(0.0)
    c_pos = count_ge(F32_TINY)
    above = want <= c_pos
    zero_tie = (want > c_pos) & (want <= c_nonneg)
    lo1 = jnp.where(above, F32_TINY, jnp.where(zero_tie, 0.0, lo0))
    hi1 = jnp.where(above, hi0, jnp.where(zero_tie, F32_TINY, 0.0))
    c_lo1 = jnp.where(above, c_pos, jnp.where(zero_tie, c_nonneg, n_valid))
    c_hi1 = jnp.where(above, 0.0, jnp.where(zero_tie, c_pos, c_nonneg))

    def bisect_step(_, carry):
        lo, hi, c_lo, c_hi = carry
        mid = 0.5 * (lo + hi)
        c_mid = count_ge(mid)
        up = c_mid >= want
        return (jnp.where(up, mid, lo), jnp.where(up, hi, mid),
                jnp.where(up, c_mid, c_lo), jnp.where(up, c_hi, c_mid))

    def bisect_cond(carry):
        it, _, _, c_lo, _ = carry
        return (it < BISECT_ITERS) & (jnp.max(jnp.where(zero_tie, 0.0, c_lo - want)) > 0.0)

    def bisect_body(carry):
        return (carry[0] + BISECT_CHECK_EVERY,) + lax.fori_loop(0, BISECT_CHECK_EVERY, bisect_step, carry[1:])

    _, lo, hi, c_lo, c_hi = lax.while_loop(bisect_cond, bisect_body, (0, lo1, hi1, c_lo1, c_hi1))
    need = want - c_hi
    banded = jnp.max(c_lo - want) > 0.0
    bc_ref[...] = jnp.zeros(bc_ref.shape, F32)

    def plain_mask(kt):
        isc_ref[kt] = jnp.where(isc_ref[kt] >= lo, 0.0, NEG)

    def band_mask(kt):
        x = isc_ref[kt]
        band = jnp.where((x >= lo) & (x < hi), 1.0, 0.0)
        before = bc_ref[...] + _dot(tri_ref[...], band.astype(BF16))
        bc_ref[...] = bc_ref[...] + jnp.sum(band, axis=0, keepdims=True)
        isc_ref[kt] = jnp.where((x >= hi) | ((band > 0.0) & (before < need)), 0.0, NEG)

    def for_tiles(one):
        def pair(j, carry):
            one(2 * j)
            one(2 * j + 1)
            return carry

        lax.fori_loop(0, (i + 1) // 2, pair, 0)

        @pl.when((i + 1) % 2 == 1)
        def _():
            one(i)

    lax.cond(banded, lambda: for_tiles(band_mask), lambda: for_tiles(plain_mask))

    def tiles(kt, kinds):
        masks = [isc_ref[kt + a] for a in range(len(kinds))]
        mask_add = masks[0] if len(masks) == 1 else jnp.concatenate(masks, axis=0)
        _attend_tiles(kt, kinds, lambda h: q_ref[:, _slot(h)], k_ref, vt_ref, bias_ref, mask_add,
                      m_ref, acc_ref)

    _for_causal_tiles(i, tiles)
    _attn_finish(o_ref, acc_ref)


def _dsa(q, k, vt, qi, ki2, wit, rel_bias, *, bsz):
    n = q.shape[0]
    seq = n // bsz
    t = ATTN_TILE
    assert seq % t == 0
    nq = seq // t
    n_sel = min(DSA_TOPK_MAX, seq // 4)
    tri = (jnp.arange(t)[None, :] < jnp.arange(t)[:, None]).astype(BF16)
    r3 = lambda a: a.reshape(bsz, seq, a.shape[-1])
    tile_spec = lambda width: pl.BlockSpec((None, t, width), lambda b, i: (b, i, 0))
    seq_spec = lambda width: pl.BlockSpec((None, seq, width), lambda b, i: (b, 0, 0))
    out = pl.pallas_call(
        functools.partial(_dsa_kernel, n_sel=n_sel),
        grid=(bsz, nq),
        in_specs=[_smem_spec(), tile_spec(SLOT_WIDTH), seq_spec(SLOT_WIDTH),
                  pl.BlockSpec((nq, N_HEADS * VT_ROWS, t), lambda b, i: (b, 0, 0)),
                  tile_spec(ATTN_WIDTH), seq_spec(128),
                  pl.BlockSpec((IDX_HEADS, t), lambda b, i: (0, b * nq + i)),
                  _const_spec((2, t, t)), _const_spec((t, t))],
        out_specs=tile_spec(ATTN_WIDTH),
        out_shape=jax.ShapeDtypeStruct((bsz, seq, ATTN_WIDTH), BF16),
        scratch_shapes=[pltpu.VMEM((N_HEADS, VT_ROWS, t), F32), pltpu.VMEM((N_HEADS, 1, t), F32),
                        pltpu.VMEM((nq, t, t), F32), pltpu.VMEM((1, t), F32),
                        pltpu.VMEM((N_HEADS, 2 * t, t), F32)],
        compiler_params=_cparams(("arbitrary", "arbitrary")),
        name="dsa_attention",
    )(rel_bias, r3(q), r3(k), vt, r3(qi), r3(ki2), wit, jnp.asarray(_bucket_tiles(t)), tri)
    return out.reshape(n, ATTN_WIDTH)


def kernel(x, rel_bias, ffn1_norm, ffn1_w_gate, ffn1_w_up, ffn1_w_down, mix_norm, ffn2_norm, ffn2_w_gate, ffn2_w_up, ffn2_w_down, ev_w_in, ev_conv_w, ev_conv_b, ev_ra_w, ev_ra_b, ev_ix_w, ev_ix_b, ev_lambda, ev_q_norm, ev_k_norm, ev_w_out, od_w_in, od_dw_w, od_dw_b, od_ln_g, od_ln_b, od_q_norm, od_k_norm, od_w_out):
    bsz, seq, d = x.shape
    depth = ffn1_norm.shape[0]
    h = x.reshape(bsz * seq, d)
    for i in range(depth):
        h = _ffn(h, ffn1_norm, ffn1_w_gate, ffn1_w_up, ffn1_w_down, i)
        j = i // 2
        if i % 2 == 0:
            gate, xr, q, k, vt, kmean = _mixin_even(h, mix_norm[i], ev_w_in[j], ev_q_norm[j], ev_k_norm[j],
                                                    n_blocks=seq // MOBA_BLOCK)
            ya = _lru(gate, xr, ev_conv_w[j], ev_conv_b[j], ev_ra_w[j], ev_ra_b[j],
                      ev_ix_w[j], ev_ix_b[j], ev_lambda[j], bsz=bsz)
            yb = _moba(q, k, vt, kmean, rel_bias, bsz=bsz)
            mix = (ya, yb, ev_w_out[j])
        else:
            yc, q, k, vt, qi, ki2, wit = _mixin_odd(h, mix_norm[i], od_w_in[j], od_q_norm[j], od_k_norm[j],
                                                    od_dw_w[j], od_dw_b[j], od_ln_g[j], od_ln_b[j], bsz=bsz)
            yd = _dsa(q, k, vt, qi, ki2, wit, rel_bias, bsz=bsz)
            mix = (yc, yd, od_w_out[j])
        h = _ffn(h, ffn2_norm, ffn2_w_gate, ffn2_w_up, ffn2_w_down, i, mix)
    return h.reshape(bsz, seq, d)
```

```python
import functools
import math

import numpy as np
import jax
import jax.numpy as jnp
from jax import lax
from jax.experimental import pallas as pl
from jax.experimental.pallas import tpu as pltpu

F32 = jnp.float32
BF16 = jnp.bfloat16

N_HEADS = 8
HEAD_DIM = 64
ATTN_WIDTH = N_HEADS * HEAD_DIM
LRU_C = 8.0
MOBA_BLOCK = 256
MOBA_TOPK = 3
IDX_HEADS = 8
IDX_DIM = 64
DSA_TOPK_MAX = 256
REL_BUCKETS = 32
REL_MAX_EXACT = REL_BUCKETS // 2
REL_MAX_DIST = 128
EPS = 1e-6
NEG = -1e30
M_INIT = -1e29
ATTN_TILE = 256
SUBLANES = 8
CONV_ROWS = 64
SLOT = 128
SLOT_WIDTH = N_HEADS * SLOT
FEAT0 = HEAD_DIM
VT_ROWS = 80
HEADS_AHEAD = 2
BISECT_ITERS = 32
BISECT_CHECK_EVERY = 4
F32_TINY = float(np.finfo(np.float32).tiny)
LOG2E = math.log2(math.e)
Q_SCALE = HEAD_DIM ** -0.5 * LOG2E
VMEM_LIMIT = 56 * 1024 * 1024
FFN_VMEM_LIMIT = 60 * 1024 * 1024


def _cparams(sem):
    return pltpu.CompilerParams(dimension_semantics=sem, vmem_limit_bytes=VMEM_LIMIT)


def _dot(a, b):
    return jnp.dot(a, b, preferred_element_type=F32)


def _dot_nt(a, b):
    return lax.dot_general(a, b, (((1,), (1,)), ((), ())), preferred_element_type=F32)


def _split_bf16(x):
    hi = x.astype(BF16)
    lo = (x - hi.astype(F32)).astype(BF16)
    return hi, lo


def _rms_rows(x, g):
    return x * lax.rsqrt(jnp.mean(x * x, axis=-1, keepdims=True) + EPS) * g


def _const_spec(shape):
    nd = len(shape)
    return pl.BlockSpec(shape, lambda *_: (0,) * nd, pipeline_mode=pl.Buffered(1))


def _slot(h):
    return slice(h * SLOT, (h + 1) * SLOT)


def _vt_rows(h):
    return slice(h * VT_ROWS, (h + 1) * VT_ROWS)


def _to_slots(x):
    low = lax.broadcasted_iota(jnp.int32, (1, SLOT), 1) < HEAD_DIM
    slots = []
    for p in range(N_HEADS // 2):
        chunk = x[:, p * SLOT:(p + 1) * SLOT]
        slots.append(jnp.where(low, chunk, 0.0))
        slots.append(jnp.where(low, pltpu.roll(chunk, HEAD_DIM, 1), 0.0))
    return slots


def _ffn_kernel(*refs, n_chunks, has_mix):
    if has_mix:
        x_ref, ya_ref, yb_ref, wo_ref, g_ref, wg_ref, wu_ref, wd_ref, o_ref, wg_s, wu_s, wd_s = refs
    else:
        x_ref, g_ref, wg_ref, wu_ref, wd_ref, o_ref, wg_s, wu_s, wd_s = refs
    step = pl.program_id(0)

    @pl.when(step < n_chunks)
    def _():
        wg_s[step] = wg_ref[...].astype(BF16)
        wu_s[step] = wu_ref[...].astype(BF16)
        wd_s[step] = wd_ref[...].astype(BF16)

    @pl.when(step >= n_chunks)
    def _():
        x = x_ref[...]
        if has_mix:
            w = ya_ref.shape[1]
            x = x + _dot(ya_ref[...], wo_ref[0:w, :]) + _dot(yb_ref[...], wo_ref[w:2 * w, :])
        hn = _rms_rows(x, g_ref[...]).astype(BF16)
        acc = jnp.zeros(x.shape, F32)
        for c in range(n_chunks):
            gt = _dot(hn, wg_s[c])
            ut = _dot(hn, wu_s[c])
            a = (gt * jax.nn.sigmoid(gt) * ut).astype(BF16)
            acc = acc + _dot(a, wd_s[c])
        o_ref[...] = x + 0.5 * acc


def _ffn(x, g, wg, wu, wd, layer, mix=None, *, tm=1024, ff_chunk=256):
    n, d = x.shape
    d_ff = wg.shape[2]
    assert n % tm == 0 and d_ff % ff_chunk == 0
    nc = d_ff // ff_chunk
    tile = lambda s: (jnp.maximum(s - nc, 0), 0)
    chunk_col = lambda s: (layer, 0, jnp.minimum(s, nc - 1))
    chunk_row = lambda s: (layer, jnp.minimum(s, nc - 1), 0)
    in_specs = [pl.BlockSpec((tm, d), tile)]
    args = [x]
    scratch = [pltpu.VMEM((nc, d, ff_chunk), BF16), pltpu.VMEM((nc, d, ff_chunk), BF16),
               pltpu.VMEM((nc, ff_chunk, d), BF16)]
    if mix is not None:
        ya, yb, w_out = mix
        w = ya.shape[1]
        assert w_out.shape == (2 * w, d)
        in_specs += [pl.BlockSpec((tm, w), tile), pl.BlockSpec((tm, w), tile), _const_spec((2 * w, d))]
        args += [ya, yb, w_out.astype(BF16)]
    in_specs += [_const_spec((1, d)), pl.BlockSpec((None, d, ff_chunk), chunk_col),
                 pl.BlockSpec((None, d, ff_chunk), chunk_col), pl.BlockSpec((None, ff_chunk, d), chunk_row)]
    args += [g[layer].reshape(1, d), wg, wu, wd]
    return pl.pallas_call(
        functools.partial(_ffn_kernel, n_chunks=nc, has_mix=mix is not None),
        grid=(nc + n // tm,),
        in_specs=in_specs,
        out_specs=pl.BlockSpec((tm, d), tile),
        out_shape=jax.ShapeDtypeStruct((n, d), F32),
        scratch_shapes=scratch,
        compiler_params=pltpu.CompilerParams(dimension_semantics=("arbitrary",),
                                             vmem_limit_bytes=FFN_VMEM_LIMIT),
        name="ffn_mix" if mix is not None else "ffn",
    )(*args)


def _slot_rms(xs, g):
    ms = jnp.sum(xs * xs, axis=-1, keepdims=True) * (1.0 / HEAD_DIM)
    return xs * lax.rsqrt(ms + EPS) * g


def _store_q_slots(z, gain_ref, q_o):
    for h, zs in enumerate(_to_slots(z)):
        q_o[:, _slot(h)] = (_slot_rms(zs, gain_ref[...]) * Q_SCALE).astype(BF16)


def _store_k_slots(z, gain_ref, k_o, one_lanes):
    slots = [_slot_rms(zs, gain_ref[...]) for zs in _to_slots(z)]
    for h, ks in enumerate(slots):
        k_o[:, _slot(h)] = (ks if one_lanes is None else jnp.where(one_lanes, 1.0, ks)).astype(BF16)
    return slots


def _store_vt_slots(zt, v_o):
    pad = VT_ROWS - HEAD_DIM
    tail = (lax.broadcasted_iota(jnp.int32, (pad, ATTN_TILE), 0) == 0).astype(BF16)
    for h in range(N_HEADS):
        for r in range(zt.shape[1] // ATTN_TILE):
            feat = zt[h * HEAD_DIM:(h + 1) * HEAD_DIM, r * ATTN_TILE:(r + 1) * ATTN_TILE]
            v_o[r, h * VT_ROWS:h * VT_ROWS + HEAD_DIM, :] = feat.astype(BF16)
            v_o[r, h * VT_ROWS + HEAD_DIM:(h + 1) * VT_ROWS, :] = tail


def _pipelined(stages):
    nxt = stages[0][0]()
    for j, (_, consume) in enumerate(stages):
        cur = nxt
        if j + 1 < len(stages):
            nxt = stages[j + 1][0]()
        consume(cur)


def _mixin_even_kernel(x_ref, g_ref, w_ref, wvt_ref, qn_ref, kn_ref,
                       gate_o, xr_o, q_o, k_o, v_o, km_o, *, n_blocks):
    tm = x_ref.shape[0]
    hn = _rms_rows(x_ref[...], g_ref[...]).astype(BF16)
    w = ATTN_WIDTH
    proj = lambda c: (lambda: _dot(hn, w_ref[:, c * w:(c + 1) * w]))
    blocks_per_tile = tm // MOBA_BLOCK
    row_block = lax.broadcasted_iota(jnp.int32, (tm, 1), 0) // MOBA_BLOCK
    block = (pl.program_id(0) * blocks_per_tile + row_block) % n_blocks
    block_lane = lax.broadcasted_iota(jnp.int32, (1, SLOT), 1) == FEAT0 + block

    def store_k(z):
        for h, ks in enumerate(_store_k_slots(z, kn_ref, k_o, block_lane)):
            for r in range(blocks_per_tile):
                km_o[r, :, _slot(h)] = jnp.mean(ks[r * MOBA_BLOCK:(r + 1) * MOBA_BLOCK], axis=0, keepdims=True)

    def store(ref):
        def consume(z):
            ref[...] = z
        return consume

    _pipelined([(proj(2), lambda z: _store_q_slots(z, qn_ref, q_o)), (proj(3), store_k),
                (lambda: _dot_nt(wvt_ref[...], hn), lambda zt: _store_vt_slots(zt, v_o)),
                (proj(0), store(gate_o)), (proj(1), store(xr_o))])


def _vt_spec(tm):
    return pl.BlockSpec((tm // ATTN_TILE, N_HEADS * VT_ROWS, ATTN_TILE), lambda i: (i, 0, 0))


def _vt_shape(n):
    return jax.ShapeDtypeStruct((n // ATTN_TILE, N_HEADS * VT_ROWS, ATTN_TILE), BF16)


def _slot_gain(g):
    return jnp.pad(g.astype(F32), (0, SLOT - HEAD_DIM)).reshape(1, SLOT)


def _mixin_even(x, g, w_in, q_norm, k_norm, *, n_blocks, tm=512):
    n, d = x.shape
    w = ATTN_WIDTH
    assert n % tm == 0 and tm % MOBA_BLOCK == 0 and tm % ATTN_TILE == 0 and w_in.shape[1] == 5 * w
    assert FEAT0 + n_blocks <= SLOT
    row = lambda i: (i, 0)
    blk = tm // MOBA_BLOCK
    slot_shape = jax.ShapeDtypeStruct((n, SLOT_WIDTH), BF16)
    return pl.pallas_call(
        functools.partial(_mixin_even_kernel, n_blocks=n_blocks),
        grid=(n // tm,),
        in_specs=[pl.BlockSpec((tm, d), row), _const_spec((1, d)), _const_spec((d, 4 * w)),
                  _const_spec((w, d)), _const_spec((1, SLOT)), _const_spec((1, SLOT))],
        out_specs=[pl.BlockSpec((tm, w), row)] * 2 + [pl.BlockSpec((tm, SLOT_WIDTH), row)] * 2
        + [_vt_spec(tm), pl.BlockSpec((blk, 1, SLOT_WIDTH), lambda i: (i, 0, 0))],
        out_shape=[jax.ShapeDtypeStruct((n, w), F32), jax.ShapeDtypeStruct((n, w), F32),
                   slot_shape, slot_shape, _vt_shape(n),
                   jax.ShapeDtypeStruct((n // MOBA_BLOCK, 1, SLOT_WIDTH), F32)],
        compiler_params=_cparams(("parallel",)),
        name="mixin_even",
    )(x, g.reshape(1, d), w_in[:, :4 * w].astype(BF16), w_in[:, 4 * w:].T.astype(BF16),
      _slot_gain(q_norm), _slot_gain(k_norm))


def _conv_ln_silu(c, first, w_ref, b_ref, g_ref, beta_ref, o_ref, cbuf, sh_ref, *, halo):
    ts = c.shape[0]
    width = w_ref.shape[0]

    @pl.when(first)
    def _():
        cbuf[0:halo, :] = jnp.zeros((halo, cbuf.shape[1]), F32)

    cbuf[halo:halo + ts, :] = c
    base = halo - (width - 1)
    span = sh_ref.shape[1]
    for r in range(1, SUBLANES):
        sh_ref[r - 1] = cbuf[r:r + span, :]

    def chunk(c0):
        y = jnp.broadcast_to(b_ref[...], (CONV_ROWS, cbuf.shape[1]))
        for k in range(width):
            r = (base + k) % SUBLANES
            u0 = c0 + base + k - r
            win = cbuf[u0:u0 + CONV_ROWS, :] if r == 0 else sh_ref[r - 1, u0:u0 + CONV_ROWS, :]
            y = y + w_ref[k:k + 1, :] * win
        mu = jnp.mean(y, axis=-1, keepdims=True)
        yc = y - mu
        var = jnp.mean(yc * yc, axis=-1, keepdims=True)
        z = yc * lax.rsqrt(var + EPS) * g_ref[...] + beta_ref[...]
        o_ref[c0:c0 + CONV_ROWS, :] = (z * jax.nn.sigmoid(z)).astype(BF16)

    def close():
        cbuf[0:halo, :] = cbuf[ts:ts + halo, :]

    return [functools.partial(chunk, c0) for c0 in range(0, ts, CONV_ROWS)], close


def _mixin_odd_kernel(x_ref, g_ref, w_ref, wvt_ref, wki_ref, wwit_ref, qn_ref, kn_ref,
                      dw_ref, db_ref, lg_ref, lb_ref,
                      yc_o, q_o, k_o, v_o, qi_o, ki_o, wi_o, cbuf, sh_ref, *, tiles_per_seq, halo):
    hn = _rms_rows(x_ref[...], g_ref[...]).astype(BF16)
    w = ATTN_WIDTH
    proj = lambda c, n=1: _dot(hn, w_ref[:, c * w:(c + n) * w])
    first = pl.program_id(0) % tiles_per_seq == 0

    z_glu = proj(0, 2)
    z_q = proj(2)
    chunks, close = _conv_ln_silu(z_glu[:, 0:w] * jax.nn.sigmoid(z_glu[:, w:2 * w]), first,
                                  dw_ref, db_ref, lg_ref, lb_ref, yc_o, cbuf, sh_ref, halo=halo)
    per_stage = -(-len(chunks) // 4)
    run_chunks = lambda j: [f() for f in chunks[j * per_stage:(j + 1) * per_stage]]
    z_k = proj(3)
    _store_q_slots(z_q, qn_ref, q_o)
    run_chunks(0)
    z_vt = _dot_nt(wvt_ref[...], hn)
    _store_k_slots(z_k, kn_ref, k_o, None)
    run_chunks(1)
    z_qi = proj(4)
    _store_vt_slots(z_vt, v_o)
    run_chunks(2)
    z_ki = _dot(hn, wki_ref[...])
    z_wit = _dot_nt(wwit_ref[...], hn)
    qi_o[...] = z_qi.astype(BF16)
    run_chunks(3)
    close()
    ki_o[...] = z_ki.astype(BF16)
    wi_o[...] = z_wit * (IDX_DIM ** -0.5 * IDX_HEADS ** -0.5)


def _mixin_odd(x, g, w_in, q_norm, k_norm, dw_w, dw_b, ln_g, ln_b, *, bsz, tm=512, halo=32):
    n, d = x.shape
    w = ATTN_WIDTH
    width = dw_w.shape[0]
    assert n % tm == 0 and tm % ATTN_TILE == 0 and w_in.shape[1] == 6 * w + IDX_DIM + IDX_HEADS
    assert (n // bsz) % tm == 0 and width - 1 <= halo <= tm and halo % SUBLANES == 0 and tm % CONV_ROWS == 0
    w_main = jnp.concatenate([w_in[:, :4 * w], w_in[:, 5 * w:6 * w]], axis=1).astype(BF16)
    w_vt = w_in[:, 4 * w:5 * w].T.astype(BF16)
    w_ki = w_in[:, 6 * w:6 * w + IDX_DIM]
    w_ki2 = jnp.concatenate([w_ki, w_ki], axis=1).astype(BF16)
    w_wit = w_in[:, 6 * w + IDX_DIM:].T.astype(BF16)
    row = lambda i: (i, 0)
    vec = lambda v: v.reshape(1, w).astype(F32)
    slot_shape = jax.ShapeDtypeStruct((n, SLOT_WIDTH), BF16)
    return pl.pallas_call(
        functools.partial(_mixin_odd_kernel, tiles_per_seq=(n // bsz) // tm, halo=halo),
        grid=(n // tm,),
        in_specs=[pl.BlockSpec((tm, d), row), _const_spec((1, d)), _const_spec((d, 5 * w)),
                  _const_spec((w, d)), _const_spec((d, 2 * IDX_DIM)), _const_spec((IDX_HEADS, d)),
                  _const_spec((1, SLOT)), _const_spec((1, SLOT)),
                  _const_spec((width, w)), _const_spec((1, w)), _const_spec((1, w)), _const_spec((1, w))],
        out_specs=[pl.BlockSpec((tm, w), row)] + [pl.BlockSpec((tm, SLOT_WIDTH), row)] * 2
        + [_vt_spec(tm), pl.BlockSpec((tm, w), row), pl.BlockSpec((tm, 128), row),
           pl.BlockSpec((IDX_HEADS, tm), lambda i: (0, i))],
        out_shape=[jax.ShapeDtypeStruct((n, w), BF16), slot_shape, slot_shape, _vt_shape(n),
                   jax.ShapeDtypeStruct((n, w), BF16),
                   jax.ShapeDtypeStruct((n, 128), BF16), jax.ShapeDtypeStruct((IDX_HEADS, n), F32)],
        scratch_shapes=[pltpu.VMEM((tm + halo, w), F32),
                        pltpu.VMEM((SUBLANES - 1, tm + halo - SUBLANES, w), F32)],
        compiler_params=_cparams(("arbitrary",)),
        name="mixin_odd",
    )(x, g.reshape(1, d), w_main, w_vt, w_ki2, w_wit, _slot_gain(q_norm), _slot_gain(k_norm),
      dw_w, vec(dw_b), vec(ln_g), vec(ln_b))


def _lru_kernel(gate_ref, xr_ref, cw_ref, cb_ref, wa_ref, ba_ref, wx_ref, bx_ref, sp_ref,
                o_ref, xbuf, a_s, u_s, h_s, hc, *, ts):
    j = pl.program_id(1)

    @pl.when(j == 0)
    def _():
        xbuf[0:8, :] = jnp.zeros((8, xbuf.shape[1]), F32)
        hc[...] = jnp.zeros(hc.shape, F32)

    xbuf[8:8 + ts, :] = xr_ref[...]
    xc = cb_ref[...] + cw_ref[0:1, :] * xbuf[5:5 + ts, :]
    for k in range(1, 4):
        xc = xc + cw_ref[k:k + 1, :] * xbuf[5 + k:5 + k + ts, :]
    xbuf[0:8, :] = xbuf[ts:ts + 8, :]

    xcb = xc.astype(BF16)
    r = jax.nn.sigmoid(_dot(xcb, wa_ref[...]) + ba_ref[...])
    ig = jax.nn.sigmoid(_dot(xcb, wx_ref[...]) + bx_ref[...])
    log_a = -LRU_C * r * sp_ref[...]
    a = jnp.exp(log_a)
    a_s[...] = a
    u_s[...] = jnp.sqrt(-jnp.tanh(log_a) * (a * a + 1.0)) * (ig * xc)

    row = lax.broadcasted_iota(jnp.int32, (8, a_s.shape[1]), 0)

    def body(g, carry):
        r0 = pl.multiple_of(g * 8, 8)
        a = a_s[pl.ds(r0, 8), :]
        u = u_s[pl.ds(r0, 8), :]
        for s in (1, 2, 4):
            ok = row >= s
            a_sh = jnp.where(ok, pltpu.roll(a, s, 0), 1.0)
            u_sh = jnp.where(ok, pltpu.roll(u, s, 0), 0.0)
            u = a * u_sh + u
            a = a * a_sh
        h = a * carry + u
        h_s[pl.ds(r0, 8), :] = h
        return h[7:8, :]

    hc[...] = lax.fori_loop(0, ts // 8, body, hc[...], unroll=4)
    o_ref[...] = (h_s[...] * jax.nn.gelu(gate_ref[...])).astype(BF16)


def _block_diag(wb):
    nb, bs, _ = wb.shape
    eye = jnp.eye(nb, dtype=wb.dtype)
    return (eye[:, None, :, None] * wb[:, :, None, :]).reshape(nb * bs, nb * bs)


def _lru(gate, xr, conv_w, conv_b, ra_w, ra_b, ix_w, ix_b, lam, *, bsz, ts=256):
    n, w = xr.shape
    seq = n // bsz
    assert seq % ts == 0
    nt = seq // ts
    row = lambda b, j: (b * nt + j, 0)
    vec = lambda v: v.reshape(1, w).astype(F32)
    return pl.pallas_call(
        functools.partial(_lru_kernel, ts=ts),
        grid=(bsz, nt),
        in_specs=[pl.BlockSpec((ts, w), row), pl.BlockSpec((ts, w), row),
                  _const_spec((conv_w.shape[0], w)), _const_spec((1, w)),
                  _const_spec((w, w)), _const_spec((1, w)), _const_spec((w, w)), _const_spec((1, w)),
                  _const_spec((1, w))],
        out_specs=pl.BlockSpec((ts, w), row),
        out_shape=jax.ShapeDtypeStruct((n, w), BF16),
        scratch_shapes=[pltpu.VMEM((ts + 8, w), F32), pltpu.VMEM((ts, w), F32),
                        pltpu.VMEM((ts, w), F32), pltpu.VMEM((ts, w), F32), pltpu.VMEM((1, w), F32)],
        compiler_params=_cparams(("arbitrary", "arbitrary")),
        name="rg_lru",
    )(gate, xr, conv_w, vec(conv_b), _block_diag(ra_w).astype(BF16), vec(ra_b),
      _block_diag(ix_w).astype(BF16), vec(ix_b), vec(jax.nn.softplus(-lam)))


def _bucket_tiles(t):
    assert t > REL_MAX_DIST
    n = np.arange(2 * t)
    nf = np.maximum(n, 1).astype(np.float32)
    large = REL_MAX_EXACT + (np.log(nf / np.float32(REL_MAX_EXACT))
                             / np.float32(math.log(REL_MAX_DIST / REL_MAX_EXACT))
                             * np.float32(REL_BUCKETS - REL_MAX_EXACT)).astype(np.int32)
    bucket = np.where(n < REL_MAX_EXACT, n, np.minimum(large, REL_BUCKETS - 1)).astype(np.int32)
    qry = np.arange(t)[None, :]
    key = np.arange(t)[:, None]
    return np.stack([bucket[np.maximum(qry - key, 0)], bucket[t + qry - key]])


def _build_bias(idx_ref, rb_ref, bias_ref):
    t = idx_ref.shape[1]
    causal = (lax.broadcasted_iota(jnp.int32, (t, t), 0) <= lax.broadcasted_iota(jnp.int32, (t, t), 1))
    for h in range(N_HEADS):
        far = rb_ref[REL_BUCKETS - 1, h]
        for which in (0, 1):
            idx = idx_ref[which]
            acc = jnp.zeros((t, t), F32)
            for b in range(REL_BUCKETS - 1):
                acc = jnp.where(idx == b, (rb_ref[b, h] - far) * LOG2E, acc)
            if which == 0:
                bias_ref[h, t:2 * t, :] = jnp.where(causal, acc, NEG)
            else:
                bias_ref[h, 0:t, :] = acc


def _attn_init(m_ref, acc_ref):
    m_ref[...] = jnp.full(m_ref.shape, M_INIT, F32)
    acc_ref[...] = jnp.zeros(acc_ref.shape, F32)


def _softmax_step(s, h, vt_h, m_ref, acc_ref):
    m_old = m_ref[h]
    m_new = jnp.maximum(m_old, jnp.max(s, axis=0, keepdims=True))
    p = jnp.exp2(s - m_new)
    m_ref[h] = m_new
    acc_ref[h] = jnp.exp2(m_old - m_new) * acc_ref[h] + _dot(vt_h, p.astype(BF16))


def _heads_pipelined(logits, vt_slot, m_ref, acc_ref):
    ahead = [logits(h) for h in range(HEADS_AHEAD)]
    for h in range(N_HEADS):
        if h + HEADS_AHEAD < N_HEADS:
            ahead.append(logits(h + HEADS_AHEAD))
        _softmax_step(ahead[h], h, vt_slot(h), m_ref, acc_ref)


FAR_PAIR = ("far", "far")
FAR_ONE = ("far",)
NEAR_DIAG = ("near", "diag")
DIAG_ONE = ("diag",)


def _attend_tiles(kt, kinds, q_slot, k_ref, vt_ref, bias_ref, mask_add, m_ref, acc_ref):
    t = ATTN_TILE
    n = len(kinds)
    rows = pl.ds(pl.multiple_of(kt * t, t), n * t)

    def logits(h):
        s = _dot_nt(k_ref[rows, _slot(h)], q_slot(h))
        if kinds == NEAR_DIAG:
            s = s + bias_ref[h]
        elif kinds == DIAG_ONE:
            s = s + bias_ref[h, t:2 * t, :]
        if mask_add is not None:
            s = s + mask_add
        return s

    def vt_slot(h):
        parts = [vt_ref[kt + a, _vt_rows(h), :] for a in range(n)]
        return parts[0] if n == 1 else jnp.concatenate(parts, axis=1)

    _heads_pipelined(logits, vt_slot, m_ref, acc_ref)


def _for_causal_tiles(i, tiles):
    n_far = jnp.maximum(i - 1, 0)

    def pair(j, carry):
        tiles(2 * j, FAR_PAIR)
        return carry

    lax.fori_loop(0, n_far // 2, pair, 0)

    @pl.when(n_far % 2 == 1)
    def _():
        tiles(n_far - 1, FAR_ONE)

    @pl.when(i >= 1)
    def _():
        tiles(i - 1, NEAR_DIAG)

    @pl.when(i == 0)
    def _():
        tiles(i, DIAG_ONE)


def _attn_finish(o_ref, acc_ref):
    parts = []
    for h in range(N_HEADS):
        acc = acc_ref[h]
        parts.append(acc[0:HEAD_DIM] * (1.0 / acc[FEAT0:FEAT0 + 1]))
    o_ref[...] = jnp.concatenate(parts, axis=0).T.astype(BF16)


def _smem_spec():
    return pl.BlockSpec(memory_space=pltpu.SMEM)


def _moba_kernel(rb_ref, q_ref, k_ref, vt_ref, km_ref, idx_ref, o_ref,
                 acc_ref, m_ref, qs_ref, bias_ref, *, n_blocks):
    i = pl.program_id(1)
    t = ATTN_TILE
    nbp = -(-n_blocks // 8) * 8

    @pl.when((pl.program_id(0) == 0) & (i == 0))
    def _():
        _build_bias(idx_ref, rb_ref, bias_ref)

    _attn_init(m_ref, acc_ref)

    blk = lax.broadcasted_iota(jnp.int32, (nbp, t), 0)
    past = blk < i
    for h in range(N_HEADS):
        qh = q_ref[:, _slot(h)]
        km_hi, km_lo = _split_bf16(km_ref[:, _slot(h)])
        gate_t = _dot_nt(km_hi, qh) + _dot_nt(km_lo, qh)
        g = jnp.where(past, gate_t[FEAT0:FEAT0 + nbp, :], NEG)
        rank = jnp.zeros((nbp, t), F32)
        for j in range(n_blocks):
            gj = g[j:j + 1, :]
            beats = (gj > g) | ((gj == g) & (blk > j))
            rank = rank + jnp.where(beats, 1.0, 0.0)
        flag = jnp.where(past & (rank >= MOBA_TOPK), NEG, 0.0)
        flag_t = jnp.concatenate([jnp.zeros((FEAT0, t), F32), flag,
                                  jnp.zeros((SLOT - FEAT0 - nbp, t), F32)], axis=0)
        qs_ref[:, _slot(h)] = (qh.astype(F32) + flag_t.T).astype(BF16)

    def tiles(kt, kinds):
        _attend_tiles(kt, kinds, lambda h: qs_ref[:, _slot(h)], k_ref, vt_ref, bias_ref, None,
                      m_ref, acc_ref)

    _for_causal_tiles(i, tiles)
    _attn_finish(o_ref, acc_ref)


def _moba(q, k, vt, kmean, rel_bias, *, bsz):
    n = q.shape[0]
    seq = n // bsz
    t = ATTN_TILE
    assert seq % t == 0 and t == MOBA_BLOCK
    nq = seq // t
    assert FEAT0 + nq <= SLOT
    km = jnp.pad(kmean.reshape(bsz, nq, SLOT_WIDTH), ((0, 0), (FEAT0, SLOT - FEAT0 - nq), (0, 0)))
    r3 = lambda a: a.reshape(bsz, seq, SLOT_WIDTH)
    seq_spec = pl.BlockSpec((None, seq, SLOT_WIDTH), lambda b, i: (b, 0, 0))
    out = pl.pallas_call(
        functools.partial(_moba_kernel, n_blocks=nq),
        grid=(bsz, nq),
        in_specs=[_smem_spec(),
                  pl.BlockSpec((None, t, SLOT_WIDTH), lambda b, i: (b, i, 0)), seq_spec,
                  pl.BlockSpec((nq, N_HEADS * VT_ROWS, t), lambda b, i: (b, 0, 0)),
                  pl.BlockSpec((None, SLOT, SLOT_WIDTH), lambda b, i: (b, 0, 0)),
                  _const_spec((2, t, t))],
        out_specs=pl.BlockSpec((None, t, ATTN_WIDTH), lambda b, i: (b, i, 0)),
        out_shape=jax.ShapeDtypeStruct((bsz, seq, ATTN_WIDTH), BF16),
        scratch_shapes=[pltpu.VMEM((N_HEADS, VT_ROWS, t), F32), pltpu.VMEM((N_HEADS, 1, t), F32),
                        pltpu.VMEM((t, SLOT_WIDTH), BF16),
                        pltpu.VMEM((N_HEADS, 2 * t, t), F32)],
        compiler_params=_cparams(("arbitrary", "arbitrary")),
        name="moba_attention",
    )(rel_bias, r3(q), r3(k), vt, km, jnp.asarray(_bucket_tiles(t)))
    return out.reshape(n, ATTN_WIDTH)


def _dsa_kernel(rb_ref, q_ref, k_ref, vt_ref, qi_ref, ki_ref, wit_ref, idx_ref, tri_ref, o_ref,
                acc_ref, m_ref, isc_ref, bc_ref, bias_ref, *, n_sel):
    i = pl.program_id(1)
    t = ATTN_TILE
    key = lax.broadcasted_iota(jnp.int32, (t, t), 0)
    qry = lax.broadcasted_iota(jnp.int32, (t, t), 1)
    lane128 = lax.broadcasted_iota(jnp.int32, (1, 128), 1)

    @pl.when((pl.program_id(0) == 0) & (i == 0))
    def _():
        _build_bias(idx_ref, rb_ref, bias_ref)

    _attn_init(m_ref, acc_ref)

    def index_tiles(kt, n, diag):
        rows = pl.ds(pl.multiple_of(kt * t, t), n * t)
        ki2 = ki_ref[rows, :]
        acc = jnp.zeros((n * t, t), F32)
        for pr in range(IDX_HEADS // 2):
            q2 = qi_ref[:, pr * 128:(pr + 1) * 128]
            for half in range(2):
                h = 2 * pr + half
                hm = (lane128 >= half * IDX_DIM) & (lane128 < (half + 1) * IDX_DIM)
                s = _dot_nt(ki2, jnp.where(hm, q2, jnp.zeros((), BF16)))
                acc = acc + jnp.maximum(s, 0.0) * wit_ref[h:h + 1, :]
        if diag:
            acc = jnp.where(key <= qry, acc, -jnp.inf)
        for a in range(n):
            isc_ref[kt + a] = acc[a * t:(a + 1) * t]

    def index_pair(j, carry):
        index_tiles(2 * j, 2, False)
        return carry

    lax.fori_loop(0, i // 2, index_pair, 0)

    @pl.when(i % 2 == 1)
    def _():
        index_tiles(i - 1, 1, False)

    index_tiles(i, 1, True)

    def fold8(x, op):
        return op(x.reshape(t // 8, 8, t), axis=0)

    def minmax_body(kt, carry):
        mn, mx = carry
        x = isc_ref[kt]
        mx = jnp.maximum(mx, fold8(x, jnp.max))
        mn = jnp.minimum(mn, fold8(jnp.where(x == -jnp.inf, jnp.inf, x), jnp.min))
        return mn, mx

    mn, mx = lax.fori_loop(0, i + 1, minmax_body,
                           (jnp.full((8, t), jnp.inf, F32), jnp.full((8, t), -jnp.inf, F32)))
    lo0 = jnp.min(mn, axis=0, keepdims=True)
    mx = jnp.max(mx, axis=0, keepdims=True)
    hi0 = mx + jnp.abs(mx) * 1e-3 + 1e-30
    n_valid = (i * t + 1 + lax.broadcasted_iota(jnp.int32, (1, t), 1)).astype(F32)
    want = jnp.minimum(n_valid, float(n_sel))

    def count_ge(thr):
        def one(kt, c):
            return c + fold8(jnp.where(isc_ref[kt] >= thr, 1.0, 0.0), jnp.sum)

        def pair(j, c):
            return one(2 * j + 1, one(2 * j, c))

        c = lax.fori_loop(0, (i + 1) // 2, pair, jnp.zeros((8, t), F32))
        c = lax.cond((i + 1) % 2 == 1, lambda c: one(i, c), lambda c: c, c)
        return jnp.sum(c, axis=0, keepdims=True)

    c_nonneg = count_ge(0.0)
    c_pos = count_ge(F32_TINY)
    above = want <= c_pos
    zero_tie = (want > c_pos) & (want <= c_nonneg)
    lo1 = jnp.where(above, F32_TINY, jnp.where(zero_tie, 0.0, lo0))
    hi1 = jnp.where(above, hi0, jnp.where(zero_tie, F32_TINY, 0.0))
    c_lo1 = jnp.where(above, c_pos, jnp.where(zero_tie, c_nonneg, n_valid))
    c_hi1 = jnp.where(above, 0.0, jnp.where(zero_tie, c_pos, c_nonneg))

    def bisect_step(_, carry):
        lo, hi, c_lo, c_hi = carry
        mid = 0.5 * (lo + hi)
        c_mid = count_ge(mid)
        up = c_mid >= want
        return (jnp.where(up, mid, lo), jnp.where(up, hi, mid),
                jnp.where(up, c_mid, c_lo), jnp.where(up, c_hi, c_mid))

    def bisect_cond(carry):
        it, _, _, c_lo, _ = carry
        return (it < BISECT_ITERS) & (jnp.max(jnp.where(zero_tie, 0.0, c_lo - want)) > 0.0)

    def bisect_body(carry):
        return (carry[0] + BISECT_CHECK_EVERY,) + lax.fori_loop(0, BISECT_CHECK_EVERY, bisect_step, carry[1:])

    _, lo, hi, c_lo, c_hi = lax.while_loop(bisect_cond, bisect_body, (0, lo1, hi1, c_lo1, c_hi1))
    need = want - c_hi
    banded = jnp.max(c_lo - want) > 0.0
    bc_ref[...] = jnp.zeros(bc_ref.shape, F32)

    def select_mask(kt):
        x = isc_ref[kt]

        def plain():
            return jnp.where(x >= lo, 0.0, NEG)

        def with_band():
            band = jnp.where((x >= lo) & (x < hi), 1.0, 0.0)
            before = bc_ref[...] + _dot(tri_ref[...], band.astype(BF16))
            bc_ref[...] = bc_ref[...] + jnp.sum(band, axis=0, keepdims=True)
            return jnp.where((x >= hi) | ((band > 0.0) & (before < need)), 0.0, NEG)

        return lax.cond(banded, with_band, plain)

    def tiles(kt, kinds):
        masks = [select_mask(kt + a) for a in range(len(kinds))]
        mask_add = masks[0] if len(masks) == 1 else jnp.concatenate(masks, axis=0)
        _attend_tiles(kt, kinds, lambda h: q_ref[:, _slot(h)], k_ref, vt_ref, bias_ref, mask_add,
                      m_ref, acc_ref)

    _for_causal_tiles(i, tiles)
    _attn_finish(o_ref, acc_ref)


def _dsa(q, k, vt, qi, ki2, wit, rel_bias, *, bsz):
    n = q.shape[0]
    seq = n // bsz
    t = ATTN_TILE
    assert seq % t == 0
    nq = seq // t
    n_sel = min(DSA_TOPK_MAX, seq // 4)
    tri = (jnp.arange(t)[None, :] < jnp.arange(t)[:, None]).astype(BF16)
    r3 = lambda a: a.reshape(bsz, seq, a.shape[-1])
    tile_spec = lambda width: pl.BlockSpec((None, t, width), lambda b, i: (b, i, 0))
    seq_spec = lambda width: pl.BlockSpec((None, seq, width), lambda b, i: (b, 0, 0))
    out = pl.pallas_call(
        functools.partial(_dsa_kernel, n_sel=n_sel),
        grid=(bsz, nq),
        in_specs=[_smem_spec(), tile_spec(SLOT_WIDTH), seq_spec(SLOT_WIDTH),
                  pl.BlockSpec((nq, N_HEADS * VT_ROWS, t), lambda b, i: (b, 0, 0)),
                  tile_spec(ATTN_WIDTH), seq_spec(128),
                  pl.BlockSpec((IDX_HEADS, t), lambda b, i: (0, b * nq + i)),
                  _const_spec((2, t, t)), _const_spec((t, t))],
        out_specs=tile_spec(ATTN_WIDTH),
        out_shape=jax.ShapeDtypeStruct((bsz, seq, ATTN_WIDTH), BF16),
        scratch_shapes=[pltpu.VMEM((N_HEADS, VT_ROWS, t), F32), pltpu.VMEM((N_HEADS, 1, t), F32),
                        pltpu.VMEM((nq, t, t), F32), pltpu.VMEM((1, t), F32),
                        pltpu.VMEM((N_HEADS, 2 * t, t), F32)],
        compiler_params=_cparams(("arbitrary", "arbitrary")),
        name="dsa_attention",
    )(rel_bias, r3(q), r3(k), vt, r3(qi), r3(ki2), wit, jnp.asarray(_bucket_tiles(t)), tri)
    return out.reshape(n, ATTN_WIDTH)


def kernel(x, rel_bias, ffn1_norm, ffn1_w_gate, ffn1_w_up, ffn1_w_down, mix_norm, ffn2_norm, ffn2_w_gate, ffn2_w_up, ffn2_w_down, ev_w_in, ev_conv_w, ev_conv_b, ev_ra_w, ev_ra_b, ev_ix_w, ev_ix_b, ev_lambda, ev_q_norm, ev_k_norm, ev_w_out, od_w_in, od_dw_w, od_dw_b, od_ln_g, od_ln_b, od_q_norm, od_k_norm, od_w_out):
    bsz, seq, d = x.shape
    depth = ffn1_norm.shape[0]
    h = x.reshape(bsz * seq, d)
    for i in range(depth):
        h = _ffn(h, ffn1_norm, ffn1_w_gate, ffn1_w_up, ffn1_w_down, i)
        j = i // 2
        if i % 2 == 0:
            gate, xr, q, k, vt, kmean = _mixin_even(h, mix_norm[i], ev_w_in[j], ev_q_norm[j], ev_k_norm[j],
                                                    n_blocks=seq // MOBA_BLOCK)
            ya = _lru(gate, xr, ev_conv_w[j], ev_conv_b[j], ev_ra_w[j], ev_ra_b[j],
                      ev_ix_w[j], ev_ix_b[j], ev_lambda[j], bsz=bsz)
            yb = _moba(q, k, vt, kmean, rel_bias, bsz=bsz)
            mix = (ya, yb, ev_w_out[j])
        else:
            yc, q, k, vt, qi, ki2, wit = _mixin_odd(h, mix_norm[i], od_w_in[j], od_q_norm[j], od_k_norm[j],
                                                    od_dw_w[j], od_dw_b[j], od_ln_g[j], od_ln_b[j], bsz=bsz)
            yd = _dsa(q, k, vt, qi, ki2, wit, rel_bias, bsz=bsz)
            mix = (yc, yd, od_w_out[j])
        h = _ffn(h, ffn2_norm, ffn2_w_gate, ffn2_w_up, ffn2_w_down, i, mix)
    return h.reshape(bsz, seq, d)
```

```python
import functools
import math

import numpy as np
import jax
import jax.numpy as jnp
from jax import lax
from jax.experimental import pallas as pl
from jax.experimental.pallas import tpu as pltpu

F32 = jnp.float32
BF16 = jnp.bfloat16

N_HEADS = 8
HEAD_DIM = 64
ATTN_WIDTH = N_HEADS * HEAD_DIM
LRU_C = 8.0
MOBA_BLOCK = 256
MOBA_TOPK = 3
IDX_HEADS = 8
IDX_DIM = 64
DSA_TOPK_MAX = 256
REL_BUCKETS = 32
REL_MAX_EXACT = REL_BUCKETS // 2
REL_MAX_DIST = 128
EPS = 1e-6
NEG = -1e30
M_INIT = -1e29
ATTN_TILE = 256
SUBLANES = 8
CONV_ROWS = 64
SLOT = 128
SLOT_WIDTH = N_HEADS * SLOT
FEAT0 = HEAD_DIM
VT_ROWS = 80
HEADS_AHEAD = 4
BISECT_ITERS = 32
BISECT_CHECK_EVERY = 4
F32_TINY = float(np.finfo(np.float32).tiny)
LOG2E = math.log2(math.e)
Q_SCALE = HEAD_DIM ** -0.5 * LOG2E
VMEM_LIMIT = 56 * 1024 * 1024
FFN_VMEM_LIMIT = 60 * 1024 * 1024


def _cparams(sem):
    return pltpu.CompilerParams(dimension_semantics=sem, vmem_limit_bytes=VMEM_LIMIT)


def _dot(a, b):
    return jnp.dot(a, b, preferred_element_type=F32)


def _dot_nt(a, b):
    return lax.dot_general(a, b, (((1,), (1,)), ((), ())), preferred_element_type=F32)


def _split_bf16(x):
    hi = x.astype(BF16)
    lo = (x - hi.astype(F32)).astype(BF16)
    return hi, lo


def _rms_rows(x, g):
    return x * lax.rsqrt(jnp.mean(x * x, axis=-1, keepdims=True) + EPS) * g


def _const_spec(shape):
    nd = len(shape)
    return pl.BlockSpec(shape, lambda *_: (0,) * nd, pipeline_mode=pl.Buffered(1))


def _slot(h):
    return slice(h * SLOT, (h + 1) * SLOT)


def _vt_rows(h):
    return slice(h * VT_ROWS, (h + 1) * VT_ROWS)


def _to_slots(x):
    low = lax.broadcasted_iota(jnp.int32, (1, SLOT), 1) < HEAD_DIM
    slots = []
    for p in range(N_HEADS // 2):
        chunk = x[:, p * SLOT:(p + 1) * SLOT]
        slots.append(jnp.where(low, chunk, 0.0))
        slots.append(jnp.where(low, pltpu.roll(chunk, HEAD_DIM, 1), 0.0))
    return slots


def _ffn_kernel(*refs, n_chunks, has_mix):
    if has_mix:
        x_ref, ya_ref, yb_ref, wo_ref, g_ref, wg_ref, wu_ref, wd_ref, o_ref, wg_s, wu_s, wd_s = refs
    else:
        x_ref, g_ref, wg_ref, wu_ref, wd_ref, o_ref, wg_s, wu_s, wd_s = refs
    step = pl.program_id(0)

    @pl.when(step < n_chunks)
    def _():
        wg_s[step] = wg_ref[...].astype(BF16)
        wu_s[step] = wu_ref[...].astype(BF16)
        wd_s[step] = wd_ref[...].astype(BF16)

    @pl.when(step >= n_chunks)
    def _():
        x = x_ref[...]
        if has_mix:
            w = ya_ref.shape[1]
            x = x + _dot(ya_ref[...], wo_ref[0:w, :]) + _dot(yb_ref[...], wo_ref[w:2 * w, :])
        hn = _rms_rows(x, g_ref[...]).astype(BF16)
        acc = jnp.zeros(x.shape, F32)
        for c in range(n_chunks):
            gt = _dot(hn, wg_s[c])
            ut = _dot(hn, wu_s[c])
            a = (gt * jax.nn.sigmoid(gt) * ut).astype(BF16)
            acc = acc + _dot(a, wd_s[c])
        o_ref[...] = x + 0.5 * acc


def _ffn(x, g, wg, wu, wd, layer, mix=None, *, tm=1024, ff_chunk=256):
    n, d = x.shape
    d_ff = wg.shape[2]
    assert n % tm == 0 and d_ff % ff_chunk == 0
    nc = d_ff // ff_chunk
    tile = lambda s: (jnp.maximum(s - nc, 0), 0)
    chunk_col = lambda s: (layer, 0, jnp.minimum(s, nc - 1))
    chunk_row = lambda s: (layer, jnp.minimum(s, nc - 1), 0)
    in_specs = [pl.BlockSpec((tm, d), tile)]
    args = [x]
    scratch = [pltpu.VMEM((nc, d, ff_chunk), BF16), pltpu.VMEM((nc, d, ff_chunk), BF16),
               pltpu.VMEM((nc, ff_chunk, d), BF16)]
    if mix is not None:
        ya, yb, w_out = mix
        w = ya.shape[1]
        assert w_out.shape == (2 * w, d)
        in_specs += [pl.BlockSpec((tm, w), tile), pl.BlockSpec((tm, w), tile), _const_spec((2 * w, d))]
        args += [ya, yb, w_out.astype(BF16)]
    in_specs += [_const_spec((1, d)), pl.BlockSpec((None, d, ff_chunk), chunk_col),
                 pl.BlockSpec((None, d, ff_chunk), chunk_col), pl.BlockSpec((None, ff_chunk, d), chunk_row)]
    args += [g[layer].reshape(1, d), wg, wu, wd]
    return pl.pallas_call(
        functools.partial(_ffn_kernel, n_chunks=nc, has_mix=mix is not None),
        grid=(nc + n // tm,),
        in_specs=in_specs,
        out_specs=pl.BlockSpec((tm, d), tile),
        out_shape=jax.ShapeDtypeStruct((n, d), F32),
        scratch_shapes=scratch,
        compiler_params=pltpu.CompilerParams(dimension_semantics=("arbitrary",),
                                             vmem_limit_bytes=FFN_VMEM_LIMIT),
        name="ffn_mix" if mix is not None else "ffn",
    )(*args)


def _slot_rms(xs, g):
    ms = jnp.sum(xs * xs, axis=-1, keepdims=True) * (1.0 / HEAD_DIM)
    return xs * lax.rsqrt(ms + EPS) * g


def _store_q_slots(z, gain_ref, q_o):
    for h, zs in enumerate(_to_slots(z)):
        q_o[:, _slot(h)] = (_slot_rms(zs, gain_ref[...]) * Q_SCALE).astype(BF16)


def _store_k_slots(z, gain_ref, k_o, one_lanes):
    slots = [_slot_rms(zs, gain_ref[...]) for zs in _to_slots(z)]
    for h, ks in enumerate(slots):
        k_o[:, _slot(h)] = (ks if one_lanes is None else jnp.where(one_lanes, 1.0, ks)).astype(BF16)
    return slots


def _store_vt_slots(zt, v_o):
    pad = VT_ROWS - HEAD_DIM
    tail = (lax.broadcasted_iota(jnp.int32, (pad, ATTN_TILE), 0) == 0).astype(BF16)
    for h in range(N_HEADS):
        for r in range(zt.shape[1] // ATTN_TILE):
            feat = zt[h * HEAD_DIM:(h + 1) * HEAD_DIM, r * ATTN_TILE:(r + 1) * ATTN_TILE]
            v_o[r, h * VT_ROWS:h * VT_ROWS + HEAD_DIM, :] = feat.astype(BF16)
            v_o[r, h * VT_ROWS + HEAD_DIM:(h + 1) * VT_ROWS, :] = tail


def _pipelined(stages):
    nxt = stages[0][0]()
    for j, (_, consume) in enumerate(stages):
        cur = nxt
        if j + 1 < len(stages):
            nxt = stages[j + 1][0]()
        consume(cur)


def _mixin_even_kernel(x_ref, g_ref, w_ref, wvt_ref, qn_ref, kn_ref,
                       gate_o, xr_o, q_o, k_o, v_o, km_o, *, n_blocks):
    tm = x_ref.shape[0]
    hn = _rms_rows(x_ref[...], g_ref[...]).astype(BF16)
    w = ATTN_WIDTH
    proj = lambda c: (lambda: _dot(hn, w_ref[:, c * w:(c + 1) * w]))
    blocks_per_tile = tm // MOBA_BLOCK
    row_block = lax.broadcasted_iota(jnp.int32, (tm, 1), 0) // MOBA_BLOCK
    block = (pl.program_id(0) * blocks_per_tile + row_block) % n_blocks
    block_lane = lax.broadcasted_iota(jnp.int32, (1, SLOT), 1) == FEAT0 + block

    def store_k(z):
        for h, ks in enumerate(_store_k_slots(z, kn_ref, k_o, block_lane)):
            for r in range(blocks_per_tile):
                km_o[r, :, _slot(h)] = jnp.mean(ks[r * MOBA_BLOCK:(r + 1) * MOBA_BLOCK], axis=0, keepdims=True)

    def store(ref):
        def consume(z):
            ref[...] = z
        return consume

    _pipelined([(proj(2), lambda z: _store_q_slots(z, qn_ref, q_o)), (proj(3), store_k),
                (lambda: _dot_nt(wvt_ref[...], hn), lambda zt: _store_vt_slots(zt, v_o)),
                (proj(0), store(gate_o)), (proj(1), store(xr_o))])


def _vt_spec(tm):
    return pl.BlockSpec((tm // ATTN_TILE, N_HEADS * VT_ROWS, ATTN_TILE), lambda i: (i, 0, 0))


def _vt_shape(n):
    return jax.ShapeDtypeStruct((n // ATTN_TILE, N_HEADS * VT_ROWS, ATTN_TILE), BF16)


def _slot_gain(g):
    return jnp.pad(g.astype(F32), (0, SLOT - HEAD_DIM)).reshape(1, SLOT)


def _mixin_even(x, g, w_in, q_norm, k_norm, *, n_blocks, tm=512):
    n, d = x.shape
    w = ATTN_WIDTH
    assert n % tm == 0 and tm % MOBA_BLOCK == 0 and tm % ATTN_TILE == 0 and w_in.shape[1] == 5 * w
    assert FEAT0 + n_blocks <= SLOT
    row = lambda i: (i, 0)
    blk = tm // MOBA_BLOCK
    slot_shape = jax.ShapeDtypeStruct((n, SLOT_WIDTH), BF16)
    return pl.pallas_call(
        functools.partial(_mixin_even_kernel, n_blocks=n_blocks),
        grid=(n // tm,),
        in_specs=[pl.BlockSpec((tm, d), row), _const_spec((1, d)), _const_spec((d, 4 * w)),
                  _const_spec((w, d)), _const_spec((1, SLOT)), _const_spec((1, SLOT))],
        out_specs=[pl.BlockSpec((tm, w), row)] * 2 + [pl.BlockSpec((tm, SLOT_WIDTH), row)] * 2
        + [_vt_spec(tm), pl.BlockSpec((blk, 1, SLOT_WIDTH), lambda i: (i, 0, 0))],
        out_shape=[jax.ShapeDtypeStruct((n, w), F32), jax.ShapeDtypeStruct((n, w), F32),
                   slot_shape, slot_shape, _vt_shape(n),
                   jax.ShapeDtypeStruct((n // MOBA_BLOCK, 1, SLOT_WIDTH), F32)],
        compiler_params=_cparams(("parallel",)),
        name="mixin_even",
    )(x, g.reshape(1, d), w_in[:, :4 * w].astype(BF16), w_in[:, 4 * w:].T.astype(BF16),
      _slot_gain(q_norm), _slot_gain(k_norm))


def _conv_ln_silu(c, first, w_ref, b_ref, g_ref, beta_ref, o_ref, cbuf, sh_ref, *, halo):
    ts = c.shape[0]
    width = w_ref.shape[0]

    @pl.when(first)
    def _():
        cbuf[0:halo, :] = jnp.zeros((halo, cbuf.shape[1]), F32)

    cbuf[halo:halo + ts, :] = c
    base = halo - (width - 1)
    span = sh_ref.shape[1]
    for r in range(1, SUBLANES):
        sh_ref[r - 1] = cbuf[r:r + span, :]

    def chunk(c0):
        y = jnp.broadcast_to(b_ref[...], (CONV_ROWS, cbuf.shape[1]))
        for k in range(width):
            r = (base + k) % SUBLANES
            u0 = c0 + base + k - r
            win = cbuf[u0:u0 + CONV_ROWS, :] if r == 0 else sh_ref[r - 1, u0:u0 + CONV_ROWS, :]
            y = y + w_ref[k:k + 1, :] * win
        mu = jnp.mean(y, axis=-1, keepdims=True)
        yc = y - mu
        var = jnp.mean(yc * yc, axis=-1, keepdims=True)
        z = yc * lax.rsqrt(var + EPS) * g_ref[...] + beta_ref[...]
        o_ref[c0:c0 + CONV_ROWS, :] = (z * jax.nn.sigmoid(z)).astype(BF16)

    def close():
        cbuf[0:halo, :] = cbuf[ts:ts + halo, :]

    return [functools.partial(chunk, c0) for c0 in range(0, ts, CONV_ROWS)], close


def _mixin_odd_kernel(x_ref, g_ref, w_ref, wvt_ref, wki_ref, wwit_ref, qn_ref, kn_ref,
                      dw_ref, db_ref, lg_ref, lb_ref,
                      yc_o, q_o, k_o, v_o, qi_o, ki_o, wi_o, cbuf, sh_ref, *, tiles_per_seq, halo):
    hn = _rms_rows(x_ref[...], g_ref[...]).astype(BF16)
    w = ATTN_WIDTH
    proj = lambda c, n=1: _dot(hn, w_ref[:, c * w:(c + n) * w])
    first = pl.program_id(0) % tiles_per_seq == 0

    z_glu = proj(0, 2)
    z_q = proj(2)
    chunks, close = _conv_ln_silu(z_glu[:, 0:w] * jax.nn.sigmoid(z_glu[:, w:2 * w]), first,
                                  dw_ref, db_ref, lg_ref, lb_ref, yc_o, cbuf, sh_ref, halo=halo)
    per_stage = -(-len(chunks) // 4)
    run_chunks = lambda j: [f() for f in chunks[j * per_stage:(j + 1) * per_stage]]
    z_k = proj(3)
    _store_q_slots(z_q, qn_ref, q_o)
    run_chunks(0)
    z_vt = _dot_nt(wvt_ref[...], hn)
    _store_k_slots(z_k, kn_ref, k_o, None)
    run_chunks(1)
    z_qi = proj(4)
    _store_vt_slots(z_vt, v_o)
    run_chunks(2)
    z_ki = _dot(hn, wki_ref[...])
    z_wit = _dot_nt(wwit_ref[...], hn)
    qi_o[...] = z_qi.astype(BF16)
    run_chunks(3)
    close()
    ki_o[...] = z_ki.astype(BF16)
    wi_o[...] = z_wit * (IDX_DIM ** -0.5 * IDX_HEADS ** -0.5)


def _mixin_odd(x, g, w_in, q_norm, k_norm, dw_w, dw_b, ln_g, ln_b, *, bsz, tm=512, halo=32):
    n, d = x.shape
    w = ATTN_WIDTH
    width = dw_w.shape[0]
    assert n % tm == 0 and tm % ATTN_TILE == 0 and w_in.shape[1] == 6 * w + IDX_DIM + IDX_HEADS
    assert (n // bsz) % tm == 0 and width - 1 <= halo <= tm and halo % SUBLANES == 0 and tm % CONV_ROWS == 0
    w_main = jnp.concatenate([w_in[:, :4 * w], w_in[:, 5 * w:6 * w]], axis=1).astype(BF16)
    w_vt = w_in[:, 4 * w:5 * w].T.astype(BF16)
    w_ki = w_in[:, 6 * w:6 * w + IDX_DIM]
    w_ki2 = jnp.concatenate([w_ki, w_ki], axis=1).astype(BF16)
    w_wit = w_in[:, 6 * w + IDX_DIM:].T.astype(BF16)
    row = lambda i: (i, 0)
    vec = lambda v: v.reshape(1, w).astype(F32)
    slot_shape = jax.ShapeDtypeStruct((n, SLOT_WIDTH), BF16)
    return pl.pallas_call(
        functools.partial(_mixin_odd_kernel, tiles_per_seq=(n // bsz) // tm, halo=halo),
        grid=(n // tm,),
        in_specs=[pl.BlockSpec((tm, d), row), _const_spec((1, d)), _const_spec((d, 5 * w)),
                  _const_spec((w, d)), _const_spec((d, 2 * IDX_DIM)), _const_spec((IDX_HEADS, d)),
                  _const_spec((1, SLOT)), _const_spec((1, SLOT)),
                  _const_spec((width, w)), _const_spec((1, w)), _const_spec((1, w)), _const_spec((1, w))],
        out_specs=[pl.BlockSpec((tm, w), row)] + [pl.BlockSpec((tm, SLOT_WIDTH), row)] * 2
        + [_vt_spec(tm), pl.BlockSpec((tm, w), row), pl.BlockSpec((tm, 128), row),
           pl.BlockSpec((IDX_HEADS, tm), lambda i: (0, i))],
        out_shape=[jax.ShapeDtypeStruct((n, w), BF16), slot_shape, slot_shape, _vt_shape(n),
                   jax.ShapeDtypeStruct((n, w), BF16),
                   jax.ShapeDtypeStruct((n, 128), BF16), jax.ShapeDtypeStruct((IDX_HEADS, n), F32)],
        scratch_shapes=[pltpu.VMEM((tm + halo, w), F32),
                        pltpu.VMEM((SUBLANES - 1, tm + halo - SUBLANES, w), F32)],
        compiler_params=_cparams(("arbitrary",)),
        name="mixin_odd",
    )(x, g.reshape(1, d), w_main, w_vt, w_ki2, w_wit, _slot_gain(q_norm), _slot_gain(k_norm),
      dw_w, vec(dw_b), vec(ln_g), vec(ln_b))


def _lru_kernel(gate_ref, xr_ref, cw_ref, cb_ref, wa_ref, ba_ref, wx_ref, bx_ref, sp_ref,
                o_ref, xbuf, a_s, u_s, h_s, hc, *, ts):
    j = pl.program_id(1)

    @pl.when(j == 0)
    def _():
        xbuf[0:8, :] = jnp.zeros((8, xbuf.shape[1]), F32)
        hc[...] = jnp.zeros(hc.shape, F32)

    xbuf[8:8 + ts, :] = xr_ref[...]
    xc = cb_ref[...] + cw_ref[0:1, :] * xbuf[5:5 + ts, :]
    for k in range(1, 4):
        xc = xc + cw_ref[k:k + 1, :] * xbuf[5 + k:5 + k + ts, :]
    xbuf[0:8, :] = xbuf[ts:ts + 8, :]

    xcb = xc.astype(BF16)
    r = jax.nn.sigmoid(_dot(xcb, wa_ref[...]) + ba_ref[...])
    ig = jax.nn.sigmoid(_dot(xcb, wx_ref[...]) + bx_ref[...])
    log_a = -LRU_C * r * sp_ref[...]
    a = jnp.exp(log_a)
    a_s[...] = a
    u_s[...] = jnp.sqrt(-jnp.tanh(log_a) * (a * a + 1.0)) * (ig * xc)

    row = lax.broadcasted_iota(jnp.int32, (8, a_s.shape[1]), 0)

    def body(g, carry):
        r0 = pl.multiple_of(g * 8, 8)
        a = a_s[pl.ds(r0, 8), :]
        u = u_s[pl.ds(r0, 8), :]
        for s in (1, 2, 4):
            ok = row >= s
            a_sh = jnp.where(ok, pltpu.roll(a, s, 0), 1.0)
            u_sh = jnp.where(ok, pltpu.roll(u, s, 0), 0.0)
            u = a * u_sh + u
            a = a * a_sh
        h = a * carry + u
        h_s[pl.ds(r0, 8), :] = h
        return h[7:8, :]

    hc[...] = lax.fori_loop(0, ts // 8, body, hc[...], unroll=4)
    o_ref[...] = (h_s[...] * jax.nn.gelu(gate_ref[...])).astype(BF16)


def _block_diag(wb):
    nb, bs, _ = wb.shape
    eye = jnp.eye(nb, dtype=wb.dtype)
    return (eye[:, None, :, None] * wb[:, :, None, :]).reshape(nb * bs, nb * bs)


def _lru(gate, xr, conv_w, conv_b, ra_w, ra_b, ix_w, ix_b, lam, *, bsz, ts=256):
    n, w = xr.shape
    seq = n // bsz
    assert seq % ts == 0
    nt = seq // ts
    row = lambda b, j: (b * nt + j, 0)
    vec = lambda v: v.reshape(1, w).astype(F32)
    return pl.pallas_call(
        functools.partial(_lru_kernel, ts=ts),
        grid=(bsz, nt),
        in_specs=[pl.BlockSpec((ts, w), row), pl.BlockSpec((ts, w), row),
                  _const_spec((conv_w.shape[0], w)), _const_spec((1, w)),
                  _const_spec((w, w)), _const_spec((1, w)), _const_spec((w, w)), _const_spec((1, w)),
                  _const_spec((1, w))],
        out_specs=pl.BlockSpec((ts, w), row),
        out_shape=jax.ShapeDtypeStruct((n, w), BF16),
        scratch_shapes=[pltpu.VMEM((ts + 8, w), F32), pltpu.VMEM((ts, w), F32),
                        pltpu.VMEM((ts, w), F32), pltpu.VMEM((ts, w), F32), pltpu.VMEM((1, w), F32)],
        compiler_params=_cparams(("arbitrary", "arbitrary")),
        name="rg_lru",
    )(gate, xr, conv_w, vec(conv_b), _block_diag(ra_w).astype(BF16), vec(ra_b),
      _block_diag(ix_w).astype(BF16), vec(ix_b), vec(jax.nn.softplus(-lam)))


def _bucket_tiles(t):
    assert t > REL_MAX_DIST
    n = np.arange(2 * t)
    nf = np.maximum(n, 1).astype(np.float32)
    large = REL_MAX_EXACT + (np.log(nf / np.float32(REL_MAX_EXACT))
                             / np.float32(math.log(REL_MAX_DIST / REL_MAX_EXACT))
                             * np.float32(REL_BUCKETS - REL_MAX_EXACT)).astype(np.int32)
    bucket = np.where(n < REL_MAX_EXACT, n, np.minimum(large, REL_BUCKETS - 1)).astype(np.int32)
    qry = np.arange(t)[None, :]
    key = np.arange(t)[:, None]
    return np.stack([bucket[np.maximum(qry - key, 0)], bucket[t + qry - key]])


def _build_bias(idx_ref, rb_ref, bias_ref):
    t = idx_ref.shape[1]
    causal = (lax.broadcasted_iota(jnp.int32, (t, t), 0) <= lax.broadcasted_iota(jnp.int32, (t, t), 1))
    for h in range(N_HEADS):
        far = rb_ref[REL_BUCKETS - 1, h]
        for which in (0, 1):
            idx = idx_ref[which]
            acc = jnp.zeros((t, t), F32)
            for b in range(REL_BUCKETS - 1):
                acc = jnp.where(idx == b, (rb_ref[b, h] - far) * LOG2E, acc)
            if which == 0:
                bias_ref[h, t:2 * t, :] = jnp.where(causal, acc, NEG)
            else:
                bias_ref[h, 0:t, :] = acc


def _attn_init(m_ref, acc_ref):
    m_ref[...] = jnp.full(m_ref.shape, M_INIT, F32)
    acc_ref[...] = jnp.zeros(acc_ref.shape, F32)


def _softmax_step(s, h, vt_h, m_ref, acc_ref):
    m_old = m_ref[h]
    m_new = jnp.maximum(m_old, jnp.max(s, axis=0, keepdims=True))
    p = jnp.exp2(s - m_new)
    m_ref[h] = m_new
    acc_ref[h] = jnp.exp2(m_old - m_new) * acc_ref[h] + _dot(vt_h, p.astype(BF16))


def _heads_pipelined(logits, vt_slot, m_ref, acc_ref):
    ahead = [logits(h) for h in range(HEADS_AHEAD)]
    for h in range(N_HEADS):
        if h + HEADS_AHEAD < N_HEADS:
            ahead.append(logits(h + HEADS_AHEAD))
        _softmax_step(ahead[h], h, vt_slot(h), m_ref, acc_ref)


FAR_PAIR = ("far", "far")
FAR_ONE = ("far",)
NEAR_DIAG = ("near", "diag")
DIAG_ONE = ("diag",)


def _attend_tiles(kt, kinds, q_slot, k_ref, vt_ref, bias_ref, mask_add, m_ref, acc_ref):
    t = ATTN_TILE
    n = len(kinds)
    rows = pl.ds(pl.multiple_of(kt * t, t), n * t)

    def logits(h):
        s = _dot_nt(k_ref[rows, _slot(h)], q_slot(h))
        if kinds == NEAR_DIAG:
            s = s + bias_ref[h]
        elif kinds == DIAG_ONE:
            s = s + bias_ref[h, t:2 * t, :]
        if mask_add is not None:
            s = s + mask_add
        return s

    def vt_slot(h):
        parts = [vt_ref[kt + a, _vt_rows(h), :] for a in range(n)]
        return parts[0] if n == 1 else jnp.concatenate(parts, axis=1)

    _heads_pipelined(logits, vt_slot, m_ref, acc_ref)


def _for_causal_tiles(i, tiles):
    n_far = jnp.maximum(i - 1, 0)

    def pair(j, carry):
        tiles(2 * j, FAR_PAIR)
        return carry

    lax.fori_loop(0, n_far // 2, pair, 0)

    @pl.when(n_far % 2 == 1)
    def _():
        tiles(n_far - 1, FAR_ONE)

    @pl.when(i >= 1)
    def _():
        tiles(i - 1, NEAR_DIAG)

    @pl.when(i == 0)
    def _():
        tiles(i, DIAG_ONE)


def _attn_finish(o_ref, acc_ref):
    parts = []
    for h in range(N_HEADS):
        acc = acc_ref[h]
        parts.append(acc[0:HEAD_DIM] * (1.0 / acc[FEAT0:FEAT0 + 1]))
    o_ref[...] = jnp.concatenate(parts, axis=0).T.astype(BF16)


def _smem_spec():
    return pl.BlockSpec(memory_space=pltpu.SMEM)


def _moba_kernel(rb_ref, q_ref, k_ref, vt_ref, km_ref, idx_ref, o_ref,
                 acc_ref, m_ref, qs_ref, bias_ref, *, n_blocks):
    i = pl.program_id(1)
    t = ATTN_TILE
    nbp = -(-n_blocks // 8) * 8

    @pl.when((pl.program_id(0) == 0) & (i == 0))
    def _():
        _build_bias(idx_ref, rb_ref, bias_ref)

    _attn_init(m_ref, acc_ref)

    blk = lax.broadcasted_iota(jnp.int32, (nbp, t), 0)
    past = blk < i
    for h in range(N_HEADS):
        qh = q_ref[:, _slot(h)]
        km_hi, km_lo = _split_bf16(km_ref[:, _slot(h)])
        gate_t = _dot_nt(km_hi, qh) + _dot_nt(km_lo, qh)
        g = jnp.where(past, gate_t[FEAT0:FEAT0 + nbp, :], NEG)
        rank = jnp.zeros((nbp, t), F32)
        for j in range(n_blocks):
            gj = g[j:j + 1, :]
            beats = (gj > g) | ((gj == g) & (blk > j))
            rank = rank + jnp.where(beats, 1.0, 0.0)
        flag = jnp.where(past & (rank >= MOBA_TOPK), NEG, 0.0)
        flag_t = jnp.concatenate([jnp.zeros((FEAT0, t), F32), flag,
                                  jnp.zeros((SLOT - FEAT0 - nbp, t), F32)], axis=0)
        qs_ref[:, _slot(h)] = (qh.astype(F32) + flag_t.T).astype(BF16)

    def tiles(kt, kinds):
        _attend_tiles(kt, kinds, lambda h: qs_ref[:, _slot(h)], k_ref, vt_ref, bias_ref, None,
                      m_ref, acc_ref)

    _for_causal_tiles(i, tiles)
    _attn_finish(o_ref, acc_ref)


def _moba(q, k, vt, kmean, rel_bias, *, bsz):
    n = q.shape[0]
    seq = n // bsz
    t = ATTN_TILE
    assert seq % t == 0 and t == MOBA_BLOCK
    nq = seq // t
    assert FEAT0 + nq <= SLOT
    km = jnp.pad(kmean.reshape(bsz, nq, SLOT_WIDTH), ((0, 0), (FEAT0, SLOT - FEAT0 - nq), (0, 0)))
    r3 = lambda a: a.reshape(bsz, seq, SLOT_WIDTH)
    seq_spec = pl.BlockSpec((None, seq, SLOT_WIDTH), lambda b, i: (b, 0, 0))
    out = pl.pallas_call(
        functools.partial(_moba_kernel, n_blocks=nq),
        grid=(bsz, nq),
        in_specs=[_smem_spec(),
                  pl.BlockSpec((None, t, SLOT_WIDTH), lambda b, i: (b, i, 0)), seq_spec,
                  pl.BlockSpec((nq, N_HEADS * VT_ROWS, t), lambda b, i: (b, 0, 0)),
                  pl.BlockSpec((None, SLOT, SLOT_WIDTH), lambda b, i: (b, 0, 0)),
                  _const_spec((2, t, t))],
        out_specs=pl.BlockSpec((None, t, ATTN_WIDTH), lambda b, i: (b, i, 0)),
        out_shape=jax.ShapeDtypeStruct((bsz, seq, ATTN_WIDTH), BF16),
        scratch_shapes=[pltpu.VMEM((N_HEADS, VT_ROWS, t), F32), pltpu.VMEM((N_HEADS, 1, t), F32),
                        pltpu.VMEM((t, SLOT_WIDTH), BF16),
                        pltpu.VMEM((N_HEADS, 2 * t, t), F32)],
        compiler_params=_cparams(("arbitrary", "arbitrary")),
        name="moba_attention",
    )(rel_bias, r3(q), r3(k), vt, km, jnp.asarray(_bucket_tiles(t)))
    return out.reshape(n, ATTN_WIDTH)


def _dsa_kernel(rb_ref, q_ref, k_ref, vt_ref, qi_ref, ki_ref, wit_ref, idx_ref, tri_ref, o_ref,
                acc_ref, m_ref, isc_ref, stat_ref, bc_ref, bias_ref, *, n_sel):
    i = pl.program_id(1)
    t = ATTN_TILE
    key = lax.broadcasted_iota(jnp.int32, (t, t), 0)
    qry = lax.broadcasted_iota(jnp.int32, (t, t), 1)
    lane128 = lax.broadcasted_iota(jnp.int32, (1, 128), 1)

    @pl.when((pl.program_id(0) == 0) & (i == 0))
    def _():
        _build_bias(idx_ref, rb_ref, bias_ref)

    _attn_init(m_ref, acc_ref)

    def index_tiles(kt, n, diag):
        rows = pl.ds(pl.multiple_of(kt * t, t), n * t)
        ki2 = ki_ref[rows, :]
        acc = jnp.zeros((n * t, t), F32)
        for pr in range(IDX_HEADS // 2):
            q2 = qi_ref[:, pr * 128:(pr + 1) * 128]
            for half in range(2):
                h = 2 * pr + half
                hm = (lane128 >= half * IDX_DIM) & (lane128 < (half + 1) * IDX_DIM)
                s = _dot_nt(ki2, jnp.where(hm, q2, jnp.zeros((), BF16)))
                acc = acc + jnp.maximum(s, 0.0) * wit_ref[h:h + 1, :]
        if diag:
            acc = jnp.where(key <= qry, acc, -jnp.inf)
        for a in range(n):
            isc_ref[kt + a] = acc[a * t:(a + 1) * t]
        fold = lambda x, op: op(x.reshape(n * t // 8, 8, t), axis=0)
        finite = jnp.where(acc == -jnp.inf, jnp.inf, acc) if diag else acc
        stat_ref[0] = jnp.minimum(stat_ref[0], fold(finite, jnp.min))
        stat_ref[1] = jnp.maximum(stat_ref[1], fold(acc, jnp.max))
        stat_ref[2] = stat_ref[2] + fold(jnp.where(acc >= 0.0, 1.0, 0.0), jnp.sum)
        stat_ref[3] = stat_ref[3] + fold(jnp.where(acc >= F32_TINY, 1.0, 0.0), jnp.sum)

    stat_ref[0] = jnp.full((8, t), jnp.inf, F32)
    stat_ref[1] = jnp.full((8, t), -jnp.inf, F32)
    stat_ref[2:4] = jnp.zeros((2, 8, t), F32)

    def index_pair(j, carry):
        index_tiles(2 * j, 2, False)
        return carry

    lax.fori_loop(0, i // 2, index_pair, 0)

    @pl.when(i % 2 == 1)
    def _():
        index_tiles(i - 1, 1, False)

    index_tiles(i, 1, True)

    def fold8(x, op):
        return op(x.reshape(t // 8, 8, t), axis=0)

    lo0 = jnp.min(stat_ref[0], axis=0, keepdims=True)
    mx = jnp.max(stat_ref[1], axis=0, keepdims=True)
    hi0 = mx + jnp.abs(mx) * 1e-3 + 1e-30
    n_valid = (i * t + 1 + lax.broadcasted_iota(jnp.int32, (1, t), 1)).astype(F32)
    want = jnp.minimum(n_valid, float(n_sel))

    def count_ge(thr):
        def one(kt, c):
            return c + fold8(jnp.where(isc_ref[kt] >= thr, 1.0, 0.0), jnp.sum)

        def pair(j, c):
            return one(2 * j + 1, one(2 * j, c))

        c = lax.fori_loop(0, (i + 1) // 2, pair, jnp.zeros((8, t), F32))
        c = lax.cond((i + 1) % 2 == 1, lambda c: one(i, c), lambda c: c, c)
        return jnp.sum(c, axis=0, keepdims=True)

    c_nonneg = jnp.sum(stat_ref[2], axis=0, keepdims=True)
    c_pos = jnp.sum(stat_ref[3], axis=0, keepdims=True)
    above = want <= c_pos
    zero_tie = (want > c_pos) & (want <= c_nonneg)
    lo1 = jnp.where(above, F32_TINY, jnp.where(zero_tie, 0.0, lo0))
    hi1 = jnp.where(above, hi0, jnp.where(zero_tie, F32_TINY, 0.0))
    c_lo1 = jnp.where(above, c_pos, jnp.where(zero_tie, c_nonneg, n_valid))
    c_hi1 = jnp.where(above, 0.0, jnp.where(zero_tie, c_pos, c_nonneg))

    def bisect_step(_, carry):
        lo, hi, c_lo, c_hi = carry
        mid = 0.5 * (lo + hi)
        c_mid = count_ge(mid)
        up = c_mid >= want
        return (jnp.where(up, mid, lo), jnp.where(up, hi, mid),
                jnp.where(up, c_mid, c_lo), jnp.where(up, c_hi, c_mid))

    def bisect_cond(carry):
        it, _, _, c_lo, _ = carry
        return (it < BISECT_ITERS) & (jnp.max(jnp.where(zero_tie, 0.0, c_lo - want)) > 0.0)

    def bisect_body(carry):
        return (carry[0] + BISECT_CHECK_EVERY,) + lax.fori_loop(0, BISECT_CHECK_EVERY, bisect_step, carry[1:])

    _, lo, hi, c_lo, c_hi = lax.while_loop(bisect_cond, bisect_body, (0, lo1, hi1, c_lo1, c_hi1))
    need = want - c_hi
    banded = jnp.max(c_lo - want) > 0.0
    bc_ref[...] = jnp.zeros(bc_ref.shape, F32)

    def select_mask(kt):
        x = isc_ref[kt]

        def plain():
            return jnp.where(x >= lo, 0.0, NEG)

        def with_band():
            band = jnp.where((x >= lo) & (x < hi), 1.0, 0.0)
            before = bc_ref[...] + _dot(tri_ref[...], band.astype(BF16))
            bc_ref[...] = bc_ref[...] + jnp.sum(band, axis=0, keepdims=True)
            return jnp.where((x >= hi) | ((band > 0.0) & (before < need)), 0.0, NEG)

        return lax.cond(banded, with_band, plain)

    def tiles(kt, kinds):
        masks = [select_mask(kt + a) for a in range(len(kinds))]
        mask_add = masks[0] if len(masks) == 1 else jnp.concatenate(masks, axis=0)
        _attend_tiles(kt, kinds, lambda h: q_ref[:, _slot(h)], k_ref, vt_ref, bias_ref, mask_add,
                      m_ref, acc_ref)

    _for_causal_tiles(i, tiles)
    _attn_finish(o_ref, acc_ref)


def _dsa(q, k, vt, qi, ki2, wit, rel_bias, *, bsz):
    n = q.shape[0]
    seq = n // bsz
    t = ATTN_TILE
    assert seq % t == 0
    nq = seq // t
    n_sel = min(DSA_TOPK_MAX, seq // 4)
    tri = (jnp.arange(t)[None, :] < jnp.arange(t)[:, None]).astype(BF16)
    r3 = lambda a: a.reshape(bsz, seq, a.shape[-1])
    tile_spec = lambda width: pl.BlockSpec((None, t, width), lambda b, i: (b, i, 0))
    seq_spec = lambda width: pl.BlockSpec((None, seq, width), lambda b, i: (b, 0, 0))
    out = pl.pallas_call(
        functools.partial(_dsa_kernel, n_sel=n_sel),
        grid=(bsz, nq),
        in_specs=[_smem_spec(), tile_spec(SLOT_WIDTH), seq_spec(SLOT_WIDTH),
                  pl.BlockSpec((nq, N_HEADS * VT_ROWS, t), lambda b, i: (b, 0, 0)),
                  tile_spec(ATTN_WIDTH), seq_spec(128),
                  pl.BlockSpec((IDX_HEADS, t), lambda b, i: (0, b * nq + i)),
                  _const_spec((2, t, t)), _const_spec((t, t))],
        out_specs=tile_spec(ATTN_WIDTH),
        out_shape=jax.ShapeDtypeStruct((bsz, seq, ATTN_WIDTH), BF16),
        scratch_shapes=[pltpu.VMEM((N_HEADS, VT_ROWS, t), F32), pltpu.VMEM((N_HEADS, 1, t), F32),
                        pltpu.VMEM((nq, t, t), F32), pltpu.VMEM((4, 8, t), F32), pltpu.VMEM((1, t), F32),
                        pltpu.VMEM((N_HEADS, 2 * t, t), F32)],
        compiler_params=_cparams(("arbitrary", "arbitrary")),
        name="dsa_attention",
    )(rel_bias, r3(q), r3(k), vt, r3(qi), r3(ki2), wit, jnp.asarray(_bucket_tiles(t)), tri)
    return out.reshape(n, ATTN_WIDTH)


def kernel(x, rel_bias, ffn1_norm, ffn1_w_gate, ffn1_w_up, ffn1_w_down, mix_norm, ffn2_norm, ffn2_w_gate, ffn2_w_up, ffn2_w_down, ev_w_in, ev_conv_w, ev_conv_b, ev_ra_w, ev_ra_b, ev_ix_w, ev_ix_b, ev_lambda, ev_q_norm, ev_k_norm, ev_w_out, od_w_in, od_dw_w, od_dw_b, od_ln_g, od_ln_b, od_q_norm, od_k_norm, od_w_out):
    bsz, seq, d = x.shape
    depth = ffn1_norm.shape[0]
    h = x.reshape(bsz * seq, d)
    for i in range(depth):
        h = _ffn(h, ffn1_norm, ffn1_w_gate, ffn1_w_up, ffn1_w_down, i)
        j = i // 2
        if i % 2 == 0:
            gate, xr, q, k, vt, kmean = _mixin_even(h, mix_norm[i], ev_w_in[j], ev_q_norm[j], ev_k_norm[j],
                                                    n_blocks=seq // MOBA_BLOCK)
            ya = _lru(gate, xr, ev_conv_w[j], ev_conv_b[j], ev_ra_w[j], ev_ra_b[j],
                      ev_ix_w[j], ev_ix_b[j], ev_lambda[j], bsz=bsz)
            yb = _moba(q, k, vt, kmean, rel_bias, bsz=bsz)
            mix = (ya, yb, ev_w_out[j])
        else:
            yc, q, k, vt, qi, ki2, wit = _mixin_odd(h, mix_norm[i], od_w_in[j], od_q_norm[j], od_k_norm[j],
                                                    od_dw_w[j], od_dw_b[j], od_ln_g[j], od_ln_b[j], bsz=bsz)
            yd = _dsa(q, k, vt, qi, ki2, wit, rel_bias, bsz=bsz)
            mix = (yc, yd, od_w_out[j])
        h = _ffn(h, ffn2_norm, ffn2_w_gate, ffn2_w_up, ffn2_w_down, i, mix)
    return h.reshape(bsz, seq, d)
```

```python
import functools
import math

import numpy as np
import jax
import jax.numpy as jnp
from jax import lax
from jax.experimental import pallas as pl
from jax.experimental.pallas import tpu as pltpu

F32 = jnp.float32
BF16 = jnp.bfloat16

N_HEADS = 8
HEAD_DIM = 64
ATTN_WIDTH = N_HEADS * HEAD_DIM
LRU_C = 8.0
MOBA_BLOCK = 256
MOBA_TOPK = 3
IDX_HEADS = 8
IDX_DIM = 64
DSA_TOPK_MAX = 256
REL_BUCKETS = 32
REL_MAX_EXACT = REL_BUCKETS // 2
REL_MAX_DIST = 128
EPS = 1e-6
NEG = -1e30
M_INIT = -1e29
ATTN_TILE = 256
SUBLANES = 8
CONV_ROWS = 64
SLOT = 128
SLOT_WIDTH = N_HEADS * SLOT
FEAT0 = HEAD_DIM
SHIFT_LANE = SLOT - 1
LOGIT_RANGE_MAX = 120.0
VT_ROWS = 80
HEADS_AHEAD = 4
BISECT_ITERS = 32
BISECT_CHECK_EVERY = 4
F32_TINY = float(np.finfo(np.float32).tiny)
LOG2E = math.log2(math.e)
Q_SCALE = HEAD_DIM ** -0.5 * LOG2E
VMEM_LIMIT = 56 * 1024 * 1024
FFN_VMEM_LIMIT = 60 * 1024 * 1024


def _cparams(sem):
    return pltpu.CompilerParams(dimension_semantics=sem, vmem_limit_bytes=VMEM_LIMIT)


def _dot(a, b):
    return jnp.dot(a, b, preferred_element_type=F32)


def _dot_nt(a, b):
    return lax.dot_general(a, b, (((1,), (1,)), ((), ())), preferred_element_type=F32)


def _split_bf16(x):
    hi = x.astype(BF16)
    lo = (x - hi.astype(F32)).astype(BF16)
    return hi, lo


def _rms_rows(x, g):
    return x * lax.rsqrt(jnp.mean(x * x, axis=-1, keepdims=True) + EPS) * g


def _const_spec(shape):
    nd = len(shape)
    return pl.BlockSpec(shape, lambda *_: (0,) * nd, pipeline_mode=pl.Buffered(1))


def _slot(h):
    return slice(h * SLOT, (h + 1) * SLOT)


def _vt_rows(h):
    return slice(h * VT_ROWS, (h + 1) * VT_ROWS)


def _to_slots(x):
    low = lax.broadcasted_iota(jnp.int32, (1, SLOT), 1) < HEAD_DIM
    slots = []
    for p in range(N_HEADS // 2):
        chunk = x[:, p * SLOT:(p + 1) * SLOT]
        slots.append(jnp.where(low, chunk, 0.0))
        slots.append(jnp.where(low, pltpu.roll(chunk, HEAD_DIM, 1), 0.0))
    return slots


def _ffn_kernel(*refs, n_chunks, has_mix):
    if has_mix:
        x_ref, ya_ref, yb_ref, wo_ref, g_ref, wg_ref, wu_ref, wd_ref, o_ref, wg_s, wu_s, wd_s = refs
    else:
        x_ref, g_ref, wg_ref, wu_ref, wd_ref, o_ref, wg_s, wu_s, wd_s = refs
    step = pl.program_id(0)

    @pl.when(step < n_chunks)
    def _():
        wg_s[step] = wg_ref[...].astype(BF16)
        wu_s[step] = wu_ref[...].astype(BF16)
        wd_s[step] = wd_ref[...].astype(BF16)

    @pl.when(step >= n_chunks)
    def _():
        x = x_ref[...]
        if has_mix:
            w = ya_ref.shape[1]
            x = x + _dot(ya_ref[...], wo_ref[0:w, :]) + _dot(yb_ref[...], wo_ref[w:2 * w, :])
        hn = _rms_rows(x, g_ref[...]).astype(BF16)
        acc = jnp.zeros(x.shape, F32)
        for c in range(n_chunks):
            gt = _dot(hn, wg_s[c])
            ut = _dot(hn, wu_s[c])
            a = (gt * jax.nn.sigmoid(gt) * ut).astype(BF16)
            acc = acc + _dot(a, wd_s[c])
        o_ref[...] = x + 0.5 * acc


def _ffn(x, g, wg, wu, wd, layer, mix=None, *, tm=1024, ff_chunk=256):
    n, d = x.shape
    d_ff = wg.shape[2]
    assert n % tm == 0 and d_ff % ff_chunk == 0
    nc = d_ff // ff_chunk
    tile = lambda s: (jnp.maximum(s - nc, 0), 0)
    chunk_col = lambda s: (layer, 0, jnp.minimum(s, nc - 1))
    chunk_row = lambda s: (layer, jnp.minimum(s, nc - 1), 0)
    in_specs = [pl.BlockSpec((tm, d), tile)]
    args = [x]
    scratch = [pltpu.VMEM((nc, d, ff_chunk), BF16), pltpu.VMEM((nc, d, ff_chunk), BF16),
               pltpu.VMEM((nc, ff_chunk, d), BF16)]
    if mix is not None:
        ya, yb, w_out = mix
        w = ya.shape[1]
        assert w_out.shape == (2 * w, d)
        in_specs += [pl.BlockSpec((tm, w), tile), pl.BlockSpec((tm, w), tile), _const_spec((2 * w, d))]
        args += [ya, yb, w_out.astype(BF16)]
    in_specs += [_const_spec((1, d)), pl.BlockSpec((None, d, ff_chunk), chunk_col),
                 pl.BlockSpec((None, d, ff_chunk), chunk_col), pl.BlockSpec((None, ff_chunk, d), chunk_row)]
    args += [g[layer].reshape(1, d), wg, wu, wd]
    return pl.pallas_call(
        functools.partial(_ffn_kernel, n_chunks=nc, has_mix=mix is not None),
        grid=(nc + n // tm,),
        in_specs=in_specs,
        out_specs=pl.BlockSpec((tm, d), tile),
        out_shape=jax.ShapeDtypeStruct((n, d), F32),
        scratch_shapes=scratch,
        compiler_params=pltpu.CompilerParams(dimension_semantics=("arbitrary",),
                                             vmem_limit_bytes=FFN_VMEM_LIMIT),
        name="ffn_mix" if mix is not None else "ffn",
    )(*args)


def _slot_rms(xs, g):
    ms = jnp.sum(xs * xs, axis=-1, keepdims=True) * (1.0 / HEAD_DIM)
    return xs * lax.rsqrt(ms + EPS) * g


def _store_q_slots(z, gain_ref, q_o):
    lane = lax.broadcasted_iota(jnp.int32, (1, SLOT), 1)
    for h, zs in enumerate(_to_slots(z)):
        qs = _slot_rms(zs, gain_ref[...]) * Q_SCALE
        q_o[:, _slot(h)] = jnp.where(lane == SHIFT_LANE, 1.0, qs).astype(BF16)


def _store_k_slots(z, gain_ref, shift_ref, k_o, one_lanes):
    slots = [_slot_rms(zs, gain_ref[...]) for zs in _to_slots(z)]
    for h, ks in enumerate(slots):
        ks = ks if one_lanes is None else jnp.where(one_lanes, 1.0, ks)
        k_o[:, _slot(h)] = (ks + shift_ref[:, _slot(h)]).astype(BF16)
    return slots


def _store_vt_slots(zt, v_o):
    pad = VT_ROWS - HEAD_DIM
    tail = (lax.broadcasted_iota(jnp.int32, (pad, ATTN_TILE), 0) == 0).astype(BF16)
    for h in range(N_HEADS):
        for r in range(zt.shape[1] // ATTN_TILE):
            feat = zt[h * HEAD_DIM:(h + 1) * HEAD_DIM, r * ATTN_TILE:(r + 1) * ATTN_TILE]
            v_o[r, h * VT_ROWS:h * VT_ROWS + HEAD_DIM, :] = feat.astype(BF16)
            v_o[r, h * VT_ROWS + HEAD_DIM:(h + 1) * VT_ROWS, :] = tail


def _pipelined(stages):
    nxt = stages[0][0]()
    for j, (_, consume) in enumerate(stages):
        cur = nxt
        if j + 1 < len(stages):
            nxt = stages[j + 1][0]()
        consume(cur)


def _mixin_even_kernel(x_ref, g_ref, w_ref, wvt_ref, qn_ref, kn_ref, ks_ref,
                       gate_o, xr_o, q_o, k_o, v_o, km_o, *, n_blocks):
    tm = x_ref.shape[0]
    hn = _rms_rows(x_ref[...], g_ref[...]).astype(BF16)
    w = ATTN_WIDTH
    proj = lambda c: (lambda: _dot(hn, w_ref[:, c * w:(c + 1) * w]))
    blocks_per_tile = tm // MOBA_BLOCK
    row_block = lax.broadcasted_iota(jnp.int32, (tm, 1), 0) // MOBA_BLOCK
    block = (pl.program_id(0) * blocks_per_tile + row_block) % n_blocks
    block_lane = lax.broadcasted_iota(jnp.int32, (1, SLOT), 1) == FEAT0 + block

    def store_k(z):
        for h, ks in enumerate(_store_k_slots(z, kn_ref, ks_ref, k_o, block_lane)):
            for r in range(blocks_per_tile):
                km_o[r, :, _slot(h)] = jnp.mean(ks[r * MOBA_BLOCK:(r + 1) * MOBA_BLOCK], axis=0, keepdims=True)

    def store(ref):
        def consume(z):
            ref[...] = z
        return consume

    _pipelined([(proj(2), lambda z: _store_q_slots(z, qn_ref, q_o)), (proj(3), store_k),
                (lambda: _dot_nt(wvt_ref[...], hn), lambda zt: _store_vt_slots(zt, v_o)),
                (proj(0), store(gate_o)), (proj(1), store(xr_o))])


def _vt_spec(tm):
    return pl.BlockSpec((tm // ATTN_TILE, N_HEADS * VT_ROWS, ATTN_TILE), lambda i: (i, 0, 0))


def _vt_shape(n):
    return jax.ShapeDtypeStruct((n // ATTN_TILE, N_HEADS * VT_ROWS, ATTN_TILE), BF16)


def _logit_bounds(q_norm, k_norm, rel_bias):
    qk = HEAD_DIM * jnp.max(jnp.abs(q_norm)) * jnp.max(jnp.abs(k_norm)) * Q_SCALE * 1.02
    rel = (rel_bias - rel_bias[REL_BUCKETS - 1]) * LOG2E
    bound = qk + jnp.maximum(jnp.max(rel, axis=0), 0.0) + 1.0
    lowest = -qk + jnp.minimum(jnp.min(rel, axis=0), 0.0)
    fixed_ok = jnp.all(bound - lowest <= LOGIT_RANGE_MAX).astype(jnp.int32).reshape(1)
    k_shift = jnp.zeros((N_HEADS, SLOT), F32).at[:, SHIFT_LANE].set(-bound).reshape(1, SLOT_WIDTH)
    return k_shift, fixed_ok


def _slot_gain(g):
    return jnp.pad(g.astype(F32), (0, SLOT - HEAD_DIM)).reshape(1, SLOT)


def _mixin_even(x, g, w_in, q_norm, k_norm, k_shift, *, n_blocks, tm=512):
    n, d = x.shape
    w = ATTN_WIDTH
    assert n % tm == 0 and tm % MOBA_BLOCK == 0 and tm % ATTN_TILE == 0 and w_in.shape[1] == 5 * w
    assert FEAT0 + n_blocks <= SHIFT_LANE
    row = lambda i: (i, 0)
    blk = tm // MOBA_BLOCK
    slot_shape = jax.ShapeDtypeStruct((n, SLOT_WIDTH), BF16)
    return pl.pallas_call(
        functools.partial(_mixin_even_kernel, n_blocks=n_blocks),
        grid=(n // tm,),
        in_specs=[pl.BlockSpec((tm, d), row), _const_spec((1, d)), _const_spec((d, 4 * w)),
                  _const_spec((w, d)), _const_spec((1, SLOT)), _const_spec((1, SLOT)),
                  _const_spec((1, SLOT_WIDTH))],
        out_specs=[pl.BlockSpec((tm, w), row)] * 2 + [pl.BlockSpec((tm, SLOT_WIDTH), row)] * 2
        + [_vt_spec(tm), pl.BlockSpec((blk, 1, SLOT_WIDTH), lambda i: (i, 0, 0))],
        out_shape=[jax.ShapeDtypeStruct((n, w), F32), jax.ShapeDtypeStruct((n, w), F32),
                   slot_shape, slot_shape, _vt_shape(n),
                   jax.ShapeDtypeStruct((n // MOBA_BLOCK, 1, SLOT_WIDTH), F32)],
        compiler_params=_cparams(("parallel",)),
        name="mixin_even",
    )(x, g.reshape(1, d), w_in[:, :4 * w].astype(BF16), w_in[:, 4 * w:].T.astype(BF16),
      _slot_gain(q_norm), _slot_gain(k_norm), k_shift)


def _conv_ln_silu(c, first, w_ref, b_ref, g_ref, beta_ref, o_ref, cbuf, sh_ref, *, halo):
    ts = c.shape[0]
    width = w_ref.shape[0]

    @pl.when(first)
    def _():
        cbuf[0:halo, :] = jnp.zeros((halo, cbuf.shape[1]), F32)

    cbuf[halo:halo + ts, :] = c
    base = halo - (width - 1)
    span = sh_ref.shape[1]
    for r in range(1, SUBLANES):
        sh_ref[r - 1] = cbuf[r:r + span, :]

    def chunk(c0):
        y = jnp.broadcast_to(b_ref[...], (CONV_ROWS, cbuf.shape[1]))
        for k in range(width):
            r = (base + k) % SUBLANES
            u0 = c0 + base + k - r
            win = cbuf[u0:u0 + CONV_ROWS, :] if r == 0 else sh_ref[r - 1, u0:u0 + CONV_ROWS, :]
            y = y + w_ref[k:k + 1, :] * win
        mu = jnp.mean(y, axis=-1, keepdims=True)
        yc = y - mu
        var = jnp.mean(yc * yc, axis=-1, keepdims=True)
        z = yc * lax.rsqrt(var + EPS) * g_ref[...] + beta_ref[...]
        o_ref[c0:c0 + CONV_ROWS, :] = (z * jax.nn.sigmoid(z)).astype(BF16)

    def close():
        cbuf[0:halo, :] = cbuf[ts:ts + halo, :]

    return [functools.partial(chunk, c0) for c0 in range(0, ts, CONV_ROWS)], close


def _mixin_odd_kernel(x_ref, g_ref, w_ref, wvt_ref, wki_ref, wwit_ref, qn_ref, kn_ref, ks_ref,
                      dw_ref, db_ref, lg_ref, lb_ref,
                      yc_o, q_o, k_o, v_o, qi_o, ki_o, wi_o, cbuf, sh_ref, *, tiles_per_seq, halo):
    hn = _rms_rows(x_ref[...], g_ref[...]).astype(BF16)
    w = ATTN_WIDTH
    proj = lambda c, n=1: _dot(hn, w_ref[:, c * w:(c + n) * w])
    first = pl.program_id(0) % tiles_per_seq == 0

    z_glu = proj(0, 2)
    z_q = proj(2)
    chunks, close = _conv_ln_silu(z_glu[:, 0:w] * jax.nn.sigmoid(z_glu[:, w:2 * w]), first,
                                  dw_ref, db_ref, lg_ref, lb_ref, yc_o, cbuf, sh_ref, halo=halo)
    per_stage = -(-len(chunks) // 4)
    run_chunks = lambda j: [f() for f in chunks[j * per_stage:(j + 1) * per_stage]]
    z_k = proj(3)
    _store_q_slots(z_q, qn_ref, q_o)
    run_chunks(0)
    z_vt = _dot_nt(wvt_ref[...], hn)
    _store_k_slots(z_k, kn_ref, ks_ref, k_o, None)
    run_chunks(1)
    z_qi = proj(4)
    _store_vt_slots(z_vt, v_o)
    run_chunks(2)
    z_ki = _dot(hn, wki_ref[...])
    z_wit = _dot_nt(wwit_ref[...], hn)
    qi_o[...] = z_qi.astype(BF16)
    run_chunks(3)
    close()
    ki_o[...] = z_ki.astype(BF16)
    wi_o[...] = z_wit * (IDX_DIM ** -0.5 * IDX_HEADS ** -0.5)


def _mixin_odd(x, g, w_in, q_norm, k_norm, k_shift, dw_w, dw_b, ln_g, ln_b, *, bsz, tm=512, halo=32):
    n, d = x.shape
    w = ATTN_WIDTH
    width = dw_w.shape[0]
    assert n % tm == 0 and tm % ATTN_TILE == 0 and w_in.shape[1] == 6 * w + IDX_DIM + IDX_HEADS
    assert (n // bsz) % tm == 0 and width - 1 <= halo <= tm and halo % SUBLANES == 0 and tm % CONV_ROWS == 0
    w_main = jnp.concatenate([w_in[:, :4 * w], w_in[:, 5 * w:6 * w]], axis=1).astype(BF16)
    w_vt = w_in[:, 4 * w:5 * w].T.astype(BF16)
    w_ki = w_in[:, 6 * w:6 * w + IDX_DIM]
    w_ki2 = jnp.concatenate([w_ki, w_ki], axis=1).astype(BF16)
    w_wit = w_in[:, 6 * w + IDX_DIM:].T.astype(BF16)
    row = lambda i: (i, 0)
    vec = lambda v: v.reshape(1, w).astype(F32)
    slot_shape = jax.ShapeDtypeStruct((n, SLOT_WIDTH), BF16)
    return pl.pallas_call(
        functools.partial(_mixin_odd_kernel, tiles_per_seq=(n // bsz) // tm, halo=halo),
        grid=(n // tm,),
        in_specs=[pl.BlockSpec((tm, d), row), _const_spec((1, d)), _const_spec((d, 5 * w)),
                  _const_spec((w, d)), _const_spec((d, 2 * IDX_DIM)), _const_spec((IDX_HEADS, d)),
                  _const_spec((1, SLOT)), _const_spec((1, SLOT)), _const_spec((1, SLOT_WIDTH)),
                  _const_spec((width, w)), _const_spec((1, w)), _const_spec((1, w)), _const_spec((1, w))],
        out_specs=[pl.BlockSpec((tm, w), row)] + [pl.BlockSpec((tm, SLOT_WIDTH), row)] * 2
        + [_vt_spec(tm), pl.BlockSpec((tm, w), row), pl.BlockSpec((tm, 128), row),
           pl.BlockSpec((IDX_HEADS, tm), lambda i: (0, i))],
        out_shape=[jax.ShapeDtypeStruct((n, w), BF16), slot_shape, slot_shape, _vt_shape(n),
                   jax.ShapeDtypeStruct((n, w), BF16),
                   jax.ShapeDtypeStruct((n, 128), BF16), jax.ShapeDtypeStruct((IDX_HEADS, n), F32)],
        scratch_shapes=[pltpu.VMEM((tm + halo, w), F32),
                        pltpu.VMEM((SUBLANES - 1, tm + halo - SUBLANES, w), F32)],
        compiler_params=_cparams(("arbitrary",)),
        name="mixin_odd",
    )(x, g.reshape(1, d), w_main, w_vt, w_ki2, w_wit, _slot_gain(q_norm), _slot_gain(k_norm), k_shift,
      dw_w, vec(dw_b), vec(ln_g), vec(ln_b))


def _lru_kernel(gate_ref, xr_ref, cw_ref, cb_ref, wa_ref, ba_ref, wx_ref, bx_ref, sp_ref,
                o_ref, xbuf, a_s, u_s, h_s, hc, *, ts):
    j = pl.program_id(1)

    @pl.when(j == 0)
    def _():
        xbuf[0:8, :] = jnp.zeros((8, xbuf.shape[1]), F32)
        hc[...] = jnp.zeros(hc.shape, F32)

    xbuf[8:8 + ts, :] = xr_ref[...]
    xc = cb_ref[...] + cw_ref[0:1, :] * xbuf[5:5 + ts, :]
    for k in range(1, 4):
        xc = xc + cw_ref[k:k + 1, :] * xbuf[5 + k:5 + k + ts, :]
    xbuf[0:8, :] = xbuf[ts:ts + 8, :]

    xcb = xc.astype(BF16)
    r = jax.nn.sigmoid(_dot(xcb, wa_ref[...]) + ba_ref[...])
    ig = jax.nn.sigmoid(_dot(xcb, wx_ref[...]) + bx_ref[...])
    log_a = -LRU_C * r * sp_ref[...]
    a = jnp.exp(log_a)
    a_s[...] = a
    u_s[...] = jnp.sqrt(-jnp.tanh(log_a) * (a * a + 1.0)) * (ig * xc)

    row = lax.broadcasted_iota(jnp.int32, (8, a_s.shape[1]), 0)

    def body(g, carry):
        r0 = pl.multiple_of(g * 8, 8)
        a = a_s[pl.ds(r0, 8), :]
        u = u_s[pl.ds(r0, 8), :]
        for s in (1, 2, 4):
            ok = row >= s
            a_sh = jnp.where(ok, pltpu.roll(a, s, 0), 1.0)
            u_sh = jnp.where(ok, pltpu.roll(u, s, 0), 0.0)
            u = a * u_sh + u
            a = a * a_sh
        h = a * carry + u
        h_s[pl.ds(r0, 8), :] = h
        return h[7:8, :]

    hc[...] = lax.fori_loop(0, ts // 8, body, hc[...], unroll=4)
    o_ref[...] = (h_s[...] * jax.nn.gelu(gate_ref[...])).astype(BF16)


def _block_diag(wb):
    nb, bs, _ = wb.shape
    eye = jnp.eye(nb, dtype=wb.dtype)
    return (eye[:, None, :, None] * wb[:, :, None, :]).reshape(nb * bs, nb * bs)


def _lru(gate, xr, conv_w, conv_b, ra_w, ra_b, ix_w, ix_b, lam, *, bsz, ts=256):
    n, w = xr.shape
    seq = n // bsz
    assert seq % ts == 0
    nt = seq // ts
    row = lambda b, j: (b * nt + j, 0)
    vec = lambda v: v.reshape(1, w).astype(F32)
    return pl.pallas_call(
        functools.partial(_lru_kernel, ts=ts),
        grid=(bsz, nt),
        in_specs=[pl.BlockSpec((ts, w), row), pl.BlockSpec((ts, w), row),
                  _const_spec((conv_w.shape[0], w)), _const_spec((1, w)),
                  _const_spec((w, w)), _const_spec((1, w)), _const_spec((w, w)), _const_spec((1, w)),
                  _const_spec((1, w))],
        out_specs=pl.BlockSpec((ts, w), row),
        out_shape=jax.ShapeDtypeStruct((n, w), BF16),
        scratch_shapes=[pltpu.VMEM((ts + 8, w), F32), pltpu.VMEM((ts, w), F32),
                        pltpu.VMEM((ts, w), F32), pltpu.VMEM((ts, w), F32), pltpu.VMEM((1, w), F32)],
        compiler_params=_cparams(("arbitrary", "arbitrary")),
        name="rg_lru",
    )(gate, xr, conv_w, vec(conv_b), _block_diag(ra_w).astype(BF16), vec(ra_b),
      _block_diag(ix_w).astype(BF16), vec(ix_b), vec(jax.nn.softplus(-lam)))


def _bucket_tiles(t):
    assert t > REL_MAX_DIST
    n = np.arange(2 * t)
    nf = np.maximum(n, 1).astype(np.float32)
    large = REL_MAX_EXACT + (np.log(nf / np.float32(REL_MAX_EXACT))
                             / np.float32(math.log(REL_MAX_DIST / REL_MAX_EXACT))
                             * np.float32(REL_BUCKETS - REL_MAX_EXACT)).astype(np.int32)
    bucket = np.where(n < REL_MAX_EXACT, n, np.minimum(large, REL_BUCKETS - 1)).astype(np.int32)
    qry = np.arange(t)[None, :]
    key = np.arange(t)[:, None]
    return np.stack([bucket[np.maximum(qry - key, 0)], bucket[t + qry - key]])


def _build_bias(idx_ref, rb_ref, bias_ref):
    t = idx_ref.shape[1]
    causal = (lax.broadcasted_iota(jnp.int32, (t, t), 0) <= lax.broadcasted_iota(jnp.int32, (t, t), 1))
    for h in range(N_HEADS):
        far = rb_ref[REL_BUCKETS - 1, h]
        for which in (0, 1):
            idx = idx_ref[which]
            acc = jnp.zeros((t, t), F32)
            for b in range(REL_BUCKETS - 1):
                acc = jnp.where(idx == b, (rb_ref[b, h] - far) * LOG2E, acc)
            if which == 0:
                bias_ref[h, t:2 * t, :] = jnp.where(causal, acc, NEG)
            else:
                bias_ref[h, 0:t, :] = acc


def _attn_init(m_ref, acc_ref):
    m_ref[...] = jnp.full(m_ref.shape, M_INIT, F32)
    acc_ref[...] = jnp.zeros(acc_ref.shape, F32)


def _softmax_step(s, h, vt_h, m_ref, acc_ref):
    m_old = m_ref[h]
    m_new = jnp.maximum(m_old, jnp.max(s, axis=0, keepdims=True))
    p = jnp.exp2(s - m_new)
    m_ref[h] = m_new
    acc_ref[h] = jnp.exp2(m_old - m_new) * acc_ref[h] + _dot(vt_h, p.astype(BF16))


def _fixed_step(s, h, vt_h, acc_ref):
    acc_ref[h] = acc_ref[h] + _dot(vt_h, jnp.exp2(s).astype(BF16))


def _heads_pipelined(logits, vt_slot, m_ref, acc_ref, online):
    ahead = [logits(h) for h in range(HEADS_AHEAD)]
    for h in range(N_HEADS):
        if h + HEADS_AHEAD < N_HEADS:
            ahead.append(logits(h + HEADS_AHEAD))
        if online:
            _softmax_step(ahead[h], h, vt_slot(h), m_ref, acc_ref)
        else:
            _fixed_step(ahead[h], h, vt_slot(h), acc_ref)


FAR_PAIR = ("far", "far")
FAR_ONE = ("far",)
NEAR_DIAG = ("near", "diag")
DIAG_ONE = ("diag",)


def _attend_tiles(kt, kinds, q_slot, k_ref, vt_ref, bias_ref, mask_add, m_ref, acc_ref, online):
    t = ATTN_TILE
    n = len(kinds)
    rows = pl.ds(pl.multiple_of(kt * t, t), n * t)

    def logits(h):
        s = _dot_nt(k_ref[rows, _slot(h)], q_slot(h))
        if kinds == NEAR_DIAG:
            s = s + bias_ref[h]
        elif kinds == DIAG_ONE:
            s = s + bias_ref[h, t:2 * t, :]
        if mask_add is not None:
            s = s + mask_add
        return s

    def vt_slot(h):
        parts = [vt_ref[kt + a, _vt_rows(h), :] for a in range(n)]
        return parts[0] if n == 1 else jnp.concatenate(parts, axis=1)

    _heads_pipelined(logits, vt_slot, m_ref, acc_ref, online)


def _for_causal_tiles(i, tiles):
    n_far = jnp.maximum(i - 1, 0)

    def pair(j, carry):
        tiles(2 * j, FAR_PAIR)
        return carry

    lax.fori_loop(0, n_far // 2, pair, 0)

    @pl.when(n_far % 2 == 1)
    def _():
        tiles(n_far - 1, FAR_ONE)

    @pl.when(i >= 1)
    def _():
        tiles(i - 1, NEAR_DIAG)

    @pl.when(i == 0)
    def _():
        tiles(i, DIAG_ONE)


def _attend_either(ok_ref, attend):
    @pl.when(ok_ref[0] != 0)
    def _():
        attend(False)

    @pl.when(ok_ref[0] == 0)
    def _():
        attend(True)


def _attn_finish(o_ref, acc_ref):
    parts = []
    for h in range(N_HEADS):
        acc = acc_ref[h]
        parts.append(acc[0:HEAD_DIM] * (1.0 / acc[FEAT0:FEAT0 + 1]))
    o_ref[...] = jnp.concatenate(parts, axis=0).T.astype(BF16)


def _smem_spec():
    return pl.BlockSpec(memory_space=pltpu.SMEM)


def _moba_kernel(rb_ref, ok_ref, q_ref, k_ref, vt_ref, km_ref, idx_ref, o_ref,
                 acc_ref, m_ref, qs_ref, bias_ref, *, n_blocks):
    i = pl.program_id(1)
    t = ATTN_TILE
    nbp = -(-n_blocks // 8) * 8

    @pl.when((pl.program_id(0) == 0) & (i == 0))
    def _():
        _build_bias(idx_ref, rb_ref, bias_ref)

    _attn_init(m_ref, acc_ref)

    blk = lax.broadcasted_iota(jnp.int32, (nbp, t), 0)
    past = blk < i
    for h in range(N_HEADS):
        qh = q_ref[:, _slot(h)]
        km_hi, km_lo = _split_bf16(km_ref[:, _slot(h)])
        gate_t = _dot_nt(km_hi, qh) + _dot_nt(km_lo, qh)
        g = jnp.where(past, gate_t[FEAT0:FEAT0 + nbp, :], NEG)
        rank = jnp.zeros((nbp, t), F32)
        for j in range(n_blocks):
            gj = g[j:j + 1, :]
            beats = (gj > g) | ((gj == g) & (blk > j))
            rank = rank + jnp.where(beats, 1.0, 0.0)
        flag = jnp.where(past & (rank >= MOBA_TOPK), NEG, 0.0)
        flag_t = jnp.concatenate([jnp.zeros((FEAT0, t), F32), flag,
                                  jnp.zeros((SLOT - FEAT0 - nbp, t), F32)], axis=0)
        qs_ref[:, _slot(h)] = (qh.astype(F32) + flag_t.T).astype(BF16)

    def attend(online):
        def tiles(kt, kinds):
            _attend_tiles(kt, kinds, lambda h: qs_ref[:, _slot(h)], k_ref, vt_ref, bias_ref, None,
                          m_ref, acc_ref, online)
        _for_causal_tiles(i, tiles)

    _attend_either(ok_ref, attend)
    _attn_finish(o_ref, acc_ref)


def _moba(q, k, vt, kmean, rel_bias, fixed_ok, *, bsz):
    n = q.shape[0]
    seq = n // bsz
    t = ATTN_TILE
    assert seq % t == 0 and t == MOBA_BLOCK
    nq = seq // t
    assert FEAT0 + nq <= SHIFT_LANE
    km = jnp.pad(kmean.reshape(bsz, nq, SLOT_WIDTH), ((0, 0), (FEAT0, SLOT - FEAT0 - nq), (0, 0)))
    r3 = lambda a: a.reshape(bsz, seq, SLOT_WIDTH)
    seq_spec = pl.BlockSpec((None, seq, SLOT_WIDTH), lambda b, i: (b, 0, 0))
    out = pl.pallas_call(
        functools.partial(_moba_kernel, n_blocks=nq),
        grid=(bsz, nq),
        in_specs=[_smem_spec(), _smem_spec(),
                  pl.BlockSpec((None, t, SLOT_WIDTH), lambda b, i: (b, i, 0)), seq_spec,
                  pl.BlockSpec((nq, N_HEADS * VT_ROWS, t), lambda b, i: (b, 0, 0)),
                  pl.BlockSpec((None, SLOT, SLOT_WIDTH), lambda b, i: (b, 0, 0)),
                  _const_spec((2, t, t))],
        out_specs=pl.BlockSpec((None, t, ATTN_WIDTH), lambda b, i: (b, i, 0)),
        out_shape=jax.ShapeDtypeStruct((bsz, seq, ATTN_WIDTH), BF16),
        scratch_shapes=[pltpu.VMEM((N_HEADS, VT_ROWS, t), F32), pltpu.VMEM((N_HEADS, 1, t), F32),
                        pltpu.VMEM((t, SLOT_WIDTH), BF16),
                        pltpu.VMEM((N_HEADS, 2 * t, t), F32)],
        compiler_params=_cparams(("arbitrary", "arbitrary")),
        name="moba_attention",
    )(rel_bias, fixed_ok, r3(q), r3(k), vt, km, jnp.asarray(_bucket_tiles(t)))
    return out.reshape(n, ATTN_WIDTH)


def _dsa_kernel(rb_ref, ok_ref, q_ref, k_ref, vt_ref, qi_ref, ki_ref, wit_ref, idx_ref, tri_ref, o_ref,
                acc_ref, m_ref, isc_ref, bc_ref, bias_ref, *, n_sel):
    i = pl.program_id(1)
    t = ATTN_TILE
    key = lax.broadcasted_iota(jnp.int32, (t, t), 0)
    qry = lax.broadcasted_iota(jnp.int32, (t, t), 1)
    lane128 = lax.broadcasted_iota(jnp.int32, (1, 128), 1)

    @pl.when((pl.program_id(0) == 0) & (i == 0))
    def _():
        _build_bias(idx_ref, rb_ref, bias_ref)

    _attn_init(m_ref, acc_ref)

    def index_tiles(kt, n, diag):
        rows = pl.ds(pl.multiple_of(kt * t, t), n * t)
        ki2 = ki_ref[rows, :]
        acc = jnp.zeros((n * t, t), F32)
        for pr in range(IDX_HEADS // 2):
            q2 = qi_ref[:, pr * 128:(pr + 1) * 128]
            for half in range(2):
                h = 2 * pr + half
                hm = (lane128 >= half * IDX_DIM) & (lane128 < (half + 1) * IDX_DIM)
                s = _dot_nt(ki2, jnp.where(hm, q2, jnp.zeros((), BF16)))
                acc = acc + jnp.maximum(s, 0.0) * wit_ref[h:h + 1, :]
        if diag:
            acc = jnp.where(key <= qry, acc, -jnp.inf)
        for a in range(n):
            isc_ref[kt + a] = acc[a * t:(a + 1) * t]

    def index_pair(j, carry):
        index_tiles(2 * j, 2, False)
        return carry

    lax.fori_loop(0, i // 2, index_pair, 0)

    @pl.when(i % 2 == 1)
    def _():
        index_tiles(i - 1, 1, False)

    index_tiles(i, 1, True)

    def fold8(x, op):
        return op(x.reshape(t // 8, 8, t), axis=0)

    def minmax_body(kt, carry):
        mn, mx = carry
        x = isc_ref[kt]
        mx = jnp.maximum(mx, fold8(x, jnp.max))
        mn = jnp.minimum(mn, fold8(jnp.where(x == -jnp.inf, jnp.inf, x), jnp.min))
        return mn, mx

    mn, mx = lax.fori_loop(0, i + 1, minmax_body,
                           (jnp.full((8, t), jnp.inf, F32), jnp.full((8, t), -jnp.inf, F32)))
    lo0 = jnp.min(mn, axis=0, keepdims=True)
    mx = jnp.max(mx, axis=0, keepdims=True)
    hi0 = mx + jnp.abs(mx) * 1e-3 + 1e-30
    n_valid = (i * t + 1 + lax.broadcasted_iota(jnp.int32, (1, t), 1)).astype(F32)
    want = jnp.minimum(n_valid, float(n_sel))

    def count_ge(thr):
        def one(kt, c):
            return c + fold8(jnp.where(isc_ref[kt] >= thr, 1.0, 0.0), jnp.sum)

        def pair(j, c):
            return one(2 * j + 1, one(2 * j, c))

        c = lax.fori_loop(0, (i + 1) // 2, pair, jnp.zeros((8, t), F32))
        c = lax.cond((i + 1) % 2 == 1, lambda c: one(i, c), lambda c: c, c)
        return jnp.sum(c, axis=0, keepdims=True)

    c_nonneg = count_ge(0.0)
    c_pos = count_ge(F32_TINY)
    above = want <= c_pos
    zero_tie = (want > c_pos) & (want <= c_nonneg)
    lo1 = jnp.where(above, F32_TINY, jnp.where(zero_tie, 0.0, lo0))
    hi1 = jnp.where(above, hi0, jnp.where(zero_tie, F32_TINY, 0.0))
    c_lo1 = jnp.where(above, c_pos, jnp.where(zero_tie, c_nonneg, n_valid))
    c_hi1 = jnp.where(above, 0.0, jnp.where(zero_tie, c_pos, c_nonneg))

    def bisect_step(_, carry):
        lo, hi, c_lo, c_hi = carry
        mid = 0.5 * (lo + hi)
        c_mid = count_ge(mid)
        up = c_mid >= want
        return (jnp.where(up, mid, lo), jnp.where(up, hi, mid),
                jnp.where(up, c_mid, c_lo), jnp.where(up, c_hi, c_mid))

    def bisect_cond(carry):
        it, _, _, c_lo, _ = carry
        return (it < BISECT_ITERS) & (jnp.max(jnp.where(zero_tie, 0.0, c_lo - want)) > 0.0)

    def bisect_body(carry):
        return (carry[0] + BISECT_CHECK_EVERY,) + lax.fori_loop(0, BISECT_CHECK_EVERY, bisect_step, carry[1:])

    _, lo, hi, c_lo, c_hi = lax.while_loop(bisect_cond, bisect_body, (0, lo1, hi1, c_lo1, c_hi1))
    need = want - c_hi
    banded = jnp.max(c_lo - want) > 0.0
    bc_ref[...] = jnp.zeros(bc_ref.shape, F32)

    def select_mask(kt):
        x = isc_ref[kt]

        def plain():
            return jnp.where(x >= lo, 0.0, NEG)

        def with_band():
            band = jnp.where((x >= lo) & (x < hi), 1.0, 0.0)
            before = bc_ref[...] + _dot(tri_ref[...], band.astype(BF16))
            bc_ref[...] = bc_ref[...] + jnp.sum(band, axis=0, keepdims=True)
            return jnp.where((x >= hi) | ((band > 0.0) & (before < need)), 0.0, NEG)

        return lax.cond(banded, with_band, plain)

    def attend(online):
        def tiles(kt, kinds):
            masks = [select_mask(kt + a) for a in range(len(kinds))]
            mask_add = masks[0] if len(masks) == 1 else jnp.concatenate(masks, axis=0)
            _attend_tiles(kt, kinds, lambda h: q_ref[:, _slot(h)], k_ref, vt_ref, bias_ref, mask_add,
                          m_ref, acc_ref, online)
        _for_causal_tiles(i, tiles)

    _attend_either(ok_ref, attend)
    _attn_finish(o_ref, acc_ref)


def _dsa(q, k, vt, qi, ki2, wit, rel_bias, fixed_ok, *, bsz):
    n = q.shape[0]
    seq = n // bsz
    t = ATTN_TILE
    assert seq % t == 0
    nq = seq // t
    n_sel = min(DSA_TOPK_MAX, seq // 4)
    tri = (jnp.arange(t)[None, :] < jnp.arange(t)[:, None]).astype(BF16)
    r3 = lambda a: a.reshape(bsz, seq, a.shape[-1])
    tile_spec = lambda width: pl.BlockSpec((None, t, width), lambda b, i: (b, i, 0))
    seq_spec = lambda width: pl.BlockSpec((None, seq, width), lambda b, i: (b, 0, 0))
    out = pl.pallas_call(
        functools.partial(_dsa_kernel, n_sel=n_sel),
        grid=(bsz, nq),
        in_specs=[_smem_spec(), _smem_spec(), tile_spec(SLOT_WIDTH), seq_spec(SLOT_WIDTH),
                  pl.BlockSpec((nq, N_HEADS * VT_ROWS, t), lambda b, i: (b, 0, 0)),
                  tile_spec(ATTN_WIDTH), seq_spec(128),
                  pl.BlockSpec((IDX_HEADS, t), lambda b, i: (0, b * nq + i)),
                  _const_spec((2, t, t)), _const_spec((t, t))],
        out_specs=tile_spec(ATTN_WIDTH),
        out_shape=jax.ShapeDtypeStruct((bsz, seq, ATTN_WIDTH), BF16),
        scratch_shapes=[pltpu.VMEM((N_HEADS, VT_ROWS, t), F32), pltpu.VMEM((N_HEADS, 1, t), F32),
                        pltpu.VMEM((nq, t, t), F32), pltpu.VMEM((1, t), F32),
                        pltpu.VMEM((N_HEADS, 2 * t, t), F32)],
        compiler_params=_cparams(("arbitrary", "arbitrary")),
        name="dsa_attention",
    )(rel_bias, fixed_ok, r3(q), r3(k), vt, r3(qi), r3(ki2), wit, jnp.asarray(_bucket_tiles(t)), tri)
    return out.reshape(n, ATTN_WIDTH)


def kernel(x, rel_bias, ffn1_norm, ffn1_w_gate, ffn1_w_up, ffn1_w_down, mix_norm, ffn2_norm, ffn2_w_gate, ffn2_w_up, ffn2_w_down, ev_w_in, ev_conv_w, ev_conv_b, ev_ra_w, ev_ra_b, ev_ix_w, ev_ix_b, ev_lambda, ev_q_norm, ev_k_norm, ev_w_out, od_w_in, od_dw_w, od_dw_b, od_ln_g, od_ln_b, od_q_norm, od_k_norm, od_w_out):
    bsz, seq, d = x.shape
    depth = ffn1_norm.shape[0]
    h = x.reshape(bsz * seq, d)
    for i in range(depth):
        h = _ffn(h, ffn1_norm, ffn1_w_gate, ffn1_w_up, ffn1_w_down, i)
        j = i // 2
        if i % 2 == 0:
            k_shift, fixed_ok = _logit_bounds(ev_q_norm[j], ev_k_norm[j], rel_bias)
            gate, xr, q, k, vt, kmean = _mixin_even(h, mix_norm[i], ev_w_in[j], ev_q_norm[j], ev_k_norm[j],
                                                    k_shift, n_blocks=seq // MOBA_BLOCK)
            ya = _lru(gate, xr, ev_conv_w[j], ev_conv_b[j], ev_ra_w[j], ev_ra_b[j],
                      ev_ix_w[j], ev_ix_b[j], ev_lambda[j], bsz=bsz)
            yb = _moba(q, k, vt, kmean, rel_bias, fixed_ok, bsz=bsz)
            mix = (ya, yb, ev_w_out[j])
        else:
            k_shift, fixed_ok = _logit_bounds(od_q_norm[j], od_k_norm[j], rel_bias)
            yc, q, k, vt, qi, ki2, wit = _mixin_odd(h, mix_norm[i], od_w_in[j], od_q_norm[j], od_k_norm[j],
                                                    k_shift, od_dw_w[j], od_dw_b[j], od_ln_g[j], od_ln_b[j],
                                                    bsz=bsz)
            yd = _dsa(q, k, vt, qi, ki2, wit, rel_bias, fixed_ok, bsz=bsz)
            mix = (yc, yd, od_w_out[j])
        h = _ffn(h, ffn2_norm, ffn2_w_gate, ffn2_w_up, ffn2_w_down, i, mix)
    return h.reshape(bsz, seq, d)
```

```python
import functools
import math

import numpy as np
import jax
import jax.numpy as jnp
from jax import lax
from jax.experimental import pallas as pl
from jax.experimental.pallas import tpu as pltpu

F32 = jnp.float32
BF16 = jnp.bfloat16

N_HEADS = 8
HEAD_DIM = 64
ATTN_WIDTH = N_HEADS * HEAD_DIM
LRU_C = 8.0
MOBA_BLOCK = 256
MOBA_TOPK = 3
IDX_HEADS = 8
IDX_DIM = 64
DSA_TOPK_MAX = 256
REL_BUCKETS = 32
REL_MAX_EXACT = REL_BUCKETS // 2
REL_MAX_DIST = 128
EPS = 1e-6
NEG = -1e30
M_INIT = -1e29
ATTN_TILE = 256
SUBLANES = 8
CONV_ROWS = 64
SLOT = 128
SLOT_WIDTH = N_HEADS * SLOT
FEAT0 = HEAD_DIM
SHIFT_LANE = SLOT - 1
LOGIT_RANGE_MAX = 120.0
VT_ROWS = 80
HEADS_AHEAD = 8
BISECT_ITERS = 32
BISECT_CHECK_EVERY = 4
F32_TINY = float(np.finfo(np.float32).tiny)
LOG2E = math.log2(math.e)
Q_SCALE = HEAD_DIM ** -0.5 * LOG2E
VMEM_LIMIT = 56 * 1024 * 1024
FFN_VMEM_LIMIT = 60 * 1024 * 1024


def _cparams(sem):
    return pltpu.CompilerParams(dimension_semantics=sem, vmem_limit_bytes=VMEM_LIMIT)


def _dot(a, b):
    return jnp.dot(a, b, preferred_element_type=F32)


def _dot_nt(a, b):
    return lax.dot_general(a, b, (((1,), (1,)), ((), ())), preferred_element_type=F32)


def _split_bf16(x):
    hi = x.astype(BF16)
    lo = (x - hi.astype(F32)).astype(BF16)
    return hi, lo


def _rms_rows(x, g):
    return x * lax.rsqrt(jnp.mean(x * x, axis=-1, keepdims=True) + EPS) * g


def _const_spec(shape):
    nd = len(shape)
    return pl.BlockSpec(shape, lambda *_: (0,) * nd, pipeline_mode=pl.Buffered(1))


def _slot(h):
    return slice(h * SLOT, (h + 1) * SLOT)


def _vt_rows(h):
    return slice(h * VT_ROWS, (h + 1) * VT_ROWS)


def _to_slots(x):
    low = lax.broadcasted_iota(jnp.int32, (1, SLOT), 1) < HEAD_DIM
    slots = []
    for p in range(N_HEADS // 2):
        chunk = x[:, p * SLOT:(p + 1) * SLOT]
        slots.append(jnp.where(low, chunk, 0.0))
        slots.append(jnp.where(low, pltpu.roll(chunk, HEAD_DIM, 1), 0.0))
    return slots


def _ffn_kernel(*refs, n_chunks, has_mix):
    if has_mix:
        x_ref, ya_ref, yb_ref, wo_ref, g_ref, wg_ref, wu_ref, wd_ref, o_ref, wg_s, wu_s, wd_s = refs
    else:
        x_ref, g_ref, wg_ref, wu_ref, wd_ref, o_ref, wg_s, wu_s, wd_s = refs
    step = pl.program_id(0)

    @pl.when(step < n_chunks)
    def _():
        wg_s[step] = wg_ref[...].astype(BF16)
        wu_s[step] = wu_ref[...].astype(BF16)
        wd_s[step] = wd_ref[...].astype(BF16)

    @pl.when(step >= n_chunks)
    def _():
        x = x_ref[...]
        if has_mix:
            w = ya_ref.shape[1]
            x = x + _dot(ya_ref[...], wo_ref[0:w, :]) + _dot(yb_ref[...], wo_ref[w:2 * w, :])
        hn = _rms_rows(x, g_ref[...]).astype(BF16)
        acc = jnp.zeros(x.shape, F32)
        for c in range(n_chunks):
            gt = _dot(hn, wg_s[c])
            ut = _dot(hn, wu_s[c])
            a = (gt * jax.nn.sigmoid(gt) * ut).astype(BF16)
            acc = acc + _dot(a, wd_s[c])
        o_ref[...] = x + 0.5 * acc


def _ffn(x, g, wg, wu, wd, layer, mix=None, *, tm=1024, ff_chunk=256):
    n, d = x.shape
    d_ff = wg.shape[2]
    assert n % tm == 0 and d_ff % ff_chunk == 0
    nc = d_ff // ff_chunk
    tile = lambda s: (jnp.maximum(s - nc, 0), 0)
    chunk_col = lambda s: (layer, 0, jnp.minimum(s, nc - 1))
    chunk_row = lambda s: (layer, jnp.minimum(s, nc - 1), 0)
    in_specs = [pl.BlockSpec((tm, d), tile)]
    args = [x]
    scratch = [pltpu.VMEM((nc, d, ff_chunk), BF16), pltpu.VMEM((nc, d, ff_chunk), BF16),
               pltpu.VMEM((nc, ff_chunk, d), BF16)]
    if mix is not None:
        ya, yb, w_out = mix
        w = ya.shape[1]
        assert w_out.shape == (2 * w, d)
        in_specs += [pl.BlockSpec((tm, w), tile), pl.BlockSpec((tm, w), tile), _const_spec((2 * w, d))]
        args += [ya, yb, w_out.astype(BF16)]
    in_specs += [_const_spec((1, d)), pl.BlockSpec((None, d, ff_chunk), chunk_col),
                 pl.BlockSpec((None, d, ff_chunk), chunk_col), pl.BlockSpec((None, ff_chunk, d), chunk_row)]
    args += [g[layer].reshape(1, d), wg, wu, wd]
    return pl.pallas_call(
        functools.partial(_ffn_kernel, n_chunks=nc, has_mix=mix is not None),
        grid=(nc + n // tm,),
        in_specs=in_specs,
        out_specs=pl.BlockSpec((tm, d), tile),
        out_shape=jax.ShapeDtypeStruct((n, d), F32),
        scratch_shapes=scratch,
        compiler_params=pltpu.CompilerParams(dimension_semantics=("arbitrary",),
                                             vmem_limit_bytes=FFN_VMEM_LIMIT),
        name="ffn_mix" if mix is not None else "ffn",
    )(*args)


def _slot_rms(xs, g):
    ms = jnp.sum(xs * xs, axis=-1, keepdims=True) * (1.0 / HEAD_DIM)
    return xs * lax.rsqrt(ms + EPS) * g


def _store_q_slots(z, gain_ref, q_o):
    lane = lax.broadcasted_iota(jnp.int32, (1, SLOT), 1)
    for h, zs in enumerate(_to_slots(z)):
        qs = _slot_rms(zs, gain_ref[...]) * Q_SCALE
        q_o[:, _slot(h)] = jnp.where(lane == SHIFT_LANE, 1.0, qs).astype(BF16)


def _store_k_slots(z, gain_ref, shift_ref, k_o, one_lanes):
    slots = [_slot_rms(zs, gain_ref[...]) for zs in _to_slots(z)]
    for h, ks in enumerate(slots):
        ks = ks if one_lanes is None else jnp.where(one_lanes, 1.0, ks)
        k_o[:, _slot(h)] = (ks + shift_ref[:, _slot(h)]).astype(BF16)
    return slots


def _store_vt_slots(zt, v_o):
    pad = VT_ROWS - HEAD_DIM
    tail = (lax.broadcasted_iota(jnp.int32, (pad, ATTN_TILE), 0) == 0).astype(BF16)
    for h in range(N_HEADS):
        for r in range(zt.shape[1] // ATTN_TILE):
            feat = zt[h * HEAD_DIM:(h + 1) * HEAD_DIM, r * ATTN_TILE:(r + 1) * ATTN_TILE]
            v_o[r, h * VT_ROWS:h * VT_ROWS + HEAD_DIM, :] = feat.astype(BF16)
            v_o[r, h * VT_ROWS + HEAD_DIM:(h + 1) * VT_ROWS, :] = tail


def _pipelined(stages):
    nxt = stages[0][0]()
    for j, (_, consume) in enumerate(stages):
        cur = nxt
        if j + 1 < len(stages):
            nxt = stages[j + 1][0]()
        consume(cur)


def _mixin_even_kernel(x_ref, g_ref, w_ref, wvt_ref, qn_ref, kn_ref, ks_ref,
                       gate_o, xr_o, q_o, k_o, v_o, km_o, *, n_blocks):
    tm = x_ref.shape[0]
    hn = _rms_rows(x_ref[...], g_ref[...]).astype(BF16)
    w = ATTN_WIDTH
    proj = lambda c: (lambda: _dot(hn, w_ref[:, c * w:(c + 1) * w]))
    blocks_per_tile = tm // MOBA_BLOCK
    row_block = lax.broadcasted_iota(jnp.int32, (tm, 1), 0) // MOBA_BLOCK
    block = (pl.program_id(0) * blocks_per_tile + row_block) % n_blocks
    block_lane = lax.broadcasted_iota(jnp.int32, (1, SLOT), 1) == FEAT0 + block

    def store_k(z):
        for h, ks in enumerate(_store_k_slots(z, kn_ref, ks_ref, k_o, block_lane)):
            for r in range(blocks_per_tile):
                km_o[r, :, _slot(h)] = jnp.mean(ks[r * MOBA_BLOCK:(r + 1) * MOBA_BLOCK], axis=0, keepdims=True)

    def store(ref):
        def consume(z):
            ref[...] = z
        return consume

    _pipelined([(proj(2), lambda z: _store_q_slots(z, qn_ref, q_o)), (proj(3), store_k),
                (lambda: _dot_nt(wvt_ref[...], hn), lambda zt: _store_vt_slots(zt, v_o)),
                (proj(0), store(gate_o)), (proj(1), store(xr_o))])


def _vt_spec(tm):
    return pl.BlockSpec((tm // ATTN_TILE, N_HEADS * VT_ROWS, ATTN_TILE), lambda i: (i, 0, 0))


def _vt_shape(n):
    return jax.ShapeDtypeStruct((n // ATTN_TILE, N_HEADS * VT_ROWS, ATTN_TILE), BF16)


def _logit_bounds(q_norm, k_norm, rel_bias):
    qk = HEAD_DIM * jnp.max(jnp.abs(q_norm)) * jnp.max(jnp.abs(k_norm)) * Q_SCALE * 1.02
    rel = (rel_bias - rel_bias[REL_BUCKETS - 1]) * LOG2E
    bound = qk + jnp.maximum(jnp.max(rel, axis=0), 0.0) + 1.0
    lowest = -qk + jnp.minimum(jnp.min(rel, axis=0), 0.0)
    fixed_ok = jnp.all(bound - lowest <= LOGIT_RANGE_MAX).astype(jnp.int32).reshape(1)
    k_shift = jnp.zeros((N_HEADS, SLOT), F32).at[:, SHIFT_LANE].set(-bound).reshape(1, SLOT_WIDTH)
    return k_shift, fixed_ok


def _slot_gain(g):
    return jnp.pad(g.astype(F32), (0, SLOT - HEAD_DIM)).reshape(1, SLOT)


def _mixin_even(x, g, w_in, q_norm, k_norm, k_shift, *, n_blocks, tm=512):
    n, d = x.shape
    w = ATTN_WIDTH
    assert n % tm == 0 and tm % MOBA_BLOCK == 0 and tm % ATTN_TILE == 0 and w_in.shape[1] == 5 * w
    assert FEAT0 + n_blocks <= SHIFT_LANE
    row = lambda i: (i, 0)
    blk = tm // MOBA_BLOCK
    slot_shape = jax.ShapeDtypeStruct((n, SLOT_WIDTH), BF16)
    return pl.pallas_call(
        functools.partial(_mixin_even_kernel, n_blocks=n_blocks),
        grid=(n // tm,),
        in_specs=[pl.BlockSpec((tm, d), row), _const_spec((1, d)), _const_spec((d, 4 * w)),
                  _const_spec((w, d)), _const_spec((1, SLOT)), _const_spec((1, SLOT)),
                  _const_spec((1, SLOT_WIDTH))],
        out_specs=[pl.BlockSpec((tm, w), row)] * 2 + [pl.BlockSpec((tm, SLOT_WIDTH), row)] * 2
        + [_vt_spec(tm), pl.BlockSpec((blk, 1, SLOT_WIDTH), lambda i: (i, 0, 0))],
        out_shape=[jax.ShapeDtypeStruct((n, w), F32), jax.ShapeDtypeStruct((n, w), F32),
                   slot_shape, slot_shape, _vt_shape(n),
                   jax.ShapeDtypeStruct((n // MOBA_BLOCK, 1, SLOT_WIDTH), F32)],
        compiler_params=_cparams(("parallel",)),
        name="mixin_even",
    )(x, g.reshape(1, d), w_in[:, :4 * w].astype(BF16), w_in[:, 4 * w:].T.astype(BF16),
      _slot_gain(q_norm), _slot_gain(k_norm), k_shift)


def _conv_ln_silu(c, first, w_ref, b_ref, g_ref, beta_ref, o_ref, cbuf, sh_ref, *, halo):
    ts = c.shape[0]
    width = w_ref.shape[0]

    @pl.when(first)
    def _():
        cbuf[0:halo, :] = jnp.zeros((halo, cbuf.shape[1]), F32)

    cbuf[halo:halo + ts, :] = c
    base = halo - (width - 1)
    span = sh_ref.shape[1]
    for r in range(1, SUBLANES):
        sh_ref[r - 1] = cbuf[r:r + span, :]

    def chunk(c0):
        y = jnp.broadcast_to(b_ref[...], (CONV_ROWS, cbuf.shape[1]))
        for k in range(width):
            r = (base + k) % SUBLANES
            u0 = c0 + base + k - r
            win = cbuf[u0:u0 + CONV_ROWS, :] if r == 0 else sh_ref[r - 1, u0:u0 + CONV_ROWS, :]
            y = y + w_ref[k:k + 1, :] * win
        mu = jnp.mean(y, axis=-1, keepdims=True)
        yc = y - mu
        var = jnp.mean(yc * yc, axis=-1, keepdims=True)
        z = yc * lax.rsqrt(var + EPS) * g_ref[...] + beta_ref[...]
        o_ref[c0:c0 + CONV_ROWS, :] = (z * jax.nn.sigmoid(z)).astype(BF16)

    def close():
        cbuf[0:halo, :] = cbuf[ts:ts + halo, :]

    return [functools.partial(chunk, c0) for c0 in range(0, ts, CONV_ROWS)], close


def _mixin_odd_kernel(x_ref, g_ref, w_ref, wvt_ref, wki_ref, wwit_ref, qn_ref, kn_ref, ks_ref,
                      dw_ref, db_ref, lg_ref, lb_ref,
                      yc_o, q_o, k_o, v_o, qi_o, ki_o, wi_o, cbuf, sh_ref, *, tiles_per_seq, halo):
    hn = _rms_rows(x_ref[...], g_ref[...]).astype(BF16)
    w = ATTN_WIDTH
    proj = lambda c, n=1: _dot(hn, w_ref[:, c * w:(c + n) * w])
    first = pl.program_id(0) % tiles_per_seq == 0

    z_glu = proj(0, 2)
    z_q = proj(2)
    chunks, close = _conv_ln_silu(z_glu[:, 0:w] * jax.nn.sigmoid(z_glu[:, w:2 * w]), first,
                                  dw_ref, db_ref, lg_ref, lb_ref, yc_o, cbuf, sh_ref, halo=halo)
    per_stage = -(-len(chunks) // 4)
    run_chunks = lambda j: [f() for f in chunks[j * per_stage:(j + 1) * per_stage]]
    z_k = proj(3)
    _store_q_slots(z_q, qn_ref, q_o)
    run_chunks(0)
    z_vt = _dot_nt(wvt_ref[...], hn)
    _store_k_slots(z_k, kn_ref, ks_ref, k_o, None)
    run_chunks(1)
    z_qi = proj(4)
    _store_vt_slots(z_vt, v_o)
    run_chunks(2)
    z_ki = _dot(hn, wki_ref[...])
    z_wit = _dot_nt(wwit_ref[...], hn)
    qi_o[...] = z_qi.astype(BF16)
    run_chunks(3)
    close()
    ki_o[...] = z_ki.astype(BF16)
    wi_o[...] = z_wit * (IDX_DIM ** -0.5 * IDX_HEADS ** -0.5)


def _mixin_odd(x, g, w_in, q_norm, k_norm, k_shift, dw_w, dw_b, ln_g, ln_b, *, bsz, tm=512, halo=32):
    n, d = x.shape
    w = ATTN_WIDTH
    width = dw_w.shape[0]
    assert n % tm == 0 and tm % ATTN_TILE == 0 and w_in.shape[1] == 6 * w + IDX_DIM + IDX_HEADS
    assert (n // bsz) % tm == 0 and width - 1 <= halo <= tm and halo % SUBLANES == 0 and tm % CONV_ROWS == 0
    w_main = jnp.concatenate([w_in[:, :4 * w], w_in[:, 5 * w:6 * w]], axis=1).astype(BF16)
    w_vt = w_in[:, 4 * w:5 * w].T.astype(BF16)
    w_ki = w_in[:, 6 * w:6 * w + IDX_DIM]
    w_ki2 = jnp.concatenate([w_ki, w_ki], axis=1).astype(BF16)
    w_wit = w_in[:, 6 * w + IDX_DIM:].T.astype(BF16)
    row = lambda i: (i, 0)
    vec = lambda v: v.reshape(1, w).astype(F32)
    slot_shape = jax.ShapeDtypeStruct((n, SLOT_WIDTH), BF16)
    return pl.pallas_call(
        functools.partial(_mixin_odd_kernel, tiles_per_seq=(n // bsz) // tm, halo=halo),
        grid=(n // tm,),
        in_specs=[pl.BlockSpec((tm, d), row), _const_spec((1, d)), _const_spec((d, 5 * w)),
                  _const_spec((w, d)), _const_spec((d, 2 * IDX_DIM)), _const_spec((IDX_HEADS, d)),
                  _const_spec((1, SLOT)), _const_spec((1, SLOT)), _const_spec((1, SLOT_WIDTH)),
                  _const_spec((width, w)), _const_spec((1, w)), _const_spec((1, w)), _const_spec((1, w))],
        out_specs=[pl.BlockSpec((tm, w), row)] + [pl.BlockSpec((tm, SLOT_WIDTH), row)] * 2
        + [_vt_spec(tm), pl.BlockSpec((tm, w), row), pl.BlockSpec((tm, 128), row),
           pl.BlockSpec((IDX_HEADS, tm), lambda i: (0, i))],
        out_shape=[jax.ShapeDtypeStruct((n, w), BF16), slot_shape, slot_shape, _vt_shape(n),
                   jax.ShapeDtypeStruct((n, w), BF16),
                   jax.ShapeDtypeStruct((n, 128), BF16), jax.ShapeDtypeStruct((IDX_HEADS, n), F32)],
        scratch_shapes=[pltpu.VMEM((tm + halo, w), F32),
                        pltpu.VMEM((SUBLANES - 1, tm + halo - SUBLANES, w), F32)],
        compiler_params=_cparams(("arbitrary",)),
        name="mixin_odd",
    )(x, g.reshape(1, d), w_main, w_vt, w_ki2, w_wit, _slot_gain(q_norm), _slot_gain(k_norm), k_shift,
      dw_w, vec(dw_b), vec(ln_g), vec(ln_b))


def _lru_kernel(gate_ref, xr_ref, cw_ref, cb_ref, wa_ref, ba_ref, wx_ref, bx_ref, sp_ref,
                o_ref, xbuf, a_s, u_s, h_s, hc, *, ts):
    j = pl.program_id(1)

    @pl.when(j == 0)
    def _():
        xbuf[0:8, :] = jnp.zeros((8, xbuf.shape[1]), F32)
        hc[...] = jnp.zeros(hc.shape, F32)

    xbuf[8:8 + ts, :] = xr_ref[...]
    xc = cb_ref[...] + cw_ref[0:1, :] * xbuf[5:5 + ts, :]
    for k in range(1, 4):
        xc = xc + cw_ref[k:k + 1, :] * xbuf[5 + k:5 + k + ts, :]
    xbuf[0:8, :] = xbuf[ts:ts + 8, :]

    xcb = xc.astype(BF16)
    r = jax.nn.sigmoid(_dot(xcb, wa_ref[...]) + ba_ref[...])
    ig = jax.nn.sigmoid(_dot(xcb, wx_ref[...]) + bx_ref[...])
    log_a = -LRU_C * r * sp_ref[...]
    a = jnp.exp(log_a)
    a_s[...] = a
    u_s[...] = jnp.sqrt(-jnp.tanh(log_a) * (a * a + 1.0)) * (ig * xc)

    row = lax.broadcasted_iota(jnp.int32, (8, a_s.shape[1]), 0)

    def body(g, carry):
        r0 = pl.multiple_of(g * 8, 8)
        a = a_s[pl.ds(r0, 8), :]
        u = u_s[pl.ds(r0, 8), :]
        for s in (1, 2, 4):
            ok = row >= s
            a_sh = jnp.where(ok, pltpu.roll(a, s, 0), 1.0)
            u_sh = jnp.where(ok, pltpu.roll(u, s, 0), 0.0)
            u = a * u_sh + u
            a = a * a_sh
        h = a * carry + u
        h_s[pl.ds(r0, 8), :] = h
        return h[7:8, :]

    hc[...] = lax.fori_loop(0, ts // 8, body, hc[...], unroll=4)
    o_ref[...] = (h_s[...] * jax.nn.gelu(gate_ref[...])).astype(BF16)


def _block_diag(wb):
    nb, bs, _ = wb.shape
    eye = jnp.eye(nb, dtype=wb.dtype)
    return (eye[:, None, :, None] * wb[:, :, None, :]).reshape(nb * bs, nb * bs)


def _lru(gate, xr, conv_w, conv_b, ra_w, ra_b, ix_w, ix_b, lam, *, bsz, ts=256):
    n, w = xr.shape
    seq = n // bsz
    assert seq % ts == 0
    nt = seq // ts
    row = lambda b, j: (b * nt + j, 0)
    vec = lambda v: v.reshape(1, w).astype(F32)
    return pl.pallas_call(
        functools.partial(_lru_kernel, ts=ts),
        grid=(bsz, nt),
        in_specs=[pl.BlockSpec((ts, w), row), pl.BlockSpec((ts, w), row),
                  _const_spec((conv_w.shape[0], w)), _const_spec((1, w)),
                  _const_spec((w, w)), _const_spec((1, w)), _const_spec((w, w)), _const_spec((1, w)),
                  _const_spec((1, w))],
        out_specs=pl.BlockSpec((ts, w), row),
        out_shape=jax.ShapeDtypeStruct((n, w), BF16),
        scratch_shapes=[pltpu.VMEM((ts + 8, w), F32), pltpu.VMEM((ts, w), F32),
                        pltpu.VMEM((ts, w), F32), pltpu.VMEM((ts, w), F32), pltpu.VMEM((1, w), F32)],
        compiler_params=_cparams(("arbitrary", "arbitrary")),
        name="rg_lru",
    )(gate, xr, conv_w, vec(conv_b), _block_diag(ra_w).astype(BF16), vec(ra_b),
      _block_diag(ix_w).astype(BF16), vec(ix_b), vec(jax.nn.softplus(-lam)))


def _bucket_tiles(t):
    assert t > REL_MAX_DIST
    n = np.arange(2 * t)
    nf = np.maximum(n, 1).astype(np.float32)
    large = REL_MAX_EXACT + (np.log(nf / np.float32(REL_MAX_EXACT))
                             / np.float32(math.log(REL_MAX_DIST / REL_MAX_EXACT))
                             * np.float32(REL_BUCKETS - REL_MAX_EXACT)).astype(np.int32)
    bucket = np.where(n < REL_MAX_EXACT, n, np.minimum(large, REL_BUCKETS - 1)).astype(np.int32)
    qry = np.arange(t)[None, :]
    key = np.arange(t)[:, None]
    return np.stack([bucket[np.maximum(qry - key, 0)], bucket[t + qry - key]])


def _build_bias(idx_ref, rb_ref, bias_ref):
    t = idx_ref.shape[1]
    causal = (lax.broadcasted_iota(jnp.int32, (t, t), 0) <= lax.broadcasted_iota(jnp.int32, (t, t), 1))
    for h in range(N_HEADS):
        far = rb_ref[REL_BUCKETS - 1, h]
        for which in (0, 1):
            idx = idx_ref[which]
            acc = jnp.zeros((t, t), F32)
            for b in range(REL_BUCKETS - 1):
                acc = jnp.where(idx == b, (rb_ref[b, h] - far) * LOG2E, acc)
            if which == 0:
                bias_ref[h, t:2 * t, :] = jnp.where(causal, acc, NEG)
            else:
                bias_ref[h, 0:t, :] = acc


def _attn_init(m_ref, acc_ref):
    m_ref[...] = jnp.full(m_ref.shape, M_INIT, F32)
    acc_ref[...] = jnp.zeros(acc_ref.shape, F32)


def _softmax_step(s, h, vt_h, m_ref, acc_ref):
    m_old = m_ref[h]
    m_new = jnp.maximum(m_old, jnp.max(s, axis=0, keepdims=True))
    p = jnp.exp2(s - m_new)
    m_ref[h] = m_new
    acc_ref[h] = jnp.exp2(m_old - m_new) * acc_ref[h] + _dot(vt_h, p.astype(BF16))


def _fixed_step(s, h, vt_h, acc_ref):
    acc_ref[h] = acc_ref[h] + _dot(vt_h, jnp.exp2(s).astype(BF16))


def _heads_pipelined(logits, vt_slot, m_ref, acc_ref, online):
    ahead = [logits(h) for h in range(HEADS_AHEAD)]
    for h in range(N_HEADS):
        if h + HEADS_AHEAD < N_HEADS:
            ahead.append(logits(h + HEADS_AHEAD))
        if online:
            _softmax_step(ahead[h], h, vt_slot(h), m_ref, acc_ref)
        else:
            _fixed_step(ahead[h], h, vt_slot(h), acc_ref)


FAR_PAIR = ("far", "far")
FAR_ONE = ("far",)
NEAR_DIAG = ("near", "diag")
DIAG_ONE = ("diag",)


def _attend_tiles(kt, kinds, q_slot, k_ref, vt_ref, bias_ref, mask_add, m_ref, acc_ref, online):
    t = ATTN_TILE
    n = len(kinds)
    rows = pl.ds(pl.multiple_of(kt * t, t), n * t)

    def logits(h):
        s = _dot_nt(k_ref[rows, _slot(h)], q_slot(h))
        if kinds == NEAR_DIAG:
            s = s + bias_ref[h]
        elif kinds == DIAG_ONE:
            s = s + bias_ref[h, t:2 * t, :]
        if mask_add is not None:
            s = s + mask_add
        return s

    def vt_slot(h):
        parts = [vt_ref[kt + a, _vt_rows(h), :] for a in range(n)]
        return parts[0] if n == 1 else jnp.concatenate(parts, axis=1)

    _heads_pipelined(logits, vt_slot, m_ref, acc_ref, online)


def _for_causal_tiles(i, tiles):
    n_far = jnp.maximum(i - 1, 0)

    def pair(j, carry):
        tiles(2 * j, FAR_PAIR)
        return carry

    lax.fori_loop(0, n_far // 2, pair, 0)

    @pl.when(n_far % 2 == 1)
    def _():
        tiles(n_far - 1, FAR_ONE)

    @pl.when(i >= 1)
    def _():
        tiles(i - 1, NEAR_DIAG)

    @pl.when(i == 0)
    def _():
        tiles(i, DIAG_ONE)


def _attend_either(ok_ref, attend):
    @pl.when(ok_ref[0] != 0)
    def _():
        attend(False)

    @pl.when(ok_ref[0] == 0)
    def _():
        attend(True)


def _attn_finish(o_ref, acc_ref):
    parts = []
    for h in range(N_HEADS):
        acc = acc_ref[h]
        parts.append(acc[0:HEAD_DIM] * (1.0 / acc[FEAT0:FEAT0 + 1]))
    o_ref[...] = jnp.concatenate(parts, axis=0).T.astype(BF16)


def _smem_spec():
    return pl.BlockSpec(memory_space=pltpu.SMEM)


def _moba_kernel(rb_ref, ok_ref, q_ref, k_ref, vt_ref, km_ref, idx_ref, o_ref,
                 acc_ref, m_ref, qs_ref, bias_ref, *, n_blocks):
    i = pl.program_id(1)
    t = ATTN_TILE
    nbp = -(-n_blocks // 8) * 8

    @pl.when((pl.program_id(0) == 0) & (i == 0))
    def _():
        _build_bias(idx_ref, rb_ref, bias_ref)

    _attn_init(m_ref, acc_ref)

    blk = lax.broadcasted_iota(jnp.int32, (nbp, t), 0)
    past = blk < i
    for h in range(N_HEADS):
        qh = q_ref[:, _slot(h)]
        km_hi, km_lo = _split_bf16(km_ref[:, _slot(h)])
        gate_t = _dot_nt(km_hi, qh) + _dot_nt(km_lo, qh)
        g = jnp.where(past, gate_t[FEAT0:FEAT0 + nbp, :], NEG)
        rank = jnp.zeros((nbp, t), F32)
        for j in range(n_blocks):
            gj = g[j:j + 1, :]
            beats = (gj > g) | ((gj == g) & (blk > j))
            rank = rank + jnp.where(beats, 1.0, 0.0)
        flag = jnp.where(past & (rank >= MOBA_TOPK), NEG, 0.0)
        flag_t = jnp.concatenate([jnp.zeros((FEAT0, t), F32), flag,
                                  jnp.zeros((SLOT - FEAT0 - nbp, t), F32)], axis=0)
        qs_ref[:, _slot(h)] = (qh.astype(F32) + flag_t.T).astype(BF16)

    def attend(online):
        def tiles(kt, kinds):
            _attend_tiles(kt, kinds, lambda h: qs_ref[:, _slot(h)], k_ref, vt_ref, bias_ref, None,
                          m_ref, acc_ref, online)
        _for_causal_tiles(i, tiles)

    _attend_either(ok_ref, attend)
    _attn_finish(o_ref, acc_ref)


def _moba(q, k, vt, kmean, rel_bias, fixed_ok, *, bsz):
    n = q.shape[0]
    seq = n // bsz
    t = ATTN_TILE
    assert seq % t == 0 and t == MOBA_BLOCK
    nq = seq // t
    assert FEAT0 + nq <= SHIFT_LANE
    km = jnp.pad(kmean.reshape(bsz, nq, SLOT_WIDTH), ((0, 0), (FEAT0, SLOT - FEAT0 - nq), (0, 0)))
    r3 = lambda a: a.reshape(bsz, seq, SLOT_WIDTH)
    seq_spec = pl.BlockSpec((None, seq, SLOT_WIDTH), lambda b, i: (b, 0, 0))
    out = pl.pallas_call(
        functools.partial(_moba_kernel, n_blocks=nq),
        grid=(bsz, nq),
        in_specs=[_smem_spec(), _smem_spec(),
                  pl.BlockSpec((None, t, SLOT_WIDTH), lambda b, i: (b, i, 0)), seq_spec,
                  pl.BlockSpec((nq, N_HEADS * VT_ROWS, t), lambda b, i: (b, 0, 0)),
                  pl.BlockSpec((None, SLOT, SLOT_WIDTH), lambda b, i: (b, 0, 0)),
                  _const_spec((2, t, t))],
        out_specs=pl.BlockSpec((None, t, ATTN_WIDTH), lambda b, i: (b, i, 0)),
        out_shape=jax.ShapeDtypeStruct((bsz, seq, ATTN_WIDTH), BF16),
        scratch_shapes=[pltpu.VMEM((N_HEADS, VT_ROWS, t), F32), pltpu.VMEM((N_HEADS, 1, t), F32),
                        pltpu.VMEM((t, SLOT_WIDTH), BF16),
                        pltpu.VMEM((N_HEADS, 2 * t, t), F32)],
        compiler_params=_cparams(("arbitrary", "arbitrary")),
        name="moba_attention",
    )(rel_bias, fixed_ok, r3(q), r3(k), vt, km, jnp.asarray(_bucket_tiles(t)))
    return out.reshape(n, ATTN_WIDTH)


def _dsa_kernel(rb_ref, ok_ref, q_ref, k_ref, vt_ref, qi_ref, ki_ref, wit_ref, idx_ref, tri_ref, o_ref,
                acc_ref, m_ref, isc_ref, bc_ref, bias_ref, *, n_sel):
    i = pl.program_id(1)
    t = ATTN_TILE
    key = lax.broadcasted_iota(jnp.int32, (t, t), 0)
    qry = lax.broadcasted_iota(jnp.int32, (t, t), 1)
    lane128 = lax.broadcasted_iota(jnp.int32, (1, 128), 1)

    @pl.when((pl.program_id(0) == 0) & (i == 0))
    def _():
        _build_bias(idx_ref, rb_ref, bias_ref)

    _attn_init(m_ref, acc_ref)

    def index_tiles(kt, n, diag):
        rows = pl.ds(pl.multiple_of(kt * t, t), n * t)
        ki2 = ki_ref[rows, :]
        acc = jnp.zeros((n * t, t), F32)
        for pr in range(IDX_HEADS // 2):
            q2 = qi_ref[:, pr * 128:(pr + 1) * 128]
            for half in range(2):
                h = 2 * pr + half
                hm = (lane128 >= half * IDX_DIM) & (lane128 < (half + 1) * IDX_DIM)
                s = _dot_nt(ki2, jnp.where(hm, q2, jnp.zeros((), BF16)))
                acc = acc + jnp.maximum(s, 0.0) * wit_ref[h:h + 1, :]
        if diag:
            acc = jnp.where(key <= qry, acc, -jnp.inf)
        for a in range(n):
            isc_ref[kt + a] = acc[a * t:(a + 1) * t]

    def index_pair(j, carry):
        index_tiles(2 * j, 2, False)
        return carry

    lax.fori_loop(0, i // 2, index_pair, 0)

    @pl.when(i % 2 == 1)
    def _():
        index_tiles(i - 1, 1, False)

    index_tiles(i, 1, True)

    def fold8(x, op):
        return op(x.reshape(t // 8, 8, t), axis=0)

    def minmax_body(kt, carry):
        mn, mx = carry
        x = isc_ref[kt]
        mx = jnp.maximum(mx, fold8(x, jnp.max))
        mn = jnp.minimum(mn, fold8(jnp.where(x == -jnp.inf, jnp.inf, x), jnp.min))
        return mn, mx

    mn, mx = lax.fori_loop(0, i + 1, minmax_body,
                           (jnp.full((8, t), jnp.inf, F32), jnp.full((8, t), -jnp.inf, F32)))
    lo0 = jnp.min(mn, axis=0, keepdims=True)
    mx = jnp.max(mx, axis=0, keepdims=True)
    hi0 = mx + jnp.abs(mx) * 1e-3 + 1e-30
    n_valid = (i * t + 1 + lax.broadcasted_iota(jnp.int32, (1, t), 1)).astype(F32)
    want = jnp.minimum(n_valid, float(n_sel))

    def count_ge(thr):
        def one(kt, c):
            return c + fold8(jnp.where(isc_ref[kt] >= thr, 1.0, 0.0), jnp.sum)

        def pair(j, c):
            return one(2 * j + 1, one(2 * j, c))

        c = lax.fori_loop(0, (i + 1) // 2, pair, jnp.zeros((8, t), F32))
        c = lax.cond((i + 1) % 2 == 1, lambda c: one(i, c), lambda c: c, c)
        return jnp.sum(c, axis=0, keepdims=True)

    c_nonneg = count_ge(0.0)
    c_pos = count_ge(F32_TINY)
    above = want <= c_pos
    zero_tie = (want > c_pos) & (want <= c_nonneg)
    lo1 = jnp.where(above, F32_TINY, jnp.where(zero_tie, 0.0, lo0))
    hi1 = jnp.where(above, hi0, jnp.where(zero_tie, F32_TINY, 0.0))
    c_lo1 = jnp.where(above, c_pos, jnp.where(zero_tie, c_nonneg, n_valid))
    c_hi1 = jnp.where(above, 0.0, jnp.where(zero_tie, c_pos, c_nonneg))

    def bisect_step(_, carry):
        lo, hi, c_lo, c_hi = carry
        mid = 0.5 * (lo + hi)
        c_mid = count_ge(mid)
        up = c_mid >= want
        return (jnp.where(up, mid, lo), jnp.where(up, hi, mid),
                jnp.where(up, c_mid, c_lo), jnp.where(up, c_hi, c_mid))

    def bisect_cond(carry):
        it, _, _, c_lo, _ = carry
        return (it < BISECT_ITERS) & (jnp.max(jnp.where(zero_tie, 0.0, c_lo - want)) > 0.0)

    def bisect_body(carry):
        return (carry[0] + BISECT_CHECK_EVERY,) + lax.fori_loop(0, BISECT_CHECK_EVERY, bisect_step, carry[1:])

    _, lo, hi, c_lo, c_hi = lax.while_loop(bisect_cond, bisect_body, (0, lo1, hi1, c_lo1, c_hi1))
    need = want - c_hi
    banded = jnp.max(c_lo - want) > 0.0
    bc_ref[...] = jnp.zeros(bc_ref.shape, F32)

    def select_mask(kt):
        x = isc_ref[kt]

        def plain():
            return jnp.where(x >= lo, 0.0, NEG)

        def with_band():
            band = jnp.where((x >= lo) & (x < hi), 1.0, 0.0)
            before = bc_ref[...] + _dot(tri_ref[...], band.astype(BF16))
            bc_ref[...] = bc_ref[...] + jnp.sum(band, axis=0, keepdims=True)
            return jnp.where((x >= hi) | ((band > 0.0) & (before < need)), 0.0, NEG)

        return lax.cond(banded, with_band, plain)

    def attend(online):
        def tiles(kt, kinds):
            masks = [select_mask(kt + a) for a in range(len(kinds))]
            mask_add = masks[0] if len(masks) == 1 else jnp.concatenate(masks, axis=0)
            _attend_tiles(kt, kinds, lambda h: q_ref[:, _slot(h)], k_ref, vt_ref, bias_ref, mask_add,
                          m_ref, acc_ref, online)
        _for_causal_tiles(i, tiles)

    _attend_either(ok_ref, attend)
    _attn_finish(o_ref, acc_ref)


def _dsa(q, k, vt, qi, ki2, wit, rel_bias, fixed_ok, *, bsz):
    n = q.shape[0]
    seq = n // bsz
    t = ATTN_TILE
    assert seq % t == 0
    nq = seq // t
    n_sel = min(DSA_TOPK_MAX, seq // 4)
    tri = (jnp.arange(t)[None, :] < jnp.arange(t)[:, None]).astype(BF16)
    r3 = lambda a: a.reshape(bsz, seq, a.shape[-1])
    tile_spec = lambda width: pl.BlockSpec((None, t, width), lambda b, i: (b, i, 0))
    seq_spec = lambda width: pl.BlockSpec((None, seq, width), lambda b, i: (b, 0, 0))
    out = pl.pallas_call(
        functools.partial(_dsa_kernel, n_sel=n_sel),
        grid=(bsz, nq),
        in_specs=[_smem_spec(), _smem_spec(), tile_spec(SLOT_WIDTH), seq_spec(SLOT_WIDTH),
                  pl.BlockSpec((nq, N_HEADS * VT_ROWS, t), lambda b, i: (b, 0, 0)),
                  tile_spec(ATTN_WIDTH), seq_spec(128),
                  pl.BlockSpec((IDX_HEADS, t), lambda b, i: (0, b * nq + i)),
                  _const_spec((2, t, t)), _const_spec((t, t))],
        out_specs=tile_spec(ATTN_WIDTH),
        out_shape=jax.ShapeDtypeStruct((bsz, seq, ATTN_WIDTH), BF16),
        scratch_shapes=[pltpu.VMEM((N_HEADS, VT_ROWS, t), F32), pltpu.VMEM((N_HEADS, 1, t), F32),
                        pltpu.VMEM((nq, t, t), F32), pltpu.VMEM((1, t), F32),
                        pltpu.VMEM((N_HEADS, 2 * t, t), F32)],
        compiler_params=_cparams(("arbitrary", "arbitrary")),
        name="dsa_attention",
    )(rel_bias, fixed_ok, r3(q), r3(k), vt, r3(qi), r3(ki2), wit, jnp.asarray(_bucket_tiles(t)), tri)
    return out.reshape(n, ATTN_WIDTH)


def kernel(x, rel_bias, ffn1_norm, ffn1_w_gate, ffn1_w_up, ffn1_w_down, mix_norm, ffn2_norm, ffn2_w_gate, ffn2_w_up, ffn2_w_down, ev_w_in, ev_conv_w, ev_conv_b, ev_ra_w, ev_ra_b, ev_ix_w, ev_ix_b, ev_lambda, ev_q_norm, ev_k_norm, ev_w_out, od_w_in, od_dw_w, od_dw_b, od_ln_g, od_ln_b, od_q_norm, od_k_norm, od_w_out):
    bsz, seq, d = x.shape
    depth = ffn1_norm.shape[0]
    h = x.reshape(bsz * seq, d)
    for i in range(depth):
        h = _ffn(h, ffn1_norm, ffn1_w_gate, ffn1_w_up, ffn1_w_down, i)
        j = i // 2
        if i % 2 == 0:
            k_shift, fixed_ok = _logit_bounds(ev_q_norm[j], ev_k_norm[j], rel_bias)
            gate, xr, q, k, vt, kmean = _mixin_even(h, mix_norm[i], ev_w_in[j], ev_q_norm[j], ev_k_norm[j],
                                                    k_shift, n_blocks=seq // MOBA_BLOCK)
            ya = _lru(gate, xr, ev_conv_w[j], ev_conv_b[j], ev_ra_w[j], ev_ra_b[j],
                      ev_ix_w[j], ev_ix_b[j], ev_lambda[j], bsz=bsz)
            yb = _moba(q, k, vt, kmean, rel_bias, fixed_ok, bsz=bsz)
            mix = (ya, yb, ev_w_out[j])
        else:
            k_shift, fixed_ok = _logit_bounds(od_q_norm[j], od_k_norm[j], rel_bias)
            yc, q, k, vt, qi, ki2, wit = _mixin_odd(h, mix_norm[i], od_w_in[j], od_q_norm[j], od_k_norm[j],
                                                    k_shift, od_dw_w[j], od_dw_b[j], od_ln_g[j], od_ln_b[j],
                                                    bsz=bsz)
            yd = _dsa(q, k, vt, qi, ki2, wit, rel_bias, fixed_ok, bsz=bsz)
            mix = (yc, yd, od_w_out[j])
        h = _ffn(h, ffn2_norm, ffn2_w_gate, ffn2_w_up, ffn2_w_down, i, mix)
    return h.reshape(bsz, seq, d)
```

```python
import functools
import math

import numpy as np
import jax
import jax.numpy as jnp
from jax import lax
from jax.experimental import pallas as pl
from jax.experimental.pallas import tpu as pltpu

F32 = jnp.float32
BF16 = jnp.bfloat16

N_HEADS = 8
HEAD_DIM = 64
ATTN_WIDTH = N_HEADS * HEAD_DIM
LRU_C = 8.0
MOBA_BLOCK = 256
MOBA_TOPK = 3
IDX_HEADS = 8
IDX_DIM = 64
DSA_TOPK_MAX = 256
REL_BUCKETS = 32
REL_MAX_EXACT = REL_BUCKETS // 2
REL_MAX_DIST = 128
EPS = 1e-6
NEG = -1e30
M_INIT = -1e29
ATTN_TILE = 256
SUBLANES = 8
CONV_ROWS = 64
SLOT = 128
SLOT_WIDTH = N_HEADS * SLOT
FEAT0 = HEAD_DIM
SHIFT_LANE = SLOT - 1
LOGIT_RANGE_MAX = 120.0
VT_ROWS = 80
MOBA_FAR_GROUP = 4
DSA_FAR_GROUP = 2
HEADS_AHEAD = 4
BISECT_ITERS = 32
BISECT_CHECK_EVERY = 4
F32_TINY = float(np.finfo(np.float32).tiny)
LOG2E = math.log2(math.e)
Q_SCALE = HEAD_DIM ** -0.5 * LOG2E
VMEM_LIMIT = 56 * 1024 * 1024
FFN_VMEM_LIMIT = 60 * 1024 * 1024


def _cparams(sem):
    return pltpu.CompilerParams(dimension_semantics=sem, vmem_limit_bytes=VMEM_LIMIT)


def _dot(a, b):
    return jnp.dot(a, b, preferred_element_type=F32)


def _dot_nt(a, b):
    return lax.dot_general(a, b, (((1,), (1,)), ((), ())), preferred_element_type=F32)


def _split_bf16(x):
    hi = x.astype(BF16)
    lo = (x - hi.astype(F32)).astype(BF16)
    return hi, lo


def _rms_rows(x, g):
    return x * lax.rsqrt(jnp.mean(x * x, axis=-1, keepdims=True) + EPS) * g


def _const_spec(shape):
    nd = len(shape)
    return pl.BlockSpec(shape, lambda *_: (0,) * nd, pipeline_mode=pl.Buffered(1))


def _slot(h):
    return slice(h * SLOT, (h + 1) * SLOT)


def _vt_rows(h):
    return slice(h * VT_ROWS, (h + 1) * VT_ROWS)


def _to_slots(x):
    low = lax.broadcasted_iota(jnp.int32, (1, SLOT), 1) < HEAD_DIM
    slots = []
    for p in range(N_HEADS // 2):
        chunk = x[:, p * SLOT:(p + 1) * SLOT]
        slots.append(jnp.where(low, chunk, 0.0))
        slots.append(jnp.where(low, pltpu.roll(chunk, HEAD_DIM, 1), 0.0))
    return slots


def _ffn_kernel(*refs, n_chunks, has_mix):
    if has_mix:
        x_ref, ya_ref, yb_ref, wo_ref, g_ref, wg_ref, wu_ref, wd_ref, o_ref, wg_s, wu_s, wd_s = refs
    else:
        x_ref, g_ref, wg_ref, wu_ref, wd_ref, o_ref, wg_s, wu_s, wd_s = refs
    step = pl.program_id(0)

    @pl.when(step < n_chunks)
    def _():
        wg_s[step] = wg_ref[...].astype(BF16)
        wu_s[step] = wu_ref[...].astype(BF16)
        wd_s[step] = wd_ref[...].astype(BF16)

    @pl.when(step >= n_chunks)
    def _():
        x = x_ref[...]
        if has_mix:
            w = ya_ref.shape[1]
            x = x + _dot(ya_ref[...], wo_ref[0:w, :]) + _dot(yb_ref[...], wo_ref[w:2 * w, :])
        hn = _rms_rows(x, g_ref[...]).astype(BF16)
        acc = jnp.zeros(x.shape, F32)
        for c in range(n_chunks):
            gt = _dot(hn, wg_s[c])
            ut = _dot(hn, wu_s[c])
            a = (gt * jax.nn.sigmoid(gt) * ut).astype(BF16)
            acc = acc + _dot(a, wd_s[c])
        o_ref[...] = x + 0.5 * acc


def _ffn(x, g, wg, wu, wd, layer, mix=None, *, tm=1024, ff_chunk=256):
    n, d = x.shape
    d_ff = wg.shape[2]
    assert n % tm == 0 and d_ff % ff_chunk == 0
    nc = d_ff // ff_chunk
    tile = lambda s: (jnp.maximum(s - nc, 0), 0)
    chunk_col = lambda s: (layer, 0, jnp.minimum(s, nc - 1))
    chunk_row = lambda s: (layer, jnp.minimum(s, nc - 1), 0)
    in_specs = [pl.BlockSpec((tm, d), tile)]
    args = [x]
    scratch = [pltpu.VMEM((nc, d, ff_chunk), BF16), pltpu.VMEM((nc, d, ff_chunk), BF16),
               pltpu.VMEM((nc, ff_chunk, d), BF16)]
    if mix is not None:
        ya, yb, w_out = mix
        w = ya.shape[1]
        assert w_out.shape == (2 * w, d)
        in_specs += [pl.BlockSpec((tm, w), tile), pl.BlockSpec((tm, w), tile), _const_spec((2 * w, d))]
        args += [ya, yb, w_out.astype(BF16)]
    in_specs += [_const_spec((1, d)), pl.BlockSpec((None, d, ff_chunk), chunk_col),
                 pl.BlockSpec((None, d, ff_chunk), chunk_col), pl.BlockSpec((None, ff_chunk, d), chunk_row)]
    args += [g[layer].reshape(1, d), wg, wu, wd]
    return pl.pallas_call(
        functools.partial(_ffn_kernel, n_chunks=nc, has_mix=mix is not None),
        grid=(nc + n // tm,),
        in_specs=in_specs,
        out_specs=pl.BlockSpec((tm, d), tile),
        out_shape=jax.ShapeDtypeStruct((n, d), F32),
        scratch_shapes=scratch,
        compiler_params=pltpu.CompilerParams(dimension_semantics=("arbitrary",),
                                             vmem_limit_bytes=FFN_VMEM_LIMIT),
        name="ffn_mix" if mix is not None else "ffn",
    )(*args)


def _slot_rms(xs, g):
    ms = jnp.sum(xs * xs, axis=-1, keepdims=True) * (1.0 / HEAD_DIM)
    return xs * lax.rsqrt(ms + EPS) * g


def _store_q_slots(z, gain_ref, q_o):
    lane = lax.broadcasted_iota(jnp.int32, (1, SLOT), 1)
    for h, zs in enumerate(_to_slots(z)):
        qs = _slot_rms(zs, gain_ref[...]) * Q_SCALE
        q_o[:, _slot(h)] = jnp.where(lane == SHIFT_LANE, 1.0, qs).astype(BF16)


def _store_k_slots(z, gain_ref, shift_ref, k_o, one_lanes):
    slots = [_slot_rms(zs, gain_ref[...]) for zs in _to_slots(z)]
    for h, ks in enumerate(slots):
        ks = ks if one_lanes is None else jnp.where(one_lanes, 1.0, ks)
        k_o[:, _slot(h)] = (ks + shift_ref[:, _slot(h)]).astype(BF16)
    return slots


def _store_vt_slots(zt, v_o):
    pad = VT_ROWS - HEAD_DIM
    tail = (lax.broadcasted_iota(jnp.int32, (pad, ATTN_TILE), 0) == 0).astype(BF16)
    for h in range(N_HEADS):
        for r in range(zt.shape[1] // ATTN_TILE):
            feat = zt[h * HEAD_DIM:(h + 1) * HEAD_DIM, r * ATTN_TILE:(r + 1) * ATTN_TILE]
            v_o[r, h * VT_ROWS:h * VT_ROWS + HEAD_DIM, :] = feat.astype(BF16)
            v_o[r, h * VT_ROWS + HEAD_DIM:(h + 1) * VT_ROWS, :] = tail


def _pipelined(stages):
    nxt = stages[0][0]()
    for j, (_, consume) in enumerate(stages):
        cur = nxt
        if j + 1 < len(stages):
            nxt = stages[j + 1][0]()
        consume(cur)


def _mixin_even_kernel(x_ref, g_ref, w_ref, wvt_ref, qn_ref, kn_ref, ks_ref,
                       gate_o, xr_o, q_o, k_o, v_o, km_o, *, n_blocks):
    tm = x_ref.shape[0]
    hn = _rms_rows(x_ref[...], g_ref[...]).astype(BF16)
    w = ATTN_WIDTH
    proj = lambda c: (lambda: _dot(hn, w_ref[:, c * w:(c + 1) * w]))
    blocks_per_tile = tm // MOBA_BLOCK
    row_block = lax.broadcasted_iota(jnp.int32, (tm, 1), 0) // MOBA_BLOCK
    block = (pl.program_id(0) * blocks_per_tile + row_block) % n_blocks
    block_lane = lax.broadcasted_iota(jnp.int32, (1, SLOT), 1) == FEAT0 + block

    def store_k(z):
        for h, ks in enumerate(_store_k_slots(z, kn_ref, ks_ref, k_o, block_lane)):
            for r in range(blocks_per_tile):
                km_o[r, :, _slot(h)] = jnp.mean(ks[r * MOBA_BLOCK:(r + 1) * MOBA_BLOCK], axis=0, keepdims=True)

    def store(ref):
        def consume(z):
            ref[...] = z
        return consume

    _pipelined([(proj(2), lambda z: _store_q_slots(z, qn_ref, q_o)), (proj(3), store_k),
                (lambda: _dot_nt(wvt_ref[...], hn), lambda zt: _store_vt_slots(zt, v_o)),
                (proj(0), store(gate_o)), (proj(1), store(xr_o))])


def _vt_spec(tm):
    return pl.BlockSpec((tm // ATTN_TILE, N_HEADS * VT_ROWS, ATTN_TILE), lambda i: (i, 0, 0))


def _vt_shape(n):
    return jax.ShapeDtypeStruct((n // ATTN_TILE, N_HEADS * VT_ROWS, ATTN_TILE), BF16)


def _logit_bounds(q_norm, k_norm, rel_bias):
    qk = HEAD_DIM * jnp.max(jnp.abs(q_norm)) * jnp.max(jnp.abs(k_norm)) * Q_SCALE * 1.02
    rel = (rel_bias - rel_bias[REL_BUCKETS - 1]) * LOG2E
    bound = qk + jnp.maximum(jnp.max(rel, axis=0), 0.0) + 1.0
    lowest = -qk + jnp.minimum(jnp.min(rel, axis=0), 0.0)
    fixed_ok = jnp.all(bound - lowest <= LOGIT_RANGE_MAX).astype(jnp.int32).reshape(1)
    k_shift = jnp.zeros((N_HEADS, SLOT), F32).at[:, SHIFT_LANE].set(-bound).reshape(1, SLOT_WIDTH)
    return k_shift, fixed_ok


def _slot_gain(g):
    return jnp.pad(g.astype(F32), (0, SLOT - HEAD_DIM)).reshape(1, SLOT)


def _mixin_even(x, g, w_in, q_norm, k_norm, k_shift, *, n_blocks, tm=512):
    n, d = x.shape
    w = ATTN_WIDTH
    assert n % tm == 0 and tm % MOBA_BLOCK == 0 and tm % ATTN_TILE == 0 and w_in.shape[1] == 5 * w
    assert FEAT0 + n_blocks <= SHIFT_LANE
    row = lambda i: (i, 0)
    blk = tm // MOBA_BLOCK
    slot_shape = jax.ShapeDtypeStruct((n, SLOT_WIDTH), BF16)
    return pl.pallas_call(
        functools.partial(_mixin_even_kernel, n_blocks=n_blocks),
        grid=(n // tm,),
        in_specs=[pl.BlockSpec((tm, d), row), _const_spec((1, d)), _const_spec((d, 4 * w)),
                  _const_spec((w, d)), _const_spec((1, SLOT)), _const_spec((1, SLOT)),
                  _const_spec((1, SLOT_WIDTH))],
        out_specs=[pl.BlockSpec((tm, w), row)] * 2 + [pl.BlockSpec((tm, SLOT_WIDTH), row)] * 2
        + [_vt_spec(tm), pl.BlockSpec((blk, 1, SLOT_WIDTH), lambda i: (i, 0, 0))],
        out_shape=[jax.ShapeDtypeStruct((n, w), F32), jax.ShapeDtypeStruct((n, w), F32),
                   slot_shape, slot_shape, _vt_shape(n),
                   jax.ShapeDtypeStruct((n // MOBA_BLOCK, 1, SLOT_WIDTH), F32)],
        compiler_params=_cparams(("parallel",)),
        name="mixin_even",
    )(x, g.reshape(1, d), w_in[:, :4 * w].astype(BF16), w_in[:, 4 * w:].T.astype(BF16),
      _slot_gain(q_norm), _slot_gain(k_norm), k_shift)


def _conv_ln_silu(c, first, w_ref, b_ref, g_ref, beta_ref, o_ref, cbuf, sh_ref, *, halo):
    ts = c.shape[0]
    width = w_ref.shape[0]

    @pl.when(first)
    def _():
        cbuf[0:halo, :] = jnp.zeros((halo, cbuf.shape[1]), F32)

    cbuf[halo:halo + ts, :] = c
    base = halo - (width - 1)
    span = sh_ref.shape[1]
    for r in range(1, SUBLANES):
        sh_ref[r - 1] = cbuf[r:r + span, :]

    def chunk(c0):
        y = jnp.broadcast_to(b_ref[...], (CONV_ROWS, cbuf.shape[1]))
        for k in range(width):
            r = (base + k) % SUBLANES
            u0 = c0 + base + k - r
            win = cbuf[u0:u0 + CONV_ROWS, :] if r == 0 else sh_ref[r - 1, u0:u0 + CONV_ROWS, :]
            y = y + w_ref[k:k + 1, :] * win
        mu = jnp.mean(y, axis=-1, keepdims=True)
        yc = y - mu
        var = jnp.mean(yc * yc, axis=-1, keepdims=True)
        z = yc * lax.rsqrt(var + EPS) * g_ref[...] + beta_ref[...]
        o_ref[c0:c0 + CONV_ROWS, :] = (z * jax.nn.sigmoid(z)).astype(BF16)

    def close():
        cbuf[0:halo, :] = cbuf[ts:ts + halo, :]

    return [functools.partial(chunk, c0) for c0 in range(0, ts, CONV_ROWS)], close


def _mixin_odd_kernel(x_ref, g_ref, w_ref, wvt_ref, wki_ref, wwit_ref, qn_ref, kn_ref, ks_ref,
                      dw_ref, db_ref, lg_ref, lb_ref,
                      yc_o, q_o, k_o, v_o, qi_o, ki_o, wi_o, cbuf, sh_ref, *, tiles_per_seq, halo):
    hn = _rms_rows(x_ref[...], g_ref[...]).astype(BF16)
    w = ATTN_WIDTH
    proj = lambda c, n=1: _dot(hn, w_ref[:, c * w:(c + n) * w])
    first = pl.program_id(0) % tiles_per_seq == 0

    z_glu = proj(0, 2)
    z_q = proj(2)
    chunks, close = _conv_ln_silu(z_glu[:, 0:w] * jax.nn.sigmoid(z_glu[:, w:2 * w]), first,
                                  dw_ref, db_ref, lg_ref, lb_ref, yc_o, cbuf, sh_ref, halo=halo)
    per_stage = -(-len(chunks) // 4)
    run_chunks = lambda j: [f() for f in chunks[j * per_stage:(j + 1) * per_stage]]
    z_k = proj(3)
    _store_q_slots(z_q, qn_ref, q_o)
    run_chunks(0)
    z_vt = _dot_nt(wvt_ref[...], hn)
    _store_k_slots(z_k, kn_ref, ks_ref, k_o, None)
    run_chunks(1)
    z_qi = proj(4)
    _store_vt_slots(z_vt, v_o)
    run_chunks(2)
    z_ki = _dot(hn, wki_ref[...])
    z_wit = _dot_nt(wwit_ref[...], hn)
    qi_o[...] = z_qi.astype(BF16)
    run_chunks(3)
    close()
    ki_o[...] = z_ki.astype(BF16)
    wi_o[...] = z_wit * (IDX_DIM ** -0.5 * IDX_HEADS ** -0.5)


def _mixin_odd(x, g, w_in, q_norm, k_norm, k_shift, dw_w, dw_b, ln_g, ln_b, *, bsz, tm=512, halo=32):
    n, d = x.shape
    w = ATTN_WIDTH
    width = dw_w.shape[0]
    assert n % tm == 0 and tm % ATTN_TILE == 0 and w_in.shape[1] == 6 * w + IDX_DIM + IDX_HEADS
    assert (n // bsz) % tm == 0 and width - 1 <= halo <= tm and halo % SUBLANES == 0 and tm % CONV_ROWS == 0
    w_main = jnp.concatenate([w_in[:, :4 * w], w_in[:, 5 * w:6 * w]], axis=1).astype(BF16)
    w_vt = w_in[:, 4 * w:5 * w].T.astype(BF16)
    w_ki = w_in[:, 6 * w:6 * w + IDX_DIM]
    w_ki2 = jnp.concatenate([w_ki, w_ki], axis=1).astype(BF16)
    w_wit = w_in[:, 6 * w + IDX_DIM:].T.astype(BF16)
    row = lambda i: (i, 0)
    vec = lambda v: v.reshape(1, w).astype(F32)
    slot_shape = jax.ShapeDtypeStruct((n, SLOT_WIDTH), BF16)
    return pl.pallas_call(
        functools.partial(_mixin_odd_kernel, tiles_per_seq=(n // bsz) // tm, halo=halo),
        grid=(n // tm,),
        in_specs=[pl.BlockSpec((tm, d), row), _const_spec((1, d)), _const_spec((d, 5 * w)),
                  _const_spec((w, d)), _const_spec((d, 2 * IDX_DIM)), _const_spec((IDX_HEADS, d)),
                  _const_spec((1, SLOT)), _const_spec((1, SLOT)), _const_spec((1, SLOT_WIDTH)),
                  _const_spec((width, w)), _const_spec((1, w)), _const_spec((1, w)), _const_spec((1, w))],
        out_specs=[pl.BlockSpec((tm, w), row)] + [pl.BlockSpec((tm, SLOT_WIDTH), row)] * 2
        + [_vt_spec(tm), pl.BlockSpec((tm, w), row), pl.BlockSpec((tm, 128), row),
           pl.BlockSpec((IDX_HEADS, tm), lambda i: (0, i))],
        out_shape=[jax.ShapeDtypeStruct((n, w), BF16), slot_shape, slot_shape, _vt_shape(n),
                   jax.ShapeDtypeStruct((n, w), BF16),
                   jax.ShapeDtypeStruct((n, 128), BF16), jax.ShapeDtypeStruct((IDX_HEADS, n), F32)],
        scratch_shapes=[pltpu.VMEM((tm + halo, w), F32),
                        pltpu.VMEM((SUBLANES - 1, tm + halo - SUBLANES, w), F32)],
        compiler_params=_cparams(("arbitrary",)),
        name="mixin_odd",
    )(x, g.reshape(1, d), w_main, w_vt, w_ki2, w_wit, _slot_gain(q_norm), _slot_gain(k_norm), k_shift,
      dw_w, vec(dw_b), vec(ln_g), vec(ln_b))


def _lru_kernel(gate_ref, xr_ref, cw_ref, cb_ref, wa_ref, ba_ref, wx_ref, bx_ref, sp_ref,
                o_ref, xbuf, a_s, u_s, h_s, hc, *, ts):
    j = pl.program_id(1)

    @pl.when(j == 0)
    def _():
        xbuf[0:8, :] = jnp.zeros((8, xbuf.shape[1]), F32)
        hc[...] = jnp.zeros(hc.shape, F32)

    xbuf[8:8 + ts, :] = xr_ref[...]
    xc = cb_ref[...] + cw_ref[0:1, :] * xbuf[5:5 + ts, :]
    for k in range(1, 4):
        xc = xc + cw_ref[k:k + 1, :] * xbuf[5 + k:5 + k + ts, :]
    xbuf[0:8, :] = xbuf[ts:ts + 8, :]

    xcb = xc.astype(BF16)
    r = jax.nn.sigmoid(_dot(xcb, wa_ref[...]) + ba_ref[...])
    ig = jax.nn.sigmoid(_dot(xcb, wx_ref[...]) + bx_ref[...])
    log_a = -LRU_C * r * sp_ref[...]
    a = jnp.exp(log_a)
    a_s[...] = a
    u_s[...] = jnp.sqrt(-jnp.tanh(log_a) * (a * a + 1.0)) * (ig * xc)

    row = lax.broadcasted_iota(jnp.int32, (8, a_s.shape[1]), 0)

    def body(g, carry):
        r0 = pl.multiple_of(g * 8, 8)
        a = a_s[pl.ds(r0, 8), :]
        u = u_s[pl.ds(r0, 8), :]
        for s in (1, 2, 4):
            ok = row >= s
            a_sh = jnp.where(ok, pltpu.roll(a, s, 0), 1.0)
            u_sh = jnp.where(ok, pltpu.roll(u, s, 0), 0.0)
            u = a * u_sh + u
            a = a * a_sh
        h = a * carry + u
        h_s[pl.ds(r0, 8), :] = h
        return h[7:8, :]

    hc[...] = lax.fori_loop(0, ts // 8, body, hc[...], unroll=4)
    o_ref[...] = (h_s[...] * jax.nn.gelu(gate_ref[...])).astype(BF16)


def _block_diag(wb):
    nb, bs, _ = wb.shape
    eye = jnp.eye(nb, dtype=wb.dtype)
    return (eye[:, None, :, None] * wb[:, :, None, :]).reshape(nb * bs, nb * bs)


def _lru(gate, xr, conv_w, conv_b, ra_w, ra_b, ix_w, ix_b, lam, *, bsz, ts=256):
    n, w = xr.shape
    seq = n // bsz
    assert seq % ts == 0
    nt = seq // ts
    row = lambda b, j: (b * nt + j, 0)
    vec = lambda v: v.reshape(1, w).astype(F32)
    return pl.pallas_call(
        functools.partial(_lru_kernel, ts=ts),
        grid=(bsz, nt),
        in_specs=[pl.BlockSpec((ts, w), row), pl.BlockSpec((ts, w), row),
                  _const_spec((conv_w.shape[0], w)), _const_spec((1, w)),
                  _const_spec((w, w)), _const_spec((1, w)), _const_spec((w, w)), _const_spec((1, w)),
                  _const_spec((1, w))],
        out_specs=pl.BlockSpec((ts, w), row),
        out_shape=jax.ShapeDtypeStruct((n, w), BF16),
        scratch_shapes=[pltpu.VMEM((ts + 8, w), F32), pltpu.VMEM((ts, w), F32),
                        pltpu.VMEM((ts, w), F32), pltpu.VMEM((ts, w), F32), pltpu.VMEM((1, w), F32)],
        compiler_params=_cparams(("arbitrary", "arbitrary")),
        name="rg_lru",
    )(gate, xr, conv_w, vec(conv_b), _block_diag(ra_w).astype(BF16), vec(ra_b),
      _block_diag(ix_w).astype(BF16), vec(ix_b), vec(jax.nn.softplus(-lam)))


def _bucket_tiles(t):
    assert t > REL_MAX_DIST
    n = np.arange(2 * t)
    nf = np.maximum(n, 1).astype(np.float32)
    large = REL_MAX_EXACT + (np.log(nf / np.float32(REL_MAX_EXACT))
                             / np.float32(math.log(REL_MAX_DIST / REL_MAX_EXACT))
                             * np.float32(REL_BUCKETS - REL_MAX_EXACT)).astype(np.int32)
    bucket = np.where(n < REL_MAX_EXACT, n, np.minimum(large, REL_BUCKETS - 1)).astype(np.int32)
    qry = np.arange(t)[None, :]
    key = np.arange(t)[:, None]
    return np.stack([bucket[np.maximum(qry - key, 0)], bucket[t + qry - key]])


def _build_bias(idx_ref, rb_ref, bias_ref):
    t = idx_ref.shape[1]
    causal = (lax.broadcasted_iota(jnp.int32, (t, t), 0) <= lax.broadcasted_iota(jnp.int32, (t, t), 1))
    for h in range(N_HEADS):
        far = rb_ref[REL_BUCKETS - 1, h]
        for which in (0, 1):
            idx = idx_ref[which]
            acc = jnp.zeros((t, t), F32)
            for b in range(REL_BUCKETS - 1):
                acc = jnp.where(idx == b, (rb_ref[b, h] - far) * LOG2E, acc)
            if which == 0:
                bias_ref[h, t:2 * t, :] = jnp.where(causal, acc, NEG)
            else:
                bias_ref[h, 0:t, :] = acc


def _attn_init(m_ref, acc_ref):
    m_ref[...] = jnp.full(m_ref.shape, M_INIT, F32)
    acc_ref[...] = jnp.zeros(acc_ref.shape, F32)


def _softmax_step(s, h, vt_h, m_ref, acc_ref):
    m_old = m_ref[h]
    m_new = jnp.maximum(m_old, jnp.max(s, axis=0, keepdims=True))
    p = jnp.exp2(s - m_new)
    m_ref[h] = m_new
    acc_ref[h] = jnp.exp2(m_old - m_new) * acc_ref[h] + _dot(vt_h, p.astype(BF16))


def _fixed_step(s, h, vt_h, acc_ref):
    acc_ref[h] = acc_ref[h] + _dot(vt_h, jnp.exp2(s).astype(BF16))


def _heads_pipelined(logits, vt_slot, m_ref, acc_ref, online):
    ahead = [logits(h) for h in range(HEADS_AHEAD)]
    for h in range(N_HEADS):
        if h + HEADS_AHEAD < N_HEADS:
            ahead.append(logits(h + HEADS_AHEAD))
        if online:
            _softmax_step(ahead[h], h, vt_slot(h), m_ref, acc_ref)
        else:
            _fixed_step(ahead[h], h, vt_slot(h), acc_ref)


FAR_QUAD = ("far", "far", "far", "far")
FAR_PAIR = ("far", "far")
FAR_ONE = ("far",)
NEAR_DIAG = ("near", "diag")
DIAG_ONE = ("diag",)


def _attend_tiles(kt, kinds, q_slot, k_ref, vt_ref, bias_ref, mask_add, m_ref, acc_ref, online):
    t = ATTN_TILE
    n = len(kinds)
    rows = pl.ds(pl.multiple_of(kt * t, t), n * t)

    def logits(h):
        s = _dot_nt(k_ref[rows, _slot(h)], q_slot(h))
        if kinds == NEAR_DIAG:
            s = s + bias_ref[h]
        elif kinds == DIAG_ONE:
            s = s + bias_ref[h, t:2 * t, :]
        if mask_add is not None:
            s = s + mask_add
        return s

    def vt_slot(h):
        parts = [vt_ref[kt + a, _vt_rows(h), :] for a in range(n)]
        return parts[0] if n == 1 else jnp.concatenate(parts, axis=1)

    _heads_pipelined(logits, vt_slot, m_ref, acc_ref, online)


def _for_causal_tiles(i, tiles, far_group):
    n_far = jnp.maximum(i - 1, 0)
    assert far_group in (2, 4)

    def group(j, carry):
        tiles(far_group * j, FAR_QUAD if far_group == 4 else FAR_PAIR)
        return carry

    lax.fori_loop(0, n_far // far_group, group, 0)

    if far_group == 4:
        @pl.when(n_far % 4 >= 2)
        def _():
            tiles((n_far // 4) * 4, FAR_PAIR)

    @pl.when(n_far % 2 == 1)
    def _():
        tiles(n_far - 1, FAR_ONE)

    @pl.when(i >= 1)
    def _():
        tiles(i - 1, NEAR_DIAG)

    @pl.when(i == 0)
    def _():
        tiles(i, DIAG_ONE)


def _attend_either(ok_ref, attend):
    @pl.when(ok_ref[0] != 0)
    def _():
        attend(False)

    @pl.when(ok_ref[0] == 0)
    def _():
        attend(True)


def _attn_finish(o_ref, acc_ref):
    parts = []
    for h in range(N_HEADS):
        acc = acc_ref[h]
        parts.append(acc[0:HEAD_DIM] * (1.0 / acc[FEAT0:FEAT0 + 1]))
    o_ref[...] = jnp.concatenate(parts, axis=0).T.astype(BF16)


def _smem_spec():
    return pl.BlockSpec(memory_space=pltpu.SMEM)


def _moba_kernel(rb_ref, ok_ref, q_ref, k_ref, vt_ref, km_ref, idx_ref, o_ref,
                 acc_ref, m_ref, qs_ref, bias_ref, *, n_blocks):
    i = pl.program_id(1)
    t = ATTN_TILE
    nbp = -(-n_blocks // 8) * 8

    @pl.when((pl.program_id(0) == 0) & (i == 0))
    def _():
        _build_bias(idx_ref, rb_ref, bias_ref)

    _attn_init(m_ref, acc_ref)

    blk = lax.broadcasted_iota(jnp.int32, (nbp, t), 0)
    past = blk < i
    for h in range(N_HEADS):
        qh = q_ref[:, _slot(h)]
        km_hi, km_lo = _split_bf16(km_ref[:, _slot(h)])
        gate_t = _dot_nt(km_hi, qh) + _dot_nt(km_lo, qh)
        g = jnp.where(past, gate_t[FEAT0:FEAT0 + nbp, :], NEG)
        rank = jnp.zeros((nbp, t), F32)
        for j in range(n_blocks):
            gj = g[j:j + 1, :]
            beats = (gj > g) | ((gj == g) & (blk > j))
            rank = rank + jnp.where(beats, 1.0, 0.0)
        flag = jnp.where(past & (rank >= MOBA_TOPK), NEG, 0.0)
        flag_t = jnp.concatenate([jnp.zeros((FEAT0, t), F32), flag,
                                  jnp.zeros((SLOT - FEAT0 - nbp, t), F32)], axis=0)
        qs_ref[:, _slot(h)] = (qh.astype(F32) + flag_t.T).astype(BF16)

    def attend(online):
        def tiles(kt, kinds):
            _attend_tiles(kt, kinds, lambda h: qs_ref[:, _slot(h)], k_ref, vt_ref, bias_ref, None,
                          m_ref, acc_ref, online)
        _for_causal_tiles(i, tiles, MOBA_FAR_GROUP)

    _attend_either(ok_ref, attend)
    _attn_finish(o_ref, acc_ref)


def _moba(q, k, vt, kmean, rel_bias, fixed_ok, *, bsz):
    n = q.shape[0]
    seq = n // bsz
    t = ATTN_TILE
    assert seq % t == 0 and t == MOBA_BLOCK
    nq = seq // t
    assert FEAT0 + nq <= SHIFT_LANE
    km = jnp.pad(kmean.reshape(bsz, nq, SLOT_WIDTH), ((0, 0), (FEAT0, SLOT - FEAT0 - nq), (0, 0)))
    r3 = lambda a: a.reshape(bsz, seq, SLOT_WIDTH)
    seq_spec = pl.BlockSpec((None, seq, SLOT_WIDTH), lambda b, i: (b, 0, 0))
    out = pl.pallas_call(
        functools.partial(_moba_kernel, n_blocks=nq),
        grid=(bsz, nq),
        in_specs=[_smem_spec(), _smem_spec(),
                  pl.BlockSpec((None, t, SLOT_WIDTH), lambda b, i: (b, i, 0)), seq_spec,
                  pl.BlockSpec((nq, N_HEADS * VT_ROWS, t), lambda b, i: (b, 0, 0)),
                  pl.BlockSpec((None, SLOT, SLOT_WIDTH), lambda b, i: (b, 0, 0)),
                  _const_spec((2, t, t))],
        out_specs=pl.BlockSpec((None, t, ATTN_WIDTH), lambda b, i: (b, i, 0)),
        out_shape=jax.ShapeDtypeStruct((bsz, seq, ATTN_WIDTH), BF16),
        scratch_shapes=[pltpu.VMEM((N_HEADS, VT_ROWS, t), F32), pltpu.VMEM((N_HEADS, 1, t), F32),
                        pltpu.VMEM((t, SLOT_WIDTH), BF16),
                        pltpu.VMEM((N_HEADS, 2 * t, t), F32)],
        compiler_params=_cparams(("arbitrary", "arbitrary")),
        name="moba_attention",
    )(rel_bias, fixed_ok, r3(q), r3(k), vt, km, jnp.asarray(_bucket_tiles(t)))
    return out.reshape(n, ATTN_WIDTH)


def _dsa_kernel(rb_ref, ok_ref, q_ref, k_ref, vt_ref, qi_ref, ki_ref, wit_ref, idx_ref, tri_ref, o_ref,
                acc_ref, m_ref, isc_ref, bc_ref, bias_ref, *, n_sel):
    i = pl.program_id(1)
    t = ATTN_TILE
    key = lax.broadcasted_iota(jnp.int32, (t, t), 0)
    qry = lax.broadcasted_iota(jnp.int32, (t, t), 1)
    lane128 = lax.broadcasted_iota(jnp.int32, (1, 128), 1)

    @pl.when((pl.program_id(0) == 0) & (i == 0))
    def _():
        _build_bias(idx_ref, rb_ref, bias_ref)

    _attn_init(m_ref, acc_ref)

    def index_tiles(kt, n, diag):
        rows = pl.ds(pl.multiple_of(kt * t, t), n * t)
        ki2 = ki_ref[rows, :]
        acc = jnp.zeros((n * t, t), F32)
        for pr in range(IDX_HEADS // 2):
            q2 = qi_ref[:, pr * 128:(pr + 1) * 128]
            for half in range(2):
                h = 2 * pr + half
                hm = (lane128 >= half * IDX_DIM) & (lane128 < (half + 1) * IDX_DIM)
                s = _dot_nt(ki2, jnp.where(hm, q2, jnp.zeros((), BF16)))
                acc = acc + jnp.maximum(s, 0.0) * wit_ref[h:h + 1, :]
        if diag:
            acc = jnp.where(key <= qry, acc, -jnp.inf)
        for a in range(n):
            isc_ref[kt + a] = acc[a * t:(a + 1) * t]

    def index_pair(j, carry):
        index_tiles(2 * j, 2, False)
        return carry

    lax.fori_loop(0, i // 2, index_pair, 0)

    @pl.when(i % 2 == 1)
    def _():
        index_tiles(i - 1, 1, False)

    index_tiles(i, 1, True)

    def fold8(x, op):
        return op(x.reshape(t // 8, 8, t), axis=0)

    def minmax_body(kt, carry):
        mn, mx = carry
        x = isc_ref[kt]
        mx = jnp.maximum(mx, fold8(x, jnp.max))
        mn = jnp.minimum(mn, fold8(jnp.where(x == -jnp.inf, jnp.inf, x), jnp.min))
        return mn, mx

    mn, mx = lax.fori_loop(0, i + 1, minmax_body,
                           (jnp.full((8, t), jnp.inf, F32), jnp.full((8, t), -jnp.inf, F32)))
    lo0 = jnp.min(mn, axis=0, keepdims=True)
    mx = jnp.max(mx, axis=0, keepdims=True)
    hi0 = mx + jnp.abs(mx) * 1e-3 + 1e-30
    n_valid = (i * t + 1 + lax.broadcasted_iota(jnp.int32, (1, t), 1)).astype(F32)
    want = jnp.minimum(n_valid, float(n_sel))

    def count_ge(thr):
        def one(kt, c):
            return c + fold8(jnp.where(isc_ref[kt] >= thr, 1.0, 0.0), jnp.sum)

        def pair(j, c):
            return one(2 * j + 1, one(2 * j, c))

        c = lax.fori_loop(0, (i + 1) // 2, pair, jnp.zeros((8, t), F32))
        c = lax.cond((i + 1) % 2 == 1, lambda c: one(i, c), lambda c: c, c)
        return jnp.sum(c, axis=0, keepdims=True)

    c_nonneg = count_ge(0.0)
    c_pos = count_ge(F32_TINY)
    above = want <= c_pos
    zero_tie = (want > c_pos) & (want <= c_nonneg)
    lo1 = jnp.where(above, F32_TINY, jnp.where(zero_tie, 0.0, lo0))
    hi1 = jnp.where(above, hi0, jnp.where(zero_tie, F32_TINY, 0.0))
    c_lo1 = jnp.where(above, c_pos, jnp.where(zero_tie, c_nonneg, n_valid))
    c_hi1 = jnp.where(above, 0.0, jnp.where(zero_tie, c_pos, c_nonneg))

    def bisect_step(_, carry):
        lo, hi, c_lo, c_hi = carry
        mid = 0.5 * (lo + hi)
        c_mid = count_ge(mid)
        up = c_mid >= want
        return (jnp.where(up, mid, lo), jnp.where(up, hi, mid),
                jnp.where(up, c_mid, c_lo), jnp.where(up, c_hi, c_mid))

    def bisect_cond(carry):
        it, _, _, c_lo, _ = carry
        return (it < BISECT_ITERS) & (jnp.max(jnp.where(zero_tie, 0.0, c_lo - want)) > 0.0)

    def bisect_body(carry):
        return (carry[0] + BISECT_CHECK_EVERY,) + lax.fori_loop(0, BISECT_CHECK_EVERY, bisect_step, carry[1:])

    _, lo, hi, c_lo, c_hi = lax.while_loop(bisect_cond, bisect_body, (0, lo1, hi1, c_lo1, c_hi1))
    need = want - c_hi
    bc_ref[...] = jnp.zeros(bc_ref.shape, F32)

    def select_mask(kt):
        x = isc_ref[kt]
        band = jnp.where((x >= lo) & (x < hi), 1.0, 0.0)
        before = bc_ref[...] + _dot(tri_ref[...], band.astype(BF16))
        bc_ref[...] = bc_ref[...] + jnp.sum(band, axis=0, keepdims=True)
        return jnp.where((x >= hi) | ((band > 0.0) & (before < need)), 0.0, NEG)

    def attend(online):
        def tiles(kt, kinds):
            masks = [select_mask(kt + a) for a in range(len(kinds))]
            mask_add = masks[0] if len(masks) == 1 else jnp.concatenate(masks, axis=0)
            _attend_tiles(kt, kinds, lambda h: q_ref[:, _slot(h)], k_ref, vt_ref, bias_ref, mask_add,
                          m_ref, acc_ref, online)
        _for_causal_tiles(i, tiles, DSA_FAR_GROUP)

    _attend_either(ok_ref, attend)
    _attn_finish(o_ref, acc_ref)


def _dsa(q, k, vt, qi, ki2, wit, rel_bias, fixed_ok, *, bsz):
    n = q.shape[0]
    seq = n // bsz
    t = ATTN_TILE
    assert seq % t == 0
    nq = seq // t
    n_sel = min(DSA_TOPK_MAX, seq // 4)
    tri = (jnp.arange(t)[None, :] < jnp.arange(t)[:, None]).astype(BF16)
    r3 = lambda a: a.reshape(bsz, seq, a.shape[-1])
    tile_spec = lambda width: pl.BlockSpec((None, t, width), lambda b, i: (b, i, 0))
    seq_spec = lambda width: pl.BlockSpec((None, seq, width), lambda b, i: (b, 0, 0))
    out = pl.pallas_call(
        functools.partial(_dsa_kernel, n_sel=n_sel),
        grid=(bsz, nq),
        in_specs=[_smem_spec(), _smem_spec(), tile_spec(SLOT_WIDTH), seq_spec(SLOT_WIDTH),
                  pl.BlockSpec((nq, N_HEADS * VT_ROWS, t), lambda b, i: (b, 0, 0)),
                  tile_spec(ATTN_WIDTH), seq_spec(128),
                  pl.BlockSpec((IDX_HEADS, t), lambda b, i: (0, b * nq + i)),
                  _const_spec((2, t, t)), _const_spec((t, t))],
        out_specs=tile_spec(ATTN_WIDTH),
        out_shape=jax.ShapeDtypeStruct((bsz, seq, ATTN_WIDTH), BF16),
        scratch_shapes=[pltpu.VMEM((N_HEADS, VT_ROWS, t), F32), pltpu.VMEM((N_HEADS, 1, t), F32),
                        pltpu.VMEM((nq, t, t), F32), pltpu.VMEM((1, t), F32),
                        pltpu.VMEM((N_HEADS, 2 * t, t), F32)],
        compiler_params=_cparams(("arbitrary", "arbitrary")),
        name="dsa_attention",
    )(rel_bias, fixed_ok, r3(q), r3(k), vt, r3(qi), r3(ki2), wit, jnp.asarray(_bucket_tiles(t)), tri)
    return out.reshape(n, ATTN_WIDTH)


def kernel(x, rel_bias, ffn1_norm, ffn1_w_gate, ffn1_w_up, ffn1_w_down, mix_norm, ffn2_norm, ffn2_w_gate, ffn2_w_up, ffn2_w_down, ev_w_in, ev_conv_w, ev_conv_b, ev_ra_w, ev_ra_b, ev_ix_w, ev_ix_b, ev_lambda, ev_q_norm, ev_k_norm, ev_w_out, od_w_in, od_dw_w, od_dw_b, od_ln_g, od_ln_b, od_q_norm, od_k_norm, od_w_out):
    bsz, seq, d = x.shape
    depth = ffn1_norm.shape[0]
    h = x.reshape(bsz * seq, d)
    for i in range(depth):
        h = _ffn(h, ffn1_norm, ffn1_w_gate, ffn1_w_up, ffn1_w_down, i)
        j = i // 2
        if i % 2 == 0:
            k_shift, fixed_ok = _logit_bounds(ev_q_norm[j], ev_k_norm[j], rel_bias)
            gate, xr, q, k, vt, kmean = _mixin_even(h, mix_norm[i], ev_w_in[j], ev_q_norm[j], ev_k_norm[j],
                                                    k_shift, n_blocks=seq // MOBA_BLOCK)
            ya = _lru(gate, xr, ev_conv_w[j], ev_conv_b[j], ev_ra_w[j], ev_ra_b[j],
                      ev_ix_w[j], ev_ix_b[j], ev_lambda[j], bsz=bsz)
            yb = _moba(q, k, vt, kmean, rel_bias, fixed_ok, bsz=bsz)
            mix = (ya, yb, ev_w_out[j])
        else:
            k_shift, fixed_ok = _logit_bounds(od_q_norm[j], od_k_norm[j], rel_bias)
            yc, q, k, vt, qi, ki2, wit = _mixin_odd(h, mix_norm[i], od_w_in[j], od_q_norm[j], od_k_norm[j],
                                                    k_shift, od_dw_w[j], od_dw_b[j], od_ln_g[j], od_ln_b[j],
                                                    bsz=bsz)
            yd = _dsa(q, k, vt, qi, ki2, wit, rel_bias, fixed_ok, bsz=bsz)
            mix = (yc, yd, od_w_out[j])
        h = _ffn(h, ffn2_norm, ffn2_w_gate, ffn2_w_up, ffn2_w_down, i, mix)
    return h.reshape(bsz, seq, d)
```

```python
import functools
import math

import numpy as np
import jax
import jax.numpy as jnp
from jax import lax
from jax.experimental import pallas as pl
from jax.experimental.pallas import tpu as pltpu

F32 = jnp.float32
BF16 = jnp.bfloat16

N_HEADS = 8
HEAD_DIM = 64
ATTN_WIDTH = N_HEADS * HEAD_DIM
LRU_C = 8.0
MOBA_BLOCK = 256
MOBA_TOPK = 3
IDX_HEADS = 8
IDX_DIM = 64
DSA_TOPK_MAX = 256
REL_BUCKETS = 32
REL_MAX_EXACT = REL_BUCKETS // 2
REL_MAX_DIST = 128
EPS = 1e-6
NEG = -1e30
M_INIT = -1e29
ATTN_TILE = 256
SUBLANES = 8
CONV_ROWS = 64
SLOT = 128
SLOT_WIDTH = N_HEADS * SLOT
FEAT0 = HEAD_DIM
SHIFT_LANE = SLOT - 1
LOGIT_RANGE_MAX = 120.0
VT_ROWS = 80
MOBA_FAR_GROUP = 4
DSA_FAR_GROUP = 2
HEADS_AHEAD = 4
BISECT_ITERS = 32
BISECT_CHECK_EVERY = 4
F32_TINY = float(np.finfo(np.float32).tiny)
LOG2E = math.log2(math.e)
Q_SCALE = HEAD_DIM ** -0.5 * LOG2E
VMEM_LIMIT = 56 * 1024 * 1024
FFN_VMEM_LIMIT = 60 * 1024 * 1024


def _cparams(sem):
    return pltpu.CompilerParams(dimension_semantics=sem, vmem_limit_bytes=VMEM_LIMIT)


def _dot(a, b):
    return jnp.dot(a, b, preferred_element_type=F32)


def _dot_nt(a, b):
    return lax.dot_general(a, b, (((1,), (1,)), ((), ())), preferred_element_type=F32)


def _split_bf16(x):
    hi = x.astype(BF16)
    lo = (x - hi.astype(F32)).astype(BF16)
    return hi, lo


def _rms_rows(x, g):
    return x * lax.rsqrt(jnp.mean(x * x, axis=-1, keepdims=True) + EPS) * g


def _const_spec(shape):
    nd = len(shape)
    return pl.BlockSpec(shape, lambda *_: (0,) * nd, pipeline_mode=pl.Buffered(1))


def _slot(h):
    return slice(h * SLOT, (h + 1) * SLOT)


def _vt_rows(h):
    return slice(h * VT_ROWS, (h + 1) * VT_ROWS)


def _to_slots(x):
    low = lax.broadcasted_iota(jnp.int32, (1, SLOT), 1) < HEAD_DIM
    slots = []
    for p in range(N_HEADS // 2):
        chunk = x[:, p * SLOT:(p + 1) * SLOT]
        slots.append(jnp.where(low, chunk, 0.0))
        slots.append(jnp.where(low, pltpu.roll(chunk, HEAD_DIM, 1), 0.0))
    return slots


def _ffn_kernel(*refs, n_chunks, has_mix):
    if has_mix:
        x_ref, ya_ref, yb_ref, wo_ref, g_ref, wg_ref, wu_ref, wd_ref, o_ref, wg_s, wu_s, wd_s = refs
    else:
        x_ref, g_ref, wg_ref, wu_ref, wd_ref, o_ref, wg_s, wu_s, wd_s = refs
    step = pl.program_id(0)

    @pl.when(step < n_chunks)
    def _():
        wg_s[step] = wg_ref[...].astype(BF16)
        wu_s[step] = wu_ref[...].astype(BF16)
        wd_s[step] = wd_ref[...].astype(BF16)

    @pl.when(step >= n_chunks)
    def _():
        x = x_ref[...]
        if has_mix:
            w = ya_ref.shape[1]
            x = x + _dot(ya_ref[...], wo_ref[0:w, :]) + _dot(yb_ref[...], wo_ref[w:2 * w, :])
        hn = _rms_rows(x, g_ref[...]).astype(BF16)
        acc = jnp.zeros(x.shape, F32)
        for c in range(n_chunks):
            gt = _dot(hn, wg_s[c])
            ut = _dot(hn, wu_s[c])
            a = (gt * jax.nn.sigmoid(gt) * ut).astype(BF16)
            acc = acc + _dot(a, wd_s[c])
        o_ref[...] = x + 0.5 * acc


def _ffn(x, g, wg, wu, wd, layer, mix=None, *, tm=1024, ff_chunk=256):
    n, d = x.shape
    d_ff = wg.shape[2]
    assert n % tm == 0 and d_ff % ff_chunk == 0
    nc = d_ff // ff_chunk
    tile = lambda s: (jnp.maximum(s - nc, 0), 0)
    chunk_col = lambda s: (layer, 0, jnp.minimum(s, nc - 1))
    chunk_row = lambda s: (layer, jnp.minimum(s, nc - 1), 0)
    in_specs = [pl.BlockSpec((tm, d), tile)]
    args = [x]
    scratch = [pltpu.VMEM((nc, d, ff_chunk), BF16), pltpu.VMEM((nc, d, ff_chunk), BF16),
               pltpu.VMEM((nc, ff_chunk, d), BF16)]
    if mix is not None:
        ya, yb, w_out = mix
        w = ya.shape[1]
        assert w_out.shape == (2 * w, d)
        in_specs += [pl.BlockSpec((tm, w), tile), pl.BlockSpec((tm, w), tile), _const_spec((2 * w, d))]
        args += [ya, yb, w_out.astype(BF16)]
    in_specs += [_const_spec((1, d)), pl.BlockSpec((None, d, ff_chunk), chunk_col),
                 pl.BlockSpec((None, d, ff_chunk), chunk_col), pl.BlockSpec((None, ff_chunk, d), chunk_row)]
    args += [g[layer].reshape(1, d), wg, wu, wd]
    return pl.pallas_call(
        functools.partial(_ffn_kernel, n_chunks=nc, has_mix=mix is not None),
        grid=(nc + n // tm,),
        in_specs=in_specs,
        out_specs=pl.BlockSpec((tm, d), tile),
        out_shape=jax.ShapeDtypeStruct((n, d), F32),
        scratch_shapes=scratch,
        compiler_params=pltpu.CompilerParams(dimension_semantics=("arbitrary",),
                                             vmem_limit_bytes=FFN_VMEM_LIMIT),
        name="ffn_mix" if mix is not None else "ffn",
    )(*args)


def _slot_rms(xs, g):
    ms = jnp.sum(xs * xs, axis=-1, keepdims=True) * (1.0 / HEAD_DIM)
    return xs * lax.rsqrt(ms + EPS) * g


def _store_q_slots(z, gain_ref, q_o):
    lane = lax.broadcasted_iota(jnp.int32, (1, SLOT), 1)
    for h, zs in enumerate(_to_slots(z)):
        qs = _slot_rms(zs, gain_ref[...]) * Q_SCALE
        q_o[:, _slot(h)] = jnp.where(lane == SHIFT_LANE, 1.0, qs).astype(BF16)


def _store_k_slots(z, gain_ref, shift_ref, k_o, one_lanes):
    slots = [_slot_rms(zs, gain_ref[...]) for zs in _to_slots(z)]
    for h, ks in enumerate(slots):
        ks = ks if one_lanes is None else jnp.where(one_lanes, 1.0, ks)
        k_o[:, _slot(h)] = (ks + shift_ref[:, _slot(h)]).astype(BF16)
    return slots


def _store_vt_slots(zt, v_o):
    pad = VT_ROWS - HEAD_DIM
    tail = (lax.broadcasted_iota(jnp.int32, (pad, ATTN_TILE), 0) == 0).astype(BF16)
    for h in range(N_HEADS):
        for r in range(zt.shape[1] // ATTN_TILE):
            feat = zt[h * HEAD_DIM:(h + 1) * HEAD_DIM, r * ATTN_TILE:(r + 1) * ATTN_TILE]
            v_o[r, h * VT_ROWS:h * VT_ROWS + HEAD_DIM, :] = feat.astype(BF16)
            v_o[r, h * VT_ROWS + HEAD_DIM:(h + 1) * VT_ROWS, :] = tail


def _pipelined(stages):
    nxt = stages[0][0]()
    for j, (_, consume) in enumerate(stages):
        cur = nxt
        if j + 1 < len(stages):
            nxt = stages[j + 1][0]()
        consume(cur)


def _mixin_even_kernel(x_ref, g_ref, w_ref, wvt_ref, qn_ref, kn_ref, ks_ref,
                       gate_o, xr_o, q_o, k_o, v_o, km_o, *, n_blocks):
    tm = x_ref.shape[0]
    hn = _rms_rows(x_ref[...], g_ref[...]).astype(BF16)
    w = ATTN_WIDTH
    proj = lambda c: (lambda: _dot(hn, w_ref[:, c * w:(c + 1) * w]))
    blocks_per_tile = tm // MOBA_BLOCK
    row_block = lax.broadcasted_iota(jnp.int32, (tm, 1), 0) // MOBA_BLOCK
    block = (pl.program_id(0) * blocks_per_tile + row_block) % n_blocks
    block_lane = lax.broadcasted_iota(jnp.int32, (1, SLOT), 1) == FEAT0 + block

    def store_k(z):
        for h, ks in enumerate(_store_k_slots(z, kn_ref, ks_ref, k_o, block_lane)):
            for r in range(blocks_per_tile):
                km_o[r, :, _slot(h)] = jnp.mean(ks[r * MOBA_BLOCK:(r + 1) * MOBA_BLOCK], axis=0, keepdims=True)

    def store(ref):
        def consume(z):
            ref[...] = z
        return consume

    _pipelined([(proj(2), lambda z: _store_q_slots(z, qn_ref, q_o)), (proj(3), store_k),
                (lambda: _dot_nt(wvt_ref[...], hn), lambda zt: _store_vt_slots(zt, v_o)),
                (proj(0), store(gate_o)), (proj(1), store(xr_o))])


def _vt_spec(tm):
    return pl.BlockSpec((tm // ATTN_TILE, N_HEADS * VT_ROWS, ATTN_TILE), lambda i: (i, 0, 0))


def _vt_shape(n):
    return jax.ShapeDtypeStruct((n // ATTN_TILE, N_HEADS * VT_ROWS, ATTN_TILE), BF16)


def _logit_bounds(q_norm, k_norm, rel_bias):
    qk = HEAD_DIM * jnp.max(jnp.abs(q_norm)) * jnp.max(jnp.abs(k_norm)) * Q_SCALE * 1.02
    rel = (rel_bias - rel_bias[REL_BUCKETS - 1]) * LOG2E
    bound = qk + jnp.maximum(jnp.max(rel, axis=0), 0.0) + 1.0
    lowest = -qk + jnp.minimum(jnp.min(rel, axis=0), 0.0)
    fixed_ok = jnp.all(bound - lowest <= LOGIT_RANGE_MAX).astype(jnp.int32).reshape(1)
    k_shift = jnp.zeros((N_HEADS, SLOT), F32).at[:, SHIFT_LANE].set(-bound).reshape(1, SLOT_WIDTH)
    return k_shift, fixed_ok


def _slot_gain(g):
    return jnp.pad(g.astype(F32), (0, SLOT - HEAD_DIM)).reshape(1, SLOT)


def _mixin_even(x, g, w_in, q_norm, k_norm, k_shift, *, n_blocks, tm=512):
    n, d = x.shape
    w = ATTN_WIDTH
    assert n % tm == 0 and tm % MOBA_BLOCK == 0 and tm % ATTN_TILE == 0 and w_in.shape[1] == 5 * w
    assert FEAT0 + n_blocks <= SHIFT_LANE
    row = lambda i: (i, 0)
    blk = tm // MOBA_BLOCK
    slot_shape = jax.ShapeDtypeStruct((n, SLOT_WIDTH), BF16)
    return pl.pallas_call(
        functools.partial(_mixin_even_kernel, n_blocks=n_blocks),
        grid=(n // tm,),
        in_specs=[pl.BlockSpec((tm, d), row), _const_spec((1, d)), _const_spec((d, 4 * w)),
                  _const_spec((w, d)), _const_spec((1, SLOT)), _const_spec((1, SLOT)),
                  _const_spec((1, SLOT_WIDTH))],
        out_specs=[pl.BlockSpec((tm, w), row)] * 2 + [pl.BlockSpec((tm, SLOT_WIDTH), row)] * 2
        + [_vt_spec(tm), pl.BlockSpec((blk, 1, SLOT_WIDTH), lambda i: (i, 0, 0))],
        out_shape=[jax.ShapeDtypeStruct((n, w), F32), jax.ShapeDtypeStruct((n, w), F32),
                   slot_shape, slot_shape, _vt_shape(n),
                   jax.ShapeDtypeStruct((n // MOBA_BLOCK, 1, SLOT_WIDTH), F32)],
        compiler_params=_cparams(("parallel",)),
        name="mixin_even",
    )(x, g.reshape(1, d), w_in[:, :4 * w].astype(BF16), w_in[:, 4 * w:].T.astype(BF16),
      _slot_gain(q_norm), _slot_gain(k_norm), k_shift)


def _conv_ln_silu(c, first, w_ref, b_ref, g_ref, beta_ref, o_ref, cbuf, sh_ref, *, halo):
    ts = c.shape[0]
    width = w_ref.shape[0]

    @pl.when(first)
    def _():
        cbuf[0:halo, :] = jnp.zeros((halo, cbuf.shape[1]), F32)

    cbuf[halo:halo + ts, :] = c
    base = halo - (width - 1)
    span = sh_ref.shape[1]
    for r in range(1, SUBLANES):
        sh_ref[r - 1] = cbuf[r:r + span, :]

    def chunk(c0):
        y = jnp.broadcast_to(b_ref[...], (CONV_ROWS, cbuf.shape[1]))
        for k in range(width):
            r = (base + k) % SUBLANES
            u0 = c0 + base + k - r
            win = cbuf[u0:u0 + CONV_ROWS, :] if r == 0 else sh_ref[r - 1, u0:u0 + CONV_ROWS, :]
            y = y + w_ref[k:k + 1, :] * win
        mu = jnp.mean(y, axis=-1, keepdims=True)
        yc = y - mu
        var = jnp.mean(yc * yc, axis=-1, keepdims=True)
        z = yc * lax.rsqrt(var + EPS) * g_ref[...] + beta_ref[...]
        o_ref[c0:c0 + CONV_ROWS, :] = (z * jax.nn.sigmoid(z)).astype(BF16)

    def close():
        cbuf[0:halo, :] = cbuf[ts:ts + halo, :]

    return [functools.partial(chunk, c0) for c0 in range(0, ts, CONV_ROWS)], close


def _mixin_odd_kernel(x_ref, g_ref, w_ref, wvt_ref, wki_ref, wwit_ref, qn_ref, kn_ref, ks_ref,
                      dw_ref, db_ref, lg_ref, lb_ref,
                      yc_o, q_o, k_o, v_o, qi_o, ki_o, wi_o, cbuf, sh_ref, *, tiles_per_seq, halo):
    hn = _rms_rows(x_ref[...], g_ref[...]).astype(BF16)
    w = ATTN_WIDTH
    proj = lambda c, n=1: _dot(hn, w_ref[:, c * w:(c + n) * w])
    first = pl.program_id(0) % tiles_per_seq == 0

    z_glu = proj(0, 2)
    z_q = proj(2)
    chunks, close = _conv_ln_silu(z_glu[:, 0:w] * jax.nn.sigmoid(z_glu[:, w:2 * w]), first,
                                  dw_ref, db_ref, lg_ref, lb_ref, yc_o, cbuf, sh_ref, halo=halo)
    per_stage = -(-len(chunks) // 4)
    run_chunks = lambda j: [f() for f in chunks[j * per_stage:(j + 1) * per_stage]]
    z_k = proj(3)
    _store_q_slots(z_q, qn_ref, q_o)
    run_chunks(0)
    z_vt = _dot_nt(wvt_ref[...], hn)
    _store_k_slots(z_k, kn_ref, ks_ref, k_o, None)
    run_chunks(1)
    z_qi = proj(4)
    _store_vt_slots(z_vt, v_o)
    run_chunks(2)
    z_ki = _dot(hn, wki_ref[...])
    z_wit = _dot_nt(wwit_ref[...], hn)
    qi_o[...] = z_qi.astype(BF16)
    run_chunks(3)
    close()
    ki_o[...] = z_ki.astype(BF16)
    wi_o[...] = z_wit * (IDX_DIM ** -0.5 * IDX_HEADS ** -0.5)


def _mixin_odd(x, g, w_in, q_norm, k_norm, k_shift, dw_w, dw_b, ln_g, ln_b, *, bsz, tm=512, halo=32):
    n, d = x.shape
    w = ATTN_WIDTH
    width = dw_w.shape[0]
    assert n % tm == 0 and tm % ATTN_TILE == 0 and w_in.shape[1] == 6 * w + IDX_DIM + IDX_HEADS
    assert (n // bsz) % tm == 0 and width - 1 <= halo <= tm and halo % SUBLANES == 0 and tm % CONV_ROWS == 0
    w_main = jnp.concatenate([w_in[:, :4 * w], w_in[:, 5 * w:6 * w]], axis=1).astype(BF16)
    w_vt = w_in[:, 4 * w:5 * w].T.astype(BF16)
    w_ki = w_in[:, 6 * w:6 * w + IDX_DIM]
    w_ki2 = jnp.concatenate([w_ki, w_ki], axis=1).astype(BF16)
    w_wit = w_in[:, 6 * w + IDX_DIM:].T.astype(BF16)
    row = lambda i: (i, 0)
    vec = lambda v: v.reshape(1, w).astype(F32)
    slot_shape = jax.ShapeDtypeStruct((n, SLOT_WIDTH), BF16)
    return pl.pallas_call(
        functools.partial(_mixin_odd_kernel, tiles_per_seq=(n // bsz) // tm, halo=halo),
        grid=(n // tm,),
        in_specs=[pl.BlockSpec((tm, d), row), _const_spec((1, d)), _const_spec((d, 5 * w)),
                  _const_spec((w, d)), _const_spec((d, 2 * IDX_DIM)), _const_spec((IDX_HEADS, d)),
                  _const_spec((1, SLOT)), _const_spec((1, SLOT)), _const_spec((1, SLOT_WIDTH)),
                  _const_spec((width, w)), _const_spec((1, w)), _const_spec((1, w)), _const_spec((1, w))],
        out_specs=[pl.BlockSpec((tm, w), row)] + [pl.BlockSpec((tm, SLOT_WIDTH), row)] * 2
        + [_vt_spec(tm), pl.BlockSpec((tm, w), row), pl.BlockSpec((tm, 128), row),
           pl.BlockSpec((IDX_HEADS, tm), lambda i: (0, i))],
        out_shape=[jax.ShapeDtypeStruct((n, w), BF16), slot_shape, slot_shape, _vt_shape(n),
                   jax.ShapeDtypeStruct((n, w), BF16),
                   jax.ShapeDtypeStruct((n, 128), BF16), jax.ShapeDtypeStruct((IDX_HEADS, n), F32)],
        scratch_shapes=[pltpu.VMEM((tm + halo, w), F32),
                        pltpu.VMEM((SUBLANES - 1, tm + halo - SUBLANES, w), F32)],
        compiler_params=_cparams(("arbitrary",)),
        name="mixin_odd",
    )(x, g.reshape(1, d), w_main, w_vt, w_ki2, w_wit, _slot_gain(q_norm), _slot_gain(k_norm), k_shift,
      dw_w, vec(dw_b), vec(ln_g), vec(ln_b))


def _lru_kernel(gate_ref, xr_ref, cw_ref, cb_ref, wa_ref, ba_ref, wx_ref, bx_ref, sp_ref,
                o_ref, xbuf, a_s, u_s, h_s, hc, *, ts):
    j = pl.program_id(1)

    @pl.when(j == 0)
    def _():
        xbuf[0:8, :] = jnp.zeros((8, xbuf.shape[1]), F32)
        hc[...] = jnp.zeros(hc.shape, F32)

    xbuf[8:8 + ts, :] = xr_ref[...]
    xc = cb_ref[...] + cw_ref[0:1, :] * xbuf[5:5 + ts, :]
    for k in range(1, 4):
        xc = xc + cw_ref[k:k + 1, :] * xbuf[5 + k:5 + k + ts, :]
    xbuf[0:8, :] = xbuf[ts:ts + 8, :]

    xcb = xc.astype(BF16)
    r = jax.nn.sigmoid(_dot(xcb, wa_ref[...]) + ba_ref[...])
    ig = jax.nn.sigmoid(_dot(xcb, wx_ref[...]) + bx_ref[...])
    log_a = -LRU_C * r * sp_ref[...]
    a = jnp.exp(log_a)
    a_s[...] = a
    u_s[...] = jnp.sqrt(-jnp.tanh(log_a) * (a * a + 1.0)) * (ig * xc)

    row = lax.broadcasted_iota(jnp.int32, (8, a_s.shape[1]), 0)

    def body(g, carry):
        r0 = pl.multiple_of(g * 8, 8)
        a = a_s[pl.ds(r0, 8), :]
        u = u_s[pl.ds(r0, 8), :]
        for s in (1, 2, 4):
            ok = row >= s
            a_sh = jnp.where(ok, pltpu.roll(a, s, 0), 1.0)
            u_sh = jnp.where(ok, pltpu.roll(u, s, 0), 0.0)
            u = a * u_sh + u
            a = a * a_sh
        h = a * carry + u
        h_s[pl.ds(r0, 8), :] = h
        return h[7:8, :]

    hc[...] = lax.fori_loop(0, ts // 8, body, hc[...], unroll=4)
    o_ref[...] = (h_s[...] * jax.nn.gelu(gate_ref[...])).astype(BF16)


def _block_diag(wb):
    nb, bs, _ = wb.shape
    eye = jnp.eye(nb, dtype=wb.dtype)
    return (eye[:, None, :, None] * wb[:, :, None, :]).reshape(nb * bs, nb * bs)


def _lru(gate, xr, conv_w, conv_b, ra_w, ra_b, ix_w, ix_b, lam, *, bsz, ts=256):
    n, w = xr.shape
    seq = n // bsz
    assert seq % ts == 0
    nt = seq // ts
    row = lambda b, j: (b * nt + j, 0)
    vec = lambda v: v.reshape(1, w).astype(F32)
    return pl.pallas_call(
        functools.partial(_lru_kernel, ts=ts),
        grid=(bsz, nt),
        in_specs=[pl.BlockSpec((ts, w), row), pl.BlockSpec((ts, w), row),
                  _const_spec((conv_w.shape[0], w)), _const_spec((1, w)),
                  _const_spec((w, w)), _const_spec((1, w)), _const_spec((w, w)), _const_spec((1, w)),
                  _const_spec((1, w))],
        out_specs=pl.BlockSpec((ts, w), row),
        out_shape=jax.ShapeDtypeStruct((n, w), BF16),
        scratch_shapes=[pltpu.VMEM((ts + 8, w), F32), pltpu.VMEM((ts, w), F32),
                        pltpu.VMEM((ts, w), F32), pltpu.VMEM((ts, w), F32), pltpu.VMEM((1, w), F32)],
        compiler_params=_cparams(("arbitrary", "arbitrary")),
        name="rg_lru",
    )(gate, xr, conv_w, vec(conv_b), _block_diag(ra_w).astype(BF16), vec(ra_b),
      _block_diag(ix_w).astype(BF16), vec(ix_b), vec(jax.nn.softplus(-lam)))


def _bucket_tiles(t):
    assert t > REL_MAX_DIST
    n = np.arange(2 * t)
    nf = np.maximum(n, 1).astype(np.float32)
    large = REL_MAX_EXACT + (np.log(nf / np.float32(REL_MAX_EXACT))
                             / np.float32(math.log(REL_MAX_DIST / REL_MAX_EXACT))
                             * np.float32(REL_BUCKETS - REL_MAX_EXACT)).astype(np.int32)
    bucket = np.where(n < REL_MAX_EXACT, n, np.minimum(large, REL_BUCKETS - 1)).astype(np.int32)
    qry = np.arange(t)[None, :]
    key = np.arange(t)[:, None]
    return np.stack([bucket[np.maximum(qry - key, 0)], bucket[t + qry - key]])


def _build_bias(idx_ref, rb_ref, bias_ref):
    t = idx_ref.shape[1]
    causal = (lax.broadcasted_iota(jnp.int32, (t, t), 0) <= lax.broadcasted_iota(jnp.int32, (t, t), 1))
    for h in range(N_HEADS):
        far = rb_ref[REL_BUCKETS - 1, h]
        for which in (0, 1):
            idx = idx_ref[which]
            acc = jnp.zeros((t, t), F32)
            for b in range(REL_BUCKETS - 1):
                acc = jnp.where(idx == b, (rb_ref[b, h] - far) * LOG2E, acc)
            if which == 0:
                bias_ref[h, t:2 * t, :] = jnp.where(causal, acc, NEG)
            else:
                bias_ref[h, 0:t, :] = acc


def _attn_init(m_ref, acc_ref):
    m_ref[...] = jnp.full(m_ref.shape, M_INIT, F32)
    acc_ref[...] = jnp.zeros(acc_ref.shape, F32)


def _softmax_step(s, h, vt_h, m_ref, acc_ref):
    m_old = m_ref[h]
    m_new = jnp.maximum(m_old, jnp.max(s, axis=0, keepdims=True))
    p = jnp.exp2(s - m_new)
    m_ref[h] = m_new
    acc_ref[h] = jnp.exp2(m_old - m_new) * acc_ref[h] + _dot(vt_h, p.astype(BF16))


def _fixed_step(s, h, vt_h, acc_ref):
    acc_ref[h] = acc_ref[h] + _dot(vt_h, jnp.exp2(s).astype(BF16))


def _heads_pipelined(logits, vt_slot, m_ref, acc_ref, online):
    ahead = [logits(h) for h in range(HEADS_AHEAD)]
    for h in range(N_HEADS):
        if h + HEADS_AHEAD < N_HEADS:
            ahead.append(logits(h + HEADS_AHEAD))
        if online:
            _softmax_step(ahead[h], h, vt_slot(h), m_ref, acc_ref)
        else:
            _fixed_step(ahead[h], h, vt_slot(h), acc_ref)


FAR_QUAD = ("far", "far", "far", "far")
FAR_PAIR = ("far", "far")
FAR_ONE = ("far",)
NEAR_DIAG = ("near", "diag")
DIAG_ONE = ("diag",)


def _attend_tiles(kt, kinds, q_slot, k_ref, vt_ref, bias_ref, mask_add, m_ref, acc_ref, online):
    t = ATTN_TILE
    n = len(kinds)
    rows = pl.ds(pl.multiple_of(kt * t, t), n * t)

    def logits(h):
        s = _dot_nt(k_ref[rows, _slot(h)], q_slot(h))
        if kinds == NEAR_DIAG:
            s = s + bias_ref[h]
        elif kinds == DIAG_ONE:
            s = s + bias_ref[h, t:2 * t, :]
        if mask_add is not None:
            s = s + mask_add
        return s

    def vt_slot(h):
        parts = [vt_ref[kt + a, _vt_rows(h), :] for a in range(n)]
        return parts[0] if n == 1 else jnp.concatenate(parts, axis=1)

    _heads_pipelined(logits, vt_slot, m_ref, acc_ref, online)


def _for_causal_tiles(i, tiles, far_group):
    n_far = jnp.maximum(i - 1, 0)
    assert far_group in (2, 4)

    def group(j, carry):
        tiles(far_group * j, FAR_QUAD if far_group == 4 else FAR_PAIR)
        return carry

    lax.fori_loop(0, n_far // far_group, group, 0)

    if far_group == 4:
        @pl.when(n_far % 4 >= 2)
        def _():
            tiles((n_far // 4) * 4, FAR_PAIR)

    @pl.when(n_far % 2 == 1)
    def _():
        tiles(n_far - 1, FAR_ONE)

    @pl.when(i >= 1)
    def _():
        tiles(i - 1, NEAR_DIAG)

    @pl.when(i == 0)
    def _():
        tiles(i, DIAG_ONE)


def _attend_either(ok_ref, attend):
    @pl.when(ok_ref[0] != 0)
    def _():
        attend(False)

    @pl.when(ok_ref[0] == 0)
    def _():
        attend(True)


def _attn_finish(o_ref, acc_ref):
    parts = []
    for h in range(N_HEADS):
        acc = acc_ref[h]
        parts.append(acc[0:HEAD_DIM] * (1.0 / acc[FEAT0:FEAT0 + 1]))
    o_ref[...] = jnp.concatenate(parts, axis=0).T.astype(BF16)


def _smem_spec():
    return pl.BlockSpec(memory_space=pltpu.SMEM)


def _moba_kernel(rb_ref, ok_ref, q_ref, k_ref, vt_ref, km_ref, idx_ref, o_ref,
                 acc_ref, m_ref, qs_ref, bias_ref, *, n_blocks):
    i = pl.program_id(1)
    t = ATTN_TILE
    nbp = -(-n_blocks // 8) * 8

    @pl.when((pl.program_id(0) == 0) & (i == 0))
    def _():
        _build_bias(idx_ref, rb_ref, bias_ref)

    _attn_init(m_ref, acc_ref)

    blk = lax.broadcasted_iota(jnp.int32, (nbp, t), 0)
    past = blk < i
    for h in range(N_HEADS):
        qh = q_ref[:, _slot(h)]
        km_hi, km_lo = _split_bf16(km_ref[:, _slot(h)])
        gate_t = _dot_nt(km_hi, qh) + _dot_nt(km_lo, qh)
        g = jnp.where(past, gate_t[FEAT0:FEAT0 + nbp, :], NEG)
        rank = jnp.zeros((nbp, t), F32)
        for j in range(n_blocks):
            gj = g[j:j + 1, :]
            beats = (gj > g) | ((gj == g) & (blk > j))
            rank = rank + jnp.where(beats, 1.0, 0.0)
        flag = jnp.where(past & (rank >= MOBA_TOPK), NEG, 0.0)
        flag_t = jnp.concatenate([jnp.zeros((FEAT0, t), F32), flag,
                                  jnp.zeros((SLOT - FEAT0 - nbp, t), F32)], axis=0)
        qs_ref[:, _slot(h)] = (qh.astype(F32) + flag_t.T).astype(BF16)

    def attend(online):
        def tiles(kt, kinds):
            _attend_tiles(kt, kinds, lambda h: qs_ref[:, _slot(h)], k_ref, vt_ref, bias_ref, None,
                          m_ref, acc_ref, online)
        _for_causal_tiles(i, tiles, MOBA_FAR_GROUP)

    _attend_either(ok_ref, attend)
    _attn_finish(o_ref, acc_ref)


def _moba(q, k, vt, kmean, rel_bias, fixed_ok, *, bsz):
    n = q.shape[0]
    seq = n // bsz
    t = ATTN_TILE
    assert seq % t == 0 and t == MOBA_BLOCK
    nq = seq // t
    assert FEAT0 + nq <= SHIFT_LANE
    km = jnp.pad(kmean.reshape(bsz, nq, SLOT_WIDTH), ((0, 0), (FEAT0, SLOT - FEAT0 - nq), (0, 0)))
    r3 = lambda a: a.reshape(bsz, seq, SLOT_WIDTH)
    seq_spec = pl.BlockSpec((None, seq, SLOT_WIDTH), lambda b, i: (b, 0, 0))
    out = pl.pallas_call(
        functools.partial(_moba_kernel, n_blocks=nq),
        grid=(bsz, nq),
        in_specs=[_smem_spec(), _smem_spec(),
                  pl.BlockSpec((None, t, SLOT_WIDTH), lambda b, i: (b, i, 0)), seq_spec,
                  pl.BlockSpec((nq, N_HEADS * VT_ROWS, t), lambda b, i: (b, 0, 0)),
                  pl.BlockSpec((None, SLOT, SLOT_WIDTH), lambda b, i: (b, 0, 0)),
                  _const_spec((2, t, t))],
        out_specs=pl.BlockSpec((None, t, ATTN_WIDTH), lambda b, i: (b, i, 0)),
        out_shape=jax.ShapeDtypeStruct((bsz, seq, ATTN_WIDTH), BF16),
        scratch_shapes=[pltpu.VMEM((N_HEADS, VT_ROWS, t), F32), pltpu.VMEM((N_HEADS, 1, t), F32),
                        pltpu.VMEM((t, SLOT_WIDTH), BF16),
                        pltpu.VMEM((N_HEADS, 2 * t, t), F32)],
        compiler_params=_cparams(("arbitrary", "arbitrary")),
        name="moba_attention",
    )(rel_bias, fixed_ok, r3(q), r3(k), vt, km, jnp.asarray(_bucket_tiles(t)))
    return out.reshape(n, ATTN_WIDTH)


def _dsa_kernel(rb_ref, ok_ref, q_ref, k_ref, vt_ref, qi_ref, ki_ref, wit_ref, idx_ref, tri_ref, o_ref,
                acc_ref, m_ref, isc_ref, bc_ref, bias_ref, *, n_sel):
    i = pl.program_id(1)
    t = ATTN_TILE
    key = lax.broadcasted_iota(jnp.int32, (t, t), 0)
    qry = lax.broadcasted_iota(jnp.int32, (t, t), 1)
    lane128 = lax.broadcasted_iota(jnp.int32, (1, 128), 1)

    @pl.when((pl.program_id(0) == 0) & (i == 0))
    def _():
        _build_bias(idx_ref, rb_ref, bias_ref)

    _attn_init(m_ref, acc_ref)

    def index_tiles(kt, n, diag):
        rows = pl.ds(pl.multiple_of(kt * t, t), n * t)
        ki2 = ki_ref[rows, :]
        acc = jnp.zeros((n * t, t), F32)
        for pr in range(IDX_HEADS // 2):
            q2 = qi_ref[:, pr * 128:(pr + 1) * 128]
            for half in range(2):
                h = 2 * pr + half
                hm = (lane128 >= half * IDX_DIM) & (lane128 < (half + 1) * IDX_DIM)
                s = _dot_nt(ki2, jnp.where(hm, q2, jnp.zeros((), BF16)))
                acc = acc + jnp.maximum(s, 0.0) * wit_ref[h:h + 1, :]
        if diag:
            acc = jnp.where(key <= qry, acc, -jnp.inf)
        for a in range(n):
            isc_ref[kt + a] = acc[a * t:(a + 1) * t]

    def index_pair(j, carry):
        index_tiles(2 * j, 2, False)
        return carry

    lax.fori_loop(0, i // 2, index_pair, 0)

    @pl.when(i % 2 == 1)
    def _():
        index_tiles(i - 1, 1, False)

    index_tiles(i, 1, True)
    isc_ref[i + 1] = jnp.full((t, t), -jnp.inf, F32)

    def fold8(x, op):
        return op(x.reshape(t // 8, 8, t), axis=0)

    def minmax_body(kt, carry):
        mn, mx = carry
        x = isc_ref[kt]
        mx = jnp.maximum(mx, fold8(x, jnp.max))
        mn = jnp.minimum(mn, fold8(jnp.where(x == -jnp.inf, jnp.inf, x), jnp.min))
        return mn, mx

    mn, mx = lax.fori_loop(0, i + 1, minmax_body,
                           (jnp.full((8, t), jnp.inf, F32), jnp.full((8, t), -jnp.inf, F32)))
    lo0 = jnp.min(mn, axis=0, keepdims=True)
    mx = jnp.max(mx, axis=0, keepdims=True)
    hi0 = mx + jnp.abs(mx) * 1e-3 + 1e-30
    n_valid = (i * t + 1 + lax.broadcasted_iota(jnp.int32, (1, t), 1)).astype(F32)
    want = jnp.minimum(n_valid, float(n_sel))

    def count_ge(thr):
        def one(kt, c):
            return c + fold8(jnp.where(isc_ref[kt] >= thr, 1.0, 0.0), jnp.sum)

        def pair(j, c):
            return one(2 * j + 1, one(2 * j, c))

        c = lax.fori_loop(0, (i + 2) // 2, pair, jnp.zeros((8, t), F32))
        return jnp.sum(c, axis=0, keepdims=True)

    c_nonneg = count_ge(0.0)
    c_pos = count_ge(F32_TINY)
    above = want <= c_pos
    zero_tie = (want > c_pos) & (want <= c_nonneg)
    lo1 = jnp.where(above, F32_TINY, jnp.where(zero_tie, 0.0, lo0))
    hi1 = jnp.where(above, hi0, jnp.where(zero_tie, F32_TINY, 0.0))
    c_lo1 = jnp.where(above, c_pos, jnp.where(zero_tie, c_nonneg, n_valid))
    c_hi1 = jnp.where(above, 0.0, jnp.where(zero_tie, c_pos, c_nonneg))

    def bisect_step(_, carry):
        lo, hi, c_lo, c_hi = carry
        mid = 0.5 * (lo + hi)
        c_mid = count_ge(mid)
        up = c_mid >= want
        return (jnp.where(up, mid, lo), jnp.where(up, hi, mid),
                jnp.where(up, c_mid, c_lo), jnp.where(up, c_hi, c_mid))

    def bisect_cond(carry):
        it, _, _, c_lo, _ = carry
        return (it < BISECT_ITERS) & (jnp.max(jnp.where(zero_tie, 0.0, c_lo - want)) > 0.0)

    def bisect_body(carry):
        return (carry[0] + BISECT_CHECK_EVERY,) + lax.fori_loop(0, BISECT_CHECK_EVERY, bisect_step, carry[1:])

    _, lo, hi, c_lo, c_hi = lax.while_loop(bisect_cond, bisect_body, (0, lo1, hi1, c_lo1, c_hi1))
    need = want - c_hi
    bc_ref[...] = jnp.zeros(bc_ref.shape, F32)

    def select_mask(kt):
        x = isc_ref[kt]
        band = jnp.where((x >= lo) & (x < hi), 1.0, 0.0)
        before = bc_ref[...] + _dot(tri_ref[...], band.astype(BF16))
        bc_ref[...] = bc_ref[...] + jnp.sum(band, axis=0, keepdims=True)
        return jnp.where((x >= hi) | ((band > 0.0) & (before < need)), 0.0, NEG)

    def attend(online):
        def tiles(kt, kinds):
            masks = [select_mask(kt + a) for a in range(len(kinds))]
            mask_add = masks[0] if len(masks) == 1 else jnp.concatenate(masks, axis=0)
            _attend_tiles(kt, kinds, lambda h: q_ref[:, _slot(h)], k_ref, vt_ref, bias_ref, mask_add,
                          m_ref, acc_ref, online)
        _for_causal_tiles(i, tiles, DSA_FAR_GROUP)

    _attend_either(ok_ref, attend)
    _attn_finish(o_ref, acc_ref)


def _dsa(q, k, vt, qi, ki2, wit, rel_bias, fixed_ok, *, bsz):
    n = q.shape[0]
    seq = n // bsz
    t = ATTN_TILE
    assert seq % t == 0
    nq = seq // t
    n_sel = min(DSA_TOPK_MAX, seq // 4)
    tri = (jnp.arange(t)[None, :] < jnp.arange(t)[:, None]).astype(BF16)
    r3 = lambda a: a.reshape(bsz, seq, a.shape[-1])
    tile_spec = lambda width: pl.BlockSpec((None, t, width), lambda b, i: (b, i, 0))
    seq_spec = lambda width: pl.BlockSpec((None, seq, width), lambda b, i: (b, 0, 0))
    out = pl.pallas_call(
        functools.partial(_dsa_kernel, n_sel=n_sel),
        grid=(bsz, nq),
        in_specs=[_smem_spec(), _smem_spec(), tile_spec(SLOT_WIDTH), seq_spec(SLOT_WIDTH),
                  pl.BlockSpec((nq, N_HEADS * VT_ROWS, t), lambda b, i: (b, 0, 0)),
                  tile_spec(ATTN_WIDTH), seq_spec(128),
                  pl.BlockSpec((IDX_HEADS, t), lambda b, i: (0, b * nq + i)),
                  _const_spec((2, t, t)), _const_spec((t, t))],
        out_specs=tile_spec(ATTN_WIDTH),
        out_shape=jax.ShapeDtypeStruct((bsz, seq, ATTN_WIDTH), BF16),
        scratch_shapes=[pltpu.VMEM((N_HEADS, VT_ROWS, t), F32), pltpu.VMEM((N_HEADS, 1, t), F32),
                        pltpu.VMEM((nq + 1, t, t), F32), pltpu.VMEM((1, t), F32),
                        pltpu.VMEM((N_HEADS, 2 * t, t), F32)],
        compiler_params=_cparams(("arbitrary", "arbitrary")),
        name="dsa_attention",
    )(rel_bias, fixed_ok, r3(q), r3(k), vt, r3(qi), r3(ki2), wit, jnp.asarray(_bucket_tiles(t)), tri)
    return out.reshape(n, ATTN_WIDTH)


def kernel(x, rel_bias, ffn1_norm, ffn1_w_gate, ffn1_w_up, ffn1_w_down, mix_norm, ffn2_norm, ffn2_w_gate, ffn2_w_up, ffn2_w_down, ev_w_in, ev_conv_w, ev_conv_b, ev_ra_w, ev_ra_b, ev_ix_w, ev_ix_b, ev_lambda, ev_q_norm, ev_k_norm, ev_w_out, od_w_in, od_dw_w, od_dw_b, od_ln_g, od_ln_b, od_q_norm, od_k_norm, od_w_out):
    bsz, seq, d = x.shape
    depth = ffn1_norm.shape[0]
    h = x.reshape(bsz * seq, d)
    for i in range(depth):
        h = _ffn(h, ffn1_norm, ffn1_w_gate, ffn1_w_up, ffn1_w_down, i)
        j = i // 2
        if i % 2 == 0:
            k_shift, fixed_ok = _logit_bounds(ev_q_norm[j], ev_k_norm[j], rel_bias)
            gate, xr, q, k, vt, kmean = _mixin_even(h, mix_norm[i], ev_w_in[j], ev_q_norm[j], ev_k_norm[j],
                                                    k_shift, n_blocks=seq // MOBA_BLOCK)
            ya = _lru(gate, xr, ev_conv_w[j], ev_conv_b[j], ev_ra_w[j], ev_ra_b[j],
                      ev_ix_w[j], ev_ix_b[j], ev_lambda[j], bsz=bsz)
            yb = _moba(q, k, vt, kmean, rel_bias, fixed_ok, bsz=bsz)
            mix = (ya, yb, ev_w_out[j])
        else:
            k_shift, fixed_ok = _logit_bounds(od_q_norm[j], od_k_norm[j], rel_bias)
            yc, q, k, vt, qi, ki2, wit = _mixin_odd(h, mix_norm[i], od_w_in[j], od_q_norm[j], od_k_norm[j],
                                                    k_shift, od_dw_w[j], od_dw_b[j], od_ln_g[j], od_ln_b[j],
                                                    bsz=bsz)
            yd = _dsa(q, k, vt, qi, ki2, wit, rel_bias, fixed_ok, bsz=bsz)
            mix = (yc, yd, od_w_out[j])
        h = _ffn(h, ffn2_norm, ffn2_w_gate, ffn2_w_up, ffn2_w_down, i, mix)
    return h.reshape(bsz, seq, d)
```

```python
import functools
import math

import numpy as np
import jax
import jax.numpy as jnp
from jax import lax
from jax.experimental import pallas as pl
from jax.experimental.pallas import tpu as pltpu

F32 = jnp.float32
BF16 = jnp.bfloat16

N_HEADS = 8
HEAD_DIM = 64
ATTN_WIDTH = N_HEADS * HEAD_DIM
LRU_C = 8.0
MOBA_BLOCK = 256
MOBA_TOPK = 3
IDX_HEADS = 8
IDX_DIM = 64
DSA_TOPK_MAX = 256
REL_BUCKETS = 32
REL_MAX_EXACT = REL_BUCKETS // 2
REL_MAX_DIST = 128
EPS = 1e-6
NEG = -1e30
M_INIT = -1e29
ATTN_TILE = 256
SUBLANES = 8
CONV_ROWS = 32
SLOT = 128
SLOT_WIDTH = N_HEADS * SLOT
FEAT0 = HEAD_DIM
SHIFT_LANE = SLOT - 1
LOGIT_RANGE_MAX = 120.0
VT_ROWS = 80
MOBA_FAR_GROUP = 4
DSA_FAR_GROUP = 2
HEADS_AHEAD = 4
BISECT_ITERS = 32
BISECT_CHECK_EVERY = 4
F32_TINY = float(np.finfo(np.float32).tiny)
LOG2E = math.log2(math.e)
Q_SCALE = HEAD_DIM ** -0.5 * LOG2E
VMEM_LIMIT = 56 * 1024 * 1024
FFN_VMEM_LIMIT = 60 * 1024 * 1024


def _cparams(sem):
    return pltpu.CompilerParams(dimension_semantics=sem, vmem_limit_bytes=VMEM_LIMIT)


def _dot(a, b):
    return jnp.dot(a, b, preferred_element_type=F32)


def _dot_nt(a, b):
    return lax.dot_general(a, b, (((1,), (1,)), ((), ())), preferred_element_type=F32)


def _split_bf16(x):
    hi = x.astype(BF16)
    lo = (x - hi.astype(F32)).astype(BF16)
    return hi, lo


def _rms_rows(x, g):
    return x * lax.rsqrt(jnp.mean(x * x, axis=-1, keepdims=True) + EPS) * g


def _const_spec(shape):
    nd = len(shape)
    return pl.BlockSpec(shape, lambda *_: (0,) * nd, pipeline_mode=pl.Buffered(1))


def _slot(h):
    return slice(h * SLOT, (h + 1) * SLOT)


def _vt_rows(h):
    return slice(h * VT_ROWS, (h + 1) * VT_ROWS)


def _to_slots(x):
    low = lax.broadcasted_iota(jnp.int32, (1, SLOT), 1) < HEAD_DIM
    slots = []
    for p in range(N_HEADS // 2):
        chunk = x[:, p * SLOT:(p + 1) * SLOT]
        slots.append(jnp.where(low, chunk, 0.0))
        slots.append(jnp.where(low, pltpu.roll(chunk, HEAD_DIM, 1), 0.0))
    return slots


def _ffn_kernel(*refs, n_chunks, has_mix):
    if has_mix:
        x_ref, ya_ref, yb_ref, wo_ref, g_ref, wg_ref, wu_ref, wd_ref, o_ref, wg_s, wu_s, wd_s = refs
    else:
        x_ref, g_ref, wg_ref, wu_ref, wd_ref, o_ref, wg_s, wu_s, wd_s = refs
    step = pl.program_id(0)

    @pl.when(step < n_chunks)
    def _():
        wg_s[step] = wg_ref[...].astype(BF16)
        wu_s[step] = wu_ref[...].astype(BF16)
        wd_s[step] = wd_ref[...].astype(BF16)

    @pl.when(step >= n_chunks)
    def _():
        x = x_ref[...]
        if has_mix:
            w = ya_ref.shape[1]
            x = x + _dot(ya_ref[...], wo_ref[0:w, :]) + _dot(yb_ref[...], wo_ref[w:2 * w, :])
        hn = _rms_rows(x, g_ref[...]).astype(BF16)
        acc = jnp.zeros(x.shape, F32)
        for c in range(n_chunks):
            gt = _dot(hn, wg_s[c])
            ut = _dot(hn, wu_s[c])
            a = (gt * jax.nn.sigmoid(gt) * ut).astype(BF16)
            acc = acc + _dot(a, wd_s[c])
        o_ref[...] = x + 0.5 * acc


def _ffn(x, g, wg, wu, wd, layer, mix=None, *, tm=1024, ff_chunk=256):
    n, d = x.shape
    d_ff = wg.shape[2]
    assert n % tm == 0 and d_ff % ff_chunk == 0
    nc = d_ff // ff_chunk
    tile = lambda s: (jnp.maximum(s - nc, 0), 0)
    chunk_col = lambda s: (layer, 0, jnp.minimum(s, nc - 1))
    chunk_row = lambda s: (layer, jnp.minimum(s, nc - 1), 0)
    in_specs = [pl.BlockSpec((tm, d), tile)]
    args = [x]
    scratch = [pltpu.VMEM((nc, d, ff_chunk), BF16), pltpu.VMEM((nc, d, ff_chunk), BF16),
               pltpu.VMEM((nc, ff_chunk, d), BF16)]
    if mix is not None:
        ya, yb, w_out = mix
        w = ya.shape[1]
        assert w_out.shape == (2 * w, d)
        in_specs += [pl.BlockSpec((tm, w), tile), pl.BlockSpec((tm, w), tile), _const_spec((2 * w, d))]
        args += [ya, yb, w_out.astype(BF16)]
    in_specs += [_const_spec((1, d)), pl.BlockSpec((None, d, ff_chunk), chunk_col),
                 pl.BlockSpec((None, d, ff_chunk), chunk_col), pl.BlockSpec((None, ff_chunk, d), chunk_row)]
    args += [g[layer].reshape(1, d), wg, wu, wd]
    return pl.pallas_call(
        functools.partial(_ffn_kernel, n_chunks=nc, has_mix=mix is not None),
        grid=(nc + n // tm,),
        in_specs=in_specs,
        out_specs=pl.BlockSpec((tm, d), tile),
        out_shape=jax.ShapeDtypeStruct((n, d), F32),
        scratch_shapes=scratch,
        compiler_params=pltpu.CompilerParams(dimension_semantics=("arbitrary",),
                                             vmem_limit_bytes=FFN_VMEM_LIMIT),
        name="ffn_mix" if mix is not None else "ffn",
    )(*args)


def _slot_rms(xs, g):
    ms = jnp.sum(xs * xs, axis=-1, keepdims=True) * (1.0 / HEAD_DIM)
    return xs * lax.rsqrt(ms + EPS) * g


def _store_q_slots(z, gain_ref, q_o):
    lane = lax.broadcasted_iota(jnp.int32, (1, SLOT), 1)
    for h, zs in enumerate(_to_slots(z)):
        qs = _slot_rms(zs, gain_ref[...]) * Q_SCALE
        q_o[:, _slot(h)] = jnp.where(lane == SHIFT_LANE, 1.0, qs).astype(BF16)


def _store_k_slots(z, gain_ref, shift_ref, k_o, one_lanes):
    slots = [_slot_rms(zs, gain_ref[...]) for zs in _to_slots(z)]
    for h, ks in enumerate(slots):
        ks = ks if one_lanes is None else jnp.where(one_lanes, 1.0, ks)
        k_o[:, _slot(h)] = (ks + shift_ref[:, _slot(h)]).astype(BF16)
    return slots


def _store_vt_slots(zt, v_o):
    pad = VT_ROWS - HEAD_DIM
    tail = (lax.broadcasted_iota(jnp.int32, (pad, ATTN_TILE), 0) == 0).astype(BF16)
    for h in range(N_HEADS):
        for r in range(zt.shape[1] // ATTN_TILE):
            feat = zt[h * HEAD_DIM:(h + 1) * HEAD_DIM, r * ATTN_TILE:(r + 1) * ATTN_TILE]
            v_o[r, h * VT_ROWS:h * VT_ROWS + HEAD_DIM, :] = feat.astype(BF16)
            v_o[r, h * VT_ROWS + HEAD_DIM:(h + 1) * VT_ROWS, :] = tail


def _pipelined(stages):
    nxt = stages[0][0]()
    for j, (_, consume) in enumerate(stages):
        cur = nxt
        if j + 1 < len(stages):
            nxt = stages[j + 1][0]()
        consume(cur)


def _mixin_even_kernel(x_ref, g_ref, w_ref, wvt_ref, qn_ref, kn_ref, ks_ref,
                       gate_o, xr_o, q_o, k_o, v_o, km_o, *, n_blocks):
    tm = x_ref.shape[0]
    hn = _rms_rows(x_ref[...], g_ref[...]).astype(BF16)
    w = ATTN_WIDTH
    proj = lambda c: (lambda: _dot(hn, w_ref[:, c * w:(c + 1) * w]))
    blocks_per_tile = tm // MOBA_BLOCK
    row_block = lax.broadcasted_iota(jnp.int32, (tm, 1), 0) // MOBA_BLOCK
    block = (pl.program_id(0) * blocks_per_tile + row_block) % n_blocks
    block_lane = lax.broadcasted_iota(jnp.int32, (1, SLOT), 1) == FEAT0 + block

    def store_k(z):
        for h, ks in enumerate(_store_k_slots(z, kn_ref, ks_ref, k_o, block_lane)):
            for r in range(blocks_per_tile):
                km_o[r, :, _slot(h)] = jnp.mean(ks[r * MOBA_BLOCK:(r + 1) * MOBA_BLOCK], axis=0, keepdims=True)

    def store(ref):
        def consume(z):
            ref[...] = z
        return consume

    _pipelined([(proj(2), lambda z: _store_q_slots(z, qn_ref, q_o)), (proj(3), store_k),
                (lambda: _dot_nt(wvt_ref[...], hn), lambda zt: _store_vt_slots(zt, v_o)),
                (proj(0), store(gate_o)), (proj(1), store(xr_o))])


def _vt_spec(tm):
    return pl.BlockSpec((tm // ATTN_TILE, N_HEADS * VT_ROWS, ATTN_TILE), lambda i: (i, 0, 0))


def _vt_shape(n):
    return jax.ShapeDtypeStruct((n // ATTN_TILE, N_HEADS * VT_ROWS, ATTN_TILE), BF16)


def _logit_bounds(q_norm, k_norm, rel_bias):
    qk = HEAD_DIM * jnp.max(jnp.abs(q_norm)) * jnp.max(jnp.abs(k_norm)) * Q_SCALE * 1.02
    rel = (rel_bias - rel_bias[REL_BUCKETS - 1]) * LOG2E
    bound = qk + jnp.maximum(jnp.max(rel, axis=0), 0.0) + 1.0
    lowest = -qk + jnp.minimum(jnp.min(rel, axis=0), 0.0)
    fixed_ok = jnp.all(bound - lowest <= LOGIT_RANGE_MAX).astype(jnp.int32).reshape(1)
    k_shift = jnp.zeros((N_HEADS, SLOT), F32).at[:, SHIFT_LANE].set(-bound).reshape(1, SLOT_WIDTH)
    return k_shift, fixed_ok


def _slot_gain(g):
    return jnp.pad(g.astype(F32), (0, SLOT - HEAD_DIM)).reshape(1, SLOT)


def _mixin_even(x, g, w_in, q_norm, k_norm, k_shift, *, n_blocks, tm=512):
    n, d = x.shape
    w = ATTN_WIDTH
    assert n % tm == 0 and tm % MOBA_BLOCK == 0 and tm % ATTN_TILE == 0 and w_in.shape[1] == 5 * w
    assert FEAT0 + n_blocks <= SHIFT_LANE
    row = lambda i: (i, 0)
    blk = tm // MOBA_BLOCK
    slot_shape = jax.ShapeDtypeStruct((n, SLOT_WIDTH), BF16)
    return pl.pallas_call(
        functools.partial(_mixin_even_kernel, n_blocks=n_blocks),
        grid=(n // tm,),
        in_specs=[pl.BlockSpec((tm, d), row), _const_spec((1, d)), _const_spec((d, 4 * w)),
                  _const_spec((w, d)), _const_spec((1, SLOT)), _const_spec((1, SLOT)),
                  _const_spec((1, SLOT_WIDTH))],
        out_specs=[pl.BlockSpec((tm, w), row)] * 2 + [pl.BlockSpec((tm, SLOT_WIDTH), row)] * 2
        + [_vt_spec(tm), pl.BlockSpec((blk, 1, SLOT_WIDTH), lambda i: (i, 0, 0))],
        out_shape=[jax.ShapeDtypeStruct((n, w), F32), jax.ShapeDtypeStruct((n, w), F32),
                   slot_shape, slot_shape, _vt_shape(n),
                   jax.ShapeDtypeStruct((n // MOBA_BLOCK, 1, SLOT_WIDTH), F32)],
        compiler_params=_cparams(("parallel",)),
        name="mixin_even",
    )(x, g.reshape(1, d), w_in[:, :4 * w].astype(BF16), w_in[:, 4 * w:].T.astype(BF16),
      _slot_gain(q_norm), _slot_gain(k_norm), k_shift)


def _conv_ln_silu(c, first, w_ref, b_ref, g_ref, beta_ref, o_ref, cbuf, sh_ref, *, halo):
    ts = c.shape[0]
    width = w_ref.shape[0]

    @pl.when(first)
    def _():
        cbuf[0:halo, :] = jnp.zeros((halo, cbuf.shape[1]), F32)

    cbuf[halo:halo + ts, :] = c
    base = halo - (width - 1)
    span = sh_ref.shape[1]
    for r in range(1, SUBLANES):
        sh_ref[r - 1] = cbuf[r:r + span, :]

    def chunk(c0):
        y = jnp.broadcast_to(b_ref[...], (CONV_ROWS, cbuf.shape[1]))
        for k in range(width):
            r = (base + k) % SUBLANES
            u0 = c0 + base + k - r
            win = cbuf[u0:u0 + CONV_ROWS, :] if r == 0 else sh_ref[r - 1, u0:u0 + CONV_ROWS, :]
            y = y + w_ref[k:k + 1, :] * win
        mu = jnp.mean(y, axis=-1, keepdims=True)
        yc = y - mu
        var = jnp.mean(yc * yc, axis=-1, keepdims=True)
        z = yc * lax.rsqrt(var + EPS) * g_ref[...] + beta_ref[...]
        o_ref[c0:c0 + CONV_ROWS, :] = (z * jax.nn.sigmoid(z)).astype(BF16)

    def close():
        cbuf[0:halo, :] = cbuf[ts:ts + halo, :]

    return [functools.partial(chunk, c0) for c0 in range(0, ts, CONV_ROWS)], close


def _mixin_odd_kernel(x_ref, g_ref, w_ref, wvt_ref, wki_ref, wwit_ref, qn_ref, kn_ref, ks_ref,
                      dw_ref, db_ref, lg_ref, lb_ref,
                      yc_o, q_o, k_o, v_o, qi_o, ki_o, wi_o, cbuf, sh_ref, *, tiles_per_seq, halo):
    hn = _rms_rows(x_ref[...], g_ref[...]).astype(BF16)
    w = ATTN_WIDTH
    proj = lambda c, n=1: _dot(hn, w_ref[:, c * w:(c + n) * w])
    first = pl.program_id(0) % tiles_per_seq == 0

    z_glu = proj(0, 2)
    z_q = proj(2)
    chunks, close = _conv_ln_silu(z_glu[:, 0:w] * jax.nn.sigmoid(z_glu[:, w:2 * w]), first,
                                  dw_ref, db_ref, lg_ref, lb_ref, yc_o, cbuf, sh_ref, halo=halo)
    per_stage = -(-len(chunks) // 4)
    run_chunks = lambda j: [f() for f in chunks[j * per_stage:(j + 1) * per_stage]]
    z_k = proj(3)
    _store_q_slots(z_q, qn_ref, q_o)
    run_chunks(0)
    z_vt = _dot_nt(wvt_ref[...], hn)
    _store_k_slots(z_k, kn_ref, ks_ref, k_o, None)
    run_chunks(1)
    z_qi = proj(4)
    _store_vt_slots(z_vt, v_o)
    run_chunks(2)
    z_ki = _dot(hn, wki_ref[...])
    z_wit = _dot_nt(wwit_ref[...], hn)
    qi_o[...] = z_qi.astype(BF16)
    run_chunks(3)
    close()
    ki_o[...] = z_ki.astype(BF16)
    wi_o[...] = z_wit * (IDX_DIM ** -0.5 * IDX_HEADS ** -0.5)


def _mixin_odd(x, g, w_in, q_norm, k_norm, k_shift, dw_w, dw_b, ln_g, ln_b, *, bsz, tm=512, halo=32):
    n, d = x.shape
    w = ATTN_WIDTH
    width = dw_w.shape[0]
    assert n % tm == 0 and tm % ATTN_TILE == 0 and w_in.shape[1] == 6 * w + IDX_DIM + IDX_HEADS
    assert (n // bsz) % tm == 0 and width - 1 <= halo <= tm and halo % SUBLANES == 0 and tm % CONV_ROWS == 0
    w_main = jnp.concatenate([w_in[:, :4 * w], w_in[:, 5 * w:6 * w]], axis=1).astype(BF16)
    w_vt = w_in[:, 4 * w:5 * w].T.astype(BF16)
    w_ki = w_in[:, 6 * w:6 * w + IDX_DIM]
    w_ki2 = jnp.concatenate([w_ki, w_ki], axis=1).astype(BF16)
    w_wit = w_in[:, 6 * w + IDX_DIM:].T.astype(BF16)
    row = lambda i: (i, 0)
    vec = lambda v: v.reshape(1, w).astype(F32)
    slot_shape = jax.ShapeDtypeStruct((n, SLOT_WIDTH), BF16)
    return pl.pallas_call(
        functools.partial(_mixin_odd_kernel, tiles_per_seq=(n // bsz) // tm, halo=halo),
        grid=(n // tm,),
        in_specs=[pl.BlockSpec((tm, d), row), _const_spec((1, d)), _const_spec((d, 5 * w)),
                  _const_spec((w, d)), _const_spec((d, 2 * IDX_DIM)), _const_spec((IDX_HEADS, d)),
                  _const_spec((1, SLOT)), _const_spec((1, SLOT)), _const_spec((1, SLOT_WIDTH)),
                  _const_spec((width, w)), _const_spec((1, w)), _const_spec((1, w)), _const_spec((1, w))],
        out_specs=[pl.BlockSpec((tm, w), row)] + [pl.BlockSpec((tm, SLOT_WIDTH), row)] * 2
        + [_vt_spec(tm), pl.BlockSpec((tm, w), row), pl.BlockSpec((tm, 128), row),
           pl.BlockSpec((IDX_HEADS, tm), lambda i: (0, i))],
        out_shape=[jax.ShapeDtypeStruct((n, w), BF16), slot_shape, slot_shape, _vt_shape(n),
                   jax.ShapeDtypeStruct((n, w), BF16),
                   jax.ShapeDtypeStruct((n, 128), BF16), jax.ShapeDtypeStruct((IDX_HEADS, n), F32)],
        scratch_shapes=[pltpu.VMEM((tm + halo, w), F32),
                        pltpu.VMEM((SUBLANES - 1, tm + halo - SUBLANES, w), F32)],
        compiler_params=_cparams(("arbitrary",)),
        name="mixin_odd",
    )(x, g.reshape(1, d), w_main, w_vt, w_ki2, w_wit, _slot_gain(q_norm), _slot_gain(k_norm), k_shift,
      dw_w, vec(dw_b), vec(ln_g), vec(ln_b))


def _lru_kernel(gate_ref, xr_ref, cw_ref, cb_ref, wa_ref, ba_ref, wx_ref, bx_ref, sp_ref,
                o_ref, xbuf, a_s, u_s, h_s, hc, *, ts):
    j = pl.program_id(1)

    @pl.when(j == 0)
    def _():
        xbuf[0:8, :] = jnp.zeros((8, xbuf.shape[1]), F32)
        hc[...] = jnp.zeros(hc.shape, F32)

    xbuf[8:8 + ts, :] = xr_ref[...]
    xc = cb_ref[...] + cw_ref[0:1, :] * xbuf[5:5 + ts, :]
    for k in range(1, 4):
        xc = xc + cw_ref[k:k + 1, :] * xbuf[5 + k:5 + k + ts, :]
    xbuf[0:8, :] = xbuf[ts:ts + 8, :]

    xcb = xc.astype(BF16)
    r = jax.nn.sigmoid(_dot(xcb, wa_ref[...]) + ba_ref[...])
    ig = jax.nn.sigmoid(_dot(xcb, wx_ref[...]) + bx_ref[...])
    log_a = -LRU_C * r * sp_ref[...]
    a = jnp.exp(log_a)
    a_s[...] = a
    u_s[...] = jnp.sqrt(-jnp.tanh(log_a) * (a * a + 1.0)) * (ig * xc)

    row = lax.broadcasted_iota(jnp.int32, (8, a_s.shape[1]), 0)

    def body(g, carry):
        r0 = pl.multiple_of(g * 8, 8)
        a = a_s[pl.ds(r0, 8), :]
        u = u_s[pl.ds(r0, 8), :]
        for s in (1, 2, 4):
            ok = row >= s
            a_sh = jnp.where(ok, pltpu.roll(a, s, 0), 1.0)
            u_sh = jnp.where(ok, pltpu.roll(u, s, 0), 0.0)
            u = a * u_sh + u
            a = a * a_sh
        h = a * carry + u
        h_s[pl.ds(r0, 8), :] = h
        return h[7:8, :]

    hc[...] = lax.fori_loop(0, ts // 8, body, hc[...], unroll=4)
    o_ref[...] = (h_s[...] * jax.nn.gelu(gate_ref[...])).astype(BF16)


def _block_diag(wb):
    nb, bs, _ = wb.shape
    eye = jnp.eye(nb, dtype=wb.dtype)
    return (eye[:, None, :, None] * wb[:, :, None, :]).reshape(nb * bs, nb * bs)


def _lru(gate, xr, conv_w, conv_b, ra_w, ra_b, ix_w, ix_b, lam, *, bsz, ts=256):
    n, w = xr.shape
    seq = n // bsz
    assert seq % ts == 0
    nt = seq // ts
    row = lambda b, j: (b * nt + j, 0)
    vec = lambda v: v.reshape(1, w).astype(F32)
    return pl.pallas_call(
        functools.partial(_lru_kernel, ts=ts),
        grid=(bsz, nt),
        in_specs=[pl.BlockSpec((ts, w), row), pl.BlockSpec((ts, w), row),
                  _const_spec((conv_w.shape[0], w)), _const_spec((1, w)),
                  _const_spec((w, w)), _const_spec((1, w)), _const_spec((w, w)), _const_spec((1, w)),
                  _const_spec((1, w))],
        out_specs=pl.BlockSpec((ts, w), row),
        out_shape=jax.ShapeDtypeStruct((n, w), BF16),
        scratch_shapes=[pltpu.VMEM((ts + 8, w), F32), pltpu.VMEM((ts, w), F32),
                        pltpu.VMEM((ts, w), F32), pltpu.VMEM((ts, w), F32), pltpu.VMEM((1, w), F32)],
        compiler_params=_cparams(("arbitrary", "arbitrary")),
        name="rg_lru",
    )(gate, xr, conv_w, vec(conv_b), _block_diag(ra_w).astype(BF16), vec(ra_b),
      _block_diag(ix_w).astype(BF16), vec(ix_b), vec(jax.nn.softplus(-lam)))


def _bucket_tiles(t):
    assert t > REL_MAX_DIST
    n = np.arange(2 * t)
    nf = np.maximum(n, 1).astype(np.float32)
    large = REL_MAX_EXACT + (np.log(nf / np.float32(REL_MAX_EXACT))
                             / np.float32(math.log(REL_MAX_DIST / REL_MAX_EXACT))
                             * np.float32(REL_BUCKETS - REL_MAX_EXACT)).astype(np.int32)
    bucket = np.where(n < REL_MAX_EXACT, n, np.minimum(large, REL_BUCKETS - 1)).astype(np.int32)
    qry = np.arange(t)[None, :]
    key = np.arange(t)[:, None]
    return np.stack([bucket[np.maximum(qry - key, 0)], bucket[t + qry - key]])


def _build_bias(idx_ref, rb_ref, bias_ref):
    t = idx_ref.shape[1]
    causal = (lax.broadcasted_iota(jnp.int32, (t, t), 0) <= lax.broadcasted_iota(jnp.int32, (t, t), 1))
    for h in range(N_HEADS):
        far = rb_ref[REL_BUCKETS - 1, h]
        for which in (0, 1):
            idx = idx_ref[which]
            acc = jnp.zeros((t, t), F32)
            for b in range(REL_BUCKETS - 1):
                acc = jnp.where(idx == b, (rb_ref[b, h] - far) * LOG2E, acc)
            if which == 0:
                bias_ref[h, t:2 * t, :] = jnp.where(causal, acc, NEG)
            else:
                bias_ref[h, 0:t, :] = acc


def _attn_init(m_ref, acc_ref):
    m_ref[...] = jnp.full(m_ref.shape, M_INIT, F32)
    acc_ref[...] = jnp.zeros(acc_ref.shape, F32)


def _softmax_step(s, h, vt_h, m_ref, acc_ref):
    m_old = m_ref[h]
    m_new = jnp.maximum(m_old, jnp.max(s, axis=0, keepdims=True))
    p = jnp.exp2(s - m_new)
    m_ref[h] = m_new
    acc_ref[h] = jnp.exp2(m_old - m_new) * acc_ref[h] + _dot(vt_h, p.astype(BF16))


def _fixed_step(s, h, vt_h, acc_ref):
    acc_ref[h] = acc_ref[h] + _dot(vt_h, jnp.exp2(s).astype(BF16))


def _heads_pipelined(logits, vt_slot, m_ref, acc_ref, online):
    ahead = [logits(h) for h in range(HEADS_AHEAD)]
    for h in range(N_HEADS):
        if h + HEADS_AHEAD < N_HEADS:
            ahead.append(logits(h + HEADS_AHEAD))
        if online:
            _softmax_step(ahead[h], h, vt_slot(h), m_ref, acc_ref)
        else:
            _fixed_step(ahead[h], h, vt_slot(h), acc_ref)


FAR_QUAD = ("far", "far", "far", "far")
FAR_PAIR = ("far", "far")
FAR_ONE = ("far",)
NEAR_DIAG = ("near", "diag")
DIAG_ONE = ("diag",)


def _attend_tiles(kt, kinds, q_slot, k_ref, vt_ref, bias_ref, mask_add, m_ref, acc_ref, online):
    t = ATTN_TILE
    n = len(kinds)
    rows = pl.ds(pl.multiple_of(kt * t, t), n * t)

    def logits(h):
        s = _dot_nt(k_ref[rows, _slot(h)], q_slot(h))
        if kinds == NEAR_DIAG:
            s = s + bias_ref[h]
        elif kinds == DIAG_ONE:
            s = s + bias_ref[h, t:2 * t, :]
        if mask_add is not None:
            s = s + mask_add
        return s

    def vt_slot(h):
        parts = [vt_ref[kt + a, _vt_rows(h), :] for a in range(n)]
        return parts[0] if n == 1 else jnp.concatenate(parts, axis=1)

    _heads_pipelined(logits, vt_slot, m_ref, acc_ref, online)


def _for_causal_tiles(i, tiles, far_group):
    n_far = jnp.maximum(i - 1, 0)
    assert far_group in (2, 4)

    def group(j, carry):
        tiles(far_group * j, FAR_QUAD if far_group == 4 else FAR_PAIR)
        return carry

    lax.fori_loop(0, n_far // far_group, group, 0)

    if far_group == 4:
        @pl.when(n_far % 4 >= 2)
        def _():
            tiles((n_far // 4) * 4, FAR_PAIR)

    @pl.when(n_far % 2 == 1)
    def _():
        tiles(n_far - 1, FAR_ONE)

    @pl.when(i >= 1)
    def _():
        tiles(i - 1, NEAR_DIAG)

    @pl.when(i == 0)
    def _():
        tiles(i, DIAG_ONE)


def _attend_either(ok_ref, attend):
    @pl.when(ok_ref[0] != 0)
    def _():
        attend(False)

    @pl.when(ok_ref[0] == 0)
    def _():
        attend(True)


def _attn_finish(o_ref, acc_ref):
    parts = []
    for h in range(N_HEADS):
        acc = acc_ref[h]
        parts.append(acc[0:HEAD_DIM] * (1.0 / acc[FEAT0:FEAT0 + 1]))
    o_ref[...] = jnp.concatenate(parts, axis=0).T.astype(BF16)


def _smem_spec():
    return pl.BlockSpec(memory_space=pltpu.SMEM)


def _moba_kernel(rb_ref, ok_ref, q_ref, k_ref, vt_ref, km_ref, idx_ref, o_ref,
                 acc_ref, m_ref, qs_ref, bias_ref, *, n_blocks):
    i = pl.program_id(1)
    t = ATTN_TILE
    nbp = -(-n_blocks // 8) * 8

    @pl.when((pl.program_id(0) == 0) & (i == 0))
    def _():
        _build_bias(idx_ref, rb_ref, bias_ref)

    _attn_init(m_ref, acc_ref)

    blk = lax.broadcasted_iota(jnp.int32, (nbp, t), 0)
    past = blk < i
    for h in range(N_HEADS):
        qh = q_ref[:, _slot(h)]
        km_hi, km_lo = _split_bf16(km_ref[:, _slot(h)])
        gate_t = _dot_nt(km_hi, qh) + _dot_nt(km_lo, qh)
        g = jnp.where(past, gate_t[FEAT0:FEAT0 + nbp, :], NEG)
        rank = jnp.zeros((nbp, t), F32)
        for j in range(n_blocks):
            gj = g[j:j + 1, :]
            beats = (gj > g) | ((gj == g) & (blk > j))
            rank = rank + jnp.where(beats, 1.0, 0.0)
        flag = jnp.where(past & (rank >= MOBA_TOPK), NEG, 0.0)
        flag_t = jnp.concatenate([jnp.zeros((FEAT0, t), F32), flag,
                                  jnp.zeros((SLOT - FEAT0 - nbp, t), F32)], axis=0)
        qs_ref[:, _slot(h)] = (qh.astype(F32) + flag_t.T).astype(BF16)

    def attend(online):
        def tiles(kt, kinds):
            _attend_tiles(kt, kinds, lambda h: qs_ref[:, _slot(h)], k_ref, vt_ref, bias_ref, None,
                          m_ref, acc_ref, online)
        _for_causal_tiles(i, tiles, MOBA_FAR_GROUP)

    _attend_either(ok_ref, attend)
    _attn_finish(o_ref, acc_ref)


def _moba(q, k, vt, kmean, rel_bias, fixed_ok, *, bsz):
    n = q.shape[0]
    seq = n // bsz
    t = ATTN_TILE
    assert seq % t == 0 and t == MOBA_BLOCK
    nq = seq // t
    assert FEAT0 + nq <= SHIFT_LANE
    km = jnp.pad(kmean.reshape(bsz, nq, SLOT_WIDTH), ((0, 0), (FEAT0, SLOT - FEAT0 - nq), (0, 0)))
    r3 = lambda a: a.reshape(bsz, seq, SLOT_WIDTH)
    seq_spec = pl.BlockSpec((None, seq, SLOT_WIDTH), lambda b, i: (b, 0, 0))
    out = pl.pallas_call(
        functools.partial(_moba_kernel, n_blocks=nq),
        grid=(bsz, nq),
        in_specs=[_smem_spec(), _smem_spec(),
                  pl.BlockSpec((None, t, SLOT_WIDTH), lambda b, i: (b, i, 0)), seq_spec,
                  pl.BlockSpec((nq, N_HEADS * VT_ROWS, t), lambda b, i: (b, 0, 0)),
                  pl.BlockSpec((None, SLOT, SLOT_WIDTH), lambda b, i: (b, 0, 0)),
                  _const_spec((2, t, t))],
        out_specs=pl.BlockSpec((None, t, ATTN_WIDTH), lambda b, i: (b, i, 0)),
        out_shape=jax.ShapeDtypeStruct((bsz, seq, ATTN_WIDTH), BF16),
        scratch_shapes=[pltpu.VMEM((N_HEADS, VT_ROWS, t), F32), pltpu.VMEM((N_HEADS, 1, t), F32),
                        pltpu.VMEM((t, SLOT_WIDTH), BF16),
                        pltpu.VMEM((N_HEADS, 2 * t, t), F32)],
        compiler_params=_cparams(("arbitrary", "arbitrary")),
        name="moba_attention",
    )(rel_bias, fixed_ok, r3(q), r3(k), vt, km, jnp.asarray(_bucket_tiles(t)))
    return out.reshape(n, ATTN_WIDTH)


def _dsa_kernel(rb_ref, ok_ref, q_ref, k_ref, vt_ref, qi_ref, ki_ref, wit_ref, idx_ref, tri_ref, o_ref,
                acc_ref, m_ref, isc_ref, bc_ref, bias_ref, *, n_sel):
    i = pl.program_id(1)
    t = ATTN_TILE
    key = lax.broadcasted_iota(jnp.int32, (t, t), 0)
    qry = lax.broadcasted_iota(jnp.int32, (t, t), 1)
    lane128 = lax.broadcasted_iota(jnp.int32, (1, 128), 1)

    @pl.when((pl.program_id(0) == 0) & (i == 0))
    def _():
        _build_bias(idx_ref, rb_ref, bias_ref)

    _attn_init(m_ref, acc_ref)

    def index_tiles(kt, n, diag):
        rows = pl.ds(pl.multiple_of(kt * t, t), n * t)
        ki2 = ki_ref[rows, :]
        acc = jnp.zeros((n * t, t), F32)
        for pr in range(IDX_HEADS // 2):
            q2 = qi_ref[:, pr * 128:(pr + 1) * 128]
            for half in range(2):
                h = 2 * pr + half
                hm = (lane128 >= half * IDX_DIM) & (lane128 < (half + 1) * IDX_DIM)
                s = _dot_nt(ki2, jnp.where(hm, q2, jnp.zeros((), BF16)))
                acc = acc + jnp.maximum(s, 0.0) * wit_ref[h:h + 1, :]
        if diag:
            acc = jnp.where(key <= qry, acc, -jnp.inf)
        for a in range(n):
            isc_ref[kt + a] = acc[a * t:(a + 1) * t]

    def index_pair(j, carry):
        index_tiles(2 * j, 2, False)
        return carry

    lax.fori_loop(0, i // 2, index_pair, 0)

    @pl.when(i % 2 == 1)
    def _():
        index_tiles(i - 1, 1, False)

    index_tiles(i, 1, True)

    def fold8(x, op):
        return op(x.reshape(t // 8, 8, t), axis=0)

    def minmax_body(kt, carry):
        mn, mx = carry
        x = isc_ref[kt]
        mx = jnp.maximum(mx, fold8(x, jnp.max))
        mn = jnp.minimum(mn, fold8(jnp.where(x == -jnp.inf, jnp.inf, x), jnp.min))
        return mn, mx

    mn, mx = lax.fori_loop(0, i + 1, minmax_body,
                           (jnp.full((8, t), jnp.inf, F32), jnp.full((8, t), -jnp.inf, F32)))
    lo0 = jnp.min(mn, axis=0, keepdims=True)
    mx = jnp.max(mx, axis=0, keepdims=True)
    hi0 = mx + jnp.abs(mx) * 1e-3 + 1e-30
    n_valid = (i * t + 1 + lax.broadcasted_iota(jnp.int32, (1, t), 1)).astype(F32)
    want = jnp.minimum(n_valid, float(n_sel))

    def count_ge(thr):
        def one(kt, c):
            return c + fold8(jnp.where(isc_ref[kt] >= thr, 1.0, 0.0), jnp.sum)

        def pair(j, c):
            return one(2 * j + 1, one(2 * j, c))

        c = lax.fori_loop(0, (i + 1) // 2, pair, jnp.zeros((8, t), F32))
        c = lax.cond((i + 1) % 2 == 1, lambda c: one(i, c), lambda c: c, c)
        return jnp.sum(c, axis=0, keepdims=True)

    c_nonneg = count_ge(0.0)
    c_pos = count_ge(F32_TINY)
    above = want <= c_pos
    zero_tie = (want > c_pos) & (want <= c_nonneg)
    lo1 = jnp.where(above, F32_TINY, jnp.where(zero_tie, 0.0, lo0))
    hi1 = jnp.where(above, hi0, jnp.where(zero_tie, F32_TINY, 0.0))
    c_lo1 = jnp.where(above, c_pos, jnp.where(zero_tie, c_nonneg, n_valid))
    c_hi1 = jnp.where(above, 0.0, jnp.where(zero_tie, c_pos, c_nonneg))

    def bisect_step(_, carry):
        lo, hi, c_lo, c_hi = carry
        mid = 0.5 * (lo + hi)
        c_mid = count_ge(mid)
        up = c_mid >= want
        return (jnp.where(up, mid, lo), jnp.where(up, hi, mid),
                jnp.where(up, c_mid, c_lo), jnp.where(up, c_hi, c_mid))

    def bisect_cond(carry):
        it, _, _, c_lo, _ = carry
        return (it < BISECT_ITERS) & (jnp.max(jnp.where(zero_tie, 0.0, c_lo - want)) > 0.0)

    def bisect_body(carry):
        return (carry[0] + BISECT_CHECK_EVERY,) + lax.fori_loop(0, BISECT_CHECK_EVERY, bisect_step, carry[1:])

    _, lo, hi, c_lo, c_hi = lax.while_loop(bisect_cond, bisect_body, (0, lo1, hi1, c_lo1, c_hi1))
    need = want - c_hi
    bc_ref[...] = jnp.zeros(bc_ref.shape, F32)

    def select_mask(kt):
        x = isc_ref[kt]
        band = jnp.where((x >= lo) & (x < hi), 1.0, 0.0)
        before = bc_ref[...] + _dot(tri_ref[...], band.astype(BF16))
        bc_ref[...] = bc_ref[...] + jnp.sum(band, axis=0, keepdims=True)
        return jnp.where((x >= hi) | ((band > 0.0) & (before < need)), 0.0, NEG)

    def attend(online):
        def tiles(kt, kinds):
            masks = [select_mask(kt + a) for a in range(len(kinds))]
            mask_add = masks[0] if len(masks) == 1 else jnp.concatenate(masks, axis=0)
            _attend_tiles(kt, kinds, lambda h: q_ref[:, _slot(h)], k_ref, vt_ref, bias_ref, mask_add,
                          m_ref, acc_ref, online)
        _for_causal_tiles(i, tiles, DSA_FAR_GROUP)

    _attend_either(ok_ref, attend)
    _attn_finish(o_ref, acc_ref)


def _dsa(q, k, vt, qi, ki2, wit, rel_bias, fixed_ok, *, bsz):
    n = q.shape[0]
    seq = n // bsz
    t = ATTN_TILE
    assert seq % t == 0
    nq = seq // t
    n_sel = min(DSA_TOPK_MAX, seq // 4)
    tri = (jnp.arange(t)[None, :] < jnp.arange(t)[:, None]).astype(BF16)
    r3 = lambda a: a.reshape(bsz, seq, a.shape[-1])
    tile_spec = lambda width: pl.BlockSpec((None, t, width), lambda b, i: (b, i, 0))
    seq_spec = lambda width: pl.BlockSpec((None, seq, width), lambda b, i: (b, 0, 0))
    out = pl.pallas_call(
        functools.partial(_dsa_kernel, n_sel=n_sel),
        grid=(bsz, nq),
        in_specs=[_smem_spec(), _smem_spec(), tile_spec(SLOT_WIDTH), seq_spec(SLOT_WIDTH),
                  pl.BlockSpec((nq, N_HEADS * VT_ROWS, t), lambda b, i: (b, 0, 0)),
                  tile_spec(ATTN_WIDTH), seq_spec(128),
                  pl.BlockSpec((IDX_HEADS, t), lambda b, i: (0, b * nq + i)),
                  _const_spec((2, t, t)), _const_spec((t, t))],
        out_specs=tile_spec(ATTN_WIDTH),
        out_shape=jax.ShapeDtypeStruct((bsz, seq, ATTN_WIDTH), BF16),
        scratch_shapes=[pltpu.VMEM((N_HEADS, VT_ROWS, t), F32), pltpu.VMEM((N_HEADS, 1, t), F32),
                        pltpu.VMEM((nq, t, t), F32), pltpu.VMEM((1, t), F32),
                        pltpu.VMEM((N_HEADS, 2 * t, t), F32)],
        compiler_params=_cparams(("arbitrary", "arbitrary")),
        name="dsa_attention",
    )(rel_bias, fixed_ok, r3(q), r3(k), vt, r3(qi), r3(ki2), wit, jnp.asarray(_bucket_tiles(t)), tri)
    return out.reshape(n, ATTN_WIDTH)


def kernel(x, rel_bias, ffn1_norm, ffn1_w_gate, ffn1_w_up, ffn1_w_down, mix_norm, ffn2_norm, ffn2_w_gate, ffn2_w_up, ffn2_w_down, ev_w_in, ev_conv_w, ev_conv_b, ev_ra_w, ev_ra_b, ev_ix_w, ev_ix_b, ev_lambda, ev_q_norm, ev_k_norm, ev_w_out, od_w_in, od_dw_w, od_dw_b, od_ln_g, od_ln_b, od_q_norm, od_k_norm, od_w_out):
    bsz, seq, d = x.shape
    depth = ffn1_norm.shape[0]
    h = x.reshape(bsz * seq, d)
    for i in range(depth):
        h = _ffn(h, ffn1_norm, ffn1_w_gate, ffn1_w_up, ffn1_w_down, i)
        j = i // 2
        if i % 2 == 0:
            k_shift, fixed_ok = _logit_bounds(ev_q_norm[j], ev_k_norm[j], rel_bias)
            gate, xr, q, k, vt, kmean = _mixin_even(h, mix_norm[i], ev_w_in[j], ev_q_norm[j], ev_k_norm[j],
                                                    k_shift, n_blocks=seq // MOBA_BLOCK)
            ya = _lru(gate, xr, ev_conv_w[j], ev_conv_b[j], ev_ra_w[j], ev_ra_b[j],
                      ev_ix_w[j], ev_ix_b[j], ev_lambda[j], bsz=bsz)
            yb = _moba(q, k, vt, kmean, rel_bias, fixed_ok, bsz=bsz)
            mix = (ya, yb, ev_w_out[j])
        else:
            k_shift, fixed_ok = _logit_bounds(od_q_norm[j], od_k_norm[j], rel_bias)
            yc, q, k, vt, qi, ki2, wit = _mixin_odd(h, mix_norm[i], od_w_in[j], od_q_norm[j], od_k_norm[j],
                                                    k_shift, od_dw_w[j], od_dw_b[j], od_ln_g[j], od_ln_b[j],
                                                    bsz=bsz)
            yd = _dsa(q, k, vt, qi, ki2, wit, rel_bias, fixed_ok, bsz=bsz)
            mix = (yc, yd, od_w_out[j])
        h = _ffn(h, ffn2_norm, ffn2_w_gate, ffn2_w_up, ffn2_w_down, i, mix)
    return h.reshape(bsz, seq, d)
```

```python
import functools
import math

import numpy as np
import jax
import jax.numpy as jnp
from jax import lax
from jax.experimental import pallas as pl
from jax.experimental.pallas import tpu as pltpu

F32 = jnp.float32
BF16 = jnp.bfloat16

N_HEADS = 8
HEAD_DIM = 64
ATTN_WIDTH = N_HEADS * HEAD_DIM
LRU_C = 8.0
MOBA_BLOCK = 256
MOBA_TOPK = 3
IDX_HEADS = 8
IDX_DIM = 64
DSA_TOPK_MAX = 256
REL_BUCKETS = 32
REL_MAX_EXACT = REL_BUCKETS // 2
REL_MAX_DIST = 128
EPS = 1e-6
NEG = -1e30
M_INIT = -1e29
ATTN_TILE = 256
SUBLANES = 8
CONV_ROWS = 64
SLOT = 128
SLOT_WIDTH = N_HEADS * SLOT
FEAT0 = HEAD_DIM
SHIFT_LANE = SLOT - 1
LOGIT_RANGE_MAX = 120.0
VT_ROWS = 80
MOBA_FAR_GROUP = 4
DSA_FAR_GROUP = 2
HEADS_AHEAD = 4
BISECT_ITERS = 32
BISECT_CHECK_EVERY = 4
F32_TINY = float(np.finfo(np.float32).tiny)
LOG2E = math.log2(math.e)
Q_SCALE = HEAD_DIM ** -0.5 * LOG2E
VMEM_LIMIT = 56 * 1024 * 1024
FFN_VMEM_LIMIT = 60 * 1024 * 1024


def _cparams(sem):
    return pltpu.CompilerParams(dimension_semantics=sem, vmem_limit_bytes=VMEM_LIMIT)


def _dot(a, b):
    return jnp.dot(a, b, preferred_element_type=F32)


def _dot_nt(a, b):
    return lax.dot_general(a, b, (((1,), (1,)), ((), ())), preferred_element_type=F32)


def _split_bf16(x):
    hi = x.astype(BF16)
    lo = (x - hi.astype(F32)).astype(BF16)
    return hi, lo


def _rms_rows(x, g):
    return x * lax.rsqrt(jnp.mean(x * x, axis=-1, keepdims=True) + EPS) * g


def _const_spec(shape):
    nd = len(shape)
    return pl.BlockSpec(shape, lambda *_: (0,) * nd, pipeline_mode=pl.Buffered(1))


def _slot(h):
    return slice(h * SLOT, (h + 1) * SLOT)


def _vt_rows(h):
    return slice(h * VT_ROWS, (h + 1) * VT_ROWS)


def _to_slots(x):
    low = lax.broadcasted_iota(jnp.int32, (1, SLOT), 1) < HEAD_DIM
    slots = []
    for p in range(N_HEADS // 2):
        chunk = x[:, p * SLOT:(p + 1) * SLOT]
        slots.append(jnp.where(low, chunk, 0.0))
        slots.append(jnp.where(low, pltpu.roll(chunk, HEAD_DIM, 1), 0.0))
    return slots


def _ffn_kernel(*refs, n_chunks, has_mix):
    if has_mix:
        x_ref, ya_ref, yb_ref, wo_ref, g_ref, wg_ref, wu_ref, wd_ref, o_ref, wg_s, wu_s, wd_s = refs
    else:
        x_ref, g_ref, wg_ref, wu_ref, wd_ref, o_ref, wg_s, wu_s, wd_s = refs
    step = pl.program_id(0)

    @pl.when(step < n_chunks)
    def _():
        wg_s[step] = wg_ref[...].astype(BF16)
        wu_s[step] = wu_ref[...].astype(BF16)
        wd_s[step] = wd_ref[...].astype(BF16)

    @pl.when(step >= n_chunks)
    def _():
        x = x_ref[...]
        if has_mix:
            w = ya_ref.shape[1]
            x = x + _dot(ya_ref[...], wo_ref[0:w, :]) + _dot(yb_ref[...], wo_ref[w:2 * w, :])
        hn = _rms_rows(x, g_ref[...]).astype(BF16)
        acc = jnp.zeros(x.shape, F32)
        for c in range(n_chunks):
            gt = _dot(hn, wg_s[c])
            ut = _dot(hn, wu_s[c])
            a = (gt * jax.nn.sigmoid(gt) * ut).astype(BF16)
            acc = acc + _dot(a, wd_s[c])
        o_ref[...] = x + 0.5 * acc


def _ffn(x, g, wg, wu, wd, layer, mix=None, *, tm=1024, ff_chunk=256):
    n, d = x.shape
    d_ff = wg.shape[2]
    assert n % tm == 0 and d_ff % ff_chunk == 0
    nc = d_ff // ff_chunk
    tile = lambda s: (jnp.maximum(s - nc, 0), 0)
    chunk_col = lambda s: (layer, 0, jnp.minimum(s, nc - 1))
    chunk_row = lambda s: (layer, jnp.minimum(s, nc - 1), 0)
    in_specs = [pl.BlockSpec((tm, d), tile)]
    args = [x]
    scratch = [pltpu.VMEM((nc, d, ff_chunk), BF16), pltpu.VMEM((nc, d, ff_chunk), BF16),
               pltpu.VMEM((nc, ff_chunk, d), BF16)]
    if mix is not None:
        ya, yb, w_out = mix
        w = ya.shape[1]
        assert w_out.shape == (2 * w, d)
        in_specs += [pl.BlockSpec((tm, w), tile), pl.BlockSpec((tm, w), tile), _const_spec((2 * w, d))]
        args += [ya, yb, w_out.astype(BF16)]
    in_specs += [_const_spec((1, d)), pl.BlockSpec((None, d, ff_chunk), chunk_col),
                 pl.BlockSpec((None, d, ff_chunk), chunk_col), pl.BlockSpec((None, ff_chunk, d), chunk_row)]
    args += [g[layer].reshape(1, d), wg, wu, wd]
    return pl.pallas_call(
        functools.partial(_ffn_kernel, n_chunks=nc, has_mix=mix is not None),
        grid=(nc + n // tm,),
        in_specs=in_specs,
        out_specs=pl.BlockSpec((tm, d), tile),
        out_shape=jax.ShapeDtypeStruct((n, d), F32),
        scratch_shapes=scratch,
        compiler_params=pltpu.CompilerParams(dimension_semantics=("arbitrary",),
                                             vmem_limit_bytes=FFN_VMEM_LIMIT),
        name="ffn_mix" if mix is not None else "ffn",
    )(*args)


def _slot_rms(xs, g):
    ms = jnp.sum(xs * xs, axis=-1, keepdims=True) * (1.0 / HEAD_DIM)
    return xs * lax.rsqrt(ms + EPS) * g


def _store_q_slots(z, gain_ref, q_o):
    lane = lax.broadcasted_iota(jnp.int32, (1, SLOT), 1)
    for h, zs in enumerate(_to_slots(z)):
        qs = _slot_rms(zs, gain_ref[...]) * Q_SCALE
        q_o[:, _slot(h)] = jnp.where(lane == SHIFT_LANE, 1.0, qs).astype(BF16)


def _store_k_slots(z, gain_ref, shift_ref, k_o, one_lanes):
    slots = [_slot_rms(zs, gain_ref[...]) for zs in _to_slots(z)]
    for h, ks in enumerate(slots):
        ks = ks if one_lanes is None else jnp.where(one_lanes, 1.0, ks)
        k_o[:, _slot(h)] = (ks + shift_ref[:, _slot(h)]).astype(BF16)
    return slots


def _store_vt_slots(zt, v_o):
    pad = VT_ROWS - HEAD_DIM
    tail = (lax.broadcasted_iota(jnp.int32, (pad, ATTN_TILE), 0) == 0).astype(BF16)
    for h in range(N_HEADS):
        for r in range(zt.shape[1] // ATTN_TILE):
            feat = zt[h * HEAD_DIM:(h + 1) * HEAD_DIM, r * ATTN_TILE:(r + 1) * ATTN_TILE]
            v_o[r, h * VT_ROWS:h * VT_ROWS + HEAD_DIM, :] = feat.astype(BF16)
            v_o[r, h * VT_ROWS + HEAD_DIM:(h + 1) * VT_ROWS, :] = tail


def _pipelined(stages):
    nxt = stages[0][0]()
    for j, (_, consume) in enumerate(stages):
        cur = nxt
        if j + 1 < len(stages):
            nxt = stages[j + 1][0]()
        consume(cur)


def _mixin_even_kernel(x_ref, g_ref, w_ref, wvt_ref, qn_ref, kn_ref, ks_ref,
                       gate_o, xr_o, q_o, k_o, v_o, km_o, *, n_blocks):
    tm = x_ref.shape[0]
    hn = _rms_rows(x_ref[...], g_ref[...]).astype(BF16)
    w = ATTN_WIDTH
    proj = lambda c: (lambda: _dot(hn, w_ref[:, c * w:(c + 1) * w]))
    blocks_per_tile = tm // MOBA_BLOCK
    row_block = lax.broadcasted_iota(jnp.int32, (tm, 1), 0) // MOBA_BLOCK
    block = (pl.program_id(0) * blocks_per_tile + row_block) % n_blocks
    block_lane = lax.broadcasted_iota(jnp.int32, (1, SLOT), 1) == FEAT0 + block

    def store_k(z):
        for h, ks in enumerate(_store_k_slots(z, kn_ref, ks_ref, k_o, block_lane)):
            for r in range(blocks_per_tile):
                km_o[r, :, _slot(h)] = jnp.mean(ks[r * MOBA_BLOCK:(r + 1) * MOBA_BLOCK], axis=0, keepdims=True)

    def store(ref):
        def consume(z):
            ref[...] = z
        return consume

    _pipelined([(proj(2), lambda z: _store_q_slots(z, qn_ref, q_o)), (proj(3), store_k),
                (lambda: _dot_nt(wvt_ref[...], hn), lambda zt: _store_vt_slots(zt, v_o)),
                (proj(0), store(gate_o)), (proj(1), store(xr_o))])


def _vt_spec(tm):
    return pl.BlockSpec((tm // ATTN_TILE, N_HEADS * VT_ROWS, ATTN_TILE), lambda i: (i, 0, 0))


def _vt_shape(n):
    return jax.ShapeDtypeStruct((n // ATTN_TILE, N_HEADS * VT_ROWS, ATTN_TILE), BF16)


def _logit_bounds(q_norm, k_norm, rel_bias):
    qk = HEAD_DIM * jnp.max(jnp.abs(q_norm)) * jnp.max(jnp.abs(k_norm)) * Q_SCALE * 1.02
    rel = (rel_bias - rel_bias[REL_BUCKETS - 1]) * LOG2E
    bound = qk + jnp.maximum(jnp.max(rel, axis=0), 0.0) + 1.0
    lowest = -qk + jnp.minimum(jnp.min(rel, axis=0), 0.0)
    fixed_ok = jnp.all(bound - lowest <= LOGIT_RANGE_MAX).astype(jnp.int32).reshape(1)
    k_shift = jnp.zeros((N_HEADS, SLOT), F32).at[:, SHIFT_LANE].set(-bound).reshape(1, SLOT_WIDTH)
    return k_shift, fixed_ok


def _slot_gain(g):
    return jnp.pad(g.astype(F32), (0, SLOT - HEAD_DIM)).reshape(1, SLOT)


def _mixin_even(x, g, w_in, q_norm, k_norm, k_shift, *, n_blocks, tm=1024):
    n, d = x.shape
    w = ATTN_WIDTH
    assert n % tm == 0 and tm % MOBA_BLOCK == 0 and tm % ATTN_TILE == 0 and w_in.shape[1] == 5 * w
    assert FEAT0 + n_blocks <= SHIFT_LANE
    row = lambda i: (i, 0)
    blk = tm // MOBA_BLOCK
    slot_shape = jax.ShapeDtypeStruct((n, SLOT_WIDTH), BF16)
    return pl.pallas_call(
        functools.partial(_mixin_even_kernel, n_blocks=n_blocks),
        grid=(n // tm,),
        in_specs=[pl.BlockSpec((tm, d), row), _const_spec((1, d)), _const_spec((d, 4 * w)),
                  _const_spec((w, d)), _const_spec((1, SLOT)), _const_spec((1, SLOT)),
                  _const_spec((1, SLOT_WIDTH))],
        out_specs=[pl.BlockSpec((tm, w), row)] * 2 + [pl.BlockSpec((tm, SLOT_WIDTH), row)] * 2
        + [_vt_spec(tm), pl.BlockSpec((blk, 1, SLOT_WIDTH), lambda i: (i, 0, 0))],
        out_shape=[jax.ShapeDtypeStruct((n, w), F32), jax.ShapeDtypeStruct((n, w), F32),
                   slot_shape, slot_shape, _vt_shape(n),
                   jax.ShapeDtypeStruct((n // MOBA_BLOCK, 1, SLOT_WIDTH), F32)],
        compiler_params=_cparams(("parallel",)),
        name="mixin_even",
    )(x, g.reshape(1, d), w_in[:, :4 * w].astype(BF16), w_in[:, 4 * w:].T.astype(BF16),
      _slot_gain(q_norm), _slot_gain(k_norm), k_shift)


def _conv_ln_silu(c, first, w_ref, b_ref, g_ref, beta_ref, o_ref, cbuf, sh_ref, *, halo):
    ts = c.shape[0]
    width = w_ref.shape[0]

    @pl.when(first)
    def _():
        cbuf[0:halo, :] = jnp.zeros((halo, cbuf.shape[1]), F32)

    cbuf[halo:halo + ts, :] = c
    base = halo - (width - 1)
    span = sh_ref.shape[1]
    for r in range(1, SUBLANES):
        sh_ref[r - 1] = cbuf[r:r + span, :]

    def chunk(c0):
        y = jnp.broadcast_to(b_ref[...], (CONV_ROWS, cbuf.shape[1]))
        for k in range(width):
            r = (base + k) % SUBLANES
            u0 = c0 + base + k - r
            win = cbuf[u0:u0 + CONV_ROWS, :] if r == 0 else sh_ref[r - 1, u0:u0 + CONV_ROWS, :]
            y = y + w_ref[k:k + 1, :] * win
        mu = jnp.mean(y, axis=-1, keepdims=True)
        yc = y - mu
        var = jnp.mean(yc * yc, axis=-1, keepdims=True)
        z = yc * lax.rsqrt(var + EPS) * g_ref[...] + beta_ref[...]
        o_ref[c0:c0 + CONV_ROWS, :] = (z * jax.nn.sigmoid(z)).astype(BF16)

    def close():
        cbuf[0:halo, :] = cbuf[ts:ts + halo, :]

    return [functools.partial(chunk, c0) for c0 in range(0, ts, CONV_ROWS)], close


def _mixin_odd_kernel(x_ref, g_ref, w_ref, wvt_ref, wki_ref, wwit_ref, qn_ref, kn_ref, ks_ref,
                      dw_ref, db_ref, lg_ref, lb_ref,
                      yc_o, q_o, k_o, v_o, qi_o, ki_o, wi_o, cbuf, sh_ref, *, tiles_per_seq, halo):
    hn = _rms_rows(x_ref[...], g_ref[...]).astype(BF16)
    w = ATTN_WIDTH
    proj = lambda c, n=1: _dot(hn, w_ref[:, c * w:(c + n) * w])
    first = pl.program_id(0) % tiles_per_seq == 0

    z_glu = proj(0, 2)
    z_q = proj(2)
    chunks, close = _conv_ln_silu(z_glu[:, 0:w] * jax.nn.sigmoid(z_glu[:, w:2 * w]), first,
                                  dw_ref, db_ref, lg_ref, lb_ref, yc_o, cbuf, sh_ref, halo=halo)
    per_stage = -(-len(chunks) // 4)
    run_chunks = lambda j: [f() for f in chunks[j * per_stage:(j + 1) * per_stage]]
    z_k = proj(3)
    _store_q_slots(z_q, qn_ref, q_o)
    run_chunks(0)
    z_vt = _dot_nt(wvt_ref[...], hn)
    _store_k_slots(z_k, kn_ref, ks_ref, k_o, None)
    run_chunks(1)
    z_qi = proj(4)
    _store_vt_slots(z_vt, v_o)
    run_chunks(2)
    z_ki = _dot(hn, wki_ref[...])
    z_wit = _dot_nt(wwit_ref[...], hn)
    qi_o[...] = z_qi.astype(BF16)
    run_chunks(3)
    close()
    ki_o[...] = z_ki.astype(BF16)
    wi_o[...] = z_wit * (IDX_DIM ** -0.5 * IDX_HEADS ** -0.5)


def _mixin_odd(x, g, w_in, q_norm, k_norm, k_shift, dw_w, dw_b, ln_g, ln_b, *, bsz, tm=512, halo=32):
    n, d = x.shape
    w = ATTN_WIDTH
    width = dw_w.shape[0]
    assert n % tm == 0 and tm % ATTN_TILE == 0 and w_in.shape[1] == 6 * w + IDX_DIM + IDX_HEADS
    assert (n // bsz) % tm == 0 and width - 1 <= halo <= tm and halo % SUBLANES == 0 and tm % CONV_ROWS == 0
    w_main = jnp.concatenate([w_in[:, :4 * w], w_in[:, 5 * w:6 * w]], axis=1).astype(BF16)
    w_vt = w_in[:, 4 * w:5 * w].T.astype(BF16)
    w_ki = w_in[:, 6 * w:6 * w + IDX_DIM]
    w_ki2 = jnp.concatenate([w_ki, w_ki], axis=1).astype(BF16)
    w_wit = w_in[:, 6 * w + IDX_DIM:].T.astype(BF16)
    row = lambda i: (i, 0)
    vec = lambda v: v.reshape(1, w).astype(F32)
    slot_shape = jax.ShapeDtypeStruct((n, SLOT_WIDTH), BF16)
    return pl.pallas_call(
        functools.partial(_mixin_odd_kernel, tiles_per_seq=(n // bsz) // tm, halo=halo),
        grid=(n // tm,),
        in_specs=[pl.BlockSpec((tm, d), row), _const_spec((1, d)), _const_spec((d, 5 * w)),
                  _const_spec((w, d)), _const_spec((d, 2 * IDX_DIM)), _const_spec((IDX_HEADS, d)),
                  _const_spec((1, SLOT)), _const_spec((1, SLOT)), _const_spec((1, SLOT_WIDTH)),
                  _const_spec((width, w)), _const_spec((1, w)), _const_spec((1, w)), _const_spec((1, w))],
        out_specs=[pl.BlockSpec((tm, w), row)] + [pl.BlockSpec((tm, SLOT_WIDTH), row)] * 2
        + [_vt_spec(tm), pl.BlockSpec((tm, w), row), pl.BlockSpec((tm, 128), row),
           pl.BlockSpec((IDX_HEADS, tm), lambda i: (0, i))],
        out_shape=[jax.ShapeDtypeStruct((n, w), BF16), slot_shape, slot_shape, _vt_shape(n),
                   jax.ShapeDtypeStruct((n, w), BF16),
                   jax.ShapeDtypeStruct((n, 128), BF16), jax.ShapeDtypeStruct((IDX_HEADS, n), F32)],
        scratch_shapes=[pltpu.VMEM((tm + halo, w), F32),
                        pltpu.VMEM((SUBLANES - 1, tm + halo - SUBLANES, w), F32)],
        compiler_params=_cparams(("arbitrary",)),
        name="mixin_odd",
    )(x, g.reshape(1, d), w_main, w_vt, w_ki2, w_wit, _slot_gain(q_norm), _slot_gain(k_norm), k_shift,
      dw_w, vec(dw_b), vec(ln_g), vec(ln_b))


def _lru_kernel(gate_ref, xr_ref, cw_ref, cb_ref, wa_ref, ba_ref, wx_ref, bx_ref, sp_ref,
                o_ref, xbuf, a_s, u_s, h_s, hc, *, ts):
    j = pl.program_id(1)

    @pl.when(j == 0)
    def _():
        xbuf[0:8, :] = jnp.zeros((8, xbuf.shape[1]), F32)
        hc[...] = jnp.zeros(hc.shape, F32)

    xbuf[8:8 + ts, :] = xr_ref[...]
    xc = cb_ref[...] + cw_ref[0:1, :] * xbuf[5:5 + ts, :]
    for k in range(1, 4):
        xc = xc + cw_ref[k:k + 1, :] * xbuf[5 + k:5 + k + ts, :]
    xbuf[0:8, :] = xbuf[ts:ts + 8, :]

    xcb = xc.astype(BF16)
    r = jax.nn.sigmoid(_dot(xcb, wa_ref[...]) + ba_ref[...])
    ig = jax.nn.sigmoid(_dot(xcb, wx_ref[...]) + bx_ref[...])
    log_a = -LRU_C * r * sp_ref[...]
    a = jnp.exp(log_a)
    a_s[...] = a
    u_s[...] = jnp.sqrt(-jnp.tanh(log_a) * (a * a + 1.0)) * (ig * xc)

    row = lax.broadcasted_iota(jnp.int32, (8, a_s.shape[1]), 0)

    def body(g, carry):
        r0 = pl.multiple_of(g * 8, 8)
        a = a_s[pl.ds(r0, 8), :]
        u = u_s[pl.ds(r0, 8), :]
        for s in (1, 2, 4):
            ok = row >= s
            a_sh = jnp.where(ok, pltpu.roll(a, s, 0), 1.0)
            u_sh = jnp.where(ok, pltpu.roll(u, s, 0), 0.0)
            u = a * u_sh + u
            a = a * a_sh
        h = a * carry + u
        h_s[pl.ds(r0, 8), :] = h
        return h[7:8, :]

    hc[...] = lax.fori_loop(0, ts // 8, body, hc[...], unroll=4)
    o_ref[...] = (h_s[...] * jax.nn.gelu(gate_ref[...])).astype(BF16)


def _block_diag(wb):
    nb, bs, _ = wb.shape
    eye = jnp.eye(nb, dtype=wb.dtype)
    return (eye[:, None, :, None] * wb[:, :, None, :]).reshape(nb * bs, nb * bs)


def _lru(gate, xr, conv_w, conv_b, ra_w, ra_b, ix_w, ix_b, lam, *, bsz, ts=512):
    n, w = xr.shape
    seq = n // bsz
    assert seq % ts == 0
    nt = seq // ts
    row = lambda b, j: (b * nt + j, 0)
    vec = lambda v: v.reshape(1, w).astype(F32)
    return pl.pallas_call(
        functools.partial(_lru_kernel, ts=ts),
        grid=(bsz, nt),
        in_specs=[pl.BlockSpec((ts, w), row), pl.BlockSpec((ts, w), row),
                  _const_spec((conv_w.shape[0], w)), _const_spec((1, w)),
                  _const_spec((w, w)), _const_spec((1, w)), _const_spec((w, w)), _const_spec((1, w)),
                  _const_spec((1, w))],
        out_specs=pl.BlockSpec((ts, w), row),
        out_shape=jax.ShapeDtypeStruct((n, w), BF16),
        scratch_shapes=[pltpu.VMEM((ts + 8, w), F32), pltpu.VMEM((ts, w), F32),
                        pltpu.VMEM((ts, w), F32), pltpu.VMEM((ts, w), F32), pltpu.VMEM((1, w), F32)],
        compiler_params=_cparams(("arbitrary", "arbitrary")),
        name="rg_lru",
    )(gate, xr, conv_w, vec(conv_b), _block_diag(ra_w).astype(BF16), vec(ra_b),
      _block_diag(ix_w).astype(BF16), vec(ix_b), vec(jax.nn.softplus(-lam)))


def _bucket_tiles(t):
    assert t > REL_MAX_DIST
    n = np.arange(2 * t)
    nf = np.maximum(n, 1).astype(np.float32)
    large = REL_MAX_EXACT + (np.log(nf / np.float32(REL_MAX_EXACT))
                             / np.float32(math.log(REL_MAX_DIST / REL_MAX_EXACT))
                             * np.float32(REL_BUCKETS - REL_MAX_EXACT)).astype(np.int32)
    bucket = np.where(n < REL_MAX_EXACT, n, np.minimum(large, REL_BUCKETS - 1)).astype(np.int32)
    qry = np.arange(t)[None, :]
    key = np.arange(t)[:, None]
    return np.stack([bucket[np.maximum(qry - key, 0)], bucket[t + qry - key]])


def _build_bias(idx_ref, rb_ref, bias_ref):
    t = idx_ref.shape[1]
    causal = (lax.broadcasted_iota(jnp.int32, (t, t), 0) <= lax.broadcasted_iota(jnp.int32, (t, t), 1))
    for h in range(N_HEADS):
        far = rb_ref[REL_BUCKETS - 1, h]
        for which in (0, 1):
            idx = idx_ref[which]
            acc = jnp.zeros((t, t), F32)
            for b in range(REL_BUCKETS - 1):
                acc = jnp.where(idx == b, (rb_ref[b, h] - far) * LOG2E, acc)
            if which == 0:
                bias_ref[h, t:2 * t, :] = jnp.where(causal, acc, NEG)
            else:
                bias_ref[h, 0:t, :] = acc


def _attn_init(m_ref, acc_ref):
    m_ref[...] = jnp.full(m_ref.shape, M_INIT, F32)
    acc_ref[...] = jnp.zeros(acc_ref.shape, F32)


def _softmax_step(s, h, vt_h, m_ref, acc_ref):
    m_old = m_ref[h]
    m_new = jnp.maximum(m_old, jnp.max(s, axis=0, keepdims=True))
    p = jnp.exp2(s - m_new)
    m_ref[h] = m_new
    acc_ref[h] = jnp.exp2(m_old - m_new) * acc_ref[h] + _dot(vt_h, p.astype(BF16))


def _fixed_step(s, h, vt_h, acc_ref):
    acc_ref[h] = acc_ref[h] + _dot(vt_h, jnp.exp2(s).astype(BF16))


def _heads_pipelined(logits, vt_slot, m_ref, acc_ref, online):
    ahead = [logits(h) for h in range(HEADS_AHEAD)]
    for h in range(N_HEADS):
        if h + HEADS_AHEAD < N_HEADS:
            ahead.append(logits(h + HEADS_AHEAD))
        if online:
            _softmax_step(ahead[h], h, vt_slot(h), m_ref, acc_ref)
        else:
            _fixed_step(ahead[h], h, vt_slot(h), acc_ref)


FAR_QUAD = ("far", "far", "far", "far")
FAR_PAIR = ("far", "far")
FAR_ONE = ("far",)
NEAR_DIAG = ("near", "diag")
DIAG_ONE = ("diag",)


def _attend_tiles(kt, kinds, q_slot, k_ref, vt_ref, bias_ref, mask_add, m_ref, acc_ref, online):
    t = ATTN_TILE
    n = len(kinds)
    rows = pl.ds(pl.multiple_of(kt * t, t), n * t)

    def logits(h):
        s = _dot_nt(k_ref[rows, _slot(h)], q_slot(h))
        if kinds == NEAR_DIAG:
            s = s + bias_ref[h]
        elif kinds == DIAG_ONE:
            s = s + bias_ref[h, t:2 * t, :]
        if mask_add is not None:
            s = s + mask_add
        return s

    def vt_slot(h):
        parts = [vt_ref[kt + a, _vt_rows(h), :] for a in range(n)]
        return parts[0] if n == 1 else jnp.concatenate(parts, axis=1)

    _heads_pipelined(logits, vt_slot, m_ref, acc_ref, online)


def _for_causal_tiles(i, tiles, far_group):
    n_far = jnp.maximum(i - 1, 0)
    assert far_group in (2, 4)

    def group(j, carry):
        tiles(far_group * j, FAR_QUAD if far_group == 4 else FAR_PAIR)
        return carry

    lax.fori_loop(0, n_far // far_group, group, 0)

    if far_group == 4:
        @pl.when(n_far % 4 >= 2)
        def _():
            tiles((n_far // 4) * 4, FAR_PAIR)

    @pl.when(n_far % 2 == 1)
    def _():
        tiles(n_far - 1, FAR_ONE)

    @pl.when(i >= 1)
    def _():
        tiles(i - 1, NEAR_DIAG)

    @pl.when(i == 0)
    def _():
        tiles(i, DIAG_ONE)


def _attend_either(ok_ref, attend):
    @pl.when(ok_ref[0] != 0)
    def _():
        attend(False)

    @pl.when(ok_ref[0] == 0)
    def _():
        attend(True)


def _attn_finish(o_ref, acc_ref):
    parts = []
    for h in range(N_HEADS):
        acc = acc_ref[h]
        parts.append(acc[0:HEAD_DIM] * (1.0 / acc[FEAT0:FEAT0 + 1]))
    o_ref[...] = jnp.concatenate(parts, axis=0).T.astype(BF16)


def _smem_spec():
    return pl.BlockSpec(memory_space=pltpu.SMEM)


def _moba_kernel(rb_ref, ok_ref, q_ref, k_ref, vt_ref, km_ref, idx_ref, o_ref,
                 acc_ref, m_ref, qs_ref, bias_ref, *, n_blocks):
    i = pl.program_id(1)
    t = ATTN_TILE
    nbp = -(-n_blocks // 8) * 8

    @pl.when((pl.program_id(0) == 0) & (i == 0))
    def _():
        _build_bias(idx_ref, rb_ref, bias_ref)

    _attn_init(m_ref, acc_ref)

    blk = lax.broadcasted_iota(jnp.int32, (nbp, t), 0)
    past = blk < i
    for h in range(N_HEADS):
        qh = q_ref[:, _slot(h)]
        km_hi, km_lo = _split_bf16(km_ref[:, _slot(h)])
        gate_t = _dot_nt(km_hi, qh) + _dot_nt(km_lo, qh)
        g = jnp.where(past, gate_t[FEAT0:FEAT0 + nbp, :], NEG)
        rank = jnp.zeros((nbp, t), F32)
        for j in range(n_blocks):
            gj = g[j:j + 1, :]
            beats = (gj > g) | ((gj == g) & (blk > j))
            rank = rank + jnp.where(beats, 1.0, 0.0)
        flag = jnp.where(past & (rank >= MOBA_TOPK), NEG, 0.0)
        flag_t = jnp.concatenate([jnp.zeros((FEAT0, t), F32), flag,
                                  jnp.zeros((SLOT - FEAT0 - nbp, t), F32)], axis=0)
        qs_ref[:, _slot(h)] = (qh.astype(F32) + flag_t.T).astype(BF16)

    def attend(online):
        def tiles(kt, kinds):
            _attend_tiles(kt, kinds, lambda h: qs_ref[:, _slot(h)], k_ref, vt_ref, bias_ref, None,
                          m_ref, acc_ref, online)
        _for_causal_tiles(i, tiles, MOBA_FAR_GROUP)

    _attend_either(ok_ref, attend)
    _attn_finish(o_ref, acc_ref)


def _moba(q, k, vt, kmean, rel_bias, fixed_ok, *, bsz):
    n = q.shape[0]
    seq = n // bsz
    t = ATTN_TILE
    assert seq % t == 0 and t == MOBA_BLOCK
    nq = seq // t
    assert FEAT0 + nq <= SHIFT_LANE
    km = jnp.pad(kmean.reshape(bsz, nq, SLOT_WIDTH), ((0, 0), (FEAT0, SLOT - FEAT0 - nq), (0, 0)))
    r3 = lambda a: a.reshape(bsz, seq, SLOT_WIDTH)
    seq_spec = pl.BlockSpec((None, seq, SLOT_WIDTH), lambda b, i: (b, 0, 0))
    out = pl.pallas_call(
        functools.partial(_moba_kernel, n_blocks=nq),
        grid=(bsz, nq),
        in_specs=[_smem_spec(), _smem_spec(),
                  pl.BlockSpec((None, t, SLOT_WIDTH), lambda b, i: (b, i, 0)), seq_spec,
                  pl.BlockSpec((nq, N_HEADS * VT_ROWS, t), lambda b, i: (b, 0, 0)),
                  pl.BlockSpec((None, SLOT, SLOT_WIDTH), lambda b, i: (b, 0, 0)),
                  _const_spec((2, t, t))],
        out_specs=pl.BlockSpec((None, t, ATTN_WIDTH), lambda b, i: (b, i, 0)),
        out_shape=jax.ShapeDtypeStruct((bsz, seq, ATTN_WIDTH), BF16),
        scratch_shapes=[pltpu.VMEM((N_HEADS, VT_ROWS, t), F32), pltpu.VMEM((N_HEADS, 1, t), F32),
                        pltpu.VMEM((t, SLOT_WIDTH), BF16),
                        pltpu.VMEM((N_HEADS, 2 * t, t), F32)],
        compiler_params=_cparams(("arbitrary", "arbitrary")),
        name="moba_attention",
    )(rel_bias, fixed_ok, r3(q), r3(k), vt, km, jnp.asarray(_bucket_tiles(t)))
    return out.reshape(n, ATTN_WIDTH)


def _dsa_kernel(rb_ref, ok_ref, q_ref, k_ref, vt_ref, qi_ref, ki_ref, wit_ref, idx_ref, tri_ref, o_ref,
                acc_ref, m_ref, isc_ref, bc_ref, bias_ref, *, n_sel):
    i = pl.program_id(1)
    t = ATTN_TILE
    key = lax.broadcasted_iota(jnp.int32, (t, t), 0)
    qry = lax.broadcasted_iota(jnp.int32, (t, t), 1)
    lane128 = lax.broadcasted_iota(jnp.int32, (1, 128), 1)

    @pl.when((pl.program_id(0) == 0) & (i == 0))
    def _():
        _build_bias(idx_ref, rb_ref, bias_ref)

    _attn_init(m_ref, acc_ref)

    def index_tiles(kt, n, diag):
        rows = pl.ds(pl.multiple_of(kt * t, t), n * t)
        ki2 = ki_ref[rows, :]
        acc = jnp.zeros((n * t, t), F32)
        for pr in range(IDX_HEADS // 2):
            q2 = qi_ref[:, pr * 128:(pr + 1) * 128]
            for half in range(2):
                h = 2 * pr + half
                hm = (lane128 >= half * IDX_DIM) & (lane128 < (half + 1) * IDX_DIM)
                s = _dot_nt(ki2, jnp.where(hm, q2, jnp.zeros((), BF16)))
                acc = acc + jnp.maximum(s, 0.0) * wit_ref[h:h + 1, :]
        if diag:
            acc = jnp.where(key <= qry, acc, -jnp.inf)
        for a in range(n):
            isc_ref[kt + a] = acc[a * t:(a + 1) * t]

    def index_pair(j, carry):
        index_tiles(2 * j, 2, False)
        return carry

    lax.fori_loop(0, i // 2, index_pair, 0)

    @pl.when(i % 2 == 1)
    def _():
        index_tiles(i - 1, 1, False)

    index_tiles(i, 1, True)

    def fold8(x, op):
        return op(x.reshape(t // 8, 8, t), axis=0)

    def minmax_body(kt, carry):
        mn, mx = carry
        x = isc_ref[kt]
        mx = jnp.maximum(mx, fold8(x, jnp.max))
        mn = jnp.minimum(mn, fold8(jnp.where(x == -jnp.inf, jnp.inf, x), jnp.min))
        return mn, mx

    mn, mx = lax.fori_loop(0, i + 1, minmax_body,
                           (jnp.full((8, t), jnp.inf, F32), jnp.full((8, t), -jnp.inf, F32)))
    lo0 = jnp.min(mn, axis=0, keepdims=True)
    mx = jnp.max(mx, axis=0, keepdims=True)
    hi0 = mx + jnp.abs(mx) * 1e-3 + 1e-30
    n_valid = (i * t + 1 + lax.broadcasted_iota(jnp.int32, (1, t), 1)).astype(F32)
    want = jnp.minimum(n_valid, float(n_sel))

    def count_ge(thr):
        def one(kt, c):
            return c + fold8(jnp.where(isc_ref[kt] >= thr, 1.0, 0.0), jnp.sum)

        def pair(j, c):
            return one(2 * j + 1, one(2 * j, c))

        c = lax.fori_loop(0, (i + 1) // 2, pair, jnp.zeros((8, t), F32))
        c = lax.cond((i + 1) % 2 == 1, lambda c: one(i, c), lambda c: c, c)
        return jnp.sum(c, axis=0, keepdims=True)

    c_nonneg = count_ge(0.0)
    c_pos = count_ge(F32_TINY)
    above = want <= c_pos
    zero_tie = (want > c_pos) & (want <= c_nonneg)
    lo1 = jnp.where(above, F32_TINY, jnp.where(zero_tie, 0.0, lo0))
    hi1 = jnp.where(above, hi0, jnp.where(zero_tie, F32_TINY, 0.0))
    c_lo1 = jnp.where(above, c_pos, jnp.where(zero_tie, c_nonneg, n_valid))
    c_hi1 = jnp.where(above, 0.0, jnp.where(zero_tie, c_pos, c_nonneg))

    def bisect_step(_, carry):
        lo, hi, c_lo, c_hi = carry
        mid = 0.5 * (lo + hi)
        c_mid = count_ge(mid)
        up = c_mid >= want
        return (jnp.where(up, mid, lo), jnp.where(up, hi, mid),
                jnp.where(up, c_mid, c_lo), jnp.where(up, c_hi, c_mid))

    def bisect_cond(carry):
        it, _, _, c_lo, _ = carry
        return (it < BISECT_ITERS) & (jnp.max(jnp.where(zero_tie, 0.0, c_lo - want)) > 0.0)

    def bisect_body(carry):
        return (carry[0] + BISECT_CHECK_EVERY,) + lax.fori_loop(0, BISECT_CHECK_EVERY, bisect_step, carry[1:])

    _, lo, hi, c_lo, c_hi = lax.while_loop(bisect_cond, bisect_body, (0, lo1, hi1, c_lo1, c_hi1))
    need = want - c_hi
    bc_ref[...] = jnp.zeros(bc_ref.shape, F32)

    def select_mask(kt):
        x = isc_ref[kt]
        band = jnp.where((x >= lo) & (x < hi), 1.0, 0.0)
        before = bc_ref[...] + _dot(tri_ref[...], band.astype(BF16))
        bc_ref[...] = bc_ref[...] + jnp.sum(band, axis=0, keepdims=True)
        return jnp.where((x >= hi) | ((band > 0.0) & (before < need)), 0.0, NEG)

    def attend(online):
        def tiles(kt, kinds):
            masks = [select_mask(kt + a) for a in range(len(kinds))]
            mask_add = masks[0] if len(masks) == 1 else jnp.concatenate(masks, axis=0)
            _attend_tiles(kt, kinds, lambda h: q_ref[:, _slot(h)], k_ref, vt_ref, bias_ref, mask_add,
                          m_ref, acc_ref, online)
        _for_causal_tiles(i, tiles, DSA_FAR_GROUP)

    _attend_either(ok_ref, attend)
    _attn_finish(o_ref, acc_ref)


def _dsa(q, k, vt, qi, ki2, wit, rel_bias, fixed_ok, *, bsz):
    n = q.shape[0]
    seq = n // bsz
    t = ATTN_TILE
    assert seq % t == 0
    nq = seq // t
    n_sel = min(DSA_TOPK_MAX, seq // 4)
    tri = (jnp.arange(t)[None, :] < jnp.arange(t)[:, None]).astype(BF16)
    r3 = lambda a: a.reshape(bsz, seq, a.shape[-1])
    tile_spec = lambda width: pl.BlockSpec((None, t, width), lambda b, i: (b, i, 0))
    seq_spec = lambda width: pl.BlockSpec((None, seq, width), lambda b, i: (b, 0, 0))
    out = pl.pallas_call(
        functools.partial(_dsa_kernel, n_sel=n_sel),
        grid=(bsz, nq),
        in_specs=[_smem_spec(), _smem_spec(), tile_spec(SLOT_WIDTH), seq_spec(SLOT_WIDTH),
                  pl.BlockSpec((nq, N_HEADS * VT_ROWS, t), lambda b, i: (b, 0, 0)),
                  tile_spec(ATTN_WIDTH), seq_spec(128),
                  pl.BlockSpec((IDX_HEADS, t), lambda b, i: (0, b * nq + i)),
                  _const_spec((2, t, t)), _const_spec((t, t))],
        out_specs=tile_spec(ATTN_WIDTH),
        out_shape=jax.ShapeDtypeStruct((bsz, seq, ATTN_WIDTH), BF16),
        scratch_shapes=[pltpu.VMEM((N_HEADS, VT_ROWS, t), F32), pltpu.VMEM((N_HEADS, 1, t), F32),
                        pltpu.VMEM((nq, t, t), F32), pltpu.VMEM((1, t), F32),
                        pltpu.VMEM((N_HEADS, 2 * t, t), F32)],
        compiler_params=_cparams(("arbitrary", "arbitrary")),
        name="dsa_attention",
    )(rel_bias, fixed_ok, r3(q), r3(k), vt, r3(qi), r3(ki2), wit, jnp.asarray(_bucket_tiles(t)), tri)
    return out.reshape(n, ATTN_WIDTH)


def kernel(x, rel_bias, ffn1_norm, ffn1_w_gate, ffn1_w_up, ffn1_w_down, mix_norm, ffn2_norm, ffn2_w_gate, ffn2_w_up, ffn2_w_down, ev_w_in, ev_conv_w, ev_conv_b, ev_ra_w, ev_ra_b, ev_ix_w, ev_ix_b, ev_lambda, ev_q_norm, ev_k_norm, ev_w_out, od_w_in, od_dw_w, od_dw_b, od_ln_g, od_ln_b, od_q_norm, od_k_norm, od_w_out):
    bsz, seq, d = x.shape
    depth = ffn1_norm.shape[0]
    h = x.reshape(bsz * seq, d)
    for i in range(depth):
        h = _ffn(h, ffn1_norm, ffn1_w_gate, ffn1_w_up, ffn1_w_down, i)
        j = i // 2
        if i % 2 == 0:
            k_shift, fixed_ok = _logit_bounds(ev_q_norm[j], ev_k_norm[j], rel_bias)
            gate, xr, q, k, vt, kmean = _mixin_even(h, mix_norm[i], ev_w_in[j], ev_q_norm[j], ev_k_norm[j],
                                                    k_shift, n_blocks=seq // MOBA_BLOCK)
            ya = _lru(gate, xr, ev_conv_w[j], ev_conv_b[j], ev_ra_w[j], ev_ra_b[j],
                      ev_ix_w[j], ev_ix_b[j], ev_lambda[j], bsz=bsz)
            yb = _moba(q, k, vt, kmean, rel_bias, fixed_ok, bsz=bsz)
            mix = (ya, yb, ev_w_out[j])
        else:
            k_shift, fixed_ok = _logit_bounds(od_q_norm[j], od_k_norm[j], rel_bias)
            yc, q, k, vt, qi, ki2, wit = _mixin_odd(h, mix_norm[i], od_w_in[j], od_q_norm[j], od_k_norm[j],
                                                    k_shift, od_dw_w[j], od_dw_b[j], od_ln_g[j], od_ln_b[j],
                                                    bsz=bsz)
            yd = _dsa(q, k, vt, qi, ki2, wit, rel_bias, fixed_ok, bsz=bsz)
            mix = (yc, yd, od_w_out[j])
        h = _ffn(h, ffn2_norm, ffn2_w_gate, ffn2_w_up, ffn2_w_down, i, mix)
    return h.reshape(bsz, seq, d)
```
